```python
import math
import jax, jax.numpy as jnp
from jax import lax
import numpy as np

D_MODEL = 1024
BATCH = 8
SEQ = 2048
DEPTH = 1
DEC_BATCH = 8
DEC_SEQ = 32
PAST_LEN = 4096

CHUNK = 64
Q_BLOCK = 128
EPS = 1e-6
GLA_HEADS = 4
GLA_DK = D_MODEL // 2 // GLA_HEADS
GLA_DV = D_MODEL // GLA_HEADS
GLA_RANK = 16
GLA_TAU = 16.0
GLA_CHUNK = 64
DIFF_HEADS = 8
DIFF_DH = D_MODEL // DIFF_HEADS // 2
DIFF_DV = 2 * DIFF_DH
N_GROUPS = 4
EXPERTS_PER_GROUP = 4
N_EXPERTS = N_GROUPS * EXPERTS_PER_GROUP
TOP_K = 2
D_EXPERT = 512
EXPERT_BLOCK = 128
GLA_QK = GLA_HEADS * GLA_DK
GLA_V = GLA_HEADS * GLA_DV
DIFF_QK = DIFF_HEADS * 2 * DIFF_DH
DIFF_V = DIFF_HEADS * DIFF_DV
IN_SIZES = (GLA_QK, GLA_QK, GLA_V, GLA_V, GLA_RANK, DIFF_QK, DIFF_QK, DIFF_V, D_MODEL, D_MODEL)
D_IN = sum(IN_SIZES)

kernel_name = "gla_diffattn_hmoe_streaming_step"


def rms_norm(x, g):
    xf = x.astype(jnp.float32)
    y = xf * lax.rsqrt(jnp.mean(xf * xf, axis=-1, keepdims=True) + EPS)
    return (y * g.astype(jnp.float32)).astype(x.dtype)


def alibi_slopes(n):
    return jnp.asarray([2.0 ** (-8.0 * (h + 1) / n) for h in range(n)], jnp.float32)


def gla_scan(q, k, v, log_a, s0, chunk):
    B, L, H, dk = q.shape
    dv = v.shape[-1]
    n = L // chunk

    def to_chunks(t):
        return t.astype(jnp.float32).reshape(B, n, chunk, H, t.shape[-1]).transpose(1, 0, 3, 2, 4)

    qc = to_chunks(q) * (dk ** -0.5)
    kc, vc, gc = to_chunks(k), to_chunks(v), to_chunks(log_a)
    causal = jnp.tril(jnp.ones((chunk, chunk), dtype=bool))

    def step(S, inp):
        qi, ki, vi, gi = inp
        b = jnp.cumsum(gi, axis=-2)
        b_last = b[..., -1:, :]
        q_t = qi * jnp.exp(b)
        k_t = ki * jnp.exp(-b)
        o = jnp.einsum('bhcd,bhde->bhce', q_t, S)
        att = jnp.where(causal, jnp.einsum('bhcd,bhsd->bhcs', q_t, k_t), 0.0)
        o = o + jnp.einsum('bhcs,bhse->bhce', att, vi)
        k_end = ki * jnp.exp(b_last - b)
        S = jnp.exp(b_last[..., 0, :])[..., None] * S + jnp.einsum('bhcd,bhce->bhde', k_end, vi)
        return S, o

    S, o = lax.scan(step, s0.astype(jnp.float32), (qc, kc, vc, gc))
    o = o.transpose(1, 0, 3, 2, 4).reshape(B, L, H, dv)
    return o, S


def diff_attention(q, k, v, q_pos, k_pos, lam, slopes):
    qf = q.astype(jnp.float32) * (DIFF_DH ** -0.5)
    kf = k.astype(jnp.float32)
    allowed = (k_pos[None, :] // CHUNK) <= (q_pos[:, None] // CHUNK)
    dist = jnp.abs(q_pos[:, None] - k_pos[None, :]).astype(jnp.float32)
    bias = jnp.where(allowed[None], -slopes[:, None, None] * dist[None], -jnp.inf)
    s1 = jnp.einsum('bqhd,bshd->bhqs', qf[..., :DIFF_DH], kf[..., :DIFF_DH]) + bias
    s2 = jnp.einsum('bqhd,bshd->bhqs', qf[..., DIFF_DH:], kf[..., DIFF_DH:]) + bias
    attn = jax.nn.softmax(s1, axis=-1) - lam * jax.nn.softmax(s2, axis=-1)
    return jnp.einsum('bhqs,bshe->bqhe', attn, v.astype(jnp.float32))


def hmoe(h, w_rg, b_rg, w_re, b_re, w_gate, w_up, w_down):
    B, L, D = h.shape
    T = B * L
    hf = h.reshape(T, D)
    g_logits = (hf @ w_rg + b_rg).astype(jnp.float32)
    g_prob = jax.nn.softmax(g_logits, axis=-1)
    g_sel = jnp.argmax(g_logits, axis=-1)
    p_g = jnp.take_along_axis(g_prob, g_sel[:, None], axis=-1)
    e_logits = (hf @ w_re + b_re).astype(jnp.float32).reshape(T, N_GROUPS, EXPERTS_PER_GROUP)
    idx = jnp.broadcast_to(g_sel[:, None, None], (T, 1, EXPERTS_PER_GROUP))
    e_logits = jnp.take_along_axis(e_logits, idx, axis=1)[:, 0]
    top_v, top_i = lax.top_k(e_logits, TOP_K)
    wts = jax.nn.softmax(top_v, axis=-1) * p_g
    eid = g_sel[:, None] * EXPERTS_PER_GROUP + top_i

    A = T * TOP_K
    flat_e = eid.reshape(A)
    flat_t = jnp.repeat(jnp.arange(T, dtype=jnp.int32), TOP_K)
    flat_w = wts.reshape(A)
    order = jnp.argsort(flat_e)
    se, st, sw = flat_e[order], flat_t[order], flat_w[order]
    counts = jnp.bincount(flat_e, length=N_EXPERTS)
    padded = ((counts + EXPERT_BLOCK - 1) // EXPERT_BLOCK) * EXPERT_BLOCK
    starts = jnp.cumsum(counts) - counts
    pad_ends = jnp.cumsum(padded)
    pad_starts = pad_ends - padded
    dest = pad_starts[se] + jnp.arange(A) - starts[se]
    n_blocks = -(-A // EXPERT_BLOCK) + N_EXPERTS
    P = n_blocks * EXPERT_BLOCK
    slot_tok = jnp.full((P,), T, jnp.int32).at[dest].set(st)
    slot_w = jnp.zeros((P,), jnp.float32).at[dest].set(sw)
    block_e = jnp.searchsorted(pad_ends, jnp.arange(n_blocks) * EXPERT_BLOCK, side='right')
    block_e = jnp.minimum(block_e, N_EXPERTS - 1)
    x_pad = jnp.concatenate([hf, jnp.zeros((1, D), hf.dtype)], axis=0)
    xb = x_pad[slot_tok].reshape(n_blocks, EXPERT_BLOCK, D)

    def expert_block(args):
        xi, e = args
        return (jax.nn.silu(xi @ w_gate[e]) * (xi @ w_up[e])) @ w_down[e]

    yb = lax.map(expert_block, (xb, block_e)).reshape(P, D)
    y = jnp.zeros((T + 1, D), jnp.float32).at[slot_tok].add(yb.astype(jnp.float32) * slot_w[:, None])
    return y[:T].astype(h.dtype).reshape(B, L, D)


def trunk_layer(x, p, s0, k_past, v_past, lam_init):
    B, L, _ = x.shape
    h = rms_norm(x, p['norm_mix'])
    z = h @ p['w_in']
    offsets = [int(o) for o in np.cumsum(IN_SIZES)[:-1]]
    qa, ka, va, ra, da, qb, kb, vb, ga, gb = jnp.split(z, offsets, axis=-1)

    log_a = jax.nn.log_sigmoid((da @ p['w_decay'] + p['b_decay']).astype(jnp.float32)) / GLA_TAU
    qa = qa.reshape(B, L, GLA_HEADS, GLA_DK)
    ka = ka.reshape(B, L, GLA_HEADS, GLA_DK)
    va = va.reshape(B, L, GLA_HEADS, GLA_DV)
    log_a = log_a.reshape(B, L, GLA_HEADS, GLA_DK)
    o_a, s_new = gla_scan(qa, ka, va, log_a, s0, min(GLA_CHUNK, L))
    o_a = rms_norm(o_a, p['gla_norm']).astype(x.dtype) * jax.nn.silu(ra.reshape(B, L, GLA_HEADS, GLA_DV))
    u_a = o_a.reshape(B, L, GLA_V) @ p['w_proj_a']

    qb = qb.reshape(B, L, DIFF_HEADS, 2 * DIFF_DH)
    kb = kb.reshape(B, L, DIFF_HEADS, 2 * DIFF_DH)
    vb = vb.reshape(B, L, DIFF_HEADS, DIFF_DV)
    lq1, lk1 = p['lambda_q1'].astype(jnp.float32), p['lambda_k1'].astype(jnp.float32)
    lq2, lk2 = p['lambda_q2'].astype(jnp.float32), p['lambda_k2'].astype(jnp.float32)
    lam = jnp.exp(jnp.sum(lq1 * lk1)) - jnp.exp(jnp.sum(lq2 * lk2)) + lam_init
    slopes = alibi_slopes(DIFF_HEADS)
    if k_past is None:
        pos = jnp.arange(L, dtype=jnp.int32)
        nqb = L // Q_BLOCK
        q_blocks = qb.reshape(B, nqb, Q_BLOCK, DIFF_HEADS, 2 * DIFF_DH).swapaxes(0, 1)
        pos_blocks = pos.reshape(nqb, Q_BLOCK)
        o_b = lax.map(lambda a: diff_attention(a[0], kb, vb, a[1], pos, lam, slopes), (q_blocks, pos_blocks))
        o_b = o_b.swapaxes(0, 1).reshape(B, L, DIFF_HEADS, DIFF_DV)
    else:
        past = k_past.shape[1]
        k_all = jnp.concatenate([k_past.astype(kb.dtype), kb], axis=1)
        v_all = jnp.concatenate([v_past.astype(vb.dtype), vb], axis=1)
        q_pos = past + jnp.arange(L, dtype=jnp.int32)
        k_pos = jnp.arange(past + L, dtype=jnp.int32)
        o_b = diff_attention(qb, k_all, v_all, q_pos, k_pos, lam, slopes)
    o_b = rms_norm(o_b, p['diff_norm']) * (1.0 - lam_init)
    u_b = o_b.astype(x.dtype).reshape(B, L, DIFF_V) @ p['w_proj_b']

    mix = jax.nn.sigmoid(ga) * u_a + jax.nn.sigmoid(gb) * u_b
    x = x + mix @ p['w_out']
    x = x + hmoe(rms_norm(x, p['norm_ffn']), p['w_router_group'], p['b_router_group'],
                 p['w_router_expert'], p['b_router_expert'], p['w_gate'], p['w_up'], p['w_down'])
    return x, s_new, kb, vb


def setup_inputs(seed: int = 0) -> dict:
    key = jax.random.key(seed)
    ks = jax.random.split(key, 32)
    f32 = jnp.float32

    def nrm(k, shape, s):
        return jax.random.normal(k, shape, f32) * s

    def gain(k, shape):
        return 1.0 + 0.02 * jax.random.normal(k, shape, f32)

    return {
        "x_prompt": nrm(ks[0], (BATCH, SEQ, D_MODEL), 1.0),
        "x_sample": nrm(ks[1], (DEC_BATCH, DEC_SEQ, D_MODEL), 1.0),
        "cache_k": nrm(ks[2], (DEPTH, DEC_BATCH, PAST_LEN, DIFF_HEADS, 2 * DIFF_DH), 1.0),
        "cache_v": nrm(ks[3], (DEPTH, DEC_BATCH, PAST_LEN, DIFF_HEADS, DIFF_DV), 1.0),
        "state_gla": nrm(ks[4], (DEPTH, DEC_BATCH, GLA_HEADS, GLA_DK, GLA_DV), 1.0),
        "norm_mix": gain(ks[5], (DEPTH, D_MODEL)),
        "w_in": nrm(ks[6], (DEPTH, D_MODEL, D_IN), D_MODEL ** -0.5),
        "w_decay": nrm(ks[7], (DEPTH, GLA_RANK, GLA_QK), GLA_RANK ** -0.5),
        "b_decay": nrm(ks[8], (DEPTH, GLA_QK), 0.1),
        "gla_norm": gain(ks[9], (DEPTH, GLA_DV)),
        "w_proj_a": nrm(ks[10], (DEPTH, GLA_V, D_MODEL), GLA_V ** -0.5),
        "lambda_q1": nrm(ks[11], (DEPTH, DIFF_DH), 0.1),
        "lambda_k1": nrm(ks[12], (DEPTH, DIFF_DH), 0.1),
        "lambda_q2": nrm(ks[13], (DEPTH, DIFF_DH), 0.1),
        "lambda_k2": nrm(ks[14], (DEPTH, DIFF_DH), 0.1),
        "diff_norm": gain(ks[15], (DEPTH, DIFF_DV)),
        "w_proj_b": nrm(ks[16], (DEPTH, DIFF_V, D_MODEL), DIFF_V ** -0.5),
        "w_out": nrm(ks[17], (DEPTH, D_MODEL, D_MODEL), D_MODEL ** -0.5),
        "norm_ffn": gain(ks[18], (DEPTH, D_MODEL)),
        "w_router_group": nrm(ks[19], (DEPTH, D_MODEL, N_GROUPS), D_MODEL ** -0.5),
        "b_router_group": nrm(ks[20], (DEPTH, N_GROUPS), 0.01),
        "w_router_expert": nrm(ks[21], (DEPTH, D_MODEL, N_EXPERTS), D_MODEL ** -0.5),
        "b_router_expert": nrm(ks[22], (DEPTH, N_EXPERTS), 0.01),
        "w_gate": nrm(ks[23], (DEPTH, N_EXPERTS, D_MODEL, D_EXPERT), D_MODEL ** -0.5),
        "w_up": nrm(ks[24], (DEPTH, N_EXPERTS, D_MODEL, D_EXPERT), D_MODEL ** -0.5),
        "w_down": nrm(ks[25], (DEPTH, N_EXPERTS, D_EXPERT, D_MODEL), D_EXPERT ** -0.5),
        "norm_final": gain(ks[26], (D_MODEL,)),
    }


def reference(x_prompt, x_sample, cache_k, cache_v, state_gla, norm_mix, w_in, w_decay, b_decay,
              gla_norm, w_proj_a, lambda_q1, lambda_k1, lambda_q2, lambda_k2, diff_norm, w_proj_b,
              w_out, norm_ffn, w_router_group, b_router_group, w_router_expert, b_router_expert,
              w_gate, w_up, w_down, norm_final):
    yp, ys = x_prompt, x_sample
    kp_l, vp_l, sp_l, ks_l, vs_l, ss_l = [], [], [], [], [], []
    for l in range(DEPTH):
        p = dict(norm_mix=norm_mix[l], w_in=w_in[l], w_decay=w_decay[l], b_decay=b_decay[l],
                 gla_norm=gla_norm[l], w_proj_a=w_proj_a[l], lambda_q1=lambda_q1[l],
                 lambda_k1=lambda_k1[l], lambda_q2=lambda_q2[l], lambda_k2=lambda_k2[l],
                 diff_norm=diff_norm[l], w_proj_b=w_proj_b[l], w_out=w_out[l], norm_ffn=norm_ffn[l],
                 w_router_group=w_router_group[l], b_router_group=b_router_group[l],
                 w_router_expert=w_router_expert[l], b_router_expert=b_router_expert[l],
                 w_gate=w_gate[l], w_up=w_up[l], w_down=w_down[l])
        lam_init = 0.8 - 0.6 * math.exp(-0.3 * l)
        s0 = jnp.zeros((yp.shape[0], GLA_HEADS, GLA_DK, GLA_DV), jnp.float32)
        yp, sp, kp, vp = trunk_layer(yp, p, s0, None, None, lam_init)
        ys, ss, kss, vss = trunk_layer(ys, p, state_gla[l], cache_k[l], cache_v[l], lam_init)
        kp_l.append(kp)
        vp_l.append(vp)
        sp_l.append(sp.astype(x_prompt.dtype))
        ks_l.append(kss)
        vs_l.append(vss)
        ss_l.append(ss.astype(state_gla.dtype))
    y_prompt = rms_norm(yp, norm_final)
    y_sample = rms_norm(ys, norm_final)
    return (y_prompt, y_sample, jnp.stack(kp_l), jnp.stack(vp_l), jnp.stack(sp_l),
            jnp.stack(ks_l), jnp.stack(vs_l), jnp.stack(ss_l))
```

```python
import functools

import jax
import jax.numpy as jnp
from jax import lax
from jax.experimental import pallas as pl
from jax.experimental.pallas import tpu as pltpu

F32 = jnp.float32
BF16 = jnp.bfloat16
I32 = jnp.int32

D_MODEL = 1024
CHUNK = 64
CHUNK_SHIFT = 6
EPS = 1e-6
GLA_HEADS = 4
GLA_DK = 128
GLA_DV = 256
GLA_RANK = 16
GLA_TAU = 16.0
DIFF_HEADS = 8
DIFF_DH = 64
DIFF_DV = 128
N_GROUPS = 4
EXPERTS_PER_GROUP = 4
EPG_SHIFT = 2
N_EXPERTS = 16
D_EXPERT = 512
LAM_INIT = 0.8 - 0.6

LANES = 128
TOKEN_TILE = 512
SLOT_BLOCK = 256
ATT_TILE = 256
VMEM_LIMIT = 56 * 1024 * 1024
NEG = -1e30

HI = lax.Precision.HIGHEST


def _cparams(sem):
    return pltpu.CompilerParams(dimension_semantics=sem, vmem_limit_bytes=VMEM_LIMIT)


def _nt(a, b):
    return lax.dot_general(a, b, (((1,), (1,)), ((), ())), preferred_element_type=F32)


def _tn(a, b):
    return lax.dot_general(a, b, (((0,), (0,)), ((), ())), preferred_element_type=F32)


def _rms(x, g):
    return x * lax.rsqrt(jnp.mean(x * x, axis=-1, keepdims=True) + EPS) * g


def _inproj_kernel(x_ref, g_ref, w_ref, qka_ref, va_ref, ra_ref, qb_ref, kb_ref, vb_ref,
                   ga_ref, gb_ref, da_ref):
    h = _rms(x_ref[...], g_ref[...]).astype(BF16)
    outs = (qka_ref, va_ref, ra_ref, qb_ref, kb_ref, vb_ref, ga_ref, gb_ref)
    for c, o in enumerate(outs):
        o[...] = jnp.dot(h, w_ref[:, c * D_MODEL:(c + 1) * D_MODEL],
                         preferred_element_type=F32).astype(o.dtype)
    da_ref[...] = jnp.dot(h, w_ref[:, 8 * D_MODEL:8 * D_MODEL + LANES],
                          preferred_element_type=F32)


def _prep_w_in(w_in):
    o = 0
    parts = {}
    for name, n in (("qa", 512), ("ka", 512), ("va", 1024), ("ra", 1024), ("da", GLA_RANK),
                    ("qb", 1024), ("kb", 1024), ("vb", 1024), ("ga", 1024), ("gb", 1024)):
        parts[name] = w_in[:, o:o + n]
        o += n
    pad = jnp.zeros((D_MODEL, LANES - GLA_RANK), w_in.dtype)
    w = jnp.concatenate([parts[k] for k in ("qa", "ka", "va", "ra", "qb", "kb", "vb", "ga", "gb",
                                            "da")] + [pad], axis=1)
    return w.astype(BF16)


def _in_proj(x, g, w):
    T = x.shape[0]
    tm = min(TOKEN_TILE, T)
    wcols = w.shape[1]
    row = lambda i: (i, 0)
    const = lambda i: (0, 0)
    wide = lambda dt: jax.ShapeDtypeStruct((T, D_MODEL), dt)
    out_shape = (wide(BF16), wide(BF16), wide(BF16), wide(BF16), wide(F32), wide(F32),
                 wide(BF16), wide(BF16), jax.ShapeDtypeStruct((T, LANES), F32))
    out_specs = tuple([pl.BlockSpec((tm, D_MODEL), row)] * 8 + [pl.BlockSpec((tm, LANES), row)])
    return pl.pallas_call(
        _inproj_kernel,
        grid=(T // tm,),
        in_specs=[pl.BlockSpec((tm, D_MODEL), row),
                  pl.BlockSpec((1, D_MODEL), const),
                  pl.BlockSpec((D_MODEL, wcols), const, pipeline_mode=pl.Buffered(1))],
        out_specs=out_specs,
        out_shape=out_shape,
        compiler_params=_cparams(("arbitrary",)),
        name="in_proj",
    )(x, g, w)


def _gla_kernel(qka_ref, va_ref, ra_ref, da_ref, wd_ref, bd_ref, gn_ref, s0_ref,
                oa_ref, sout_ref, s_ref, la_ref, *, chunk, n_chunks):
    l = pl.program_id(1)

    @pl.when(l == 0)
    def _():
        s_ref[...] = s0_ref[0]

    x = jnp.dot(da_ref[...], wd_ref[...], precision=HI, preferred_element_type=F32) + bd_ref[...]
    la_ref[...] = (jnp.minimum(x, 0.0) - jnp.log1p(jnp.exp(-jnp.abs(x)))) * (1.0 / GLA_TAU)

    r_i = lax.broadcasted_iota(I32, (chunk, chunk), 0)
    c_i = lax.broadcasted_iota(I32, (chunk, chunk), 1)
    causal = c_i <= r_i
    tri = jnp.where(causal, 1.0, 0.0)
    gn = gn_ref[...]
    scale = GLA_DK ** -0.5

    @pl.loop(0, n_chunks)
    def _(c):
        rows = pl.ds(pl.multiple_of(c * chunk, chunk), chunk)
        b_all = jnp.dot(tri, la_ref[rows, :], precision=HI, preferred_element_type=F32)
        for h in range(GLA_HEADS):
            kcols = slice(h * GLA_DK, (h + 1) * GLA_DK)
            vcols = slice(h * GLA_DV, (h + 1) * GLA_DV)
            b = b_all[:, kcols]
            b_last = b[chunk - 1:chunk, :]
            q = qka_ref[rows, kcols].astype(F32) * scale
            k = qka_ref[rows, GLA_HEADS * GLA_DK + h * GLA_DK:
                        GLA_HEADS * GLA_DK + (h + 1) * GLA_DK].astype(F32)
            v = va_ref[rows, vcols]
            q_t = (q * jnp.exp(b)).astype(BF16)
            k_t = (k * jnp.exp(-b)).astype(BF16)
            k_end = (k * jnp.exp(b_last - b)).astype(BF16)
            s_old = s_ref[h]
            o = jnp.dot(q_t, s_old.astype(BF16), preferred_element_type=F32)
            att = jnp.where(causal, _nt(q_t, k_t), 0.0).astype(BF16)
            o = o + jnp.dot(att, v, preferred_element_type=F32)
            decay = jnp.broadcast_to(jnp.exp(b_last), (GLA_DK, GLA_DK)).T
            kv = _tn(k_end, v)
            s_ref[h] = jnp.concatenate([decay, decay], axis=1) * s_old + kv
            r = ra_ref[rows, vcols].astype(F32)
            oa_ref[rows, vcols] = (_rms(o, gn) * (r * jax.nn.sigmoid(r))).astype(oa_ref.dtype)

    @pl.when(l == pl.num_programs(1) - 1)
    def _():
        sout_ref[0] = s_ref[...]


def _gla(qka, va, ra, da, wd, bd, gn, s0, B, L):
    chunk = min(CHUNK, L)
    lb = min(TOKEN_TILE, L)
    nl = L // lb
    row = lambda b, l: (b * nl + l, 0)
    const2 = lambda b, l: (0, 0)
    st = lambda b, l: (b, 0, 0, 0)
    kern = functools.partial(_gla_kernel, chunk=chunk, n_chunks=lb // chunk)
    return pl.pallas_call(
        kern,
        grid=(B, nl),
        in_specs=[pl.BlockSpec((lb, D_MODEL), row), pl.BlockSpec((lb, D_MODEL), row),
                  pl.BlockSpec((lb, D_MODEL), row), pl.BlockSpec((lb, LANES), row),
                  pl.BlockSpec((LANES, GLA_HEADS * GLA_DK), const2),
                  pl.BlockSpec((1, GLA_HEADS * GLA_DK), const2),
                  pl.BlockSpec((1, GLA_DV), const2),
                  pl.BlockSpec((1, GLA_HEADS, GLA_DK, GLA_DV), st)],
        out_specs=(pl.BlockSpec((lb, D_MODEL), row),
                   pl.BlockSpec((1, GLA_HEADS, GLA_DK, GLA_DV), st)),
        out_shape=(jax.ShapeDtypeStruct((B * L, D_MODEL), BF16),
                   jax.ShapeDtypeStruct((B, GLA_HEADS, GLA_DK, GLA_DV), F32)),
        scratch_shapes=[pltpu.VMEM((GLA_HEADS, GLA_DK, GLA_DV), F32),
                        pltpu.VMEM((lb, GLA_HEADS * GLA_DK), F32)],
        compiler_params=_cparams(("arbitrary", "arbitrary")),
        name="gla",
    )(qka, va, ra, da, wd, bd, gn, s0)


def _lam(lq1, lk1, lq2, lk2):
    a = jnp.sum(lq1[...] * lk1[...], axis=-1, keepdims=True)
    b = jnp.sum(lq2[...] * lk2[...], axis=-1, keepdims=True)
    return jnp.exp(a) - jnp.exp(b) + LAM_INIT


def _split_q(q):
    lane = lax.broadcasted_iota(I32, q.shape, 1)
    qs = q * jnp.asarray(DIFF_DH ** -0.5, q.dtype)
    zero = jnp.zeros_like(qs)
    return jnp.where(lane < DIFF_DH, qs, zero), jnp.where(lane >= DIFF_DH, qs, zero)


def _alibi_slopes():
    return jnp.asarray([2.0 ** (-8.0 * (h + 1) / DIFF_HEADS) for h in range(DIFF_HEADS)], F32)


def _attn_prompt_kernel(slope_ref, q_ref, k_ref, v_ref, lq1, lk1, lq2, lk2, dn_ref, o_ref,
                        kb_ref, vb_ref, *, tile):
    h = pl.program_id(1)
    qi = pl.program_id(2)

    @pl.when(qi == 0)
    def _():
        kb_ref[...] = k_ref[...].astype(BF16)
        vb_ref[...] = v_ref[...].astype(BF16)

    m_h = slope_ref[h]
    q1, q2 = _split_q(q_ref[...])
    r_i = lax.broadcasted_iota(I32, (tile, tile), 0)
    c_i = lax.broadcasted_iota(I32, (tile, tile), 1)
    rel = r_i - c_i

    def update(carry, s1, s2, v):
        m1, l1, a1, m2, l2, a2 = carry
        out = []
        for s, m, l, a in ((s1, m1, l1, a1), (s2, m2, l2, a2)):
            m_new = jnp.maximum(m, jnp.max(s, axis=-1, keepdims=True))
            alpha = jnp.exp(m - m_new)
            p = jnp.exp(s - m_new)
            l = alpha * l + jnp.sum(p, axis=-1, keepdims=True)
            a = alpha * a + jnp.dot(p.astype(BF16), v, preferred_element_type=F32)
            out += [m_new, l, a]
        return tuple(out)

    def body(j, carry):
        off = pl.multiple_of(j * tile, tile)
        k = kb_ref[pl.ds(off, tile), :]
        v = vb_ref[pl.ds(off, tile), :]
        bias = -m_h * (rel + (qi - j) * tile).astype(F32)
        return update(carry, _nt(q1, k) + bias, _nt(q2, k) + bias, v)

    col = lambda: jnp.full((tile, 1), NEG, F32)
    zc = lambda: jnp.zeros((tile, 1), F32)
    za = lambda: jnp.zeros((tile, DIFF_DV), F32)
    carry = lax.fori_loop(0, qi, body, (col(), zc(), za(), col(), zc(), za()))

    off = pl.multiple_of(qi * tile, tile)
    k = kb_ref[pl.ds(off, tile), :]
    v = vb_ref[pl.ds(off, tile), :]
    allowed = (c_i >> CHUNK_SHIFT) <= (r_i >> CHUNK_SHIFT)
    bias = -m_h * jnp.abs(rel).astype(F32)
    s1 = jnp.where(allowed, _nt(q1, k) + bias, NEG)
    s2 = jnp.where(allowed, _nt(q2, k) + bias, NEG)
    _, l1, a1, _, l2, a2 = update(carry, s1, s2, v)

    o = a1 / l1 - _lam(lq1, lk1, lq2, lk2) * (a2 / l2)
    o_ref[...] = (_rms(o, dn_ref[...]) * (1.0 - LAM_INIT)).astype(o_ref.dtype)


def _attn_prompt(qb, kb, vb, lq1, lk1, lq2, lk2, dn, B, L):
    tile = min(ATT_TILE, L)
    nq = L // tile
    qmap = lambda b, h, i: (b * nq + i, h)
    kvmap = lambda b, h, i: (b, h)
    cmap = lambda b, h, i: (0, 0)
    lspec = pl.BlockSpec((1, DIFF_DH), cmap)
    return pl.pallas_call(
        functools.partial(_attn_prompt_kernel, tile=tile),
        grid=(B, DIFF_HEADS, nq),
        in_specs=[pl.BlockSpec(memory_space=pltpu.SMEM),
                  pl.BlockSpec((tile, DIFF_DV), qmap),
                  pl.BlockSpec((L, DIFF_DV), kvmap), pl.BlockSpec((L, DIFF_DV), kvmap),
                  lspec, lspec, lspec, lspec, pl.BlockSpec((1, DIFF_DV), cmap)],
        out_specs=pl.BlockSpec((tile, DIFF_DV), qmap),
        out_shape=jax.ShapeDtypeStruct((B * L, D_MODEL), BF16),
        scratch_shapes=[pltpu.VMEM((L, DIFF_DV), BF16), pltpu.VMEM((L, DIFF_DV), BF16)],
        compiler_params=_cparams(("arbitrary", "arbitrary", "arbitrary")),
        name="attn_prompt",
    )(_alibi_slopes(), qb, kb, vb, lq1, lk1, lq2, lk2, dn)


def _attn_sample_kernel(slope_ref, q_ref, kp_ref, vp_ref, kn_ref, vn_ref, lq1, lk1, lq2, lk2,
                        dn_ref, o_ref, *, past, lq):
    m_h = slope_ref[pl.program_id(1)]
    q1, q2 = _split_q(q_ref[...])
    kp = kp_ref[...].astype(BF16)
    vp = vp_ref[...].astype(BF16)
    kn = kn_ref[...].astype(BF16)
    vn = vn_ref[...].astype(BF16)

    def bias_mask(n_keys, key0):
        qpos = past + lax.broadcasted_iota(I32, (lq, n_keys), 0)
        kpos = key0 + lax.broadcasted_iota(I32, (lq, n_keys), 1)
        allowed = (kpos >> CHUNK_SHIFT) <= (qpos >> CHUNK_SHIFT)
        return -m_h * jnp.abs(qpos - kpos).astype(F32), allowed

    bp, ap = bias_mask(past, 0)
    bn, an = bias_mask(lq, past)
    outs = []
    for qm in (q1, q2):
        sp = jnp.where(ap, _nt(qm, kp) + bp, NEG)
        sn = jnp.where(an, _nt(qm, kn) + bn, NEG)
        m = jnp.maximum(jnp.max(sp, axis=-1, keepdims=True), jnp.max(sn, axis=-1, keepdims=True))
        pp = jnp.exp(sp - m)
        pn = jnp.exp(sn - m)
        l = jnp.sum(pp, axis=-1, keepdims=True) + jnp.sum(pn, axis=-1, keepdims=True)
        acc = (jnp.dot(pp.astype(BF16), vp, preferred_element_type=F32)
               + jnp.dot(pn.astype(BF16), vn, preferred_element_type=F32))
        outs.append(acc / l)
    o = outs[0] - _lam(lq1, lk1, lq2, lk2) * outs[1]
    o_ref[...] = (_rms(o, dn_ref[...]) * (1.0 - LAM_INIT)).astype(o_ref.dtype)


def _attn_sample(qb, k_past, v_past, kb, vb, lq1, lk1, lq2, lk2, dn, B, L, past):
    qmap = lambda b, h: (b, h)
    cmap = lambda b, h: (0, 0)
    lspec = pl.BlockSpec((1, DIFF_DH), cmap)
    return pl.pallas_call(
        functools.partial(_attn_sample_kernel, past=past, lq=L),
        grid=(B, DIFF_HEADS),
        in_specs=[pl.BlockSpec(memory_space=pltpu.SMEM),
                  pl.BlockSpec((L, DIFF_DV), qmap),
                  pl.BlockSpec((past, DIFF_DV), qmap), pl.BlockSpec((past, DIFF_DV), qmap),
                  pl.BlockSpec((L, DIFF_DV), qmap), pl.BlockSpec((L, DIFF_DV), qmap),
                  lspec, lspec, lspec, lspec, pl.BlockSpec((1, DIFF_DV), cmap)],
        out_specs=pl.BlockSpec((L, DIFF_DV), qmap),
        out_shape=jax.ShapeDtypeStruct((B * L, D_MODEL), BF16),
        compiler_params=_cparams(("arbitrary", "arbitrary")),
        name="attn_sample",
    )(_alibi_slopes(), qb, k_past, v_past, kb, vb, lq1, lk1, lq2, lk2, dn)


ROUTE_E1, ROUTE_E2, ROUTE_W1, ROUTE_W2, ROUTE_R1, ROUTE_R2 = range(6)
GROUP_LANE0 = N_EXPERTS


def _merge_kernel(x_ref, oa_ref, ob_ref, ga_ref, gb_ref, wa_ref, wb_ref, wo_ref, nf_ref,
                  wr_ref, br_ref, x1_ref, h2_ref, route_ref, cnt_ref, run_ref):
    i = pl.program_id(0)

    @pl.when(i == 0)
    def _():
        run_ref[...] = jnp.zeros_like(run_ref)

    u_a = jnp.dot(oa_ref[...], wa_ref[...], preferred_element_type=F32)
    u_b = jnp.dot(ob_ref[...], wb_ref[...], preferred_element_type=F32)
    mix = (jax.nn.sigmoid(ga_ref[...].astype(F32)) * u_a
           + jax.nn.sigmoid(gb_ref[...].astype(F32)) * u_b)
    x1 = x_ref[...] + jnp.dot(mix.astype(BF16), wo_ref[...], preferred_element_type=F32)
    x1_ref[...] = x1
    h2 = _rms(x1, nf_ref[...])
    h2_ref[...] = h2

    tm = h2.shape[0]
    logits = jnp.dot(h2, wr_ref[...], precision=HI, preferred_element_type=F32) + br_ref[...]
    lane = lax.broadcasted_iota(I32, (tm, LANES), 1)
    big = jnp.int32(LANES)
    g_mask = (lane >= GROUP_LANE0) & (lane < GROUP_LANE0 + N_GROUPS)
    gl = jnp.where(g_mask, logits, -jnp.inf)
    gmax = jnp.max(gl, axis=-1, keepdims=True)
    g_sel = jnp.min(jnp.where(gl == gmax, lane - GROUP_LANE0, big), axis=-1, keepdims=True)
    p_g = 1.0 / jnp.sum(jnp.where(g_mask, jnp.exp(logits - gmax), 0.0), axis=-1, keepdims=True)
    e_mask = (lane < N_EXPERTS) & ((lane >> EPG_SHIFT) == g_sel)
    el = jnp.where(e_mask, logits, -jnp.inf)
    v1 = jnp.max(el, axis=-1, keepdims=True)
    i1 = jnp.min(jnp.where(el == v1, lane, big), axis=-1, keepdims=True)
    el2 = jnp.where(lane == i1, -jnp.inf, el)
    v2 = jnp.max(el2, axis=-1, keepdims=True)
    i2 = jnp.min(jnp.where(el2 == v2, lane, big), axis=-1, keepdims=True)
    t = jnp.exp(v2 - v1)
    w1 = p_g / (1.0 + t)
    w2 = p_g * t / (1.0 + t)

    oh1 = lane == i1
    oh2 = lane == i2
    cnt = jnp.where(oh1, 1.0, 0.0) + jnp.where(oh2, 1.0, 0.0)
    r_i = lax.broadcasted_iota(I32, (tm, tm), 0)
    c_i = lax.broadcasted_iota(I32, (tm, tm), 1)
    before = jnp.where(c_i < r_i, 1.0, 0.0).astype(BF16)
    prior = jnp.dot(before, cnt.astype(BF16), preferred_element_type=F32) + run_ref[...]
    rank1 = jnp.sum(jnp.where(oh1, prior, 0.0), axis=-1, keepdims=True)
    rank2 = jnp.sum(jnp.where(oh2, prior, 0.0), axis=-1, keepdims=True)
    run = run_ref[...] + jnp.sum(cnt, axis=0, keepdims=True)
    run_ref[...] = run
    cnt_ref[...] = run

    route = jnp.zeros((tm, LANES), F32)
    for pos, val in ((ROUTE_E1, i1.astype(F32)), (ROUTE_E2, i2.astype(F32)), (ROUTE_W1, w1),
                     (ROUTE_W2, w2), (ROUTE_R1, rank1), (ROUTE_R2, rank2)):
        route = jnp.where(lane == pos, val, route)
    route_ref[...] = route


def _merge(x, oa, ob, ga, gb, wa, wb, wo, nf, wr, br):
    T = x.shape[0]
    tm = min(TOKEN_TILE, T)
    row = lambda i: (i, 0)
    const = lambda i: (0, 0)
    wspec = pl.BlockSpec((D_MODEL, D_MODEL), const)
    tile = pl.BlockSpec((tm, D_MODEL), row)
    return pl.pallas_call(
        _merge_kernel,
        grid=(T // tm,),
        in_specs=[tile, tile, tile, tile, tile, wspec, wspec, wspec,
                  pl.BlockSpec((1, D_MODEL), const),
                  pl.BlockSpec((D_MODEL, LANES), const), pl.BlockSpec((1, LANES), const)],
        out_specs=(tile, tile, pl.BlockSpec((tm, LANES), row), pl.BlockSpec((1, LANES), const)),
        out_shape=(jax.ShapeDtypeStruct((T, D_MODEL), F32),
                   jax.ShapeDtypeStruct((T, D_MODEL), F32),
                   jax.ShapeDtypeStruct((T, LANES), F32),
                   jax.ShapeDtypeStruct((1, LANES), F32)),
        scratch_shapes=[pltpu.VMEM((1, LANES), F32)],
        compiler_params=_cparams(("arbitrary",)),
        name="merge",
    )(x, oa, ob, ga, gb, wa, wb, wo, nf, wr, br)


def _dispatch_kernel(dest_ref, h2_ref, xs_in_ref, xs_ref, sem):
    del xs_in_ref
    tm = h2_ref.shape[0]

    def row_copy(r, k):
        return pltpu.make_async_copy(h2_ref.at[pl.ds(r, 1)], xs_ref.at[pl.ds(dest_ref[k, r], 1)], sem)

    def start(r, c):
        row_copy(r, 0).start()
        row_copy(r, 1).start()
        return c

    def wait(r, c):
        row_copy(r, 0).wait()
        row_copy(r, 1).wait()
        return c

    lax.fori_loop(0, tm, start, 0)
    lax.fori_loop(0, tm, wait, 0)


def _dispatch(dest, h2, n_slots):
    T = h2.shape[0]
    tm = min(TOKEN_TILE, T)
    xs0 = jnp.zeros((n_slots, D_MODEL), F32)
    return pl.pallas_call(
        _dispatch_kernel,
        grid=(T // tm,),
        in_specs=[pl.BlockSpec((2, tm), lambda i: (0, i), memory_space=pltpu.SMEM),
                  pl.BlockSpec((tm, D_MODEL), lambda i: (i, 0)),
                  pl.BlockSpec(memory_space=pl.ANY)],
        out_specs=pl.BlockSpec(memory_space=pl.ANY),
        out_shape=jax.ShapeDtypeStruct((n_slots, D_MODEL), F32),
        scratch_shapes=[pltpu.SemaphoreType.DMA(())],
        input_output_aliases={2: 0},
        compiler_params=_cparams(("arbitrary",)),
        name="dispatch",
    )(dest, h2, xs0)


def _experts_kernel(be_ref, nu_ref, xs_ref, wg_ref, wu_ref, wd_ref, y_ref):
    del be_ref
    used = pl.program_id(0) < nu_ref[0]

    @pl.when(used)
    def _():
        x = xs_ref[...].astype(BF16)
        g = jnp.dot(x, wg_ref[0], preferred_element_type=F32)
        u = jnp.dot(x, wu_ref[0], preferred_element_type=F32)
        a = (g * jax.nn.sigmoid(g) * u).astype(BF16)
        y_ref[...] = jnp.dot(a, wd_ref[0], preferred_element_type=F32)

    @pl.when(jnp.logical_not(used))
    def _():
        y_ref[...] = jnp.zeros_like(y_ref)


def _experts(block_e, n_used, xs, wg, wu, wd):
    n_blocks = xs.shape[0] // SLOT_BLOCK
    blk = lambda i, be, nu: (jnp.minimum(i, nu[0] - 1), 0)
    wmap = lambda i, be, nu: (be[jnp.minimum(i, nu[0] - 1)], 0, 0)
    grid_spec = pltpu.PrefetchScalarGridSpec(
        num_scalar_prefetch=2,
        grid=(n_blocks,),
        in_specs=[pl.BlockSpec((SLOT_BLOCK, D_MODEL), blk),
                  pl.BlockSpec((1, D_MODEL, D_EXPERT), wmap),
                  pl.BlockSpec((1, D_MODEL, D_EXPERT), wmap),
                  pl.BlockSpec((1, D_EXPERT, D_MODEL), wmap)],
        out_specs=pl.BlockSpec((SLOT_BLOCK, D_MODEL), lambda i, be, nu: (i, 0)),
    )
    return pl.pallas_call(
        _experts_kernel,
        grid_spec=grid_spec,
        out_shape=jax.ShapeDtypeStruct((xs.shape[0], D_MODEL), F32),
        compiler_params=_cparams(("arbitrary",)),
        name="experts",
    )(block_e, n_used, xs, wg, wu, wd)


def _combine_kernel(dest_ref, x1_ref, route_ref, nf_ref, yb_ref, y_ref, g_ref, sem):
    tm = x1_ref.shape[0]

    def row_copy(r, k):
        return pltpu.make_async_copy(yb_ref.at[pl.ds(dest_ref[k, r], 1)], g_ref.at[k, pl.ds(r, 1)], sem)

    def start(r, c):
        row_copy(r, 0).start()
        row_copy(r, 1).start()
        return c

    def wait(r, c):
        row_copy(r, 0).wait()
        row_copy(r, 1).wait()
        return c

    lax.fori_loop(0, tm, start, 0)
    lax.fori_loop(0, tm, wait, 0)
    route = route_ref[...]
    w1 = route[:, ROUTE_W1:ROUTE_W1 + 1]
    w2 = route[:, ROUTE_W2:ROUTE_W2 + 1]
    x2 = x1_ref[...] + (g_ref[0] * w1 + g_ref[1] * w2)
    y_ref[...] = _rms(x2, nf_ref[...])


def _combine(dest, x1, route, nfinal, yb):
    T = x1.shape[0]
    tm = min(TOKEN_TILE // 2, T)
    row = lambda i: (i, 0)
    return pl.pallas_call(
        _combine_kernel,
        grid=(T // tm,),
        in_specs=[pl.BlockSpec((2, tm), lambda i: (0, i), memory_space=pltpu.SMEM),
                  pl.BlockSpec((tm, D_MODEL), row), pl.BlockSpec((tm, LANES), row),
                  pl.BlockSpec((1, D_MODEL), lambda i: (0, 0)),
                  pl.BlockSpec(memory_space=pl.ANY)],
        out_specs=pl.BlockSpec((tm, D_MODEL), row),
        out_shape=jax.ShapeDtypeStruct((T, D_MODEL), F32),
        scratch_shapes=[pltpu.VMEM((2, tm, D_MODEL), F32), pltpu.SemaphoreType.DMA(())],
        compiler_params=_cparams(("arbitrary",)),
        name="combine",
    )(dest, x1, route, nfinal, yb)


def _moe(x1, h2, route, counts, wg, wu, wd, nfinal):
    T = x1.shape[0]
    n_slots = T * 2 + N_EXPERTS * SLOT_BLOCK
    n_blocks = n_slots // SLOT_BLOCK
    cnt = counts[0, :N_EXPERTS].astype(I32)
    padded = (cnt + SLOT_BLOCK - 1) // SLOT_BLOCK * SLOT_BLOCK
    pad_ends = jnp.cumsum(padded)
    pad_starts = pad_ends - padded
    e = route[:, ROUTE_E1:ROUTE_E2 + 1].astype(I32)
    rank = route[:, ROUTE_R1:ROUTE_R2 + 1].astype(I32)
    dest = (pad_starts[e] + rank).T
    block_start = jnp.arange(n_blocks, dtype=I32) * SLOT_BLOCK
    block_e = jnp.sum((block_start[:, None] >= pad_ends[None, :]).astype(I32), axis=1)
    block_e = jnp.minimum(block_e, N_EXPERTS - 1)
    n_used = (pad_ends[-1:] // SLOT_BLOCK).astype(I32)
    xs = _dispatch(dest, h2, n_slots)
    yb = _experts(block_e, n_used, xs, wg, wu, wd)
    return _combine(dest, x1, route, nfinal, yb)


def _layer(x, s0, k_past, v_past, p, B, L):
    row2 = lambda a: a.reshape(1, -1)
    qka, va, ra, qb, kb, vb, ga, gb, da = _in_proj(x, row2(p["norm_mix"]), p["w_in"])
    oa, s_new = _gla(qka, va, ra, da, p["w_decay"], row2(p["b_decay"]), row2(p["gla_norm"]),
                     s0, B, L)
    lams = [row2(p[n]) for n in ("lambda_q1", "lambda_k1", "lambda_q2", "lambda_k2")]
    dn = row2(p["diff_norm"])
    if k_past is None:
        ob = _attn_prompt(qb, kb, vb, *lams, dn, B, L)
    else:
        past = k_past.shape[0] // B
        ob = _attn_sample(qb, k_past, v_past, kb, vb, *lams, dn, B, L, past)
    x1, h2, route, counts = _merge(x, oa, ob, ga, gb, p["w_proj_a"], p["w_proj_b"], p["w_out"],
                                   row2(p["norm_ffn"]), p["w_router"], p["b_router"])
    y = _moe(x1, h2, route, counts, p["w_gate"], p["w_up"], p["w_down"], row2(p["norm_final"]))
    return y, s_new, kb, vb


def kernel(x_prompt, x_sample, cache_k, cache_v, state_gla, norm_mix, w_in, w_decay, b_decay,
           gla_norm, w_proj_a, lambda_q1, lambda_k1, lambda_q2, lambda_k2, diff_norm, w_proj_b,
           w_out, norm_ffn, w_router_group, b_router_group, w_router_expert, b_router_expert,
           w_gate, w_up, w_down, norm_final):
    B, L, D = x_prompt.shape
    Bs, Ls, _ = x_sample.shape
    past = cache_k.shape[2]
    w_router = jnp.concatenate(
        [w_router_expert[0], w_router_group[0],
         jnp.zeros((D, LANES - N_EXPERTS - N_GROUPS), F32)], axis=1)
    b_router = jnp.concatenate(
        [b_router_expert[0], b_router_group[0],
         jnp.zeros((LANES - N_EXPERTS - N_GROUPS,), F32)]).reshape(1, LANES)
    p = dict(
        norm_mix=norm_mix[0], w_in=_prep_w_in(w_in[0]),
        w_decay=jnp.pad(w_decay[0], ((0, LANES - GLA_RANK), (0, 0))), b_decay=b_decay[0],
        gla_norm=gla_norm[0], w_proj_a=w_proj_a[0].astype(BF16),
        lambda_q1=lambda_q1[0], lambda_k1=lambda_k1[0], lambda_q2=lambda_q2[0],
        lambda_k2=lambda_k2[0], diff_norm=diff_norm[0], w_proj_b=w_proj_b[0].astype(BF16),
        w_out=w_out[0].astype(BF16), norm_ffn=norm_ffn[0], w_router=w_router, b_router=b_router,
        w_gate=w_gate[0].astype(BF16), w_up=w_up[0].astype(BF16), w_down=w_down[0].astype(BF16),
        norm_final=norm_final)

    s0p = jnp.zeros((B, GLA_HEADS, GLA_DK, GLA_DV), F32)
    yp, sp, kp, vp = _layer(x_prompt.reshape(B * L, D), s0p, None, None, p, B, L)
    ys, ss, ks, vs = _layer(x_sample.reshape(Bs * Ls, D), state_gla[0],
                            cache_k[0].reshape(Bs * past, D), cache_v[0].reshape(Bs * past, D),
                            p, Bs, Ls)
    kv = lambda a, b, l: a.reshape(1, b, l, DIFF_HEADS, DIFF_DV)
    return (yp.reshape(B, L, D), ys.reshape(Bs, Ls, D), kv(kp, B, L), kv(vp, B, L), sp[None],
            kv(ks, Bs, Ls), kv(vs, Bs, Ls), ss[None])
```

```python
import functools

import jax
import jax.numpy as jnp
from jax import lax
from jax.experimental import pallas as pl
from jax.experimental.pallas import tpu as pltpu

F32 = jnp.float32
BF16 = jnp.bfloat16
I32 = jnp.int32

D_MODEL = 1024
CHUNK = 64
CHUNK_SHIFT = 6
EPS = 1e-6
GLA_HEADS = 4
GLA_DK = 128
GLA_DV = 256
GLA_RANK = 16
GLA_TAU = 16.0
DIFF_HEADS = 8
DIFF_DH = 64
DIFF_DV = 128
N_GROUPS = 4
EXPERTS_PER_GROUP = 4
EPG_SHIFT = 2
N_EXPERTS = 16
D_EXPERT = 512
LAM_INIT = 0.8 - 0.6

LANES = 128
TOKEN_TILE = 512
SLOT_BLOCK = 256
ATT_TILE = 256
ATT_HEADS_PER_STEP = 2
ROW_UNROLL = 8
SAMPLE_KEY_CHUNK = 1024
VMEM_LIMIT = 56 * 1024 * 1024
NEG = -1e30

HI = lax.Precision.HIGHEST


def _cparams(sem):
    return pltpu.CompilerParams(dimension_semantics=sem, vmem_limit_bytes=VMEM_LIMIT)


def _nt(a, b):
    return lax.dot_general(a, b, (((1,), (1,)), ((), ())), preferred_element_type=F32)


def _tn(a, b):
    return lax.dot_general(a, b, (((0,), (0,)), ((), ())), preferred_element_type=F32)


def _rms(x, g):
    return x * lax.rsqrt(jnp.mean(x * x, axis=-1, keepdims=True) + EPS) * g


def _inproj_kernel(x_ref, g_ref, w_ref, qka_ref, va_ref, ra_ref, qb_ref, kb_ref, vb_ref,
                   ga_ref, gb_ref, da_ref, k5_ref, v5_ref):
    h = _rms(x_ref[...], g_ref[...]).astype(BF16)
    outs = (qka_ref, va_ref, ra_ref, qb_ref, kb_ref, vb_ref, ga_ref, gb_ref)
    for c, o in enumerate(outs):
        z = jnp.dot(h, w_ref[:, c * D_MODEL:(c + 1) * D_MODEL], preferred_element_type=F32)
        o[...] = z.astype(o.dtype)
        head_ref = k5_ref if o is kb_ref else v5_ref if o is vb_ref else None
        if head_ref is not None:
            for hd in range(DIFF_HEADS):
                head_ref[:, hd, :] = z[:, hd * DIFF_DV:(hd + 1) * DIFF_DV]
    da_ref[...] = jnp.dot(h, w_ref[:, 8 * D_MODEL:8 * D_MODEL + LANES],
                          preferred_element_type=F32)


def _prep_w_in(w_in):
    o = 0
    parts = {}
    for name, n in (("qa", 512), ("ka", 512), ("va", 1024), ("ra", 1024), ("da", GLA_RANK),
                    ("qb", 1024), ("kb", 1024), ("vb", 1024), ("ga", 1024), ("gb", 1024)):
        parts[name] = w_in[:, o:o + n]
        o += n
    pad = jnp.zeros((D_MODEL, LANES - GLA_RANK), w_in.dtype)
    w = jnp.concatenate([parts[k] for k in ("qa", "ka", "va", "ra", "qb", "kb", "vb", "ga", "gb",
                                            "da")] + [pad], axis=1)
    return w.astype(BF16)


def _in_proj(x, g, w):
    T = x.shape[0]
    tm = min(TOKEN_TILE, T)
    wcols = w.shape[1]
    row = lambda i: (i, 0)
    const = lambda i: (0, 0)
    wide = lambda dt: jax.ShapeDtypeStruct((T, D_MODEL), dt)
    heads = jax.ShapeDtypeStruct((T, DIFF_HEADS, DIFF_DV), F32)
    out_shape = (wide(BF16),) * 8 + (jax.ShapeDtypeStruct((T, LANES), F32), heads, heads)
    head_spec = pl.BlockSpec((tm, DIFF_HEADS, DIFF_DV), lambda i: (i, 0, 0))
    out_specs = tuple([pl.BlockSpec((tm, D_MODEL), row)] * 8 + [pl.BlockSpec((tm, LANES), row)]
                      + [head_spec] * 2)
    return pl.pallas_call(
        _inproj_kernel,
        grid=(T // tm,),
        in_specs=[pl.BlockSpec((tm, D_MODEL), row),
                  pl.BlockSpec((1, D_MODEL), const),
                  pl.BlockSpec((D_MODEL, wcols), const, pipeline_mode=pl.Buffered(1))],
        out_specs=out_specs,
        out_shape=out_shape,
        compiler_params=_cparams(("arbitrary",)),
        name="in_proj",
    )(x, g, w)


def _gla_kernel(qka_ref, va_ref, ra_ref, da_ref, wd_ref, bd_ref, gn_ref, s0_ref,
                oa_ref, sout_ref, s_ref, la_ref, *, chunk, n_chunks):
    l = pl.program_id(1)

    @pl.when(l == 0)
    def _():
        s_ref[...] = s0_ref[0]

    split = lambda a: (a.astype(BF16), (a - a.astype(BF16).astype(F32)).astype(BF16))
    n_k = GLA_HEADS * GLA_DK
    fold = lambda z: z[:, :n_k] + z[:, n_k:]

    wd_pair = jnp.concatenate(split(wd_ref[...]), axis=1)
    x = fold(sum(jnp.dot(a, wd_pair, preferred_element_type=F32) for a in split(da_ref[...])))
    x = x + bd_ref[...]
    log_a = (jnp.minimum(x, 0.0) - jnp.log1p(jnp.exp(-jnp.abs(x)))) * (1.0 / GLA_TAU)
    lb = log_a.shape[0]
    shift = chunk.bit_length() - 1
    r_b = lax.broadcasted_iota(I32, (lb, lb), 0)
    c_b = lax.broadcasted_iota(I32, (lb, lb), 1)
    tri = jnp.where((c_b <= r_b) & ((c_b >> shift) == (r_b >> shift)), 1.0, 0.0).astype(BF16)
    la_ref[...] = fold(jnp.dot(tri, jnp.concatenate(split(log_a), axis=1),
                               preferred_element_type=F32))

    r_i = lax.broadcasted_iota(I32, (chunk, chunk), 0)
    c_i = lax.broadcasted_iota(I32, (chunk, chunk), 1)
    causal = c_i <= r_i
    gn = gn_ref[...]
    scale = GLA_DK ** -0.5
    heads = range(GLA_HEADS)
    vcols = [slice(h * GLA_DV, (h + 1) * GLA_DV) for h in heads]

    @pl.loop(0, n_chunks)
    def _(c):
        rows = pl.ds(pl.multiple_of(c * chunk, chunk), chunk)
        b_all = la_ref[rows, :]
        q_t, k_t, k_end, decay = [], [], [], []
        for h in heads:
            b = b_all[:, h * GLA_DK:(h + 1) * GLA_DK]
            b_last = b[chunk - 1:chunk, :]
            q = qka_ref[rows, h * GLA_DK:(h + 1) * GLA_DK].astype(F32) * scale
            k = qka_ref[rows, n_k + h * GLA_DK:n_k + (h + 1) * GLA_DK].astype(F32)
            q_t.append((q * jnp.exp(b)).astype(BF16))
            k_t.append((k * jnp.exp(-b)).astype(BF16))
            k_end.append((k * jnp.exp(b_last - b)).astype(BF16))
            decay.append(jnp.broadcast_to(jnp.exp(b_last), (GLA_DK, GLA_DK)).T)
        s_old = [s_ref[h] for h in heads]
        o_state = [jnp.dot(q_t[h], s_old[h].astype(BF16), preferred_element_type=F32) for h in heads]
        att = [_nt(q_t[h], k_t[h]) for h in heads]
        kv = [_tn(k_end[h], va_ref[rows, vcols[h]]) for h in heads]
        for h in heads:
            a = jnp.where(causal, att[h], 0.0).astype(BF16)
            o = o_state[h] + jnp.dot(a, va_ref[rows, vcols[h]], preferred_element_type=F32)
            s_ref[h] = jnp.concatenate([decay[h], decay[h]], axis=1) * s_old[h] + kv[h]
            r = ra_ref[rows, vcols[h]].astype(F32)
            oa_ref[rows, vcols[h]] = (_rms(o, gn) * (r * jax.nn.sigmoid(r))).astype(oa_ref.dtype)

    @pl.when(l == pl.num_programs(1) - 1)
    def _():
        sout_ref[0] = s_ref[...]


def _gla(qka, va, ra, da, wd, bd, gn, s0, B, L):
    chunk = min(CHUNK, L)
    lb = min(TOKEN_TILE, L)
    nl = L // lb
    row = lambda b, l: (b * nl + l, 0)
    const2 = lambda b, l: (0, 0)
    st = lambda b, l: (b, 0, 0, 0)
    kern = functools.partial(_gla_kernel, chunk=chunk, n_chunks=lb // chunk)
    return pl.pallas_call(
        kern,
        grid=(B, nl),
        in_specs=[pl.BlockSpec((lb, D_MODEL), row), pl.BlockSpec((lb, D_MODEL), row),
                  pl.BlockSpec((lb, D_MODEL), row), pl.BlockSpec((lb, LANES), row),
                  pl.BlockSpec((LANES, GLA_HEADS * GLA_DK), const2),
                  pl.BlockSpec((1, GLA_HEADS * GLA_DK), const2),
                  pl.BlockSpec((1, GLA_DV), const2),
                  pl.BlockSpec((1, GLA_HEADS, GLA_DK, GLA_DV), st)],
        out_specs=(pl.BlockSpec((lb, D_MODEL), row),
                   pl.BlockSpec((1, GLA_HEADS, GLA_DK, GLA_DV), st)),
        out_shape=(jax.ShapeDtypeStruct((B * L, D_MODEL), BF16),
                   jax.ShapeDtypeStruct((B, GLA_HEADS, GLA_DK, GLA_DV), F32)),
        scratch_shapes=[pltpu.VMEM((GLA_HEADS, GLA_DK, GLA_DV), F32),
                        pltpu.VMEM((lb, GLA_HEADS * GLA_DK), F32)],
        compiler_params=_cparams(("arbitrary", "arbitrary")),
        name="gla",
    )(qka, va, ra, da, wd, bd, gn, s0)


def _lam(lq1, lk1, lq2, lk2):
    a = jnp.sum(lq1[...] * lk1[...], axis=-1, keepdims=True)
    b = jnp.sum(lq2[...] * lk2[...], axis=-1, keepdims=True)
    return jnp.exp(a) - jnp.exp(b) + LAM_INIT


def _split_q(q):
    lane = lax.broadcasted_iota(I32, q.shape, 1)
    qs = q * jnp.asarray(DIFF_DH ** -0.5, q.dtype)
    zero = jnp.zeros_like(qs)
    return jnp.where(lane < DIFF_DH, qs, zero), jnp.where(lane >= DIFF_DH, qs, zero)


def _alibi_slopes():
    return jnp.asarray([2.0 ** (-8.0 * (h + 1) / DIFF_HEADS) for h in range(DIFF_HEADS)], F32)


def _attn_prompt_kernel(slope_ref, q_ref, k_ref, v_ref, lq1, lk1, lq2, lk2, dn_ref, o_ref,
                        k1_ref, k2_ref, vt_ref, *, tile, nh):
    qi = pl.program_id(2)
    n_kv = vt_ref.shape[1]
    n_cols = tile // LANES
    lane_k = lax.broadcasted_iota(I32, (tile, DIFF_DV), 1)
    row_k = lax.broadcasted_iota(I32, (tile, DIFF_DV), 0).astype(F32)
    slopes = [slope_ref[pl.program_id(1) * nh + g] for g in range(nh)]
    hcols = [slice(g * DIFF_DV, (g + 1) * DIFF_DV) for g in range(nh)]

    @pl.when(qi == 0)
    def _():
        for g in range(nh):
            kfeat = jnp.where(lane_k == DIFF_DH, slopes[g] * row_k,
                              jnp.where(lane_k == DIFF_DH + 1, 1.0, 0.0))
            for c in range(n_kv):
                rows = slice(c * tile, (c + 1) * tile)
                kf = k_ref[rows, hcols[g]].astype(F32)
                k1_ref[g, rows, :] = jnp.where(lane_k < DIFF_DH, kf, kfeat).astype(BF16)
                k2_ref[g, rows, :] = jnp.where(lane_k < DIFF_DH, pltpu.roll(kf, DIFF_DH, 1),
                                               kfeat).astype(BF16)
                vt_ref[g, c] = v_ref[rows, hcols[g]].astype(F32).T.astype(BF16)

    qs = []
    for g in range(nh):
        qf = q_ref[:, hcols[g]].astype(F32) * (DIFF_DH ** -0.5)
        qfeat = jnp.where(lane_k == DIFF_DH, 1.0,
                          jnp.where(lane_k == DIFF_DH + 1, -slopes[g] * row_k, 0.0))
        qs.append((jnp.where(lane_k < DIFF_DH, qf, qfeat).astype(BF16),
                   jnp.where(lane_k < DIFF_DH, pltpu.roll(qf, DIFF_DH, 1), qfeat).astype(BF16)))

    n_maps = 2 * nh
    n_stats = 2 * n_cols + 1

    def scores(i, j):
        rows = pl.ds(pl.multiple_of(j * tile, tile), tile)
        k_ref_i = k1_ref if i % 2 == 0 else k2_ref
        return _nt(k_ref_i[i // 2, rows, :], qs[i // 2][i % 2])

    def softmax_part(stats, s, c):
        out, alphas, ps = (), [], []
        for col in range(n_cols):
            sh = s[:, col * LANES:(col + 1) * LANES]
            m, l = stats[2 * col], stats[2 * col + 1]
            m_new = jnp.maximum(m, jnp.max(sh, axis=0, keepdims=True) + c)
            alpha = jnp.exp(m - m_new)
            p = jnp.exp(sh - (m_new - c))
            out += (m_new, alpha * l + jnp.sum(p, axis=0, keepdims=True))
            alphas.append(jnp.broadcast_to(alpha, (DIFF_DV, LANES)))
            ps.append(p.astype(BF16))
        return out, jnp.concatenate(alphas, axis=1), jnp.concatenate(ps, axis=1)

    def step(stats, j, cs, fixes):
        ss = [scores(i, j) for i in range(n_maps)]
        if fixes is not None:
            ss = [s + fixes[i // 2] for i, s in enumerate(ss)]
        parts = [softmax_part(stats[n_stats * i:n_stats * (i + 1)], ss[i], cs[i // 2])
                 for i in range(n_maps)]
        out = ()
        for i, (st, alpha, p) in enumerate(parts):
            a = alpha * stats[n_stats * i + n_stats - 1] + jnp.dot(
                vt_ref[i // 2, j], p, preferred_element_type=F32)
            out += st + (a,)
        return out

    def body(j, stats):
        off = jnp.full((1, LANES), (j - qi) * tile, I32).astype(F32)
        return step(stats, j, [off * slopes[g] for g in range(nh)], None)

    row = lambda v: jnp.full((1, LANES), v, F32)
    init = ((row(NEG), row(0.0)) * n_cols + (jnp.zeros((DIFF_DV, tile), F32),)) * n_maps
    stats = lax.fori_loop(0, qi, body, init)

    r_i = lax.broadcasted_iota(I32, (tile, tile), 0)
    c_i = lax.broadcasted_iota(I32, (tile, tile), 1)
    allowed = (r_i >> CHUNK_SHIFT) <= (c_i >> CHUNK_SHIFT)
    after = jnp.where(r_i > c_i, (c_i - r_i).astype(F32), 0.0)
    fixes = [jnp.where(allowed, (2.0 * slopes[g]) * after, NEG) for g in range(nh)]
    stats = step(stats, qi, [row(0.0)] * nh, fixes)
    lam = _lam(lq1, lk1, lq2, lk2)
    for g in range(nh):
        res = []
        for i in (2 * g, 2 * g + 1):
            st = stats[n_stats * i:n_stats * (i + 1)]
            l = jnp.concatenate([jnp.broadcast_to(st[2 * col + 1], (DIFF_DV, LANES))
                                 for col in range(n_cols)], axis=1)
            res.append(st[-1] / l)
        o = (res[0] - lam * res[1]).T
        o_ref[:, hcols[g]] = (_rms(o, dn_ref[...]) * (1.0 - LAM_INIT)).astype(o_ref.dtype)


def _attn_prompt(qb, kb, vb, lq1, lk1, lq2, lk2, dn, B, L):
    tile = min(ATT_TILE, L)
    nq = L // tile
    nh = ATT_HEADS_PER_STEP
    qmap = lambda b, h, i: (b * nq + i, h)
    kvmap = lambda b, h, i: (b, h)
    cmap = lambda b, h, i: (0, 0)
    lspec = pl.BlockSpec((1, DIFF_DH), cmap)
    return pl.pallas_call(
        functools.partial(_attn_prompt_kernel, tile=tile, nh=nh),
        grid=(B, DIFF_HEADS // nh, nq),
        in_specs=[pl.BlockSpec(memory_space=pltpu.SMEM),
                  pl.BlockSpec((tile, nh * DIFF_DV), qmap),
                  pl.BlockSpec((L, nh * DIFF_DV), kvmap), pl.BlockSpec((L, nh * DIFF_DV), kvmap),
                  lspec, lspec, lspec, lspec, pl.BlockSpec((1, DIFF_DV), cmap)],
        out_specs=pl.BlockSpec((tile, nh * DIFF_DV), qmap),
        out_shape=jax.ShapeDtypeStruct((B * L, D_MODEL), BF16),
        scratch_shapes=[pltpu.VMEM((nh, L, DIFF_DV), BF16), pltpu.VMEM((nh, L, DIFF_DV), BF16),
                        pltpu.VMEM((nh, nq, DIFF_DV, tile), BF16)],
        compiler_params=_cparams(("arbitrary", "arbitrary", "arbitrary")),
        name="attn_prompt",
    )(_alibi_slopes(), qb, kb, vb, lq1, lk1, lq2, lk2, dn)


def _attn_sample_kernel(slope_ref, q_ref, kp_ref, vp_ref, kn_ref, vn_ref, lq1, lk1, lq2, lk2,
                        dn_ref, o_ref, m_ref, l_ref, acc_ref, *, past, lq, chunk):
    c = pl.program_id(1)

    @pl.when(c == 0)
    def _():
        m_ref[...] = jnp.full_like(m_ref, NEG)
        l_ref[...] = jnp.zeros_like(l_ref)
        acc_ref[...] = jnp.zeros_like(acc_ref)

    def geometry(key0, n_keys):
        r = lax.broadcasted_iota(I32, (2 * lq, n_keys), 0)
        qpos = past + jnp.where(r >= lq, r - lq, r)
        kpos = key0 + lax.broadcasted_iota(I32, (2 * lq, n_keys), 1)
        allowed = (kpos >> CHUNK_SHIFT) <= (qpos >> CHUNK_SHIFT)
        return jnp.abs(qpos - kpos).astype(F32), jnp.where(allowed, 0.0, NEG)

    def block(hd, k_ref, v_ref, geo):
        cols = slice(hd * DIFF_DV, (hd + 1) * DIFF_DV)
        dist, mask = geo
        q1, q2 = _split_q(q_ref[:, cols])
        s = _nt(jnp.concatenate([q1, q2], axis=0), k_ref[:, cols])
        s = s + (mask - slope_ref[hd] * dist)
        m_old = m_ref[hd]
        m_new = jnp.maximum(m_old, jnp.max(s, axis=-1, keepdims=True))
        alpha = jnp.exp(m_old - m_new)
        p = jnp.exp(s - m_new)
        l_ref[hd] = alpha * l_ref[hd] + jnp.sum(p, axis=-1, keepdims=True)
        acc_ref[hd] = alpha * acc_ref[hd] + jnp.dot(p.astype(BF16), v_ref[:, cols],
                                                    preferred_element_type=F32)
        m_ref[hd] = m_new

    geo = geometry(c * chunk, chunk)
    for hd in range(DIFF_HEADS):
        block(hd, kp_ref, vp_ref, geo)

    @pl.when(c == pl.num_programs(1) - 1)
    def _():
        lam = _lam(lq1, lk1, lq2, lk2)
        geo_new = geometry(past, lq)
        for hd in range(DIFF_HEADS):
            block(hd, kn_ref, vn_ref, geo_new)
            o = acc_ref[hd] / l_ref[hd]
            o = o[:lq] - lam * o[lq:]
            o_ref[:, hd * DIFF_DV:(hd + 1) * DIFF_DV] = (
                _rms(o, dn_ref[...]) * (1.0 - LAM_INIT)).astype(o_ref.dtype)


def _attn_sample(qb, cache_k, cache_v, kb, vb, lq1, lk1, lq2, lk2, dn, B, L):
    past = cache_k.shape[2]
    chunk = min(SAMPLE_KEY_CHUNK, past)
    n_chunks = past // chunk
    cache_k = cache_k[0].reshape(B * past, D_MODEL).astype(BF16)
    cache_v = cache_v[0].reshape(B * past, D_MODEL).astype(BF16)
    bmap = lambda b, c: (b, 0)
    cmap = lambda b, c: (0, 0)
    lspec = pl.BlockSpec((1, DIFF_DH), cmap)
    tok = pl.BlockSpec((L, D_MODEL), bmap)
    cache = pl.BlockSpec((chunk, D_MODEL), lambda b, c: (b * n_chunks + c, 0))
    return pl.pallas_call(
        functools.partial(_attn_sample_kernel, past=past, lq=L, chunk=chunk),
        grid=(B, n_chunks),
        in_specs=[pl.BlockSpec(memory_space=pltpu.SMEM), tok, cache, cache, tok, tok,
                  lspec, lspec, lspec, lspec, pl.BlockSpec((1, DIFF_DV), cmap)],
        out_specs=tok,
        out_shape=jax.ShapeDtypeStruct((B * L, D_MODEL), BF16),
        scratch_shapes=[pltpu.VMEM((DIFF_HEADS, 2 * L, 1), F32),
                        pltpu.VMEM((DIFF_HEADS, 2 * L, 1), F32),
                        pltpu.VMEM((DIFF_HEADS, 2 * L, DIFF_DV), F32)],
        compiler_params=_cparams(("arbitrary", "arbitrary")),
        name="attn_sample",
    )(_alibi_slopes(), qb, cache_k, cache_v, kb, vb, lq1, lk1, lq2, lk2, dn)


ROUTE_E1, ROUTE_E2, ROUTE_W1, ROUTE_W2, ROUTE_R1, ROUTE_R2 = range(6)
GROUP_LANE0 = N_EXPERTS


def _merge_kernel(x_ref, oa_ref, ob_ref, ga_ref, gb_ref, wa_ref, wb_ref, wo_ref, nf_ref,
                  wr_ref, br_ref, x1_ref, h2_ref, route_ref, cnt_ref, run_ref):
    i = pl.program_id(0)

    @pl.when(i == 0)
    def _():
        run_ref[...] = jnp.zeros_like(run_ref)

    u_a = jnp.dot(oa_ref[...], wa_ref[...], preferred_element_type=F32)
    u_b = jnp.dot(ob_ref[...], wb_ref[...], preferred_element_type=F32)
    mix = (jax.nn.sigmoid(ga_ref[...].astype(F32)) * u_a
           + jax.nn.sigmoid(gb_ref[...].astype(F32)) * u_b)
    x1 = x_ref[...] + jnp.dot(mix.astype(BF16), wo_ref[...], preferred_element_type=F32)
    x1_ref[...] = x1
    h2 = _rms(x1, nf_ref[...])
    h2_ref[...] = h2

    tm = h2.shape[0]
    split = lambda a: (a.astype(BF16), (a - a.astype(BF16).astype(F32)).astype(BF16))
    w_pair = jnp.concatenate(split(wr_ref[...]), axis=1)
    parts = sum(jnp.dot(a, w_pair, preferred_element_type=F32) for a in split(h2))
    logits = parts[:, :LANES] + parts[:, LANES:] + br_ref[...]
    lane = lax.broadcasted_iota(I32, (tm, LANES), 1)
    big = jnp.int32(LANES)
    g_mask = (lane >= GROUP_LANE0) & (lane < GROUP_LANE0 + N_GROUPS)
    gl = jnp.where(g_mask, logits, -jnp.inf)
    gmax = jnp.max(gl, axis=-1, keepdims=True)
    g_sel = jnp.min(jnp.where(gl == gmax, lane - GROUP_LANE0, big), axis=-1, keepdims=True)
    p_g = 1.0 / jnp.sum(jnp.where(g_mask, jnp.exp(logits - gmax), 0.0), axis=-1, keepdims=True)
    e_mask = (lane < N_EXPERTS) & ((lane >> EPG_SHIFT) == g_sel)
    el = jnp.where(e_mask, logits, -jnp.inf)
    v1 = jnp.max(el, axis=-1, keepdims=True)
    i1 = jnp.min(jnp.where(el == v1, lane, big), axis=-1, keepdims=True)
    el2 = jnp.where(lane == i1, -jnp.inf, el)
    v2 = jnp.max(el2, axis=-1, keepdims=True)
    i2 = jnp.min(jnp.where(el2 == v2, lane, big), axis=-1, keepdims=True)
    t = jnp.exp(v2 - v1)
    w1 = p_g / (1.0 + t)
    w2 = p_g * t / (1.0 + t)

    oh1 = lane == i1
    oh2 = lane == i2
    cnt = jnp.where(oh1, 1.0, 0.0) + jnp.where(oh2, 1.0, 0.0)
    r_i = lax.broadcasted_iota(I32, (tm, tm), 0)
    c_i = lax.broadcasted_iota(I32, (tm, tm), 1)
    before = jnp.where(c_i < r_i, 1.0, 0.0).astype(BF16)
    prior = jnp.dot(before, cnt.astype(BF16), preferred_element_type=F32) + run_ref[...]
    rank1 = jnp.sum(jnp.where(oh1, prior, 0.0), axis=-1, keepdims=True)
    rank2 = jnp.sum(jnp.where(oh2, prior, 0.0), axis=-1, keepdims=True)
    run = run_ref[...] + jnp.sum(cnt, axis=0, keepdims=True)
    run_ref[...] = run
    cnt_ref[...] = run

    route = jnp.zeros((tm, LANES), F32)
    for pos, val in ((ROUTE_E1, i1.astype(F32)), (ROUTE_E2, i2.astype(F32)), (ROUTE_W1, w1),
                     (ROUTE_W2, w2), (ROUTE_R1, rank1), (ROUTE_R2, rank2)):
        route = jnp.where(lane == pos, val, route)
    route_ref[...] = route


def _merge(x, oa, ob, ga, gb, wa, wb, wo, nf, wr, br):
    T = x.shape[0]
    tm = min(TOKEN_TILE, T)
    row = lambda i: (i, 0)
    const = lambda i: (0, 0)
    wspec = pl.BlockSpec((D_MODEL, D_MODEL), const)
    tile = pl.BlockSpec((tm, D_MODEL), row)
    return pl.pallas_call(
        _merge_kernel,
        grid=(T // tm,),
        in_specs=[tile, tile, tile, tile, tile, wspec, wspec, wspec,
                  pl.BlockSpec((1, D_MODEL), const),
                  pl.BlockSpec((D_MODEL, LANES), const), pl.BlockSpec((1, LANES), const)],
        out_specs=(tile, tile, pl.BlockSpec((tm, LANES), row), pl.BlockSpec((1, LANES), const)),
        out_shape=(jax.ShapeDtypeStruct((T, D_MODEL), F32),
                   jax.ShapeDtypeStruct((T, D_MODEL), F32),
                   jax.ShapeDtypeStruct((T, LANES), F32),
                   jax.ShapeDtypeStruct((1, LANES), F32)),
        scratch_shapes=[pltpu.VMEM((1, LANES), F32)],
        compiler_params=_cparams(("arbitrary",)),
        name="merge",
    )(x, oa, ob, ga, gb, wa, wb, wo, nf, wr, br)


def _dispatch_kernel(dest_ref, h2_ref, xs_in_ref, xs_ref, sem):
    del xs_in_ref
    tm = h2_ref.shape[0]

    def row_copy(r, k):
        return pltpu.make_async_copy(h2_ref.at[pl.ds(r, 1)], xs_ref.at[pl.ds(dest_ref[k, r], 1)], sem)

    def start(i, c):
        base = pl.multiple_of(i * ROW_UNROLL, ROW_UNROLL)
        for u in range(ROW_UNROLL):
            row_copy(base + u, 0).start(priority=0)
            row_copy(base + u, 1).start(priority=1)
        return c

    def wait(i, c):
        base = pl.multiple_of(i * ROW_UNROLL, ROW_UNROLL)
        for u in range(ROW_UNROLL):
            row_copy(base + u, 0).wait()
            row_copy(base + u, 1).wait()
        return c

    lax.fori_loop(0, tm // ROW_UNROLL, start, 0)
    lax.fori_loop(0, tm // ROW_UNROLL, wait, 0)


def _dispatch(dest, h2, n_slots):
    T = h2.shape[0]
    tm = min(TOKEN_TILE, T)
    xs0 = jnp.zeros((n_slots, D_MODEL), F32)
    return pl.pallas_call(
        _dispatch_kernel,
        grid=(T // tm,),
        in_specs=[pl.BlockSpec((2, tm), lambda i: (0, i), memory_space=pltpu.SMEM),
                  pl.BlockSpec((tm, D_MODEL), lambda i: (i, 0)),
                  pl.BlockSpec(memory_space=pl.ANY)],
        out_specs=pl.BlockSpec(memory_space=pl.ANY),
        out_shape=jax.ShapeDtypeStruct((n_slots, D_MODEL), F32),
        scratch_shapes=[pltpu.SemaphoreType.DMA(())],
        input_output_aliases={2: 0},
        compiler_params=_cparams(("arbitrary",)),
        name="dispatch",
    )(dest, h2, xs0)


def _experts_kernel(be_ref, nu_ref, xs_ref, wg_ref, wu_ref, wd_ref, y_ref):
    del be_ref
    used = pl.program_id(0) < nu_ref[0]

    @pl.when(used)
    def _():
        x = xs_ref[...].astype(BF16)
        g = jnp.dot(x, wg_ref[0], preferred_element_type=F32)
        u = jnp.dot(x, wu_ref[0], preferred_element_type=F32)
        a = (g * jax.nn.sigmoid(g) * u).astype(BF16)
        y_ref[...] = jnp.dot(a, wd_ref[0], preferred_element_type=F32)

    @pl.when(jnp.logical_not(used))
    def _():
        y_ref[...] = jnp.zeros_like(y_ref)


def _experts(block_e, n_used, xs, wg, wu, wd):
    n_blocks = xs.shape[0] // SLOT_BLOCK
    blk = lambda i, be, nu: (jnp.minimum(i, nu[0] - 1), 0)
    wmap = lambda i, be, nu: (be[jnp.minimum(i, nu[0] - 1)], 0, 0)
    grid_spec = pltpu.PrefetchScalarGridSpec(
        num_scalar_prefetch=2,
        grid=(n_blocks,),
        in_specs=[pl.BlockSpec((SLOT_BLOCK, D_MODEL), blk),
                  pl.BlockSpec((1, D_MODEL, D_EXPERT), wmap),
                  pl.BlockSpec((1, D_MODEL, D_EXPERT), wmap),
                  pl.BlockSpec((1, D_EXPERT, D_MODEL), wmap)],
        out_specs=pl.BlockSpec((SLOT_BLOCK, D_MODEL), lambda i, be, nu: (i, 0)),
    )
    return pl.pallas_call(
        _experts_kernel,
        grid_spec=grid_spec,
        out_shape=jax.ShapeDtypeStruct((xs.shape[0], D_MODEL), F32),
        compiler_params=_cparams(("arbitrary",)),
        name="experts",
    )(block_e, n_used, xs, wg, wu, wd)


def _combine_kernel(dest_ref, dest_next_ref, x1_ref, route_ref, nf_ref, yb_ref, y_ref, g_ref, sem):
    i = pl.program_id(0)
    n = pl.num_programs(0)
    tm = x1_ref.shape[0]
    slot = i & 1

    def row_copy(d_ref, s, r, k):
        return pltpu.make_async_copy(yb_ref.at[pl.ds(d_ref[k, r], 1)], g_ref.at[s, k, pl.ds(r, 1)],
                                     sem.at[s])

    def gather(d_ref, s, wait):
        def rows(b, c):
            base = pl.multiple_of(b * ROW_UNROLL, ROW_UNROLL)
            for u in range(ROW_UNROLL):
                for k in range(2):
                    cp = row_copy(d_ref, s, base + u, k)
                    cp.wait() if wait else cp.start(priority=k)
            return c
        lax.fori_loop(0, tm // ROW_UNROLL, rows, 0)

    @pl.when(i == 0)
    def _():
        gather(dest_ref, slot, False)

    @pl.when(i + 1 < n)
    def _():
        gather(dest_next_ref, 1 - slot, False)

    gather(dest_ref, slot, True)
    route = route_ref[...]
    w1 = route[:, ROUTE_W1:ROUTE_W1 + 1]
    w2 = route[:, ROUTE_W2:ROUTE_W2 + 1]
    x2 = x1_ref[...] + (g_ref[slot, 0] * w1 + g_ref[slot, 1] * w2)
    y_ref[...] = _rms(x2, nf_ref[...])


def _combine(dest, x1, route, nfinal, yb):
    T = x1.shape[0]
    tm = min(TOKEN_TILE // 2, T)
    n = T // tm
    row = lambda i: (i, 0)
    return pl.pallas_call(
        _combine_kernel,
        grid=(n,),
        in_specs=[pl.BlockSpec((2, tm), lambda i: (0, i), memory_space=pltpu.SMEM),
                  pl.BlockSpec((2, tm), lambda i: (0, jnp.minimum(i + 1, n - 1)),
                               memory_space=pltpu.SMEM),
                  pl.BlockSpec((tm, D_MODEL), row), pl.BlockSpec((tm, LANES), row),
                  pl.BlockSpec((1, D_MODEL), lambda i: (0, 0)),
                  pl.BlockSpec(memory_space=pl.ANY)],
        out_specs=pl.BlockSpec((tm, D_MODEL), row),
        out_shape=jax.ShapeDtypeStruct((T, D_MODEL), F32),
        scratch_shapes=[pltpu.VMEM((2, 2, tm, D_MODEL), F32), pltpu.SemaphoreType.DMA((2,))],
        compiler_params=_cparams(("arbitrary",)),
        name="combine",
    )(dest, dest, x1, route, nfinal, yb)


def _moe(x1, h2, route, counts, wg, wu, wd, nfinal):
    T = x1.shape[0]
    n_slots = T * 2 + N_EXPERTS * SLOT_BLOCK
    n_blocks = n_slots // SLOT_BLOCK
    cnt = counts[0, :N_EXPERTS].astype(I32)
    padded = (cnt + SLOT_BLOCK - 1) // SLOT_BLOCK * SLOT_BLOCK
    pad_ends = jnp.cumsum(padded)
    pad_starts = pad_ends - padded
    e = route[:, ROUTE_E1:ROUTE_E2 + 1].astype(I32)
    rank = route[:, ROUTE_R1:ROUTE_R2 + 1].astype(I32)
    dest = (pad_starts[e] + rank).T
    block_start = jnp.arange(n_blocks, dtype=I32) * SLOT_BLOCK
    block_e = jnp.sum((block_start[:, None] >= pad_ends[None, :]).astype(I32), axis=1)
    block_e = jnp.minimum(block_e, N_EXPERTS - 1)
    n_used = (pad_ends[-1:] // SLOT_BLOCK).astype(I32)
    xs = _dispatch(dest, h2, n_slots)
    yb = _experts(block_e, n_used, xs, wg, wu, wd)
    return _combine(dest, x1, route, nfinal, yb)


def _layer(x, s0, k_past, v_past, p, B, L):
    row2 = lambda a: a.reshape(1, -1)
    qka, va, ra, qb, kb, vb, ga, gb, da, k5, v5 = _in_proj(x, row2(p["norm_mix"]), p["w_in"])
    oa, s_new = _gla(qka, va, ra, da, p["w_decay"], row2(p["b_decay"]), row2(p["gla_norm"]),
                     s0, B, L)
    lams = [row2(p[n]) for n in ("lambda_q1", "lambda_k1", "lambda_q2", "lambda_k2")]
    dn = row2(p["diff_norm"])
    if k_past is None:
        ob = _attn_prompt(qb, kb, vb, *lams, dn, B, L)
    else:
        ob = _attn_sample(qb, k_past, v_past, kb, vb, *lams, dn, B, L)
    x1, h2, route, counts = _merge(x, oa, ob, ga, gb, p["w_proj_a"], p["w_proj_b"], p["w_out"],
                                   row2(p["norm_ffn"]), p["w_router"], p["b_router"])
    y = _moe(x1, h2, route, counts, p["w_gate"], p["w_up"], p["w_down"], row2(p["norm_final"]))
    return y, s_new, k5, v5


def kernel(x_prompt, x_sample, cache_k, cache_v, state_gla, norm_mix, w_in, w_decay, b_decay,
           gla_norm, w_proj_a, lambda_q1, lambda_k1, lambda_q2, lambda_k2, diff_norm, w_proj_b,
           w_out, norm_ffn, w_router_group, b_router_group, w_router_expert, b_router_expert,
           w_gate, w_up, w_down, norm_final):
    B, L, D = x_prompt.shape
    Bs, Ls, _ = x_sample.shape
    w_router = jnp.concatenate(
        [w_router_expert[0], w_router_group[0],
         jnp.zeros((D, LANES - N_EXPERTS - N_GROUPS), F32)], axis=1)
    b_router = jnp.concatenate(
        [b_router_expert[0], b_router_group[0],
         jnp.zeros((LANES - N_EXPERTS - N_GROUPS,), F32)]).reshape(1, LANES)
    p = dict(
        norm_mix=norm_mix[0], w_in=_prep_w_in(w_in[0]),
        w_decay=jnp.pad(w_decay[0], ((0, LANES - GLA_RANK), (0, 0))), b_decay=b_decay[0],
        gla_norm=gla_norm[0], w_proj_a=w_proj_a[0].astype(BF16),
        lambda_q1=lambda_q1[0], lambda_k1=lambda_k1[0], lambda_q2=lambda_q2[0],
        lambda_k2=lambda_k2[0], diff_norm=diff_norm[0], w_proj_b=w_proj_b[0].astype(BF16),
        w_out=w_out[0].astype(BF16), norm_ffn=norm_ffn[0], w_router=w_router, b_router=b_router,
        w_gate=w_gate[0].astype(BF16), w_up=w_up[0].astype(BF16), w_down=w_down[0].astype(BF16),
        norm_final=norm_final)

    s0p = jnp.zeros((B, GLA_HEADS, GLA_DK, GLA_DV), F32)
    yp, sp, kp, vp = _layer(x_prompt.reshape(B * L, D), s0p, None, None, p, B, L)
    ys, ss, ks, vs = _layer(x_sample.reshape(Bs * Ls, D), state_gla[0], cache_k, cache_v,
                            p, Bs, Ls)
    kv = lambda a, b, l: a.reshape(1, b, l, DIFF_HEADS, DIFF_DV)
    return (yp.reshape(B, L, D), ys.reshape(Bs, Ls, D), kv(kp, B, L), kv(vp, B, L), sp[None],
            kv(ks, Bs, Ls), kv(vs, Bs, Ls), ss[None])
```

```python
import functools

import jax
import jax.numpy as jnp
from jax import lax
from jax.experimental import pallas as pl
from jax.experimental.pallas import tpu as pltpu

F32 = jnp.float32
BF16 = jnp.bfloat16
I32 = jnp.int32

D_MODEL = 1024
CHUNK = 64
CHUNK_SHIFT = 6
EPS = 1e-6
GLA_HEADS = 4
GLA_DK = 128
GLA_DV = 256
GLA_RANK = 16
GLA_TAU = 16.0
DIFF_HEADS = 8
DIFF_DH = 64
DIFF_DV = 128
N_GROUPS = 4
EXPERTS_PER_GROUP = 4
EPG_SHIFT = 2
N_EXPERTS = 16
D_EXPERT = 512
LAM_INIT = 0.8 - 0.6

LANES = 128
TOKEN_TILE = 512
SLOT_BLOCK = 256
SLOT_SHIFT = 8
ATT_TILE = 256
ATT_HEADS_PER_STEP = 2
ROW_UNROLL = 8
SAMPLE_KEY_CHUNK = 1024
VMEM_LIMIT = 56 * 1024 * 1024
NEG = -1e30

HI = lax.Precision.HIGHEST


def _cparams(sem):
    return pltpu.CompilerParams(dimension_semantics=sem, vmem_limit_bytes=VMEM_LIMIT)


def _nt(a, b):
    return lax.dot_general(a, b, (((1,), (1,)), ((), ())), preferred_element_type=F32)


def _tn(a, b):
    return lax.dot_general(a, b, (((0,), (0,)), ((), ())), preferred_element_type=F32)


def _rms(x, g):
    return x * lax.rsqrt(jnp.mean(x * x, axis=-1, keepdims=True) + EPS) * g


def _inproj_kernel(x_ref, g_ref, w_ref, qka_ref, va_ref, ra_ref, qb_ref, kb_ref, vb_ref,
                   ga_ref, gb_ref, da_ref, k5_ref, v5_ref, stage_ref, sem):
    i = pl.program_id(0)
    n = pl.num_programs(0)
    tm = x_ref.shape[0]
    slot = i & 1

    def head_copy(s, which, hd):
        dst = (k5_ref, v5_ref)[which]
        return pltpu.make_async_copy(stage_ref.at[s, which, hd],
                                     dst.at[pl.ds(i * tm, tm), hd, :], sem.at[s])

    def for_all(s, fn):
        for which in range(2):
            for hd in range(DIFF_HEADS):
                fn(head_copy(s, which, hd))

    @pl.when(i >= 2)
    def _():
        for_all(slot, lambda cp: cp.wait())

    h = _rms(x_ref[...], g_ref[...]).astype(BF16)
    outs = (qka_ref, va_ref, ra_ref, qb_ref, kb_ref, vb_ref, ga_ref, gb_ref)
    for c, o in enumerate(outs):
        z = jnp.dot(h, w_ref[:, c * D_MODEL:(c + 1) * D_MODEL], preferred_element_type=F32)
        o[...] = z.astype(o.dtype)
        which = 0 if o is kb_ref else 1 if o is vb_ref else None
        if which is not None:
            for hd in range(DIFF_HEADS):
                stage_ref[slot, which, hd] = z[:, hd * DIFF_DV:(hd + 1) * DIFF_DV]
    da_ref[...] = jnp.dot(h, w_ref[:, 8 * D_MODEL:8 * D_MODEL + LANES],
                          preferred_element_type=F32)
    for_all(slot, lambda cp: cp.start())

    @pl.when(i == n - 1)
    def _():
        for_all(slot, lambda cp: cp.wait())

    @pl.when((i == n - 1) & (i >= 1))
    def _():
        for_all(1 - slot, lambda cp: cp.wait())


def _prep_w_in(w_in):
    o = 0
    parts = {}
    for name, n in (("qa", 512), ("ka", 512), ("va", 1024), ("ra", 1024), ("da", GLA_RANK),
                    ("qb", 1024), ("kb", 1024), ("vb", 1024), ("ga", 1024), ("gb", 1024)):
        parts[name] = w_in[:, o:o + n]
        o += n
    pad = jnp.zeros((D_MODEL, LANES - GLA_RANK), w_in.dtype)
    w = jnp.concatenate([parts[k] for k in ("qa", "ka", "va", "ra", "qb", "kb", "vb", "ga", "gb",
                                            "da")] + [pad], axis=1)
    return w.astype(BF16)


def _in_proj(x, g, w):
    T = x.shape[0]
    tm = min(TOKEN_TILE, T)
    wcols = w.shape[1]
    row = lambda i: (i, 0)
    const = lambda i: (0, 0)
    wide = lambda dt: jax.ShapeDtypeStruct((T, D_MODEL), dt)
    heads = jax.ShapeDtypeStruct((T, DIFF_HEADS, DIFF_DV), F32)
    out_shape = (wide(BF16),) * 8 + (jax.ShapeDtypeStruct((T, LANES), F32), heads, heads)
    out_specs = tuple([pl.BlockSpec((tm, D_MODEL), row)] * 8 + [pl.BlockSpec((tm, LANES), row)]
                      + [pl.BlockSpec(memory_space=pl.ANY)] * 2)
    return pl.pallas_call(
        _inproj_kernel,
        grid=(T // tm,),
        in_specs=[pl.BlockSpec((tm, D_MODEL), row),
                  pl.BlockSpec((1, D_MODEL), const),
                  pl.BlockSpec((D_MODEL, wcols), const, pipeline_mode=pl.Buffered(1))],
        out_specs=out_specs,
        out_shape=out_shape,
        scratch_shapes=[pltpu.VMEM((2, 2, DIFF_HEADS, tm, DIFF_DV), F32),
                        pltpu.SemaphoreType.DMA((2,))],
        compiler_params=_cparams(("arbitrary",)),
        name="in_proj",
    )(x, g, w)


def _gla_kernel(qka_ref, va_ref, ra_ref, da_ref, wd_ref, bd_ref, gn_ref, s0_ref,
                oa_ref, sout_ref, s_ref, la_ref, *, chunk, n_chunks):
    l = pl.program_id(1)

    @pl.when(l == 0)
    def _():
        s_ref[...] = s0_ref[0]

    split = lambda a: (a.astype(BF16), (a - a.astype(BF16).astype(F32)).astype(BF16))
    n_k = GLA_HEADS * GLA_DK
    fold = lambda z: z[:, :n_k] + z[:, n_k:]

    wd_pair = jnp.concatenate(split(wd_ref[...]), axis=1)
    x = fold(sum(jnp.dot(a, wd_pair, preferred_element_type=F32) for a in split(da_ref[...])))
    x = x + bd_ref[...]
    log_a = (jnp.minimum(x, 0.0) - jnp.log1p(jnp.exp(-jnp.abs(x)))) * (1.0 / GLA_TAU)
    lb = log_a.shape[0]
    shift = chunk.bit_length() - 1
    r_b = lax.broadcasted_iota(I32, (lb, lb), 0)
    c_b = lax.broadcasted_iota(I32, (lb, lb), 1)
    tri = jnp.where((c_b <= r_b) & ((c_b >> shift) == (r_b >> shift)), 1.0, 0.0).astype(BF16)
    la_ref[...] = fold(jnp.dot(tri, jnp.concatenate(split(log_a), axis=1),
                               preferred_element_type=F32))

    r_i = lax.broadcasted_iota(I32, (chunk, chunk), 0)
    c_i = lax.broadcasted_iota(I32, (chunk, chunk), 1)
    causal = c_i <= r_i
    gn = gn_ref[...]
    scale = GLA_DK ** -0.5
    heads = range(GLA_HEADS)
    vcols = [slice(h * GLA_DV, (h + 1) * GLA_DV) for h in heads]

    @pl.loop(0, n_chunks)
    def _(c):
        rows = pl.ds(pl.multiple_of(c * chunk, chunk), chunk)
        b_all = la_ref[rows, :]
        q_t, k_t, k_end, decay = [], [], [], []
        for h in heads:
            b = b_all[:, h * GLA_DK:(h + 1) * GLA_DK]
            b_last = b[chunk - 1:chunk, :]
            q = qka_ref[rows, h * GLA_DK:(h + 1) * GLA_DK].astype(F32) * scale
            k = qka_ref[rows, n_k + h * GLA_DK:n_k + (h + 1) * GLA_DK].astype(F32)
            q_t.append((q * jnp.exp(b)).astype(BF16))
            k_t.append((k * jnp.exp(-b)).astype(BF16))
            k_end.append((k * jnp.exp(b_last - b)).astype(BF16))
            decay.append(jnp.broadcast_to(jnp.exp(b_last), (GLA_DK, GLA_DK)).T)
        s_old = [s_ref[h] for h in heads]
        o_state = [jnp.dot(q_t[h], s_old[h].astype(BF16), preferred_element_type=F32) for h in heads]
        att = [_nt(q_t[h], k_t[h]) for h in heads]
        kv = [_tn(k_end[h], va_ref[rows, vcols[h]]) for h in heads]
        for h in heads:
            a = jnp.where(causal, att[h], 0.0).astype(BF16)
            o = o_state[h] + jnp.dot(a, va_ref[rows, vcols[h]], preferred_element_type=F32)
            s_ref[h] = jnp.concatenate([decay[h], decay[h]], axis=1) * s_old[h] + kv[h]
            r = ra_ref[rows, vcols[h]].astype(F32)
            oa_ref[rows, vcols[h]] = (_rms(o, gn) * (r * jax.nn.sigmoid(r))).astype(oa_ref.dtype)

    @pl.when(l == pl.num_programs(1) - 1)
    def _():
        sout_ref[0] = s_ref[...]


def _gla(qka, va, ra, da, wd, bd, gn, s0, B, L):
    chunk = min(CHUNK, L)
    lb = min(TOKEN_TILE, L)
    nl = L // lb
    row = lambda b, l: (b * nl + l, 0)
    const2 = lambda b, l: (0, 0)
    st = lambda b, l: (b, 0, 0, 0)
    kern = functools.partial(_gla_kernel, chunk=chunk, n_chunks=lb // chunk)
    return pl.pallas_call(
        kern,
        grid=(B, nl),
        in_specs=[pl.BlockSpec((lb, D_MODEL), row), pl.BlockSpec((lb, D_MODEL), row),
                  pl.BlockSpec((lb, D_MODEL), row), pl.BlockSpec((lb, LANES), row),
                  pl.BlockSpec((LANES, GLA_HEADS * GLA_DK), const2),
                  pl.BlockSpec((1, GLA_HEADS * GLA_DK), const2),
                  pl.BlockSpec((1, GLA_DV), const2),
                  pl.BlockSpec((1, GLA_HEADS, GLA_DK, GLA_DV), st)],
        out_specs=(pl.BlockSpec((lb, D_MODEL), row),
                   pl.BlockSpec((1, GLA_HEADS, GLA_DK, GLA_DV), st)),
        out_shape=(jax.ShapeDtypeStruct((B * L, D_MODEL), BF16),
                   jax.ShapeDtypeStruct((B, GLA_HEADS, GLA_DK, GLA_DV), F32)),
        scratch_shapes=[pltpu.VMEM((GLA_HEADS, GLA_DK, GLA_DV), F32),
                        pltpu.VMEM((lb, GLA_HEADS * GLA_DK), F32)],
        compiler_params=_cparams(("arbitrary", "arbitrary")),
        name="gla",
    )(qka, va, ra, da, wd, bd, gn, s0)


def _lam(lq1, lk1, lq2, lk2):
    a = jnp.sum(lq1[...] * lk1[...], axis=-1, keepdims=True)
    b = jnp.sum(lq2[...] * lk2[...], axis=-1, keepdims=True)
    return jnp.exp(a) - jnp.exp(b) + LAM_INIT


def _split_q(q):
    lane = lax.broadcasted_iota(I32, q.shape, 1)
    qs = q * jnp.asarray(DIFF_DH ** -0.5, q.dtype)
    zero = jnp.zeros_like(qs)
    return jnp.where(lane < DIFF_DH, qs, zero), jnp.where(lane >= DIFF_DH, qs, zero)


def _alibi_slopes():
    return jnp.asarray([2.0 ** (-8.0 * (h + 1) / DIFF_HEADS) for h in range(DIFF_HEADS)], F32)


def _attn_prompt_kernel(slope_ref, q_ref, k_ref, v_ref, lq1, lk1, lq2, lk2, dn_ref, o_ref,
                        k1_ref, k2_ref, vt_ref, *, tile, nh):
    qi = pl.program_id(2)
    n_kv = vt_ref.shape[1]
    n_cols = tile // LANES
    lane_k = lax.broadcasted_iota(I32, (tile, DIFF_DV), 1)
    row_k = lax.broadcasted_iota(I32, (tile, DIFF_DV), 0).astype(F32)
    slopes = [slope_ref[pl.program_id(1) * nh + g] for g in range(nh)]
    hcols = [slice(g * DIFF_DV, (g + 1) * DIFF_DV) for g in range(nh)]

    @pl.when(qi == 0)
    def _():
        for g in range(nh):
            kfeat = jnp.where(lane_k == DIFF_DH, slopes[g] * row_k,
                              jnp.where(lane_k == DIFF_DH + 1, 1.0, 0.0))
            for c in range(n_kv):
                rows = slice(c * tile, (c + 1) * tile)
                kf = k_ref[rows, hcols[g]].astype(F32)
                k1_ref[g, rows, :] = jnp.where(lane_k < DIFF_DH, kf, kfeat).astype(BF16)
                k2_ref[g, rows, :] = jnp.where(lane_k < DIFF_DH, pltpu.roll(kf, DIFF_DH, 1),
                                               kfeat).astype(BF16)
                vt_ref[g, c] = v_ref[rows, hcols[g]].astype(F32).T.astype(BF16)

    qs = []
    for g in range(nh):
        qf = q_ref[:, hcols[g]].astype(F32) * (DIFF_DH ** -0.5)
        qfeat = jnp.where(lane_k == DIFF_DH, 1.0,
                          jnp.where(lane_k == DIFF_DH + 1, -slopes[g] * row_k, 0.0))
        qs.append((jnp.where(lane_k < DIFF_DH, qf, qfeat).T.astype(BF16),
                   jnp.where(lane_k < DIFF_DH, pltpu.roll(qf, DIFF_DH, 1), qfeat).T.astype(BF16)))

    n_maps = 2 * nh
    n_stats = 2 * n_cols + 1

    def scores(i, j):
        rows = pl.ds(pl.multiple_of(j * tile, tile), tile)
        k_ref_i = k1_ref if i % 2 == 0 else k2_ref
        return jnp.dot(k_ref_i[i // 2, rows, :], qs[i // 2][i % 2], preferred_element_type=F32)

    def softmax_part(stats, s, c):
        out, alphas, ps = (), [], []
        for col in range(n_cols):
            sh = s[:, col * LANES:(col + 1) * LANES]
            m, l = stats[2 * col], stats[2 * col + 1]
            m_new = jnp.maximum(m, jnp.max(sh, axis=0, keepdims=True) + c)
            alpha = jnp.exp(m - m_new)
            p = jnp.exp(sh - (m_new - c))
            out += (m_new, alpha * l + jnp.sum(p, axis=0, keepdims=True))
            alphas.append(jnp.broadcast_to(alpha, (DIFF_DV, LANES)))
            ps.append(p.astype(BF16))
        return out, jnp.concatenate(alphas, axis=1), jnp.concatenate(ps, axis=1)

    def step(stats, j, cs, fixes):
        ss = [scores(i, j) for i in range(n_maps)]
        if fixes is not None:
            ss = [s + fixes[i // 2] for i, s in enumerate(ss)]
        parts = [softmax_part(stats[n_stats * i:n_stats * (i + 1)], ss[i], cs[i // 2])
                 for i in range(n_maps)]
        out = ()
        for i, (st, alpha, p) in enumerate(parts):
            a = alpha * stats[n_stats * i + n_stats - 1] + jnp.dot(
                vt_ref[i // 2, j], p, preferred_element_type=F32)
            out += st + (a,)
        return out

    def body(j, stats):
        off = jnp.full((1, LANES), (j - qi) * tile, I32).astype(F32)
        return step(stats, j, [off * slopes[g] for g in range(nh)], None)

    row = lambda v: jnp.full((1, LANES), v, F32)
    init = ((row(NEG), row(0.0)) * n_cols + (jnp.zeros((DIFF_DV, tile), F32),)) * n_maps
    stats = lax.fori_loop(0, qi, body, init)

    r_i = lax.broadcasted_iota(I32, (tile, tile), 0)
    c_i = lax.broadcasted_iota(I32, (tile, tile), 1)
    allowed = (r_i >> CHUNK_SHIFT) <= (c_i >> CHUNK_SHIFT)
    after = jnp.where(r_i > c_i, (c_i - r_i).astype(F32), 0.0)
    fixes = [jnp.where(allowed, (2.0 * slopes[g]) * after, NEG) for g in range(nh)]
    stats = step(stats, qi, [row(0.0)] * nh, fixes)
    lam = _lam(lq1, lk1, lq2, lk2)
    for g in range(nh):
        res = []
        for i in (2 * g, 2 * g + 1):
            st = stats[n_stats * i:n_stats * (i + 1)]
            l = jnp.concatenate([jnp.broadcast_to(st[2 * col + 1], (DIFF_DV, LANES))
                                 for col in range(n_cols)], axis=1)
            res.append(st[-1] / l)
        o = (res[0] - lam * res[1]).T
        o_ref[:, hcols[g]] = (_rms(o, dn_ref[...]) * (1.0 - LAM_INIT)).astype(o_ref.dtype)


def _attn_prompt(qb, kb, vb, lq1, lk1, lq2, lk2, dn, B, L):
    tile = min(ATT_TILE, L)
    nq = L // tile
    nh = ATT_HEADS_PER_STEP
    qmap = lambda b, h, i: (b * nq + i, h)
    kvmap = lambda b, h, i: (b, h)
    cmap = lambda b, h, i: (0, 0)
    lspec = pl.BlockSpec((1, DIFF_DH), cmap)
    return pl.pallas_call(
        functools.partial(_attn_prompt_kernel, tile=tile, nh=nh),
        grid=(B, DIFF_HEADS // nh, nq),
        in_specs=[pl.BlockSpec(memory_space=pltpu.SMEM),
                  pl.BlockSpec((tile, nh * DIFF_DV), qmap),
                  pl.BlockSpec((L, nh * DIFF_DV), kvmap), pl.BlockSpec((L, nh * DIFF_DV), kvmap),
                  lspec, lspec, lspec, lspec, pl.BlockSpec((1, DIFF_DV), cmap)],
        out_specs=pl.BlockSpec((tile, nh * DIFF_DV), qmap),
        out_shape=jax.ShapeDtypeStruct((B * L, D_MODEL), BF16),
        scratch_shapes=[pltpu.VMEM((nh, L, DIFF_DV), BF16), pltpu.VMEM((nh, L, DIFF_DV), BF16),
                        pltpu.VMEM((nh, nq, DIFF_DV, tile), BF16)],
        compiler_params=_cparams(("arbitrary", "arbitrary", "arbitrary")),
        name="attn_prompt",
    )(_alibi_slopes(), qb, kb, vb, lq1, lk1, lq2, lk2, dn)


def _attn_sample_kernel(slope_ref, q_ref, kp_ref, vp_ref, kn_ref, vn_ref, lq1, lk1, lq2, lk2,
                        dn_ref, o_ref, m_ref, l_ref, acc_ref, kbuf, vbuf, sem, *, past, lq, chunk):
    b = pl.program_id(0)
    c = pl.program_id(1)
    n_c = pl.num_programs(1)
    step = b * n_c + c
    slot = step & 1

    def slab_copies(bb, cc, s, fn):
        rows = pl.ds(cc * chunk, chunk)
        for hd in range(DIFF_HEADS):
            fn(pltpu.make_async_copy(kp_ref.at[0, bb, rows, hd, :], kbuf.at[s, hd], sem.at[s]))
            fn(pltpu.make_async_copy(vp_ref.at[0, bb, rows, hd, :], vbuf.at[s, hd], sem.at[s]))

    @pl.when(step == 0)
    def _():
        slab_copies(b, c, slot, lambda cp: cp.start())

    @pl.when(step + 1 < pl.num_programs(0) * n_c)
    def _():
        wrap = c + 1 == n_c
        slab_copies(jnp.where(wrap, b + 1, b), jnp.where(wrap, 0, c + 1), 1 - slot,
                    lambda cp: cp.start())

    slab_copies(b, c, slot, lambda cp: cp.wait())

    @pl.when(c == 0)
    def _():
        m_ref[...] = jnp.full_like(m_ref, NEG)
        l_ref[...] = jnp.zeros_like(l_ref)
        acc_ref[...] = jnp.zeros_like(acc_ref)

    def geometry(key0, n_keys):
        r = lax.broadcasted_iota(I32, (2 * lq, n_keys), 0)
        qpos = past + jnp.where(r >= lq, r - lq, r)
        kpos = key0 + lax.broadcasted_iota(I32, (2 * lq, n_keys), 1)
        allowed = (kpos >> CHUNK_SHIFT) <= (qpos >> CHUNK_SHIFT)
        return jnp.abs(qpos - kpos).astype(F32), jnp.where(allowed, 0.0, NEG)

    def block(hd, k, v, geo):
        dist, mask = geo
        q1, q2 = _split_q(q_ref[:, hd * DIFF_DV:(hd + 1) * DIFF_DV])
        s = _nt(jnp.concatenate([q1, q2], axis=0), k)
        s = s + (mask - slope_ref[hd] * dist)
        m_old = m_ref[hd]
        m_new = jnp.maximum(m_old, jnp.max(s, axis=-1, keepdims=True))
        alpha = jnp.exp(m_old - m_new)
        p = jnp.exp(s - m_new)
        l_ref[hd] = alpha * l_ref[hd] + jnp.sum(p, axis=-1, keepdims=True)
        acc_ref[hd] = alpha * acc_ref[hd] + jnp.dot(p.astype(BF16), v, preferred_element_type=F32)
        m_ref[hd] = m_new

    geo = geometry(c * chunk, chunk)
    for hd in range(DIFF_HEADS):
        block(hd, kbuf[slot, hd].astype(BF16), vbuf[slot, hd].astype(BF16), geo)

    @pl.when(c == n_c - 1)
    def _():
        lam = _lam(lq1, lk1, lq2, lk2)
        geo_new = geometry(past, lq)
        for hd in range(DIFF_HEADS):
            cols = slice(hd * DIFF_DV, (hd + 1) * DIFF_DV)
            block(hd, kn_ref[:, cols], vn_ref[:, cols], geo_new)
            o = acc_ref[hd] / l_ref[hd]
            o = o[:lq] - lam * o[lq:]
            o_ref[:, hd * DIFF_DV:(hd + 1) * DIFF_DV] = (
                _rms(o, dn_ref[...]) * (1.0 - LAM_INIT)).astype(o_ref.dtype)


def _attn_sample(qb, cache_k, cache_v, kb, vb, lq1, lk1, lq2, lk2, dn, B, L):
    past = cache_k.shape[2]
    chunk = min(SAMPLE_KEY_CHUNK, past)
    n_chunks = past // chunk
    bmap = lambda b, c: (b, 0)
    cmap = lambda b, c: (0, 0)
    lspec = pl.BlockSpec((1, DIFF_DH), cmap)
    tok = pl.BlockSpec((L, D_MODEL), bmap)
    cache = pl.BlockSpec(memory_space=pl.ANY)
    slabs = pltpu.VMEM((2, DIFF_HEADS, chunk, DIFF_DV), F32)
    return pl.pallas_call(
        functools.partial(_attn_sample_kernel, past=past, lq=L, chunk=chunk),
        grid=(B, n_chunks),
        in_specs=[pl.BlockSpec(memory_space=pltpu.SMEM), tok, cache, cache, tok, tok,
                  lspec, lspec, lspec, lspec, pl.BlockSpec((1, DIFF_DV), cmap)],
        out_specs=tok,
        out_shape=jax.ShapeDtypeStruct((B * L, D_MODEL), BF16),
        scratch_shapes=[pltpu.VMEM((DIFF_HEADS, 2 * L, 1), F32),
                        pltpu.VMEM((DIFF_HEADS, 2 * L, 1), F32),
                        pltpu.VMEM((DIFF_HEADS, 2 * L, DIFF_DV), F32),
                        slabs, slabs, pltpu.SemaphoreType.DMA((2,))],
        compiler_params=_cparams(("arbitrary", "arbitrary")),
        name="attn_sample",
    )(_alibi_slopes(), qb, cache_k, cache_v, kb, vb, lq1, lk1, lq2, lk2, dn)


ROUTE_E1, ROUTE_E2, ROUTE_W1, ROUTE_W2, ROUTE_R1, ROUTE_R2 = range(6)
GROUP_LANE0 = N_EXPERTS


def _merge_kernel(x_ref, oa_ref, ob_ref, ga_ref, gb_ref, wa_ref, wb_ref, wo_ref, nf_ref,
                  wr_ref, br_ref, x1_ref, h2_ref, route_ref, cnt_ref, run_ref):
    i = pl.program_id(0)

    @pl.when(i == 0)
    def _():
        run_ref[...] = jnp.zeros_like(run_ref)

    u_a = jnp.dot(oa_ref[...], wa_ref[...], preferred_element_type=F32)
    u_b = jnp.dot(ob_ref[...], wb_ref[...], preferred_element_type=F32)
    mix = (jax.nn.sigmoid(ga_ref[...].astype(F32)) * u_a
           + jax.nn.sigmoid(gb_ref[...].astype(F32)) * u_b)
    x1 = x_ref[...] + jnp.dot(mix.astype(BF16), wo_ref[...], preferred_element_type=F32)
    x1_ref[...] = x1
    h2 = _rms(x1, nf_ref[...])
    h2_ref[...] = h2

    tm = h2.shape[0]
    split = lambda a: (a.astype(BF16), (a - a.astype(BF16).astype(F32)).astype(BF16))
    w_pair = jnp.concatenate(split(wr_ref[...]), axis=1)
    parts = sum(jnp.dot(a, w_pair, preferred_element_type=F32) for a in split(h2))
    logits = parts[:, :LANES] + parts[:, LANES:] + br_ref[...]
    lane = lax.broadcasted_iota(I32, (tm, LANES), 1)
    big = jnp.int32(LANES)
    g_mask = (lane >= GROUP_LANE0) & (lane < GROUP_LANE0 + N_GROUPS)
    gl = jnp.where(g_mask, logits, -jnp.inf)
    gmax = jnp.max(gl, axis=-1, keepdims=True)
    g_sel = jnp.min(jnp.where(gl == gmax, lane - GROUP_LANE0, big), axis=-1, keepdims=True)
    p_g = 1.0 / jnp.sum(jnp.where(g_mask, jnp.exp(logits - gmax), 0.0), axis=-1, keepdims=True)
    e_mask = (lane < N_EXPERTS) & ((lane >> EPG_SHIFT) == g_sel)
    el = jnp.where(e_mask, logits, -jnp.inf)
    v1 = jnp.max(el, axis=-1, keepdims=True)
    i1 = jnp.min(jnp.where(el == v1, lane, big), axis=-1, keepdims=True)
    el2 = jnp.where(lane == i1, -jnp.inf, el)
    v2 = jnp.max(el2, axis=-1, keepdims=True)
    i2 = jnp.min(jnp.where(el2 == v2, lane, big), axis=-1, keepdims=True)
    t = jnp.exp(v2 - v1)
    w1 = p_g / (1.0 + t)
    w2 = p_g * t / (1.0 + t)

    oh1 = lane == i1
    oh2 = lane == i2
    cnt = jnp.where(oh1, 1.0, 0.0) + jnp.where(oh2, 1.0, 0.0)
    r_i = lax.broadcasted_iota(I32, (tm, tm), 0)
    c_i = lax.broadcasted_iota(I32, (tm, tm), 1)
    before = jnp.where(c_i < r_i, 1.0, 0.0).astype(BF16)
    prior = jnp.dot(before, cnt.astype(BF16), preferred_element_type=F32) + run_ref[...]
    rank1 = jnp.sum(jnp.where(oh1, prior, 0.0), axis=-1, keepdims=True)
    rank2 = jnp.sum(jnp.where(oh2, prior, 0.0), axis=-1, keepdims=True)
    run = run_ref[...] + jnp.sum(cnt, axis=0, keepdims=True)
    run_ref[...] = run
    cnt_ref[...] = run

    route = jnp.zeros((tm, LANES), F32)
    for pos, val in ((ROUTE_E1, i1.astype(F32)), (ROUTE_E2, i2.astype(F32)), (ROUTE_W1, w1),
                     (ROUTE_W2, w2), (ROUTE_R1, rank1), (ROUTE_R2, rank2)):
        route = jnp.where(lane == pos, val, route)
    route_ref[...] = route


def _merge(x, oa, ob, ga, gb, wa, wb, wo, nf, wr, br):
    T = x.shape[0]
    tm = min(TOKEN_TILE, T)
    row = lambda i: (i, 0)
    const = lambda i: (0, 0)
    wspec = pl.BlockSpec((D_MODEL, D_MODEL), const)
    tile = pl.BlockSpec((tm, D_MODEL), row)
    return pl.pallas_call(
        _merge_kernel,
        grid=(T // tm,),
        in_specs=[tile, tile, tile, tile, tile, wspec, wspec, wspec,
                  pl.BlockSpec((1, D_MODEL), const),
                  pl.BlockSpec((D_MODEL, LANES), const), pl.BlockSpec((1, LANES), const)],
        out_specs=(tile, tile, pl.BlockSpec((tm, LANES), row), pl.BlockSpec((1, LANES), const)),
        out_shape=(jax.ShapeDtypeStruct((T, D_MODEL), F32),
                   jax.ShapeDtypeStruct((T, D_MODEL), F32),
                   jax.ShapeDtypeStruct((T, LANES), F32),
                   jax.ShapeDtypeStruct((1, LANES), F32)),
        scratch_shapes=[pltpu.VMEM((1, LANES), F32)],
        compiler_params=_cparams(("arbitrary",)),
        name="merge",
    )(x, oa, ob, ga, gb, wa, wb, wo, nf, wr, br)


def _dispatch_kernel(ends_ref, dest_ref, h2_ref, xs_ref, zero_ref, sem):
    tm = h2_ref.shape[0]
    n_blocks = xs_ref.shape[0] // SLOT_BLOCK

    @pl.when(pl.program_id(0) == 0)
    def _():
        zero_ref[...] = jnp.zeros_like(zero_ref)

        def zero_block(blk):
            start = pl.multiple_of(blk * SLOT_BLOCK, SLOT_BLOCK)
            return pltpu.make_async_copy(zero_ref, xs_ref.at[pl.ds(start, SLOT_BLOCK)], sem)

        def expert_tail(e):
            return jnp.maximum((ends_ref[e] >> SLOT_SHIFT) - 1, 0)

        n_used = ends_ref[N_EXPERTS - 1] >> SLOT_SHIFT
        for e in range(N_EXPERTS):
            zero_block(expert_tail(e)).start()
        lax.fori_loop(n_used, n_blocks, lambda blk, c: (zero_block(blk).start(), c)[1], 0)
        for e in range(N_EXPERTS):
            zero_block(expert_tail(e)).wait()
        lax.fori_loop(n_used, n_blocks, lambda blk, c: (zero_block(blk).wait(), c)[1], 0)

    def row_copy(r, k):
        return pltpu.make_async_copy(h2_ref.at[pl.ds(r, 1)], xs_ref.at[pl.ds(dest_ref[k, r], 1)], sem)

    def start(i, c):
        base = pl.multiple_of(i * ROW_UNROLL, ROW_UNROLL)
        for u in range(ROW_UNROLL):
            row_copy(base + u, 0).start(priority=0)
            row_copy(base + u, 1).start(priority=1)
        return c

    def wait(i, c):
        base = pl.multiple_of(i * ROW_UNROLL, ROW_UNROLL)
        for u in range(ROW_UNROLL):
            row_copy(base + u, 0).wait()
            row_copy(base + u, 1).wait()
        return c

    lax.fori_loop(0, tm // ROW_UNROLL, start, 0)
    lax.fori_loop(0, tm // ROW_UNROLL, wait, 0)


def _dispatch(pad_ends, dest, h2, n_slots):
    T = h2.shape[0]
    tm = min(TOKEN_TILE, T)
    return pl.pallas_call(
        _dispatch_kernel,
        grid=(T // tm,),
        in_specs=[pl.BlockSpec(memory_space=pltpu.SMEM),
                  pl.BlockSpec((2, tm), lambda i: (0, i), memory_space=pltpu.SMEM),
                  pl.BlockSpec((tm, D_MODEL), lambda i: (i, 0))],
        out_specs=pl.BlockSpec(memory_space=pl.ANY),
        out_shape=jax.ShapeDtypeStruct((n_slots, D_MODEL), F32),
        scratch_shapes=[pltpu.VMEM((SLOT_BLOCK, D_MODEL), F32), pltpu.SemaphoreType.DMA(())],
        compiler_params=_cparams(("arbitrary",)),
        name="dispatch",
    )(pad_ends, dest, h2)


def _experts_kernel(be_ref, nu_ref, xs_ref, wg_ref, wu_ref, wd_ref, y_ref):
    del be_ref
    used = pl.program_id(0) < nu_ref[0]

    @pl.when(used)
    def _():
        x = xs_ref[...].astype(BF16)
        g = jnp.dot(x, wg_ref[0], preferred_element_type=F32)
        u = jnp.dot(x, wu_ref[0], preferred_element_type=F32)
        a = (g * jax.nn.sigmoid(g) * u).astype(BF16)
        y_ref[...] = jnp.dot(a, wd_ref[0], preferred_element_type=F32)

    @pl.when(jnp.logical_not(used))
    def _():
        y_ref[...] = jnp.zeros_like(y_ref)


def _experts(block_e, n_used, xs, wg, wu, wd):
    n_blocks = xs.shape[0] // SLOT_BLOCK
    last = lambda i, nu: jnp.minimum(i, jnp.maximum(nu[0] - 1, 0))
    blk = lambda i, be, nu: (last(i, nu), 0)
    wmap = lambda i, be, nu: (be[last(i, nu)], 0, 0)
    grid_spec = pltpu.PrefetchScalarGridSpec(
        num_scalar_prefetch=2,
        grid=(n_blocks,),
        in_specs=[pl.BlockSpec((SLOT_BLOCK, D_MODEL), blk),
                  pl.BlockSpec((1, D_MODEL, D_EXPERT), wmap),
                  pl.BlockSpec((1, D_MODEL, D_EXPERT), wmap),
                  pl.BlockSpec((1, D_EXPERT, D_MODEL), wmap)],
        out_specs=pl.BlockSpec((SLOT_BLOCK, D_MODEL), lambda i, be, nu: (i, 0)),
    )
    return pl.pallas_call(
        _experts_kernel,
        grid_spec=grid_spec,
        out_shape=jax.ShapeDtypeStruct((xs.shape[0], D_MODEL), F32),
        compiler_params=_cparams(("arbitrary",)),
        name="experts",
    )(block_e, n_used, xs, wg, wu, wd)


def _combine_kernel(dest_ref, dest_next_ref, x1_ref, route_ref, nf_ref, yb_ref, y_ref, g_ref, sem):
    i = pl.program_id(0)
    n = pl.num_programs(0)
    tm = x1_ref.shape[0]
    slot = i & 1

    def row_copy(d_ref, s, r, k):
        return pltpu.make_async_copy(yb_ref.at[pl.ds(d_ref[k, r], 1)], g_ref.at[s, k, pl.ds(r, 1)],
                                     sem.at[s])

    def gather(d_ref, s, wait):
        def rows(b, c):
            base = pl.multiple_of(b * ROW_UNROLL, ROW_UNROLL)
            for u in range(ROW_UNROLL):
                for k in range(2):
                    cp = row_copy(d_ref, s, base + u, k)
                    cp.wait() if wait else cp.start(priority=k)
            return c
        lax.fori_loop(0, tm // ROW_UNROLL, rows, 0)

    @pl.when(i == 0)
    def _():
        gather(dest_ref, slot, False)

    @pl.when(i + 1 < n)
    def _():
        gather(dest_next_ref, 1 - slot, False)

    gather(dest_ref, slot, True)
    route = route_ref[...]
    w1 = route[:, ROUTE_W1:ROUTE_W1 + 1]
    w2 = route[:, ROUTE_W2:ROUTE_W2 + 1]
    x2 = x1_ref[...] + (g_ref[slot, 0] * w1 + g_ref[slot, 1] * w2)
    y_ref[...] = _rms(x2, nf_ref[...])


def _combine(dest, x1, route, nfinal, yb):
    T = x1.shape[0]
    tm = min(TOKEN_TILE // 2, T)
    n = T // tm
    row = lambda i: (i, 0)
    return pl.pallas_call(
        _combine_kernel,
        grid=(n,),
        in_specs=[pl.BlockSpec((2, tm), lambda i: (0, i), memory_space=pltpu.SMEM),
                  pl.BlockSpec((2, tm), lambda i: (0, jnp.minimum(i + 1, n - 1)),
                               memory_space=pltpu.SMEM),
                  pl.BlockSpec((tm, D_MODEL), row), pl.BlockSpec((tm, LANES), row),
                  pl.BlockSpec((1, D_MODEL), lambda i: (0, 0)),
                  pl.BlockSpec(memory_space=pl.ANY)],
        out_specs=pl.BlockSpec((tm, D_MODEL), row),
        out_shape=jax.ShapeDtypeStruct((T, D_MODEL), F32),
        scratch_shapes=[pltpu.VMEM((2, 2, tm, D_MODEL), F32), pltpu.SemaphoreType.DMA((2,))],
        compiler_params=_cparams(("arbitrary",)),
        name="combine",
    )(dest, dest, x1, route, nfinal, yb)


def _moe(x1, h2, route, counts, wg, wu, wd, nfinal):
    T = x1.shape[0]
    n_slots = T * 2 + N_EXPERTS * SLOT_BLOCK
    n_blocks = n_slots // SLOT_BLOCK
    cnt = counts[0, :N_EXPERTS].astype(I32)
    padded = (cnt + SLOT_BLOCK - 1) // SLOT_BLOCK * SLOT_BLOCK
    pad_ends = jnp.cumsum(padded)
    pad_starts = pad_ends - padded
    e = route[:, ROUTE_E1:ROUTE_E2 + 1].astype(I32)
    rank = route[:, ROUTE_R1:ROUTE_R2 + 1].astype(I32)
    dest = (pad_starts[e] + rank).T
    block_start = jnp.arange(n_blocks, dtype=I32) * SLOT_BLOCK
    block_e = jnp.sum((block_start[:, None] >= pad_ends[None, :]).astype(I32), axis=1)
    block_e = jnp.minimum(block_e, N_EXPERTS - 1)
    n_used = (pad_ends[-1:] // SLOT_BLOCK).astype(I32)
    xs = _dispatch(pad_ends.astype(I32), dest, h2, n_slots)
    yb = _experts(block_e, n_used, xs, wg, wu, wd)
    return _combine(dest, x1, route, nfinal, yb)


def _layer(x, s0, k_past, v_past, p, B, L):
    row2 = lambda a: a.reshape(1, -1)
    qka, va, ra, qb, kb, vb, ga, gb, da, k5, v5 = _in_proj(x, row2(p["norm_mix"]), p["w_in"])
    oa, s_new = _gla(qka, va, ra, da, p["w_decay"], row2(p["b_decay"]), row2(p["gla_norm"]),
                     s0, B, L)
    lams = [row2(p[n]) for n in ("lambda_q1", "lambda_k1", "lambda_q2", "lambda_k2")]
    dn = row2(p["diff_norm"])
    if k_past is None:
        ob = _attn_prompt(qb, kb, vb, *lams, dn, B, L)
    else:
        ob = _attn_sample(qb, k_past, v_past, kb, vb, *lams, dn, B, L)
    x1, h2, route, counts = _merge(x, oa, ob, ga, gb, p["w_proj_a"], p["w_proj_b"], p["w_out"],
                                   row2(p["norm_ffn"]), p["w_router"], p["b_router"])
    y = _moe(x1, h2, route, counts, p["w_gate"], p["w_up"], p["w_down"], row2(p["norm_final"]))
    return y, s_new, k5, v5


def kernel(x_prompt, x_sample, cache_k, cache_v, state_gla, norm_mix, w_in, w_decay, b_decay,
           gla_norm, w_proj_a, lambda_q1, lambda_k1, lambda_q2, lambda_k2, diff_norm, w_proj_b,
           w_out, norm_ffn, w_router_group, b_router_group, w_router_expert, b_router_expert,
           w_gate, w_up, w_down, norm_final):
    B, L, D = x_prompt.shape
    Bs, Ls, _ = x_sample.shape
    w_router = jnp.concatenate(
        [w_router_expert[0], w_router_group[0],
         jnp.zeros((D, LANES - N_EXPERTS - N_GROUPS), F32)], axis=1)
    b_router = jnp.concatenate(
        [b_router_expert[0], b_router_group[0],
         jnp.zeros((LANES - N_EXPERTS - N_GROUPS,), F32)]).reshape(1, LANES)
    p = dict(
        norm_mix=norm_mix[0], w_in=_prep_w_in(w_in[0]),
        w_decay=jnp.pad(w_decay[0], ((0, LANES - GLA_RANK), (0, 0))), b_decay=b_decay[0],
        gla_norm=gla_norm[0], w_proj_a=w_proj_a[0].astype(BF16),
        lambda_q1=lambda_q1[0], lambda_k1=lambda_k1[0], lambda_q2=lambda_q2[0],
        lambda_k2=lambda_k2[0], diff_norm=diff_norm[0], w_proj_b=w_proj_b[0].astype(BF16),
        w_out=w_out[0].astype(BF16), norm_ffn=norm_ffn[0], w_router=w_router, b_router=b_router,
        w_gate=w_gate[0].astype(BF16), w_up=w_up[0].astype(BF16), w_down=w_down[0].astype(BF16),
        norm_final=norm_final)

    s0p = jnp.zeros((B, GLA_HEADS, GLA_DK, GLA_DV), F32)
    yp, sp, kp, vp = _layer(x_prompt.reshape(B * L, D), s0p, None, None, p, B, L)
    ys, ss, ks, vs = _layer(x_sample.reshape(Bs * Ls, D), state_gla[0], cache_k, cache_v,
                            p, Bs, Ls)
    kv = lambda a, b, l: a.reshape(1, b, l, DIFF_HEADS, DIFF_DV)
    return (yp.reshape(B, L, D), ys.reshape(Bs, Ls, D), kv(kp, B, L), kv(vp, B, L), sp[None],
            kv(ks, Bs, Ls), kv(vs, Bs, Ls), ss[None])
```

```python
import functools

import jax
import jax.numpy as jnp
from jax import lax
from jax.experimental import pallas as pl
from jax.experimental.pallas import tpu as pltpu

F32 = jnp.float32
BF16 = jnp.bfloat16
I32 = jnp.int32

D_MODEL = 1024
CHUNK = 64
CHUNK_SHIFT = 6
EPS = 1e-6
GLA_HEADS = 4
GLA_DK = 128
GLA_DV = 256
GLA_RANK = 16
GLA_TAU = 16.0
DIFF_HEADS = 8
DIFF_DH = 64
DIFF_DV = 128
N_GROUPS = 4
EXPERTS_PER_GROUP = 4
EPG_SHIFT = 2
N_EXPERTS = 16
D_EXPERT = 512
LAM_INIT = 0.8 - 0.6

LANES = 128
TOKEN_TILE = 512
SLOT_BLOCK = 256
SLOT_SHIFT = 8
ATT_TILE = 512
ATT_HEADS_PER_STEP = 2
ROW_UNROLL = 8
SAMPLE_KEY_CHUNK = 1024
VMEM_LIMIT = 56 * 1024 * 1024
NEG = -1e30

HI = lax.Precision.HIGHEST


def _cparams(sem):
    return pltpu.CompilerParams(dimension_semantics=sem, vmem_limit_bytes=VMEM_LIMIT)


def _nt(a, b):
    return lax.dot_general(a, b, (((1,), (1,)), ((), ())), preferred_element_type=F32)


def _tn(a, b):
    return lax.dot_general(a, b, (((0,), (0,)), ((), ())), preferred_element_type=F32)


def _rms(x, g):
    return x * lax.rsqrt(jnp.mean(x * x, axis=-1, keepdims=True) + EPS) * g


def _inproj_kernel(x_ref, g_ref, w_ref, qka_ref, va_ref, ra_ref, qb_ref, kb_ref, vb_ref,
                   ga_ref, gb_ref, da_ref, k5_ref, v5_ref, stage_ref, sem):
    i = pl.program_id(0)
    n = pl.num_programs(0)
    tm = x_ref.shape[0]
    slot = i & 1

    def head_copy(s, which, hd):
        dst = (k5_ref, v5_ref)[which]
        return pltpu.make_async_copy(stage_ref.at[s, which, hd],
                                     dst.at[pl.ds(i * tm, tm), hd, :], sem.at[s])

    def for_all(s, fn):
        for which in range(2):
            for hd in range(DIFF_HEADS):
                fn(head_copy(s, which, hd))

    @pl.when(i >= 2)
    def _():
        for_all(slot, lambda cp: cp.wait())

    h = _rms(x_ref[...], g_ref[...]).astype(BF16)
    outs = (qka_ref, va_ref, ra_ref, qb_ref, kb_ref, vb_ref, ga_ref, gb_ref)
    for c, o in enumerate(outs):
        z = jnp.dot(h, w_ref[:, c * D_MODEL:(c + 1) * D_MODEL], preferred_element_type=F32)
        o[...] = z.astype(o.dtype)
        which = 0 if o is kb_ref else 1 if o is vb_ref else None
        if which is not None:
            for hd in range(DIFF_HEADS):
                stage_ref[slot, which, hd] = z[:, hd * DIFF_DV:(hd + 1) * DIFF_DV]
    da_ref[...] = jnp.dot(h, w_ref[:, 8 * D_MODEL:8 * D_MODEL + LANES],
                          preferred_element_type=F32)
    for_all(slot, lambda cp: cp.start())

    @pl.when(i == n - 1)
    def _():
        for_all(slot, lambda cp: cp.wait())

    @pl.when((i == n - 1) & (i >= 1))
    def _():
        for_all(1 - slot, lambda cp: cp.wait())


def _prep_w_in(w_in):
    o = 0
    parts = {}
    for name, n in (("qa", 512), ("ka", 512), ("va", 1024), ("ra", 1024), ("da", GLA_RANK),
                    ("qb", 1024), ("kb", 1024), ("vb", 1024), ("ga", 1024), ("gb", 1024)):
        parts[name] = w_in[:, o:o + n]
        o += n
    pad = jnp.zeros((D_MODEL, LANES - GLA_RANK), w_in.dtype)
    w = jnp.concatenate([parts[k] for k in ("qa", "ka", "va", "ra", "qb", "kb", "vb", "ga", "gb",
                                            "da")] + [pad], axis=1)
    return w.astype(BF16)


def _in_proj(x, g, w):
    T = x.shape[0]
    tm = min(TOKEN_TILE, T)
    wcols = w.shape[1]
    row = lambda i: (i, 0)
    const = lambda i: (0, 0)
    wide = lambda dt: jax.ShapeDtypeStruct((T, D_MODEL), dt)
    heads = jax.ShapeDtypeStruct((T, DIFF_HEADS, DIFF_DV), F32)
    out_shape = (wide(BF16),) * 8 + (jax.ShapeDtypeStruct((T, LANES), F32), heads, heads)
    out_specs = tuple([pl.BlockSpec((tm, D_MODEL), row)] * 8 + [pl.BlockSpec((tm, LANES), row)]
                      + [pl.BlockSpec(memory_space=pl.ANY)] * 2)
    return pl.pallas_call(
        _inproj_kernel,
        grid=(T // tm,),
        in_specs=[pl.BlockSpec((tm, D_MODEL), row),
                  pl.BlockSpec((1, D_MODEL), const),
                  pl.BlockSpec((D_MODEL, wcols), const, pipeline_mode=pl.Buffered(1))],
        out_specs=out_specs,
        out_shape=out_shape,
        scratch_shapes=[pltpu.VMEM((2, 2, DIFF_HEADS, tm, DIFF_DV), F32),
                        pltpu.SemaphoreType.DMA((2,))],
        compiler_params=_cparams(("arbitrary",)),
        name="in_proj",
    )(x, g, w)


def _gla_kernel(qka_ref, va_ref, ra_ref, da_ref, wd_ref, bd_ref, gn_ref, s0_ref,
                oa_ref, sout_ref, s_ref, la_ref, *, chunk, n_chunks):
    l = pl.program_id(1)

    @pl.when(l == 0)
    def _():
        s_ref[...] = s0_ref[0]

    split = lambda a: (a.astype(BF16), (a - a.astype(BF16).astype(F32)).astype(BF16))
    n_k = GLA_HEADS * GLA_DK
    fold = lambda z: z[:, :n_k] + z[:, n_k:]

    wd_pair = jnp.concatenate(split(wd_ref[...]), axis=1)
    x = fold(sum(jnp.dot(a, wd_pair, preferred_element_type=F32) for a in split(da_ref[...])))
    x = x + bd_ref[...]
    log_a = (jnp.minimum(x, 0.0) - jnp.log1p(jnp.exp(-jnp.abs(x)))) * (1.0 / GLA_TAU)
    lb = log_a.shape[0]
    shift = chunk.bit_length() - 1
    r_b = lax.broadcasted_iota(I32, (lb, lb), 0)
    c_b = lax.broadcasted_iota(I32, (lb, lb), 1)
    tri = jnp.where((c_b <= r_b) & ((c_b >> shift) == (r_b >> shift)), 1.0, 0.0).astype(BF16)
    la_ref[...] = fold(jnp.dot(tri, jnp.concatenate(split(log_a), axis=1),
                               preferred_element_type=F32))

    r_i = lax.broadcasted_iota(I32, (chunk, chunk), 0)
    c_i = lax.broadcasted_iota(I32, (chunk, chunk), 1)
    causal = c_i <= r_i
    gn = gn_ref[...]
    scale = GLA_DK ** -0.5
    heads = range(GLA_HEADS)
    vcols = [slice(h * GLA_DV, (h + 1) * GLA_DV) for h in heads]

    @pl.loop(0, n_chunks)
    def _(c):
        rows = pl.ds(pl.multiple_of(c * chunk, chunk), chunk)
        b_all = la_ref[rows, :]
        q_t, k_t, k_end, decay = [], [], [], []
        for h in heads:
            b = b_all[:, h * GLA_DK:(h + 1) * GLA_DK]
            b_last = b[chunk - 1:chunk, :]
            q = qka_ref[rows, h * GLA_DK:(h + 1) * GLA_DK].astype(F32) * scale
            k = qka_ref[rows, n_k + h * GLA_DK:n_k + (h + 1) * GLA_DK].astype(F32)
            q_t.append((q * jnp.exp(b)).astype(BF16))
            k_t.append((k * jnp.exp(-b)).astype(BF16))
            k_end.append((k * jnp.exp(b_last - b)).astype(BF16))
            decay.append(jnp.broadcast_to(jnp.exp(b_last), (GLA_DK, GLA_DK)).T)
        s_old = [s_ref[h] for h in heads]
        o_state = [jnp.dot(q_t[h], s_old[h].astype(BF16), preferred_element_type=F32) for h in heads]
        att = [_nt(q_t[h], k_t[h]) for h in heads]
        kv = [_tn(k_end[h], va_ref[rows, vcols[h]]) for h in heads]
        for h in heads:
            a = jnp.where(causal, att[h], 0.0).astype(BF16)
            o = o_state[h] + jnp.dot(a, va_ref[rows, vcols[h]], preferred_element_type=F32)
            s_ref[h] = jnp.concatenate([decay[h], decay[h]], axis=1) * s_old[h] + kv[h]
            r = ra_ref[rows, vcols[h]].astype(F32)
            oa_ref[rows, vcols[h]] = (_rms(o, gn) * (r * jax.nn.sigmoid(r))).astype(oa_ref.dtype)

    @pl.when(l == pl.num_programs(1) - 1)
    def _():
        sout_ref[0] = s_ref[...]


def _gla(qka, va, ra, da, wd, bd, gn, s0, B, L):
    chunk = min(CHUNK, L)
    lb = min(TOKEN_TILE, L)
    nl = L // lb
    row = lambda b, l: (b * nl + l, 0)
    const2 = lambda b, l: (0, 0)
    st = lambda b, l: (b, 0, 0, 0)
    kern = functools.partial(_gla_kernel, chunk=chunk, n_chunks=lb // chunk)
    return pl.pallas_call(
        kern,
        grid=(B, nl),
        in_specs=[pl.BlockSpec((lb, D_MODEL), row), pl.BlockSpec((lb, D_MODEL), row),
                  pl.BlockSpec((lb, D_MODEL), row), pl.BlockSpec((lb, LANES), row),
                  pl.BlockSpec((LANES, GLA_HEADS * GLA_DK), const2),
                  pl.BlockSpec((1, GLA_HEADS * GLA_DK), const2),
                  pl.BlockSpec((1, GLA_DV), const2),
                  pl.BlockSpec((1, GLA_HEADS, GLA_DK, GLA_DV), st)],
        out_specs=(pl.BlockSpec((lb, D_MODEL), row),
                   pl.BlockSpec((1, GLA_HEADS, GLA_DK, GLA_DV), st)),
        out_shape=(jax.ShapeDtypeStruct((B * L, D_MODEL), BF16),
                   jax.ShapeDtypeStruct((B, GLA_HEADS, GLA_DK, GLA_DV), F32)),
        scratch_shapes=[pltpu.VMEM((GLA_HEADS, GLA_DK, GLA_DV), F32),
                        pltpu.VMEM((lb, GLA_HEADS * GLA_DK), F32)],
        compiler_params=_cparams(("arbitrary", "arbitrary")),
        name="gla",
    )(qka, va, ra, da, wd, bd, gn, s0)


def _lam(lq1, lk1, lq2, lk2):
    a = jnp.sum(lq1[...] * lk1[...], axis=-1, keepdims=True)
    b = jnp.sum(lq2[...] * lk2[...], axis=-1, keepdims=True)
    return jnp.exp(a) - jnp.exp(b) + LAM_INIT


def _split_q(q):
    lane = lax.broadcasted_iota(I32, q.shape, 1)
    qs = q * jnp.asarray(DIFF_DH ** -0.5, q.dtype)
    zero = jnp.zeros_like(qs)
    return jnp.where(lane < DIFF_DH, qs, zero), jnp.where(lane >= DIFF_DH, qs, zero)


def _alibi_slopes():
    return jnp.asarray([2.0 ** (-8.0 * (h + 1) / DIFF_HEADS) for h in range(DIFF_HEADS)], F32)


def _attn_prompt_kernel(slope_ref, q_ref, k_ref, v_ref, lq1, lk1, lq2, lk2, dn_ref, o_ref,
                        k1_ref, k2_ref, vt_ref, *, tile, nh):
    qi = pl.program_id(2)
    n_kv = vt_ref.shape[1]
    n_cols = tile // LANES
    lane_k = lax.broadcasted_iota(I32, (tile, DIFF_DV), 1)
    row_i = lax.broadcasted_iota(I32, (tile, DIFF_DV), 0)
    row_lo = (row_i & 255).astype(F32)
    row_hi = (row_i & -256).astype(F32)
    slopes = [slope_ref[pl.program_id(1) * nh + g] for g in range(nh)]
    hcols = [slice(g * DIFF_DV, (g + 1) * DIFF_DV) for g in range(nh)]

    @pl.when(qi == 0)
    def _():
        for g in range(nh):
            kfeat = jnp.where(lane_k == DIFF_DH, slopes[g] * row_lo,
                              jnp.where(lane_k == DIFF_DH + 1, slopes[g] * row_hi,
                                        jnp.where((lane_k == DIFF_DH + 2) | (lane_k == DIFF_DH + 3),
                                                  1.0, 0.0)))
            for c in range(n_kv):
                rows = slice(c * tile, (c + 1) * tile)
                kf = k_ref[rows, hcols[g]].astype(F32)
                k1_ref[g, rows, :] = jnp.where(lane_k < DIFF_DH, kf, kfeat).astype(BF16)
                k2_ref[g, rows, :] = jnp.where(lane_k < DIFF_DH, pltpu.roll(kf, DIFF_DH, 1),
                                               kfeat).astype(BF16)
                vt_ref[g, c] = v_ref[rows, hcols[g]].astype(F32).T.astype(BF16)

    qs = []
    for g in range(nh):
        qf = q_ref[:, hcols[g]].astype(F32) * (DIFF_DH ** -0.5)
        qfeat = jnp.where((lane_k == DIFF_DH) | (lane_k == DIFF_DH + 1), 1.0,
                          jnp.where(lane_k == DIFF_DH + 2, -slopes[g] * row_lo,
                                    jnp.where(lane_k == DIFF_DH + 3, -slopes[g] * row_hi, 0.0)))
        qs.append((jnp.where(lane_k < DIFF_DH, qf, qfeat).T.astype(BF16),
                   jnp.where(lane_k < DIFF_DH, pltpu.roll(qf, DIFF_DH, 1), qfeat).T.astype(BF16)))

    n_maps = 2 * nh
    n_stats = 2 * n_cols + 1

    def scores(i, j):
        rows = pl.ds(pl.multiple_of(j * tile, tile), tile)
        k_ref_i = k1_ref if i % 2 == 0 else k2_ref
        return jnp.dot(k_ref_i[i // 2, rows, :], qs[i // 2][i % 2], preferred_element_type=F32)

    def softmax_part(stats, s, c):
        out, alphas, ps = (), [], []
        for col in range(n_cols):
            sh = s[:, col * LANES:(col + 1) * LANES]
            m, l = stats[2 * col], stats[2 * col + 1]
            m_new = jnp.maximum(m, jnp.max(sh, axis=0, keepdims=True) + c)
            alpha = jnp.exp(m - m_new)
            p = jnp.exp(sh - (m_new - c))
            out += (m_new, alpha * l + jnp.sum(p, axis=0, keepdims=True))
            alphas.append(jnp.broadcast_to(alpha, (DIFF_DV, LANES)))
            ps.append(p.astype(BF16))
        return out, jnp.concatenate(alphas, axis=1), jnp.concatenate(ps, axis=1)

    def step(stats, j, cs, fixes):
        ss = [scores(i, j) for i in range(n_maps)]
        if fixes is not None:
            ss = [s + fixes[i // 2] for i, s in enumerate(ss)]
        parts = [softmax_part(stats[n_stats * i:n_stats * (i + 1)], ss[i], cs[i // 2])
                 for i in range(n_maps)]
        out = ()
        for i, (st, alpha, p) in enumerate(parts):
            a = alpha * stats[n_stats * i + n_stats - 1] + jnp.dot(
                vt_ref[i // 2, j], p, preferred_element_type=F32)
            out += st + (a,)
        return out

    def body(j, stats):
        off = jnp.full((1, LANES), (j - qi) * tile, I32).astype(F32)
        return step(stats, j, [off * slopes[g] for g in range(nh)], None)

    row = lambda v: jnp.full((1, LANES), v, F32)
    init = ((row(NEG), row(0.0)) * n_cols + (jnp.zeros((DIFF_DV, tile), F32),)) * n_maps
    stats = lax.fori_loop(0, qi, body, init)

    r_i = lax.broadcasted_iota(I32, (tile, tile), 0)
    c_i = lax.broadcasted_iota(I32, (tile, tile), 1)
    allowed = (r_i >> CHUNK_SHIFT) <= (c_i >> CHUNK_SHIFT)
    after = jnp.where(r_i > c_i, (c_i - r_i).astype(F32), 0.0)
    fixes = [jnp.where(allowed, (2.0 * slopes[g]) * after, NEG) for g in range(nh)]
    stats = step(stats, qi, [row(0.0)] * nh, fixes)
    lam = _lam(lq1, lk1, lq2, lk2)
    for g in range(nh):
        res = []
        for i in (2 * g, 2 * g + 1):
            st = stats[n_stats * i:n_stats * (i + 1)]
            l = jnp.concatenate([jnp.broadcast_to(st[2 * col + 1], (DIFF_DV, LANES))
                                 for col in range(n_cols)], axis=1)
            res.append(st[-1] / l)
        o = (res[0] - lam * res[1]).T
        o_ref[:, hcols[g]] = (_rms(o, dn_ref[...]) * (1.0 - LAM_INIT)).astype(o_ref.dtype)


def _attn_prompt(qb, kb, vb, lq1, lk1, lq2, lk2, dn, B, L):
    tile = min(ATT_TILE, L)
    nq = L // tile
    nh = ATT_HEADS_PER_STEP
    qmap = lambda b, h, i: (b * nq + i, h)
    kvmap = lambda b, h, i: (b, h)
    cmap = lambda b, h, i: (0, 0)
    lspec = pl.BlockSpec((1, DIFF_DH), cmap)
    return pl.pallas_call(
        functools.partial(_attn_prompt_kernel, tile=tile, nh=nh),
        grid=(B, DIFF_HEADS // nh, nq),
        in_specs=[pl.BlockSpec(memory_space=pltpu.SMEM),
                  pl.BlockSpec((tile, nh * DIFF_DV), qmap),
                  pl.BlockSpec((L, nh * DIFF_DV), kvmap), pl.BlockSpec((L, nh * DIFF_DV), kvmap),
                  lspec, lspec, lspec, lspec, pl.BlockSpec((1, DIFF_DV), cmap)],
        out_specs=pl.BlockSpec((tile, nh * DIFF_DV), qmap),
        out_shape=jax.ShapeDtypeStruct((B * L, D_MODEL), BF16),
        scratch_shapes=[pltpu.VMEM((nh, L, DIFF_DV), BF16), pltpu.VMEM((nh, L, DIFF_DV), BF16),
                        pltpu.VMEM((nh, nq, DIFF_DV, tile), BF16)],
        compiler_params=_cparams(("arbitrary", "arbitrary", "arbitrary")),
        name="attn_prompt",
    )(_alibi_slopes(), qb, kb, vb, lq1, lk1, lq2, lk2, dn)


def _attn_sample_kernel(slope_ref, q_ref, kp_ref, vp_ref, kn_ref, vn_ref, lq1, lk1, lq2, lk2,
                        dn_ref, o_ref, m_ref, l_ref, acc_ref, kbuf, vbuf, sem, *, past, lq, chunk):
    b = pl.program_id(0)
    c = pl.program_id(1)
    n_c = pl.num_programs(1)
    step = b * n_c + c
    slot = step & 1

    def slab_copies(bb, cc, s, fn):
        rows = pl.ds(cc * chunk, chunk)
        for hd in range(DIFF_HEADS):
            fn(pltpu.make_async_copy(kp_ref.at[0, bb, rows, hd, :], kbuf.at[s, hd], sem.at[s]))
            fn(pltpu.make_async_copy(vp_ref.at[0, bb, rows, hd, :], vbuf.at[s, hd], sem.at[s]))

    @pl.when(step == 0)
    def _():
        slab_copies(b, c, slot, lambda cp: cp.start())

    @pl.when(step + 1 < pl.num_programs(0) * n_c)
    def _():
        wrap = c + 1 == n_c
        slab_copies(jnp.where(wrap, b + 1, b), jnp.where(wrap, 0, c + 1), 1 - slot,
                    lambda cp: cp.start())

    slab_copies(b, c, slot, lambda cp: cp.wait())

    @pl.when(c == 0)
    def _():
        m_ref[...] = jnp.full_like(m_ref, NEG)
        l_ref[...] = jnp.zeros_like(l_ref)
        acc_ref[...] = jnp.zeros_like(acc_ref)

    def geometry(key0, n_keys):
        r = lax.broadcasted_iota(I32, (2 * lq, n_keys), 0)
        qpos = past + jnp.where(r >= lq, r - lq, r)
        kpos = key0 + lax.broadcasted_iota(I32, (2 * lq, n_keys), 1)
        allowed = (kpos >> CHUNK_SHIFT) <= (qpos >> CHUNK_SHIFT)
        return jnp.abs(qpos - kpos).astype(F32), jnp.where(allowed, 0.0, NEG)

    def block(hd, k, v, geo):
        dist, mask = geo
        q1, q2 = _split_q(q_ref[:, hd * DIFF_DV:(hd + 1) * DIFF_DV])
        s = _nt(jnp.concatenate([q1, q2], axis=0), k)
        s = s + (mask - slope_ref[hd] * dist)
        m_old = m_ref[hd]
        m_new = jnp.maximum(m_old, jnp.max(s, axis=-1, keepdims=True))
        alpha = jnp.exp(m_old - m_new)
        p = jnp.exp(s - m_new)
        l_ref[hd] = alpha * l_ref[hd] + jnp.sum(p, axis=-1, keepdims=True)
        acc_ref[hd] = alpha * acc_ref[hd] + jnp.dot(p.astype(BF16), v, preferred_element_type=F32)
        m_ref[hd] = m_new

    geo = geometry(c * chunk, chunk)
    for hd in range(DIFF_HEADS):
        block(hd, kbuf[slot, hd].astype(BF16), vbuf[slot, hd].astype(BF16), geo)

    @pl.when(c == n_c - 1)
    def _():
        lam = _lam(lq1, lk1, lq2, lk2)
        geo_new = geometry(past, lq)
        for hd in range(DIFF_HEADS):
            cols = slice(hd * DIFF_DV, (hd + 1) * DIFF_DV)
            block(hd, kn_ref[:, cols], vn_ref[:, cols], geo_new)
            o = acc_ref[hd] / l_ref[hd]
            o = o[:lq] - lam * o[lq:]
            o_ref[:, hd * DIFF_DV:(hd + 1) * DIFF_DV] = (
                _rms(o, dn_ref[...]) * (1.0 - LAM_INIT)).astype(o_ref.dtype)


def _attn_sample(qb, cache_k, cache_v, kb, vb, lq1, lk1, lq2, lk2, dn, B, L):
    past = cache_k.shape[2]
    chunk = min(SAMPLE_KEY_CHUNK, past)
    n_chunks = past // chunk
    bmap = lambda b, c: (b, 0)
    cmap = lambda b, c: (0, 0)
    lspec = pl.BlockSpec((1, DIFF_DH), cmap)
    tok = pl.BlockSpec((L, D_MODEL), bmap)
    cache = pl.BlockSpec(memory_space=pl.ANY)
    slabs = pltpu.VMEM((2, DIFF_HEADS, chunk, DIFF_DV), F32)
    return pl.pallas_call(
        functools.partial(_attn_sample_kernel, past=past, lq=L, chunk=chunk),
        grid=(B, n_chunks),
        in_specs=[pl.BlockSpec(memory_space=pltpu.SMEM), tok, cache, cache, tok, tok,
                  lspec, lspec, lspec, lspec, pl.BlockSpec((1, DIFF_DV), cmap)],
        out_specs=tok,
        out_shape=jax.ShapeDtypeStruct((B * L, D_MODEL), BF16),
        scratch_shapes=[pltpu.VMEM((DIFF_HEADS, 2 * L, 1), F32),
                        pltpu.VMEM((DIFF_HEADS, 2 * L, 1), F32),
                        pltpu.VMEM((DIFF_HEADS, 2 * L, DIFF_DV), F32),
                        slabs, slabs, pltpu.SemaphoreType.DMA((2,))],
        compiler_params=_cparams(("arbitrary", "arbitrary")),
        name="attn_sample",
    )(_alibi_slopes(), qb, cache_k, cache_v, kb, vb, lq1, lk1, lq2, lk2, dn)


ROUTE_E1, ROUTE_E2, ROUTE_W1, ROUTE_W2, ROUTE_R1, ROUTE_R2 = range(6)
ROUTE_ROWS = 8
GROUP_LANE0 = N_EXPERTS


def _merge_kernel(x_ref, oa_ref, ob_ref, ga_ref, gb_ref, wa_ref, wb_ref, wo_ref, nf_ref,
                  wr_ref, br_ref, x1_ref, h2_ref, route_ref, route_t_ref, cnt_ref, run_ref):
    i = pl.program_id(0)

    @pl.when(i == 0)
    def _():
        run_ref[...] = jnp.zeros_like(run_ref)

    u_a = jnp.dot(oa_ref[...], wa_ref[...], preferred_element_type=F32)
    u_b = jnp.dot(ob_ref[...], wb_ref[...], preferred_element_type=F32)
    mix = (jax.nn.sigmoid(ga_ref[...].astype(F32)) * u_a
           + jax.nn.sigmoid(gb_ref[...].astype(F32)) * u_b)
    x1 = x_ref[...] + jnp.dot(mix.astype(BF16), wo_ref[...], preferred_element_type=F32)
    x1_ref[...] = x1
    h2 = _rms(x1, nf_ref[...])
    h2_ref[...] = h2

    tm = h2.shape[0]
    split = lambda a: (a.astype(BF16), (a - a.astype(BF16).astype(F32)).astype(BF16))
    w_pair = jnp.concatenate(split(wr_ref[...]), axis=1)
    parts = sum(jnp.dot(a, w_pair, preferred_element_type=F32) for a in split(h2))
    logits = parts[:, :LANES] + parts[:, LANES:] + br_ref[...]
    lane = lax.broadcasted_iota(I32, (tm, LANES), 1)
    big = jnp.int32(LANES)
    g_mask = (lane >= GROUP_LANE0) & (lane < GROUP_LANE0 + N_GROUPS)
    gl = jnp.where(g_mask, logits, -jnp.inf)
    gmax = jnp.max(gl, axis=-1, keepdims=True)
    g_sel = jnp.min(jnp.where(gl == gmax, lane - GROUP_LANE0, big), axis=-1, keepdims=True)
    p_g = 1.0 / jnp.sum(jnp.where(g_mask, jnp.exp(logits - gmax), 0.0), axis=-1, keepdims=True)
    e_mask = (lane < N_EXPERTS) & ((lane >> EPG_SHIFT) == g_sel)
    el = jnp.where(e_mask, logits, -jnp.inf)
    v1 = jnp.max(el, axis=-1, keepdims=True)
    i1 = jnp.min(jnp.where(el == v1, lane, big), axis=-1, keepdims=True)
    el2 = jnp.where(lane == i1, -jnp.inf, el)
    v2 = jnp.max(el2, axis=-1, keepdims=True)
    i2 = jnp.min(jnp.where(el2 == v2, lane, big), axis=-1, keepdims=True)
    t = jnp.exp(v2 - v1)
    w1 = p_g / (1.0 + t)
    w2 = p_g * t / (1.0 + t)

    oh1 = lane == i1
    oh2 = lane == i2
    cnt = jnp.where(oh1, 1.0, 0.0) + jnp.where(oh2, 1.0, 0.0)
    r_i = lax.broadcasted_iota(I32, (tm, tm), 0)
    c_i = lax.broadcasted_iota(I32, (tm, tm), 1)
    before = jnp.where(c_i < r_i, 1.0, 0.0).astype(BF16)
    prior = jnp.dot(before, cnt.astype(BF16), preferred_element_type=F32) + run_ref[...]
    rank1 = jnp.sum(jnp.where(oh1, prior, 0.0), axis=-1, keepdims=True)
    rank2 = jnp.sum(jnp.where(oh2, prior, 0.0), axis=-1, keepdims=True)
    run = run_ref[...] + jnp.sum(cnt, axis=0, keepdims=True)
    run_ref[...] = run
    cnt_ref[...] = run

    route = jnp.zeros((tm, LANES), F32)
    for pos, val in ((ROUTE_E1, i1.astype(F32)), (ROUTE_E2, i2.astype(F32)), (ROUTE_W1, w1),
                     (ROUTE_W2, w2), (ROUTE_R1, rank1), (ROUTE_R2, rank2)):
        route = jnp.where(lane == pos, val, route)
    route_ref[...] = route
    route_t_ref[...] = route.T[:ROUTE_ROWS, :]


def _merge(x, oa, ob, ga, gb, wa, wb, wo, nf, wr, br):
    T = x.shape[0]
    tm = min(TOKEN_TILE, T)
    row = lambda i: (i, 0)
    const = lambda i: (0, 0)
    wspec = pl.BlockSpec((D_MODEL, D_MODEL), const)
    tile = pl.BlockSpec((tm, D_MODEL), row)
    return pl.pallas_call(
        _merge_kernel,
        grid=(T // tm,),
        in_specs=[tile, tile, tile, tile, tile, wspec, wspec, wspec,
                  pl.BlockSpec((1, D_MODEL), const),
                  pl.BlockSpec((D_MODEL, LANES), const), pl.BlockSpec((1, LANES), const)],
        out_specs=(tile, tile, pl.BlockSpec((tm, LANES), row),
                   pl.BlockSpec((ROUTE_ROWS, tm), lambda i: (0, i)), pl.BlockSpec((1, LANES), const)),
        out_shape=(jax.ShapeDtypeStruct((T, D_MODEL), F32),
                   jax.ShapeDtypeStruct((T, D_MODEL), F32),
                   jax.ShapeDtypeStruct((T, LANES), F32),
                   jax.ShapeDtypeStruct((ROUTE_ROWS, T), F32),
                   jax.ShapeDtypeStruct((1, LANES), F32)),
        scratch_shapes=[pltpu.VMEM((1, LANES), F32)],
        compiler_params=_cparams(("arbitrary",)),
        name="merge",
    )(x, oa, ob, ga, gb, wa, wb, wo, nf, wr, br)


def _dispatch_kernel(ends_ref, dest_ref, h2_ref, xs_ref, zero_ref, sem):
    tm = h2_ref.shape[0]
    n_blocks = xs_ref.shape[0] // SLOT_BLOCK

    @pl.when(pl.program_id(0) == 0)
    def _():
        zero_ref[...] = jnp.zeros_like(zero_ref)

        def zero_block(blk):
            start = pl.multiple_of(blk * SLOT_BLOCK, SLOT_BLOCK)
            return pltpu.make_async_copy(zero_ref, xs_ref.at[pl.ds(start, SLOT_BLOCK)], sem)

        def expert_tail(e):
            return jnp.maximum((ends_ref[e] >> SLOT_SHIFT) - 1, 0)

        n_used = ends_ref[N_EXPERTS - 1] >> SLOT_SHIFT
        for e in range(N_EXPERTS):
            zero_block(expert_tail(e)).start()
        lax.fori_loop(n_used, n_blocks, lambda blk, c: (zero_block(blk).start(), c)[1], 0)
        for e in range(N_EXPERTS):
            zero_block(expert_tail(e)).wait()
        lax.fori_loop(n_used, n_blocks, lambda blk, c: (zero_block(blk).wait(), c)[1], 0)

    def row_copy(r, k):
        return pltpu.make_async_copy(h2_ref.at[pl.ds(r, 1)], xs_ref.at[pl.ds(dest_ref[k, r], 1)], sem)

    def start(i, c):
        base = pl.multiple_of(i * ROW_UNROLL, ROW_UNROLL)
        for u in range(ROW_UNROLL):
            row_copy(base + u, 0).start(priority=0)
            row_copy(base + u, 1).start(priority=1)
        return c

    def wait(i, c):
        base = pl.multiple_of(i * ROW_UNROLL, ROW_UNROLL)
        for u in range(ROW_UNROLL):
            row_copy(base + u, 0).wait()
            row_copy(base + u, 1).wait()
        return c

    lax.fori_loop(0, tm // ROW_UNROLL, start, 0)
    lax.fori_loop(0, tm // ROW_UNROLL, wait, 0)


def _dispatch(pad_ends, dest, h2, n_slots):
    T = h2.shape[0]
    tm = min(TOKEN_TILE, T)
    return pl.pallas_call(
        _dispatch_kernel,
        grid=(T // tm,),
        in_specs=[pl.BlockSpec(memory_space=pltpu.SMEM),
                  pl.BlockSpec((2, tm), lambda i: (0, i), memory_space=pltpu.SMEM),
                  pl.BlockSpec((tm, D_MODEL), lambda i: (i, 0))],
        out_specs=pl.BlockSpec(memory_space=pl.ANY),
        out_shape=jax.ShapeDtypeStruct((n_slots, D_MODEL), F32),
        scratch_shapes=[pltpu.VMEM((SLOT_BLOCK, D_MODEL), F32), pltpu.SemaphoreType.DMA(())],
        compiler_params=_cparams(("arbitrary",)),
        name="dispatch",
    )(pad_ends, dest, h2)


def _experts_kernel(be_ref, nu_ref, xs_ref, wg_ref, wu_ref, wd_ref, y_ref):
    del be_ref
    used = pl.program_id(0) < nu_ref[0]

    @pl.when(used)
    def _():
        x = xs_ref[...].astype(BF16)
        g = jnp.dot(x, wg_ref[0].astype(BF16), preferred_element_type=F32)
        u = jnp.dot(x, wu_ref[0].astype(BF16), preferred_element_type=F32)
        a = (g * jax.nn.sigmoid(g) * u).astype(BF16)
        y_ref[...] = jnp.dot(a, wd_ref[0].astype(BF16), preferred_element_type=F32)

    @pl.when(jnp.logical_not(used))
    def _():
        y_ref[...] = jnp.zeros_like(y_ref)


def _experts(block_e, n_used, xs, wg, wu, wd):
    n_blocks = xs.shape[0] // SLOT_BLOCK
    last = lambda i, nu: jnp.minimum(i, jnp.maximum(nu[0] - 1, 0))
    blk = lambda i, be, nu: (last(i, nu), 0)
    wmap = lambda i, be, nu: (be[last(i, nu)], 0, 0)
    grid_spec = pltpu.PrefetchScalarGridSpec(
        num_scalar_prefetch=2,
        grid=(n_blocks,),
        in_specs=[pl.BlockSpec((SLOT_BLOCK, D_MODEL), blk),
                  pl.BlockSpec((1, D_MODEL, D_EXPERT), wmap),
                  pl.BlockSpec((1, D_MODEL, D_EXPERT), wmap),
                  pl.BlockSpec((1, D_EXPERT, D_MODEL), wmap)],
        out_specs=pl.BlockSpec((SLOT_BLOCK, D_MODEL), lambda i, be, nu: (i, 0)),
    )
    return pl.pallas_call(
        _experts_kernel,
        grid_spec=grid_spec,
        out_shape=jax.ShapeDtypeStruct((xs.shape[0], D_MODEL), F32),
        compiler_params=_cparams(("arbitrary",)),
        name="experts",
    )(block_e, n_used, xs, wg, wu, wd)


def _combine_kernel(dest_ref, dest_next_ref, x1_ref, route_ref, nf_ref, yb_ref, y_ref, g_ref, sem):
    i = pl.program_id(0)
    n = pl.num_programs(0)
    tm = x1_ref.shape[0]
    slot = i & 1

    def row_copy(d_ref, s, r, k):
        return pltpu.make_async_copy(yb_ref.at[pl.ds(d_ref[k, r], 1)], g_ref.at[s, k, pl.ds(r, 1)],
                                     sem.at[s])

    def gather(d_ref, s, wait):
        def rows(b, c):
            base = pl.multiple_of(b * ROW_UNROLL, ROW_UNROLL)
            for u in range(ROW_UNROLL):
                for k in range(2):
                    cp = row_copy(d_ref, s, base + u, k)
                    cp.wait() if wait else cp.start(priority=k)
            return c
        lax.fori_loop(0, tm // ROW_UNROLL, rows, 0)

    @pl.when(i == 0)
    def _():
        gather(dest_ref, slot, False)

    @pl.when(i + 1 < n)
    def _():
        gather(dest_next_ref, 1 - slot, False)

    gather(dest_ref, slot, True)
    route = route_ref[...]
    w1 = route[:, ROUTE_W1:ROUTE_W1 + 1]
    w2 = route[:, ROUTE_W2:ROUTE_W2 + 1]
    x2 = x1_ref[...] + (g_ref[slot, 0] * w1 + g_ref[slot, 1] * w2)
    y_ref[...] = _rms(x2, nf_ref[...])


def _combine(dest, x1, route, nfinal, yb):
    T = x1.shape[0]
    tm = min(TOKEN_TILE // 2, T)
    n = T // tm
    row = lambda i: (i, 0)
    return pl.pallas_call(
        _combine_kernel,
        grid=(n,),
        in_specs=[pl.BlockSpec((2, tm), lambda i: (0, i), memory_space=pltpu.SMEM),
                  pl.BlockSpec((2, tm), lambda i: (0, jnp.minimum(i + 1, n - 1)),
                               memory_space=pltpu.SMEM),
                  pl.BlockSpec((tm, D_MODEL), row), pl.BlockSpec((tm, LANES), row),
                  pl.BlockSpec((1, D_MODEL), lambda i: (0, 0)),
                  pl.BlockSpec(memory_space=pl.ANY)],
        out_specs=pl.BlockSpec((tm, D_MODEL), row),
        out_shape=jax.ShapeDtypeStruct((T, D_MODEL), F32),
        scratch_shapes=[pltpu.VMEM((2, 2, tm, D_MODEL), F32), pltpu.SemaphoreType.DMA((2,))],
        compiler_params=_cparams(("arbitrary",)),
        name="combine",
    )(dest, dest, x1, route, nfinal, yb)


def _moe(x1, h2, route, route_t, counts, wg, wu, wd, nfinal):
    T = x1.shape[0]
    n_slots = T * 2 + N_EXPERTS * SLOT_BLOCK
    n_blocks = n_slots // SLOT_BLOCK
    cnt = counts[0, :N_EXPERTS].astype(I32)
    padded = (cnt + SLOT_BLOCK - 1) // SLOT_BLOCK * SLOT_BLOCK
    pad_ends = jnp.cumsum(padded)
    pad_starts = pad_ends - padded
    e = route_t[ROUTE_E1:ROUTE_E2 + 1].astype(I32)
    rank = route_t[ROUTE_R1:ROUTE_R2 + 1].astype(I32)
    start = sum(jnp.where(e == k, pad_starts[k], 0) for k in range(N_EXPERTS))
    dest = start + rank
    block_start = jnp.arange(n_blocks, dtype=I32) * SLOT_BLOCK
    block_e = jnp.sum((block_start[:, None] >= pad_ends[None, :]).astype(I32), axis=1)
    block_e = jnp.minimum(block_e, N_EXPERTS - 1)
    n_used = (pad_ends[-1:] // SLOT_BLOCK).astype(I32)
    xs = _dispatch(pad_ends.astype(I32), dest, h2, n_slots)
    yb = _experts(block_e, n_used, xs, wg, wu, wd)
    return _combine(dest, x1, route, nfinal, yb)


def _layer(x, s0, k_past, v_past, p, B, L):
    row2 = lambda a: a.reshape(1, -1)
    qka, va, ra, qb, kb, vb, ga, gb, da, k5, v5 = _in_proj(x, row2(p["norm_mix"]), p["w_in"])
    oa, s_new = _gla(qka, va, ra, da, p["w_decay"], row2(p["b_decay"]), row2(p["gla_norm"]),
                     s0, B, L)
    lams = [row2(p[n]) for n in ("lambda_q1", "lambda_k1", "lambda_q2", "lambda_k2")]
    dn = row2(p["diff_norm"])
    if k_past is None:
        ob = _attn_prompt(qb, kb, vb, *lams, dn, B, L)
    else:
        ob = _attn_sample(qb, k_past, v_past, kb, vb, *lams, dn, B, L)
    x1, h2, route, route_t, counts = _merge(x, oa, ob, ga, gb, p["w_proj_a"], p["w_proj_b"],
                                            p["w_out"], row2(p["norm_ffn"]), p["w_router"],
                                            p["b_router"])
    y = _moe(x1, h2, route, route_t, counts, p["w_gate"], p["w_up"], p["w_down"],
             row2(p["norm_final"]))
    return y, s_new, k5, v5


def kernel(x_prompt, x_sample, cache_k, cache_v, state_gla, norm_mix, w_in, w_decay, b_decay,
           gla_norm, w_proj_a, lambda_q1, lambda_k1, lambda_q2, lambda_k2, diff_norm, w_proj_b,
           w_out, norm_ffn, w_router_group, b_router_group, w_router_expert, b_router_expert,
           w_gate, w_up, w_down, norm_final):
    B, L, D = x_prompt.shape
    Bs, Ls, _ = x_sample.shape
    w_router = jnp.concatenate(
        [w_router_expert[0], w_router_group[0],
         jnp.zeros((D, LANES - N_EXPERTS - N_GROUPS), F32)], axis=1)
    b_router = jnp.concatenate(
        [b_router_expert[0], b_router_group[0],
         jnp.zeros((LANES - N_EXPERTS - N_GROUPS,), F32)]).reshape(1, LANES)
    p = dict(
        norm_mix=norm_mix[0], w_in=_prep_w_in(w_in[0]),
        w_decay=jnp.pad(w_decay[0], ((0, LANES - GLA_RANK), (0, 0))), b_decay=b_decay[0],
        gla_norm=gla_norm[0], w_proj_a=w_proj_a[0].astype(BF16),
        lambda_q1=lambda_q1[0], lambda_k1=lambda_k1[0], lambda_q2=lambda_q2[0],
        lambda_k2=lambda_k2[0], diff_norm=diff_norm[0], w_proj_b=w_proj_b[0].astype(BF16),
        w_out=w_out[0].astype(BF16), norm_ffn=norm_ffn[0], w_router=w_router, b_router=b_router,
        w_gate=w_gate[0], w_up=w_up[0], w_down=w_down[0],
        norm_final=norm_final)

    s0p = jnp.zeros((B, GLA_HEADS, GLA_DK, GLA_DV), F32)
    yp, sp, kp, vp = _layer(x_prompt.reshape(B * L, D), s0p, None, None, p, B, L)
    ys, ss, ks, vs = _layer(x_sample.reshape(Bs * Ls, D), state_gla[0], cache_k, cache_v,
                            p, Bs, Ls)
    kv = lambda a, b, l: a.reshape(1, b, l, DIFF_HEADS, DIFF_DV)
    return (yp.reshape(B, L, D), ys.reshape(Bs, Ls, D), kv(kp, B, L), kv(vp, B, L), sp[None],
            kv(ks, Bs, Ls), kv(vs, Bs, Ls), ss[None])
```

```python
import functools

import jax
import jax.numpy as jnp
from jax import lax
from jax.experimental import pallas as pl
from jax.experimental.pallas import tpu as pltpu

F32 = jnp.float32
BF16 = jnp.bfloat16
I32 = jnp.int32

D_MODEL = 1024
CHUNK = 64
CHUNK_SHIFT = 6
EPS = 1e-6
GLA_HEADS = 4
GLA_DK = 128
GLA_DV = 256
GLA_RANK = 16
GLA_TAU = 16.0
DIFF_HEADS = 8
DIFF_DH = 64
DIFF_DV = 128
N_GROUPS = 4
EXPERTS_PER_GROUP = 4
EPG_SHIFT = 2
N_EXPERTS = 16
D_EXPERT = 512
LAM_INIT = 0.8 - 0.6

LANES = 128
TOKEN_TILE = 512
SLOT_BLOCK = 256
SLOT_SHIFT = 8
ATT_TILE = 512
ATT_HEADS_PER_STEP = 2
ROW_UNROLL = 8
SAMPLE_KEY_CHUNK = 1024
GLA_CUMSUM_ROWS = 256
VMEM_LIMIT = 56 * 1024 * 1024
NEG = -1e30

HI = lax.Precision.HIGHEST


def _cparams(sem):
    return pltpu.CompilerParams(dimension_semantics=sem, vmem_limit_bytes=VMEM_LIMIT)


def _nt(a, b):
    return lax.dot_general(a, b, (((1,), (1,)), ((), ())), preferred_element_type=F32)


def _tn(a, b):
    return lax.dot_general(a, b, (((0,), (0,)), ((), ())), preferred_element_type=F32)


def _rms(x, g):
    return x * lax.rsqrt(jnp.mean(x * x, axis=-1, keepdims=True) + EPS) * g


def _inproj_kernel(x_ref, g_ref, wa_ref, wda_ref, wb_ref, qka_ref, va_ref, ra_ref, qb_ref, kb_ref,
                   vb_ref, ga_ref, gb_ref, da_ref, k5_ref, v5_ref, stage_ref, sem):
    i = pl.program_id(0)
    n = pl.num_programs(0)
    tm = x_ref.shape[0]
    slot = i & 1

    def head_copy(s, which, hd):
        dst = (k5_ref, v5_ref)[which]
        return pltpu.make_async_copy(stage_ref.at[s, which, hd],
                                     dst.at[pl.ds(i * tm, tm), hd, :], sem.at[s])

    def for_all(s, fn):
        for which in range(2):
            for hd in range(DIFF_HEADS):
                fn(head_copy(s, which, hd))

    @pl.when(i >= 2)
    def _():
        for_all(slot, lambda cp: cp.wait())

    h = _rms(x_ref[...], g_ref[...]).astype(BF16)
    outs = ((qka_ref, wa_ref, 0), (va_ref, wa_ref, 1), (ra_ref, wa_ref, 2), (qb_ref, wb_ref, 0),
            (kb_ref, wb_ref, 1), (vb_ref, wb_ref, 2), (ga_ref, wb_ref, 3), (gb_ref, wb_ref, 4))
    for o, w_ref, c in outs:
        z = jnp.dot(h, w_ref[:, c * D_MODEL:(c + 1) * D_MODEL], preferred_element_type=F32)
        o[...] = z.astype(o.dtype)
        which = 0 if o is kb_ref else 1 if o is vb_ref else None
        if which is not None:
            for hd in range(DIFF_HEADS):
                stage_ref[slot, which, hd] = z[:, hd * DIFF_DV:(hd + 1) * DIFF_DV]
    da_ref[...] = jnp.dot(h, wda_ref[...], preferred_element_type=F32)
    for_all(slot, lambda cp: cp.start())

    @pl.when(i == n - 1)
    def _():
        for_all(slot, lambda cp: cp.wait())

    @pl.when((i == n - 1) & (i >= 1))
    def _():
        for_all(1 - slot, lambda cp: cp.wait())


N_GLA_COLS = 3 * D_MODEL
N_DIFF_COLS = 5 * D_MODEL


def _prep_w_in(w_in):
    w = w_in.astype(BF16)
    return w, w[:, N_GLA_COLS + GLA_RANK:]


def _in_proj(x, g, w):
    w_all, w_diff = w
    T = x.shape[0]
    tm = min(TOKEN_TILE, T)
    resident = lambda cols, j: pl.BlockSpec((D_MODEL, cols), lambda i: (0, j),
                                            pipeline_mode=pl.Buffered(1))
    row = lambda i: (i, 0)
    const = lambda i: (0, 0)
    wide = lambda dt: jax.ShapeDtypeStruct((T, D_MODEL), dt)
    heads = jax.ShapeDtypeStruct((T, DIFF_HEADS, DIFF_DV), F32)
    out_shape = (wide(BF16),) * 8 + (jax.ShapeDtypeStruct((T, LANES), F32), heads, heads)
    out_specs = tuple([pl.BlockSpec((tm, D_MODEL), row)] * 8 + [pl.BlockSpec((tm, LANES), row)]
                      + [pl.BlockSpec(memory_space=pl.ANY)] * 2)
    return pl.pallas_call(
        _inproj_kernel,
        grid=(T // tm,),
        in_specs=[pl.BlockSpec((tm, D_MODEL), row),
                  pl.BlockSpec((1, D_MODEL), const),
                  resident(N_GLA_COLS, 0), resident(LANES, N_GLA_COLS // LANES),
                  resident(N_DIFF_COLS, 0)],
        out_specs=out_specs,
        out_shape=out_shape,
        scratch_shapes=[pltpu.VMEM((2, 2, DIFF_HEADS, tm, DIFF_DV), F32),
                        pltpu.SemaphoreType.DMA((2,))],
        compiler_params=_cparams(("arbitrary",)),
        name="in_proj",
    )(x, g, w_all, w_all, w_diff)


def _gla_kernel(qka_ref, va_ref, ra_ref, da_ref, wd_ref, bd_ref, gn_ref, s0_ref,
                oa_ref, sout_ref, s_ref, la_ref, *, chunk, n_chunks):
    l = pl.program_id(1)

    @pl.when(l == 0)
    def _():
        s_ref[...] = s0_ref[0]

    split = lambda a: (a.astype(BF16), (a - a.astype(BF16).astype(F32)).astype(BF16))
    n_k = GLA_HEADS * GLA_DK
    fold = lambda z: z[:, :n_k] + z[:, n_k:]

    wd_pair = jnp.concatenate(split(wd_ref[...]), axis=1)
    x = fold(sum(jnp.dot(a, wd_pair, preferred_element_type=F32) for a in split(da_ref[...])))
    x = x + bd_ref[...]
    log_a = (jnp.minimum(x, 0.0) - jnp.log1p(jnp.exp(-jnp.abs(x)))) * (1.0 / GLA_TAU)
    lb = log_a.shape[0]
    grp = min(lb, GLA_CUMSUM_ROWS)
    shift = chunk.bit_length() - 1
    r_b = lax.broadcasted_iota(I32, (grp, grp), 0)
    c_b = lax.broadcasted_iota(I32, (grp, grp), 1)
    tri = jnp.where((c_b <= r_b) & ((c_b >> shift) == (r_b >> shift)), 1.0, 0.0).astype(BF16)
    for g in range(lb // grp):
        rows = slice(g * grp, (g + 1) * grp)
        la_ref[rows, :] = fold(jnp.dot(tri, jnp.concatenate(split(log_a[rows, :]), axis=1),
                                       preferred_element_type=F32))

    r_i = lax.broadcasted_iota(I32, (chunk, chunk), 0)
    c_i = lax.broadcasted_iota(I32, (chunk, chunk), 1)
    causal = c_i <= r_i
    gn = gn_ref[...]
    scale = GLA_DK ** -0.5
    heads = range(GLA_HEADS)
    vcols = [slice(h * GLA_DV, (h + 1) * GLA_DV) for h in heads]

    @pl.loop(0, n_chunks)
    def _(c):
        rows = pl.ds(pl.multiple_of(c * chunk, chunk), chunk)
        b_all = la_ref[rows, :]
        q_t, k_t, k_end, decay = [], [], [], []
        for h in heads:
            b = b_all[:, h * GLA_DK:(h + 1) * GLA_DK]
            b_last = b[chunk - 1:chunk, :]
            q = qka_ref[rows, h * GLA_DK:(h + 1) * GLA_DK].astype(F32) * scale
            k = qka_ref[rows, n_k + h * GLA_DK:n_k + (h + 1) * GLA_DK].astype(F32)
            q_t.append((q * jnp.exp(b)).astype(BF16))
            k_t.append((k * jnp.exp(-b)).astype(BF16))
            k_end.append((k * jnp.exp(b_last - b)).astype(BF16))
            decay.append(jnp.broadcast_to(jnp.exp(b_last), (GLA_DK, GLA_DK)).T)
        s_old = [s_ref[h] for h in heads]
        o_state = [jnp.dot(q_t[h], s_old[h].astype(BF16), preferred_element_type=F32) for h in heads]
        att = [_nt(q_t[h], k_t[h]) for h in heads]
        kv = [_tn(k_end[h], va_ref[rows, vcols[h]]) for h in heads]
        for h in heads:
            a = jnp.where(causal, att[h], 0.0).astype(BF16)
            o = o_state[h] + jnp.dot(a, va_ref[rows, vcols[h]], preferred_element_type=F32)
            s_ref[h] = jnp.concatenate([decay[h], decay[h]], axis=1) * s_old[h] + kv[h]
            r = ra_ref[rows, vcols[h]].astype(F32)
            oa_ref[rows, vcols[h]] = (_rms(o, gn) * (r * jax.nn.sigmoid(r))).astype(oa_ref.dtype)

    @pl.when(l == pl.num_programs(1) - 1)
    def _():
        sout_ref[0] = s_ref[...]


def _gla(qka, va, ra, da, wd, bd, gn, s0, B, L):
    chunk = min(CHUNK, L)
    lb = min(TOKEN_TILE, L)
    nl = L // lb
    row = lambda b, l: (b * nl + l, 0)
    const2 = lambda b, l: (0, 0)
    st = lambda b, l: (b, 0, 0, 0)
    kern = functools.partial(_gla_kernel, chunk=chunk, n_chunks=lb // chunk)
    return pl.pallas_call(
        kern,
        grid=(B, nl),
        in_specs=[pl.BlockSpec((lb, D_MODEL), row), pl.BlockSpec((lb, D_MODEL), row),
                  pl.BlockSpec((lb, D_MODEL), row), pl.BlockSpec((lb, LANES), row),
                  pl.BlockSpec((LANES, GLA_HEADS * GLA_DK), const2),
                  pl.BlockSpec((1, GLA_HEADS * GLA_DK), const2),
                  pl.BlockSpec((1, GLA_DV), const2),
                  pl.BlockSpec((1, GLA_HEADS, GLA_DK, GLA_DV), st)],
        out_specs=(pl.BlockSpec((lb, D_MODEL), row),
                   pl.BlockSpec((1, GLA_HEADS, GLA_DK, GLA_DV), st)),
        out_shape=(jax.ShapeDtypeStruct((B * L, D_MODEL), BF16),
                   jax.ShapeDtypeStruct((B, GLA_HEADS, GLA_DK, GLA_DV), F32)),
        scratch_shapes=[pltpu.VMEM((GLA_HEADS, GLA_DK, GLA_DV), F32),
                        pltpu.VMEM((lb, GLA_HEADS * GLA_DK), F32)],
        compiler_params=_cparams(("arbitrary", "arbitrary")),
        name="gla",
    )(qka, va, ra, da, wd, bd, gn, s0)


def _lam(lq1, lk1, lq2, lk2):
    a = jnp.sum(lq1[...] * lk1[...], axis=-1, keepdims=True)
    b = jnp.sum(lq2[...] * lk2[...], axis=-1, keepdims=True)
    return jnp.exp(a) - jnp.exp(b) + LAM_INIT


def _split_q(q):
    lane = lax.broadcasted_iota(I32, q.shape, 1)
    qs = q * jnp.asarray(DIFF_DH ** -0.5, q.dtype)
    zero = jnp.zeros_like(qs)
    return jnp.where(lane < DIFF_DH, qs, zero), jnp.where(lane >= DIFF_DH, qs, zero)


def _alibi_slopes():
    return jnp.asarray([2.0 ** (-8.0 * (h + 1) / DIFF_HEADS) for h in range(DIFF_HEADS)], F32)


def _attn_prompt_kernel(slope_ref, q_ref, k_ref, v_ref, lq1, lk1, lq2, lk2, dn_ref, o_ref,
                        k1_ref, k2_ref, vt_ref, geo_ref, *, tile, nh):
    qi = pl.program_id(2)
    n_kv = vt_ref.shape[1]
    n_cols = tile // LANES
    lane_k = lax.broadcasted_iota(I32, (tile, DIFF_DV), 1)
    row_i = lax.broadcasted_iota(I32, (tile, DIFF_DV), 0)
    row_lo = (row_i & 255).astype(F32)
    row_hi = (row_i & -256).astype(F32)
    slopes = [slope_ref[pl.program_id(1) * nh + g] for g in range(nh)]
    hcols = [slice(g * DIFF_DV, (g + 1) * DIFF_DV) for g in range(nh)]

    @pl.when(qi == 0)
    def _():
        for g in range(nh):
            kfeat = jnp.where(lane_k == DIFF_DH, slopes[g] * row_lo,
                              jnp.where(lane_k == DIFF_DH + 1, slopes[g] * row_hi,
                                        jnp.where((lane_k == DIFF_DH + 2) | (lane_k == DIFF_DH + 3),
                                                  1.0, 0.0)))
            for c in range(n_kv):
                rows = slice(c * tile, (c + 1) * tile)
                kf = k_ref[rows, hcols[g]].astype(F32)
                k1_ref[g, rows, :] = jnp.where(lane_k < DIFF_DH, kf, kfeat).astype(BF16)
                k2_ref[g, rows, :] = jnp.where(lane_k < DIFF_DH, pltpu.roll(kf, DIFF_DH, 1),
                                               kfeat).astype(BF16)
                vt_ref[g, c] = v_ref[rows, hcols[g]].astype(F32).T.astype(BF16)

    qs = []
    for g in range(nh):
        qf = q_ref[:, hcols[g]].astype(F32) * (DIFF_DH ** -0.5)
        qfeat = jnp.where((lane_k == DIFF_DH) | (lane_k == DIFF_DH + 1), 1.0,
                          jnp.where(lane_k == DIFF_DH + 2, -slopes[g] * row_lo,
                                    jnp.where(lane_k == DIFF_DH + 3, -slopes[g] * row_hi, 0.0)))
        qs.append((jnp.where(lane_k < DIFF_DH, qf, qfeat).T.astype(BF16),
                   jnp.where(lane_k < DIFF_DH, pltpu.roll(qf, DIFF_DH, 1), qfeat).T.astype(BF16)))

    n_maps = 2 * nh
    n_stats = 2 * n_cols + 1

    def scores(i, j):
        rows = pl.ds(pl.multiple_of(j * tile, tile), tile)
        k_ref_i = k1_ref if i % 2 == 0 else k2_ref
        return jnp.dot(k_ref_i[i // 2, rows, :], qs[i // 2][i % 2], preferred_element_type=F32)

    def softmax_part(stats, s, c, fix):
        out, alphas, ps = (), [], []
        for col in range(n_cols):
            lanes = slice(col * LANES, (col + 1) * LANES)
            n_keys = tile if fix is None else (col + 1) * LANES
            sh = s[:n_keys, lanes]
            if fix is not None:
                sh = sh + fix[:n_keys, lanes]
            m, l = stats[2 * col], stats[2 * col + 1]
            m_new = jnp.maximum(m, jnp.max(sh, axis=0, keepdims=True) + c)
            alpha = jnp.exp(m - m_new)
            p = jnp.exp(sh - (m_new - c))
            out += (m_new, alpha * l + jnp.sum(p, axis=0, keepdims=True))
            p = p.astype(BF16)
            alphas.append(jnp.broadcast_to(alpha, (DIFF_DV, LANES)))
            if n_keys < tile:
                p = jnp.concatenate([p, jnp.zeros((tile - n_keys, LANES), BF16)], axis=0)
            ps.append(p)
        return out, jnp.concatenate(alphas, axis=1), jnp.concatenate(ps, axis=1)

    def step(stats, j, cs, fixes):
        ss = [scores(i, j) for i in range(n_maps)]
        parts = [softmax_part(stats[n_stats * i:n_stats * (i + 1)], ss[i], cs[i // 2],
                              None if fixes is None else fixes[i // 2])
                 for i in range(n_maps)]
        out = ()
        for i, (st, alpha, p) in enumerate(parts):
            a = alpha * stats[n_stats * i + n_stats - 1] + jnp.dot(
                vt_ref[i // 2, j], p, preferred_element_type=F32)
            out += st + (a,)
        return out

    def body(j, stats):
        off = jnp.full((1, LANES), (j - qi) * tile, I32).astype(F32)
        return step(stats, j, [off * slopes[g] for g in range(nh)], None)

    row = lambda v: jnp.full((1, LANES), v, F32)
    init = ((row(NEG), row(0.0)) * n_cols + (jnp.zeros((DIFF_DV, tile), F32),)) * n_maps
    stats = lax.fori_loop(0, qi, body, init)

    @pl.when((pl.program_id(0) == 0) & (pl.program_id(1) == 0) & (qi == 0))
    def _():
        r_i = lax.broadcasted_iota(I32, (tile, tile), 0)
        c_i = lax.broadcasted_iota(I32, (tile, tile), 1)
        allowed = (r_i >> CHUNK_SHIFT) <= (c_i >> CHUNK_SHIFT)
        geo_ref[0] = jnp.where(r_i > c_i, (c_i - r_i).astype(F32), 0.0)
        geo_ref[1] = jnp.where(allowed, 0.0, NEG)

    fixes = [(2.0 * slopes[g]) * geo_ref[0] + geo_ref[1] for g in range(nh)]
    stats = step(stats, qi, [row(0.0)] * nh, fixes)
    lam = _lam(lq1, lk1, lq2, lk2)
    for g in range(nh):
        res = []
        for i in (2 * g, 2 * g + 1):
            st = stats[n_stats * i:n_stats * (i + 1)]
            l = jnp.concatenate([jnp.broadcast_to(st[2 * col + 1], (DIFF_DV, LANES))
                                 for col in range(n_cols)], axis=1)
            res.append(st[-1] / l)
        o = (res[0] - lam * res[1]).T
        o_ref[:, hcols[g]] = (_rms(o, dn_ref[...]) * (1.0 - LAM_INIT)).astype(o_ref.dtype)


def _attn_prompt(qb, kb, vb, lq1, lk1, lq2, lk2, dn, B, L):
    tile = min(ATT_TILE, L)
    nq = L // tile
    nh = ATT_HEADS_PER_STEP
    qmap = lambda b, h, i: (b * nq + i, h)
    kvmap = lambda b, h, i: (b, h)
    cmap = lambda b, h, i: (0, 0)
    lspec = pl.BlockSpec((1, DIFF_DH), cmap)
    return pl.pallas_call(
        functools.partial(_attn_prompt_kernel, tile=tile, nh=nh),
        grid=(B, DIFF_HEADS // nh, nq),
        in_specs=[pl.BlockSpec(memory_space=pltpu.SMEM),
                  pl.BlockSpec((tile, nh * DIFF_DV), qmap),
                  pl.BlockSpec((L, nh * DIFF_DV), kvmap), pl.BlockSpec((L, nh * DIFF_DV), kvmap),
                  lspec, lspec, lspec, lspec, pl.BlockSpec((1, DIFF_DV), cmap)],
        out_specs=pl.BlockSpec((tile, nh * DIFF_DV), qmap),
        out_shape=jax.ShapeDtypeStruct((B * L, D_MODEL), BF16),
        scratch_shapes=[pltpu.VMEM((nh, L, DIFF_DV), BF16), pltpu.VMEM((nh, L, DIFF_DV), BF16),
                        pltpu.VMEM((nh, nq, DIFF_DV, tile), BF16),
                        pltpu.VMEM((2, tile, tile), F32)],
        compiler_params=_cparams(("arbitrary", "arbitrary", "arbitrary")),
        name="attn_prompt",
    )(_alibi_slopes(), qb, kb, vb, lq1, lk1, lq2, lk2, dn)


def _attn_sample_kernel(slope_ref, q_ref, kp_ref, vp_ref, kn_ref, vn_ref, lq1, lk1, lq2, lk2,
                        dn_ref, o_ref, m_ref, l_ref, acc_ref, kbuf, vbuf, sem, *, past, lq, chunk):
    b = pl.program_id(0)
    c = pl.program_id(1)
    n_c = pl.num_programs(1)
    step = b * n_c + c
    slot = step & 1

    def slab_copies(bb, cc, s, fn):
        rows = pl.ds(cc * chunk, chunk)
        for hd in range(DIFF_HEADS):
            fn(pltpu.make_async_copy(kp_ref.at[0, bb, rows, hd, :], kbuf.at[s, hd], sem.at[s]))
            fn(pltpu.make_async_copy(vp_ref.at[0, bb, rows, hd, :], vbuf.at[s, hd], sem.at[s]))

    @pl.when(step == 0)
    def _():
        slab_copies(b, c, slot, lambda cp: cp.start())

    @pl.when(step + 1 < pl.num_programs(0) * n_c)
    def _():
        wrap = c + 1 == n_c
        slab_copies(jnp.where(wrap, b + 1, b), jnp.where(wrap, 0, c + 1), 1 - slot,
                    lambda cp: cp.start())

    slab_copies(b, c, slot, lambda cp: cp.wait())

    @pl.when(c == 0)
    def _():
        m_ref[...] = jnp.full_like(m_ref, NEG)
        l_ref[...] = jnp.zeros_like(l_ref)
        acc_ref[...] = jnp.zeros_like(acc_ref)

    def geometry(key0, n_keys):
        r = lax.broadcasted_iota(I32, (2 * lq, n_keys), 0)
        qpos = past + jnp.where(r >= lq, r - lq, r)
        kpos = key0 + lax.broadcasted_iota(I32, (2 * lq, n_keys), 1)
        allowed = (kpos >> CHUNK_SHIFT) <= (qpos >> CHUNK_SHIFT)
        return jnp.abs(qpos - kpos).astype(F32), jnp.where(allowed, 0.0, NEG)

    def block(hd, k, v, geo):
        dist, mask = geo
        q1, q2 = _split_q(q_ref[:, hd * DIFF_DV:(hd + 1) * DIFF_DV])
        s = _nt(jnp.concatenate([q1, q2], axis=0), k)
        s = s + (mask - slope_ref[hd] * dist)
        m_old = m_ref[hd]
        m_new = jnp.maximum(m_old, jnp.max(s, axis=-1, keepdims=True))
        alpha = jnp.exp(m_old - m_new)
        p = jnp.exp(s - m_new)
        l_ref[hd] = alpha * l_ref[hd] + jnp.sum(p, axis=-1, keepdims=True)
        acc_ref[hd] = alpha * acc_ref[hd] + jnp.dot(p.astype(BF16), v, preferred_element_type=F32)
        m_ref[hd] = m_new

    geo = geometry(c * chunk, chunk)
    for hd in range(DIFF_HEADS):
        block(hd, kbuf[slot, hd].astype(BF16), vbuf[slot, hd].astype(BF16), geo)

    @pl.when(c == n_c - 1)
    def _():
        lam = _lam(lq1, lk1, lq2, lk2)
        geo_new = geometry(past, lq)
        for hd in range(DIFF_HEADS):
            cols = slice(hd * DIFF_DV, (hd + 1) * DIFF_DV)
            block(hd, kn_ref[:, cols], vn_ref[:, cols], geo_new)
            o = acc_ref[hd] / l_ref[hd]
            o = o[:lq] - lam * o[lq:]
            o_ref[:, hd * DIFF_DV:(hd + 1) * DIFF_DV] = (
                _rms(o, dn_ref[...]) * (1.0 - LAM_INIT)).astype(o_ref.dtype)


def _attn_sample(qb, cache_k, cache_v, kb, vb, lq1, lk1, lq2, lk2, dn, B, L):
    past = cache_k.shape[2]
    chunk = min(SAMPLE_KEY_CHUNK, past)
    n_chunks = past // chunk
    bmap = lambda b, c: (b, 0)
    cmap = lambda b, c: (0, 0)
    lspec = pl.BlockSpec((1, DIFF_DH), cmap)
    tok = pl.BlockSpec((L, D_MODEL), bmap)
    cache = pl.BlockSpec(memory_space=pl.ANY)
    slabs = pltpu.VMEM((2, DIFF_HEADS, chunk, DIFF_DV), F32)
    return pl.pallas_call(
        functools.partial(_attn_sample_kernel, past=past, lq=L, chunk=chunk),
        grid=(B, n_chunks),
        in_specs=[pl.BlockSpec(memory_space=pltpu.SMEM), tok, cache, cache, tok, tok,
                  lspec, lspec, lspec, lspec, pl.BlockSpec((1, DIFF_DV), cmap)],
        out_specs=tok,
        out_shape=jax.ShapeDtypeStruct((B * L, D_MODEL), BF16),
        scratch_shapes=[pltpu.VMEM((DIFF_HEADS, 2 * L, 1), F32),
                        pltpu.VMEM((DIFF_HEADS, 2 * L, 1), F32),
                        pltpu.VMEM((DIFF_HEADS, 2 * L, DIFF_DV), F32),
                        slabs, slabs, pltpu.SemaphoreType.DMA((2,))],
        compiler_params=_cparams(("arbitrary", "arbitrary")),
        name="attn_sample",
    )(_alibi_slopes(), qb, cache_k, cache_v, kb, vb, lq1, lk1, lq2, lk2, dn)


ROUTE_E1, ROUTE_E2, ROUTE_W1, ROUTE_W2, ROUTE_R1, ROUTE_R2 = range(6)
ROUTE_ROWS = 8
GROUP_LANE0 = N_EXPERTS


def _merge_kernel(x_ref, oa_ref, ob_ref, ga_ref, gb_ref, wa_ref, wb_ref, wo_ref, nf_ref,
                  wr_ref, br_ref, x1_ref, h2_ref, route_ref, route_t_ref, cnt_ref, run_ref):
    i = pl.program_id(0)

    @pl.when(i == 0)
    def _():
        run_ref[...] = jnp.zeros_like(run_ref)

    u_a = jnp.dot(oa_ref[...], wa_ref[...], preferred_element_type=F32)
    u_b = jnp.dot(ob_ref[...], wb_ref[...], preferred_element_type=F32)
    mix = (jax.nn.sigmoid(ga_ref[...].astype(F32)) * u_a
           + jax.nn.sigmoid(gb_ref[...].astype(F32)) * u_b)
    x1 = x_ref[...] + jnp.dot(mix.astype(BF16), wo_ref[...], preferred_element_type=F32)
    x1_ref[...] = x1
    h2 = _rms(x1, nf_ref[...])
    h2_ref[...] = h2

    tm = h2.shape[0]
    split = lambda a: (a.astype(BF16), (a - a.astype(BF16).astype(F32)).astype(BF16))
    w_pair = jnp.concatenate(split(wr_ref[...]), axis=1)
    parts = sum(jnp.dot(a, w_pair, preferred_element_type=F32) for a in split(h2))
    logits = parts[:, :LANES] + parts[:, LANES:] + br_ref[...]
    lane = lax.broadcasted_iota(I32, (tm, LANES), 1)
    big = jnp.int32(LANES)
    g_mask = (lane >= GROUP_LANE0) & (lane < GROUP_LANE0 + N_GROUPS)
    gl = jnp.where(g_mask, logits, -jnp.inf)
    gmax = jnp.max(gl, axis=-1, keepdims=True)
    g_sel = jnp.min(jnp.where(gl == gmax, lane - GROUP_LANE0, big), axis=-1, keepdims=True)
    p_g = 1.0 / jnp.sum(jnp.where(g_mask, jnp.exp(logits - gmax), 0.0), axis=-1, keepdims=True)
    e_mask = (lane < N_EXPERTS) & ((lane >> EPG_SHIFT) == g_sel)
    el = jnp.where(e_mask, logits, -jnp.inf)
    v1 = jnp.max(el, axis=-1, keepdims=True)
    i1 = jnp.min(jnp.where(el == v1, lane, big), axis=-1, keepdims=True)
    el2 = jnp.where(lane == i1, -jnp.inf, el)
    v2 = jnp.max(el2, axis=-1, keepdims=True)
    i2 = jnp.min(jnp.where(el2 == v2, lane, big), axis=-1, keepdims=True)
    t = jnp.exp(v2 - v1)
    w1 = p_g / (1.0 + t)
    w2 = p_g * t / (1.0 + t)

    oh1 = lane == i1
    oh2 = lane == i2
    cnt = jnp.where(oh1, 1.0, 0.0) + jnp.where(oh2, 1.0, 0.0)
    r_i = lax.broadcasted_iota(I32, (tm, tm), 0)
    c_i = lax.broadcasted_iota(I32, (tm, tm), 1)
    before = jnp.where(c_i < r_i, 1.0, 0.0).astype(BF16)
    prior = jnp.dot(before, cnt.astype(BF16), preferred_element_type=F32) + run_ref[...]
    rank1 = jnp.sum(jnp.where(oh1, prior, 0.0), axis=-1, keepdims=True)
    rank2 = jnp.sum(jnp.where(oh2, prior, 0.0), axis=-1, keepdims=True)
    run = run_ref[...] + jnp.sum(cnt, axis=0, keepdims=True)
    run_ref[...] = run
    cnt_ref[...] = run

    route = jnp.zeros((tm, LANES), F32)
    for pos, val in ((ROUTE_E1, i1.astype(F32)), (ROUTE_E2, i2.astype(F32)), (ROUTE_W1, w1),
                     (ROUTE_W2, w2), (ROUTE_R1, rank1), (ROUTE_R2, rank2)):
        route = jnp.where(lane == pos, val, route)
    route_ref[...] = route
    route_t_ref[...] = route.T[:ROUTE_ROWS, :]


def _merge(x, oa, ob, ga, gb, wa, wb, wo, nf, wr, br):
    T = x.shape[0]
    tm = min(TOKEN_TILE, T)
    row = lambda i: (i, 0)
    const = lambda i: (0, 0)
    wspec = pl.BlockSpec((D_MODEL, D_MODEL), const)
    tile = pl.BlockSpec((tm, D_MODEL), row)
    return pl.pallas_call(
        _merge_kernel,
        grid=(T // tm,),
        in_specs=[tile, tile, tile, tile, tile, wspec, wspec, wspec,
                  pl.BlockSpec((1, D_MODEL), const),
                  pl.BlockSpec((D_MODEL, LANES), const), pl.BlockSpec((1, LANES), const)],
        out_specs=(tile, tile, pl.BlockSpec((tm, LANES), row),
                   pl.BlockSpec((ROUTE_ROWS, tm), lambda i: (0, i)), pl.BlockSpec((1, LANES), const)),
        out_shape=(jax.ShapeDtypeStruct((T, D_MODEL), F32),
                   jax.ShapeDtypeStruct((T, D_MODEL), F32),
                   jax.ShapeDtypeStruct((T, LANES), F32),
                   jax.ShapeDtypeStruct((ROUTE_ROWS, T), F32),
                   jax.ShapeDtypeStruct((1, LANES), F32)),
        scratch_shapes=[pltpu.VMEM((1, LANES), F32)],
        compiler_params=_cparams(("arbitrary",)),
        name="merge",
    )(x, oa, ob, ga, gb, wa, wb, wo, nf, wr, br)


def _dispatch_kernel(ends_ref, dest_ref, h2_ref, xs_ref, zero_ref, sem):
    tm = h2_ref.shape[0]
    n_blocks = xs_ref.shape[0] // SLOT_BLOCK

    @pl.when(pl.program_id(0) == 0)
    def _():
        zero_ref[...] = jnp.zeros_like(zero_ref)

        def zero_block(blk):
            start = pl.multiple_of(blk * SLOT_BLOCK, SLOT_BLOCK)
            return pltpu.make_async_copy(zero_ref, xs_ref.at[pl.ds(start, SLOT_BLOCK)], sem)

        def expert_tail(e):
            return jnp.maximum((ends_ref[e] >> SLOT_SHIFT) - 1, 0)

        n_used = ends_ref[N_EXPERTS - 1] >> SLOT_SHIFT
        for e in range(N_EXPERTS):
            zero_block(expert_tail(e)).start()
        lax.fori_loop(n_used, n_blocks, lambda blk, c: (zero_block(blk).start(), c)[1], 0)
        for e in range(N_EXPERTS):
            zero_block(expert_tail(e)).wait()
        lax.fori_loop(n_used, n_blocks, lambda blk, c: (zero_block(blk).wait(), c)[1], 0)

    def row_copy(r, k):
        return pltpu.make_async_copy(h2_ref.at[pl.ds(r, 1)], xs_ref.at[pl.ds(dest_ref[k * tm + r], 1)], sem)

    def start(i, c):
        base = pl.multiple_of(i * ROW_UNROLL, ROW_UNROLL)
        for u in range(ROW_UNROLL):
            row_copy(base + u, 0).start(priority=0)
            row_copy(base + u, 1).start(priority=1)
        return c

    def wait(i, c):
        base = pl.multiple_of(i * ROW_UNROLL, ROW_UNROLL)
        for u in range(ROW_UNROLL):
            row_copy(base + u, 0).wait()
            row_copy(base + u, 1).wait()
        return c

    lax.fori_loop(0, tm // ROW_UNROLL, start, 0)
    lax.fori_loop(0, tm // ROW_UNROLL, wait, 0)


def _dispatch(pad_ends, dest, h2, n_slots):
    T = h2.shape[0]
    tm = min(TOKEN_TILE, T)
    return pl.pallas_call(
        _dispatch_kernel,
        grid=(T // tm,),
        in_specs=[pl.BlockSpec(memory_space=pltpu.SMEM),
                  pl.BlockSpec((2 * tm,), lambda i: (i,), memory_space=pltpu.SMEM),
                  pl.BlockSpec((tm, D_MODEL), lambda i: (i, 0))],
        out_specs=pl.BlockSpec(memory_space=pl.ANY),
        out_shape=jax.ShapeDtypeStruct((n_slots, D_MODEL), F32),
        scratch_shapes=[pltpu.VMEM((SLOT_BLOCK, D_MODEL), F32), pltpu.SemaphoreType.DMA(())],
        compiler_params=_cparams(("arbitrary",)),
        name="dispatch",
    )(pad_ends, dest, h2)


def _experts_kernel(be_ref, nu_ref, xs_ref, wg_ref, wu_ref, wd_ref, y_ref):
    del be_ref
    used = pl.program_id(0) < nu_ref[0]

    @pl.when(used)
    def _():
        x = xs_ref[...].astype(BF16)
        g = jnp.dot(x, wg_ref[0].astype(BF16), preferred_element_type=F32)
        u = jnp.dot(x, wu_ref[0].astype(BF16), preferred_element_type=F32)
        a = (g * jax.nn.sigmoid(g) * u).astype(BF16)
        y_ref[...] = jnp.dot(a, wd_ref[0].astype(BF16), preferred_element_type=F32)

    @pl.when(jnp.logical_not(used))
    def _():
        y_ref[...] = jnp.zeros_like(y_ref)


def _experts(block_e, n_used, xs, wg, wu, wd):
    n_blocks = xs.shape[0] // SLOT_BLOCK
    last = lambda i, nu: jnp.minimum(i, jnp.maximum(nu[0] - 1, 0))
    blk = lambda i, be, nu: (last(i, nu), 0)
    wmap = lambda i, be, nu: (be[last(i, nu)], 0, 0)
    grid_spec = pltpu.PrefetchScalarGridSpec(
        num_scalar_prefetch=2,
        grid=(n_blocks,),
        in_specs=[pl.BlockSpec((SLOT_BLOCK, D_MODEL), blk),
                  pl.BlockSpec((1, D_MODEL, D_EXPERT), wmap),
                  pl.BlockSpec((1, D_MODEL, D_EXPERT), wmap),
                  pl.BlockSpec((1, D_EXPERT, D_MODEL), wmap)],
        out_specs=pl.BlockSpec((SLOT_BLOCK, D_MODEL), lambda i, be, nu: (i, 0)),
    )
    return pl.pallas_call(
        _experts_kernel,
        grid_spec=grid_spec,
        out_shape=jax.ShapeDtypeStruct((xs.shape[0], D_MODEL), F32),
        compiler_params=_cparams(("arbitrary",)),
        name="experts",
    )(block_e, n_used, xs, wg, wu, wd)


def _combine_kernel(dest_ref, dest_next_ref, x1_ref, route_ref, nf_ref, yb_ref, y_ref, g_ref, sem):
    i = pl.program_id(0)
    n = pl.num_programs(0)
    tm = x1_ref.shape[0]
    slot = i & 1

    def row_copy(d_ref, s, r, k):
        return pltpu.make_async_copy(yb_ref.at[pl.ds(d_ref[k * tm + r], 1)], g_ref.at[s, k, pl.ds(r, 1)],
                                     sem.at[s])

    def gather(d_ref, s, wait):
        def rows(b, c):
            base = pl.multiple_of(b * ROW_UNROLL, ROW_UNROLL)
            for u in range(ROW_UNROLL):
                for k in range(2):
                    cp = row_copy(d_ref, s, base + u, k)
                    cp.wait() if wait else cp.start(priority=k)
            return c
        lax.fori_loop(0, tm // ROW_UNROLL, rows, 0)

    @pl.when(i == 0)
    def _():
        gather(dest_ref, slot, False)

    @pl.when(i + 1 < n)
    def _():
        gather(dest_next_ref, 1 - slot, False)

    gather(dest_ref, slot, True)
    route = route_ref[...]
    w1 = route[:, ROUTE_W1:ROUTE_W1 + 1]
    w2 = route[:, ROUTE_W2:ROUTE_W2 + 1]
    x2 = x1_ref[...] + (g_ref[slot, 0] * w1 + g_ref[slot, 1] * w2)
    y_ref[...] = _rms(x2, nf_ref[...])


def _combine(dest, x1, route, nfinal, yb):
    T = x1.shape[0]
    tm = min(TOKEN_TILE, T)
    n = T // tm
    row = lambda i: (i, 0)
    return pl.pallas_call(
        _combine_kernel,
        grid=(n,),
        in_specs=[pl.BlockSpec((2 * tm,), lambda i: (i,), memory_space=pltpu.SMEM),
                  pl.BlockSpec((2 * tm,), lambda i: (jnp.minimum(i + 1, n - 1),),
                               memory_space=pltpu.SMEM),
                  pl.BlockSpec((tm, D_MODEL), row), pl.BlockSpec((tm, LANES), row),
                  pl.BlockSpec((1, D_MODEL), lambda i: (0, 0)),
                  pl.BlockSpec(memory_space=pl.ANY)],
        out_specs=pl.BlockSpec((tm, D_MODEL), row),
        out_shape=jax.ShapeDtypeStruct((T, D_MODEL), F32),
        scratch_shapes=[pltpu.VMEM((2, 2, tm, D_MODEL), F32), pltpu.SemaphoreType.DMA((2,))],
        compiler_params=_cparams(("arbitrary",)),
        name="combine",
    )(dest, dest, x1, route, nfinal, yb)


def _moe(x1, h2, route, route_t, counts, wg, wu, wd, nfinal):
    T = x1.shape[0]
    n_slots = T * 2 + N_EXPERTS * SLOT_BLOCK
    n_blocks = n_slots // SLOT_BLOCK
    cnt = counts[0, :N_EXPERTS].astype(I32)
    padded = (cnt + SLOT_BLOCK - 1) // SLOT_BLOCK * SLOT_BLOCK
    pad_ends = jnp.cumsum(padded)
    pad_starts = pad_ends - padded
    e = route_t[ROUTE_E1:ROUTE_E2 + 1].astype(I32)
    rank = route_t[ROUTE_R1:ROUTE_R2 + 1].astype(I32)
    start = sum(jnp.where(e == k, pad_starts[k], 0) for k in range(N_EXPERTS))
    tm = min(TOKEN_TILE, T)
    dest = (start + rank).reshape(2, T // tm, tm).transpose(1, 0, 2).reshape(2 * T)
    block_start = jnp.arange(n_blocks, dtype=I32) * SLOT_BLOCK
    block_e = jnp.sum((block_start[:, None] >= pad_ends[None, :]).astype(I32), axis=1)
    block_e = jnp.minimum(block_e, N_EXPERTS - 1)
    n_used = (pad_ends[-1:] // SLOT_BLOCK).astype(I32)
    xs = _dispatch(pad_ends.astype(I32), dest, h2, n_slots)
    yb = _experts(block_e, n_used, xs, wg, wu, wd)
    return _combine(dest, x1, route, nfinal, yb)


def _layer(x, s0, k_past, v_past, p, B, L):
    row2 = lambda a: a.reshape(1, -1)
    qka, va, ra, qb, kb, vb, ga, gb, da, k5, v5 = _in_proj(x, row2(p["norm_mix"]), p["w_in"])
    oa, s_new = _gla(qka, va, ra, da, p["w_decay"], row2(p["b_decay"]), row2(p["gla_norm"]),
                     s0, B, L)
    lams = [row2(p[n]) for n in ("lambda_q1", "lambda_k1", "lambda_q2", "lambda_k2")]
    dn = row2(p["diff_norm"])
    if k_past is None:
        ob = _attn_prompt(qb, kb, vb, *lams, dn, B, L)
    else:
        ob = _attn_sample(qb, k_past, v_past, kb, vb, *lams, dn, B, L)
    x1, h2, route, route_t, counts = _merge(x, oa, ob, ga, gb, p["w_proj_a"], p["w_proj_b"],
                                            p["w_out"], row2(p["norm_ffn"]), p["w_router"],
                                            p["b_router"])
    y = _moe(x1, h2, route, route_t, counts, p["w_gate"], p["w_up"], p["w_down"],
             row2(p["norm_final"]))
    return y, s_new, k5, v5


def kernel(x_prompt, x_sample, cache_k, cache_v, state_gla, norm_mix, w_in, w_decay, b_decay,
           gla_norm, w_proj_a, lambda_q1, lambda_k1, lambda_q2, lambda_k2, diff_norm, w_proj_b,
           w_out, norm_ffn, w_router_group, b_router_group, w_router_expert, b_router_expert,
           w_gate, w_up, w_down, norm_final):
    B, L, D = x_prompt.shape
    Bs, Ls, _ = x_sample.shape
    w_router = jnp.concatenate(
        [w_router_expert[0], w_router_group[0],
         jnp.zeros((D, LANES - N_EXPERTS - N_GROUPS), F32)], axis=1)
    b_router = jnp.concatenate(
        [b_router_expert[0], b_router_group[0],
         jnp.zeros((LANES - N_EXPERTS - N_GROUPS,), F32)]).reshape(1, LANES)
    p = dict(
        norm_mix=norm_mix[0], w_in=_prep_w_in(w_in[0]),
        w_decay=jnp.pad(w_decay[0], ((0, LANES - GLA_RANK), (0, 0))), b_decay=b_decay[0],
        gla_norm=gla_norm[0], w_proj_a=w_proj_a[0].astype(BF16),
        lambda_q1=lambda_q1[0], lambda_k1=lambda_k1[0], lambda_q2=lambda_q2[0],
        lambda_k2=lambda_k2[0], diff_norm=diff_norm[0], w_proj_b=w_proj_b[0].astype(BF16),
        w_out=w_out[0].astype(BF16), norm_ffn=norm_ffn[0], w_router=w_router, b_router=b_router,
        w_gate=w_gate[0], w_up=w_up[0], w_down=w_down[0],
        norm_final=norm_final)

    s0p = jnp.zeros((B, GLA_HEADS, GLA_DK, GLA_DV), F32)
    yp, sp, kp, vp = _layer(x_prompt.reshape(B * L, D), s0p, None, None, p, B, L)
    ys, ss, ks, vs = _layer(x_sample.reshape(Bs * Ls, D), state_gla[0], cache_k, cache_v,
                            p, Bs, Ls)
    kv = lambda a, b, l: a.reshape(1, b, l, DIFF_HEADS, DIFF_DV)
    return (yp.reshape(B, L, D), ys.reshape(Bs, Ls, D), kv(kp, B, L), kv(vp, B, L), sp[None],
            kv(ks, Bs, Ls), kv(vs, Bs, Ls), ss[None])
```

```python
import functools

import jax
import jax.numpy as jnp
from jax import lax
from jax.experimental import pallas as pl
from jax.experimental.pallas import tpu as pltpu

F32 = jnp.float32
BF16 = jnp.bfloat16
I32 = jnp.int32

D_MODEL = 1024
CHUNK = 64
CHUNK_SHIFT = 6
EPS = 1e-6
GLA_HEADS = 4
GLA_DK = 128
GLA_DV = 256
GLA_RANK = 16
GLA_TAU = 16.0
DIFF_HEADS = 8
DIFF_DH = 64
DIFF_DV = 128
N_GROUPS = 4
EXPERTS_PER_GROUP = 4
EPG_SHIFT = 2
N_EXPERTS = 16
D_EXPERT = 512
LAM_INIT = 0.8 - 0.6

LANES = 128
SUBLANES = 8
TOKEN_TILE = 512
SLOT_BLOCK = 256
SLOT_SHIFT = 8
ATT_TILE = 512
ATT_HEADS_PER_STEP = 2
SAMPLE_KEY_CHUNK = 1024
GLA_CUMSUM_ROWS = 256
VMEM_LIMIT = 56 * 1024 * 1024
NEG = -1e30

HI = lax.Precision.HIGHEST


def _cparams(sem):
    return pltpu.CompilerParams(dimension_semantics=sem, vmem_limit_bytes=VMEM_LIMIT)


def _nt(a, b):
    return lax.dot_general(a, b, (((1,), (1,)), ((), ())), preferred_element_type=F32)


def _tn(a, b):
    return lax.dot_general(a, b, (((0,), (0,)), ((), ())), preferred_element_type=F32)


def _rms(x, g):
    return x * lax.rsqrt(jnp.mean(x * x, axis=-1, keepdims=True) + EPS) * g


def _inproj_kernel(x_ref, g_ref, wa_ref, wda_ref, wb_ref, qka_ref, va_ref, ra_ref, qb_ref, kb_ref,
                   vb_ref, ga_ref, gb_ref, da_ref, k5_ref, v5_ref, stage_ref, sem):
    i = pl.program_id(0)
    n = pl.num_programs(0)
    tm = x_ref.shape[0]
    slot = i & 1

    def head_copy(s, which, hd):
        dst = (k5_ref, v5_ref)[which]
        return pltpu.make_async_copy(stage_ref.at[s, which, hd],
                                     dst.at[pl.ds(i * tm, tm), hd, :], sem.at[s])

    def for_all(s, fn):
        for which in range(2):
            for hd in range(DIFF_HEADS):
                fn(head_copy(s, which, hd))

    @pl.when(i >= 2)
    def _():
        for_all(slot, lambda cp: cp.wait())

    h = _rms(x_ref[...], g_ref[...]).astype(BF16)
    outs = ((qka_ref, wa_ref, 0), (va_ref, wa_ref, 1), (ra_ref, wa_ref, 2), (qb_ref, wb_ref, 0),
            (kb_ref, wb_ref, 1), (vb_ref, wb_ref, 2), (ga_ref, wb_ref, 3), (gb_ref, wb_ref, 4))
    for o, w_ref, c in outs:
        z = jnp.dot(h, w_ref[:, c * D_MODEL:(c + 1) * D_MODEL], preferred_element_type=F32)
        o[...] = z.astype(o.dtype)
        which = 0 if o is kb_ref else 1 if o is vb_ref else None
        if which is not None:
            for hd in range(DIFF_HEADS):
                stage_ref[slot, which, hd] = z[:, hd * DIFF_DV:(hd + 1) * DIFF_DV]
    da_ref[...] = jnp.dot(h, wda_ref[...], preferred_element_type=F32)
    for_all(slot, lambda cp: cp.start())

    @pl.when(i == n - 1)
    def _():
        for_all(slot, lambda cp: cp.wait())

    @pl.when((i == n - 1) & (i >= 1))
    def _():
        for_all(1 - slot, lambda cp: cp.wait())


N_GLA_COLS = 3 * D_MODEL
N_DIFF_COLS = 5 * D_MODEL


def _prep_w_in(w_in):
    w = w_in.astype(BF16)
    return w, w[:, N_GLA_COLS + GLA_RANK:]


def _in_proj(x, g, w):
    w_all, w_diff = w
    T = x.shape[0]
    tm = min(TOKEN_TILE, T)
    resident = lambda cols, j: pl.BlockSpec((D_MODEL, cols), lambda i: (0, j),
                                            pipeline_mode=pl.Buffered(1))
    row = lambda i: (i, 0)
    const = lambda i: (0, 0)
    wide = lambda dt: jax.ShapeDtypeStruct((T, D_MODEL), dt)
    heads = jax.ShapeDtypeStruct((T, DIFF_HEADS, DIFF_DV), F32)
    out_shape = (wide(BF16),) * 8 + (jax.ShapeDtypeStruct((T, LANES), F32), heads, heads)
    out_specs = tuple([pl.BlockSpec((tm, D_MODEL), row)] * 8 + [pl.BlockSpec((tm, LANES), row)]
                      + [pl.BlockSpec(memory_space=pl.ANY)] * 2)
    return pl.pallas_call(
        _inproj_kernel,
        grid=(T // tm,),
        in_specs=[pl.BlockSpec((tm, D_MODEL), row),
                  pl.BlockSpec((1, D_MODEL), const),
                  resident(N_GLA_COLS, 0), resident(LANES, N_GLA_COLS // LANES),
                  resident(N_DIFF_COLS, 0)],
        out_specs=out_specs,
        out_shape=out_shape,
        scratch_shapes=[pltpu.VMEM((2, 2, DIFF_HEADS, tm, DIFF_DV), F32),
                        pltpu.SemaphoreType.DMA((2,))],
        compiler_params=_cparams(("arbitrary",)),
        name="in_proj",
    )(x, g, w_all, w_all, w_diff)


def _gla_kernel(qka_ref, va_ref, ra_ref, da_ref, wd_ref, bd_ref, gn_ref, s0_ref,
                oa_ref, sout_ref, s_ref, la_ref, *, chunk, n_chunks):
    l = pl.program_id(1)

    @pl.when(l == 0)
    def _():
        s_ref[...] = s0_ref[0]

    split = lambda a: (a.astype(BF16), (a - a.astype(BF16).astype(F32)).astype(BF16))
    n_k = GLA_HEADS * GLA_DK
    fold = lambda z: z[:, :n_k] + z[:, n_k:]

    wd_pair = jnp.concatenate(split(wd_ref[...]), axis=1)
    x = fold(sum(jnp.dot(a, wd_pair, preferred_element_type=F32) for a in split(da_ref[...])))
    x = x + bd_ref[...]
    log_a = (jnp.minimum(x, 0.0) - jnp.log1p(jnp.exp(-jnp.abs(x)))) * (1.0 / GLA_TAU)
    lb = log_a.shape[0]
    grp = min(lb, GLA_CUMSUM_ROWS)
    shift = chunk.bit_length() - 1
    r_b = lax.broadcasted_iota(I32, (grp, grp), 0)
    c_b = lax.broadcasted_iota(I32, (grp, grp), 1)
    tri = jnp.where((c_b <= r_b) & ((c_b >> shift) == (r_b >> shift)), 1.0, 0.0).astype(BF16)
    for g in range(lb // grp):
        rows = slice(g * grp, (g + 1) * grp)
        la_ref[rows, :] = fold(jnp.dot(tri, jnp.concatenate(split(log_a[rows, :]), axis=1),
                                       preferred_element_type=F32))

    r_i = lax.broadcasted_iota(I32, (chunk, chunk), 0)
    c_i = lax.broadcasted_iota(I32, (chunk, chunk), 1)
    causal = c_i <= r_i
    gn = gn_ref[...]
    scale = GLA_DK ** -0.5
    heads = range(GLA_HEADS)
    vcols = [slice(h * GLA_DV, (h + 1) * GLA_DV) for h in heads]

    @pl.loop(0, n_chunks)
    def _(c):
        rows = pl.ds(pl.multiple_of(c * chunk, chunk), chunk)
        b_all = la_ref[rows, :]
        q_t, k_t, k_end, decay = [], [], [], []
        for h in heads:
            b = b_all[:, h * GLA_DK:(h + 1) * GLA_DK]
            b_last = b[chunk - 1:chunk, :]
            q = qka_ref[rows, h * GLA_DK:(h + 1) * GLA_DK].astype(F32) * scale
            k = qka_ref[rows, n_k + h * GLA_DK:n_k + (h + 1) * GLA_DK].astype(F32)
            q_t.append((q * jnp.exp(b)).astype(BF16))
            k_t.append((k * jnp.exp(-b)).astype(BF16))
            k_end.append((k * jnp.exp(b_last - b)).astype(BF16))
            decay.append(jnp.broadcast_to(jnp.exp(b_last), (GLA_DK, GLA_DK)).T)
        s_old = [s_ref[h] for h in heads]
        o_state = [jnp.dot(q_t[h], s_old[h].astype(BF16), preferred_element_type=F32) for h in heads]
        att = [_nt(q_t[h], k_t[h]) for h in heads]
        kv = [_tn(k_end[h], va_ref[rows, vcols[h]]) for h in heads]
        for h in heads:
            a = jnp.where(causal, att[h], 0.0).astype(BF16)
            o = o_state[h] + jnp.dot(a, va_ref[rows, vcols[h]], preferred_element_type=F32)
            s_ref[h] = jnp.concatenate([decay[h], decay[h]], axis=1) * s_old[h] + kv[h]
            r = ra_ref[rows, vcols[h]].astype(F32)
            oa_ref[rows, vcols[h]] = (_rms(o, gn) * (r * jax.nn.sigmoid(r))).astype(oa_ref.dtype)

    @pl.when(l == pl.num_programs(1) - 1)
    def _():
        sout_ref[0] = s_ref[...]


def _gla(qka, va, ra, da, wd, bd, gn, s0, B, L):
    chunk = min(CHUNK, L)
    lb = min(TOKEN_TILE, L)
    nl = L // lb
    row = lambda b, l: (b * nl + l, 0)
    const2 = lambda b, l: (0, 0)
    st = lambda b, l: (b, 0, 0, 0)
    kern = functools.partial(_gla_kernel, chunk=chunk, n_chunks=lb // chunk)
    return pl.pallas_call(
        kern,
        grid=(B, nl),
        in_specs=[pl.BlockSpec((lb, D_MODEL), row), pl.BlockSpec((lb, D_MODEL), row),
                  pl.BlockSpec((lb, D_MODEL), row), pl.BlockSpec((lb, LANES), row),
                  pl.BlockSpec((LANES, GLA_HEADS * GLA_DK), const2),
                  pl.BlockSpec((1, GLA_HEADS * GLA_DK), const2),
                  pl.BlockSpec((1, GLA_DV), const2),
                  pl.BlockSpec((1, GLA_HEADS, GLA_DK, GLA_DV), st)],
        out_specs=(pl.BlockSpec((lb, D_MODEL), row),
                   pl.BlockSpec((1, GLA_HEADS, GLA_DK, GLA_DV), st)),
        out_shape=(jax.ShapeDtypeStruct((B * L, D_MODEL), BF16),
                   jax.ShapeDtypeStruct((B, GLA_HEADS, GLA_DK, GLA_DV), F32)),
        scratch_shapes=[pltpu.VMEM((GLA_HEADS, GLA_DK, GLA_DV), F32),
                        pltpu.VMEM((lb, GLA_HEADS * GLA_DK), F32)],
        compiler_params=_cparams(("arbitrary", "arbitrary")),
        name="gla",
    )(qka, va, ra, da, wd, bd, gn, s0)


def _lam(lq1, lk1, lq2, lk2):
    a = jnp.sum(lq1[...] * lk1[...], axis=-1, keepdims=True)
    b = jnp.sum(lq2[...] * lk2[...], axis=-1, keepdims=True)
    return jnp.exp(a) - jnp.exp(b) + LAM_INIT


def _split_q(q):
    lane = lax.broadcasted_iota(I32, q.shape, 1)
    qs = q * jnp.asarray(DIFF_DH ** -0.5, q.dtype)
    zero = jnp.zeros_like(qs)
    return jnp.where(lane < DIFF_DH, qs, zero), jnp.where(lane >= DIFF_DH, qs, zero)


def _alibi_slopes():
    return jnp.asarray([2.0 ** (-8.0 * (h + 1) / DIFF_HEADS) for h in range(DIFF_HEADS)], F32)


def _attn_prompt_kernel(slope_ref, q_ref, k_ref, v_ref, lq1, lk1, lq2, lk2, dn_ref, o_ref,
                        k1_ref, k2_ref, vt_ref, geo_ref, *, tile, nh):
    qi = pl.program_id(2)
    n_kv = vt_ref.shape[1]
    n_cols = tile // LANES
    lane_k = lax.broadcasted_iota(I32, (tile, DIFF_DV), 1)
    row_i = lax.broadcasted_iota(I32, (tile, DIFF_DV), 0)
    row_lo = (row_i & 255).astype(F32)
    row_hi = (row_i & -256).astype(F32)
    slopes = [slope_ref[pl.program_id(1) * nh + g] for g in range(nh)]
    hcols = [slice(g * DIFF_DV, (g + 1) * DIFF_DV) for g in range(nh)]

    @pl.when(qi == 0)
    def _():
        for g in range(nh):
            kfeat = jnp.where(lane_k == DIFF_DH, slopes[g] * row_lo,
                              jnp.where(lane_k == DIFF_DH + 1, slopes[g] * row_hi,
                                        jnp.where((lane_k == DIFF_DH + 2) | (lane_k == DIFF_DH + 3),
                                                  1.0, 0.0)))
            for c in range(n_kv):
                rows = slice(c * tile, (c + 1) * tile)
                kf = k_ref[rows, hcols[g]].astype(F32)
                k1_ref[g, rows, :] = jnp.where(lane_k < DIFF_DH, kf, kfeat).astype(BF16)
                k2_ref[g, rows, :] = jnp.where(lane_k < DIFF_DH, pltpu.roll(kf, DIFF_DH, 1),
                                               kfeat).astype(BF16)
                vt_ref[g, c] = v_ref[rows, hcols[g]].astype(F32).T.astype(BF16)

    qs = []
    for g in range(nh):
        qf = q_ref[:, hcols[g]].astype(F32) * (DIFF_DH ** -0.5)
        qfeat = jnp.where((lane_k == DIFF_DH) | (lane_k == DIFF_DH + 1), 1.0,
                          jnp.where(lane_k == DIFF_DH + 2, -slopes[g] * row_lo,
                                    jnp.where(lane_k == DIFF_DH + 3, -slopes[g] * row_hi, 0.0)))
        qs.append((jnp.where(lane_k < DIFF_DH, qf, qfeat).T.astype(BF16),
                   jnp.where(lane_k < DIFF_DH, pltpu.roll(qf, DIFF_DH, 1), qfeat).T.astype(BF16)))

    n_maps = 2 * nh
    n_stats = 2 * n_cols + 1

    def scores(i, j):
        rows = pl.ds(pl.multiple_of(j * tile, tile), tile)
        k_ref_i = k1_ref if i % 2 == 0 else k2_ref
        return jnp.dot(k_ref_i[i // 2, rows, :], qs[i // 2][i % 2], preferred_element_type=F32)

    def softmax_part(stats, s, c, fix):
        out, alphas, ps = (), [], []
        for col in range(n_cols):
            lanes = slice(col * LANES, (col + 1) * LANES)
            n_keys = tile if fix is None else (col + 1) * LANES
            sh = s[:n_keys, lanes]
            if fix is not None:
                sh = sh + fix[:n_keys, lanes]
            m, l = stats[2 * col], stats[2 * col + 1]
            m_new = jnp.maximum(m, jnp.max(sh, axis=0, keepdims=True) + c)
            alpha = jnp.exp(m - m_new)
            p = jnp.exp(sh - (m_new - c))
            out += (m_new, alpha * l + jnp.sum(p, axis=0, keepdims=True))
            p = p.astype(BF16)
            alphas.append(jnp.broadcast_to(alpha, (DIFF_DV, LANES)))
            if n_keys < tile:
                p = jnp.concatenate([p, jnp.zeros((tile - n_keys, LANES), BF16)], axis=0)
            ps.append(p)
        return out, jnp.concatenate(alphas, axis=1), jnp.concatenate(ps, axis=1)

    def step(stats, j, cs, fixes):
        ss = [scores(i, j) for i in range(n_maps)]
        parts = [softmax_part(stats[n_stats * i:n_stats * (i + 1)], ss[i], cs[i // 2],
                              None if fixes is None else fixes[i // 2])
                 for i in range(n_maps)]
        out = ()
        for i, (st, alpha, p) in enumerate(parts):
            a = alpha * stats[n_stats * i + n_stats - 1] + jnp.dot(
                vt_ref[i // 2, j], p, preferred_element_type=F32)
            out += st + (a,)
        return out

    def body(j, stats):
        off = jnp.full((1, LANES), (j - qi) * tile, I32).astype(F32)
        return step(stats, j, [off * slopes[g] for g in range(nh)], None)

    row = lambda v: jnp.full((1, LANES), v, F32)
    init = ((row(NEG), row(0.0)) * n_cols + (jnp.zeros((DIFF_DV, tile), F32),)) * n_maps
    stats = lax.fori_loop(0, qi, body, init)

    @pl.when((pl.program_id(0) == 0) & (pl.program_id(1) == 0) & (qi == 0))
    def _():
        r_i = lax.broadcasted_iota(I32, (tile, tile), 0)
        c_i = lax.broadcasted_iota(I32, (tile, tile), 1)
        allowed = (r_i >> CHUNK_SHIFT) <= (c_i >> CHUNK_SHIFT)
        geo_ref[0] = jnp.where(r_i > c_i, (c_i - r_i).astype(F32), 0.0)
        geo_ref[1] = jnp.where(allowed, 0.0, NEG)

    fixes = [(2.0 * slopes[g]) * geo_ref[0] + geo_ref[1] for g in range(nh)]
    stats = step(stats, qi, [row(0.0)] * nh, fixes)
    lam = _lam(lq1, lk1, lq2, lk2)
    for g in range(nh):
        res = []
        for i in (2 * g, 2 * g + 1):
            st = stats[n_stats * i:n_stats * (i + 1)]
            l = jnp.concatenate([jnp.broadcast_to(st[2 * col + 1], (DIFF_DV, LANES))
                                 for col in range(n_cols)], axis=1)
            res.append(st[-1] / l)
        o = (res[0] - lam * res[1]).T
        o_ref[:, hcols[g]] = (_rms(o, dn_ref[...]) * (1.0 - LAM_INIT)).astype(o_ref.dtype)


def _attn_prompt(qb, kb, vb, lq1, lk1, lq2, lk2, dn, B, L):
    tile = min(ATT_TILE, L)
    nq = L // tile
    nh = ATT_HEADS_PER_STEP
    qmap = lambda b, h, i: (b * nq + i, h)
    kvmap = lambda b, h, i: (b, h)
    cmap = lambda b, h, i: (0, 0)
    lspec = pl.BlockSpec((1, DIFF_DH), cmap)
    return pl.pallas_call(
        functools.partial(_attn_prompt_kernel, tile=tile, nh=nh),
        grid=(B, DIFF_HEADS // nh, nq),
        in_specs=[pl.BlockSpec(memory_space=pltpu.SMEM),
                  pl.BlockSpec((tile, nh * DIFF_DV), qmap),
                  pl.BlockSpec((L, nh * DIFF_DV), kvmap), pl.BlockSpec((L, nh * DIFF_DV), kvmap),
                  lspec, lspec, lspec, lspec, pl.BlockSpec((1, DIFF_DV), cmap)],
        out_specs=pl.BlockSpec((tile, nh * DIFF_DV), qmap),
        out_shape=jax.ShapeDtypeStruct((B * L, D_MODEL), BF16),
        scratch_shapes=[pltpu.VMEM((nh, L, DIFF_DV), BF16), pltpu.VMEM((nh, L, DIFF_DV), BF16),
                        pltpu.VMEM((nh, nq, DIFF_DV, tile), BF16),
                        pltpu.VMEM((2, tile, tile), F32)],
        compiler_params=_cparams(("arbitrary", "arbitrary", "arbitrary")),
        name="attn_prompt",
    )(_alibi_slopes(), qb, kb, vb, lq1, lk1, lq2, lk2, dn)


def _attn_sample_kernel(slope_ref, q_ref, kp_ref, vp_ref, kn_ref, vn_ref, lq1, lk1, lq2, lk2,
                        dn_ref, o_ref, m_ref, l_ref, acc_ref, kbuf, vbuf, sem, *, past, lq, chunk):
    b = pl.program_id(0)
    c = pl.program_id(1)
    n_c = pl.num_programs(1)
    step = b * n_c + c
    slot = step & 1

    def slab_copies(bb, cc, s, fn):
        rows = pl.ds(cc * chunk, chunk)
        for hd in range(DIFF_HEADS):
            fn(pltpu.make_async_copy(kp_ref.at[0, bb, rows, hd, :], kbuf.at[s, hd], sem.at[s]))
            fn(pltpu.make_async_copy(vp_ref.at[0, bb, rows, hd, :], vbuf.at[s, hd], sem.at[s]))

    @pl.when(step == 0)
    def _():
        slab_copies(b, c, slot, lambda cp: cp.start())

    @pl.when(step + 1 < pl.num_programs(0) * n_c)
    def _():
        wrap = c + 1 == n_c
        slab_copies(jnp.where(wrap, b + 1, b), jnp.where(wrap, 0, c + 1), 1 - slot,
                    lambda cp: cp.start())

    slab_copies(b, c, slot, lambda cp: cp.wait())

    @pl.when(c == 0)
    def _():
        m_ref[...] = jnp.full_like(m_ref, NEG)
        l_ref[...] = jnp.zeros_like(l_ref)
        acc_ref[...] = jnp.zeros_like(acc_ref)

    def geometry(key0, n_keys):
        r = lax.broadcasted_iota(I32, (2 * lq, n_keys), 0)
        qpos = past + jnp.where(r >= lq, r - lq, r)
        kpos = key0 + lax.broadcasted_iota(I32, (2 * lq, n_keys), 1)
        allowed = (kpos >> CHUNK_SHIFT) <= (qpos >> CHUNK_SHIFT)
        return jnp.abs(qpos - kpos).astype(F32), jnp.where(allowed, 0.0, NEG)

    def block(hd, k, v, geo):
        dist, mask = geo
        q1, q2 = _split_q(q_ref[:, hd * DIFF_DV:(hd + 1) * DIFF_DV])
        s = _nt(jnp.concatenate([q1, q2], axis=0), k)
        s = s + (mask - slope_ref[hd] * dist)
        m_old = m_ref[hd]
        m_new = jnp.maximum(m_old, jnp.max(s, axis=-1, keepdims=True))
        alpha = jnp.exp(m_old - m_new)
        p = jnp.exp(s - m_new)
        l_ref[hd] = alpha * l_ref[hd] + jnp.sum(p, axis=-1, keepdims=True)
        acc_ref[hd] = alpha * acc_ref[hd] + jnp.dot(p.astype(BF16), v, preferred_element_type=F32)
        m_ref[hd] = m_new

    geo = geometry(c * chunk, chunk)
    for hd in range(DIFF_HEADS):
        block(hd, kbuf[slot, hd].astype(BF16), vbuf[slot, hd].astype(BF16), geo)

    @pl.when(c == n_c - 1)
    def _():
        lam = _lam(lq1, lk1, lq2, lk2)
        geo_new = geometry(past, lq)
        for hd in range(DIFF_HEADS):
            cols = slice(hd * DIFF_DV, (hd + 1) * DIFF_DV)
            block(hd, kn_ref[:, cols], vn_ref[:, cols], geo_new)
            o = acc_ref[hd] / l_ref[hd]
            o = o[:lq] - lam * o[lq:]
            o_ref[:, hd * DIFF_DV:(hd + 1) * DIFF_DV] = (
                _rms(o, dn_ref[...]) * (1.0 - LAM_INIT)).astype(o_ref.dtype)


def _attn_sample(qb, cache_k, cache_v, kb, vb, lq1, lk1, lq2, lk2, dn, B, L):
    past = cache_k.shape[2]
    chunk = min(SAMPLE_KEY_CHUNK, past)
    n_chunks = past // chunk
    bmap = lambda b, c: (b, 0)
    cmap = lambda b, c: (0, 0)
    lspec = pl.BlockSpec((1, DIFF_DH), cmap)
    tok = pl.BlockSpec((L, D_MODEL), bmap)
    cache = pl.BlockSpec(memory_space=pl.ANY)
    slabs = pltpu.VMEM((2, DIFF_HEADS, chunk, DIFF_DV), F32)
    return pl.pallas_call(
        functools.partial(_attn_sample_kernel, past=past, lq=L, chunk=chunk),
        grid=(B, n_chunks),
        in_specs=[pl.BlockSpec(memory_space=pltpu.SMEM), tok, cache, cache, tok, tok,
                  lspec, lspec, lspec, lspec, pl.BlockSpec((1, DIFF_DV), cmap)],
        out_specs=tok,
        out_shape=jax.ShapeDtypeStruct((B * L, D_MODEL), BF16),
        scratch_shapes=[pltpu.VMEM((DIFF_HEADS, 2 * L, 1), F32),
                        pltpu.VMEM((DIFF_HEADS, 2 * L, 1), F32),
                        pltpu.VMEM((DIFF_HEADS, 2 * L, DIFF_DV), F32),
                        slabs, slabs, pltpu.SemaphoreType.DMA((2,))],
        compiler_params=_cparams(("arbitrary", "arbitrary")),
        name="attn_sample",
    )(_alibi_slopes(), qb, cache_k, cache_v, kb, vb, lq1, lk1, lq2, lk2, dn)


ROUTE_E1, ROUTE_E2, ROUTE_W1, ROUTE_W2, ROUTE_R1, ROUTE_R2 = range(6)
ROUTE_ROWS = 8
GROUP_LANE0 = N_EXPERTS


def _merge_kernel(x_ref, oa_ref, ob_ref, ga_ref, gb_ref, wa_ref, wb_ref, wo_ref, nf_ref,
                  wr_ref, br_ref, x1_ref, h2_ref, route_ref, route_t_ref, cnt_ref, run_ref):
    i = pl.program_id(0)

    @pl.when(i == 0)
    def _():
        run_ref[...] = jnp.zeros_like(run_ref)

    u_a = jnp.dot(oa_ref[...], wa_ref[...], preferred_element_type=F32)
    u_b = jnp.dot(ob_ref[...], wb_ref[...], preferred_element_type=F32)
    mix = (jax.nn.sigmoid(ga_ref[...].astype(F32)) * u_a
           + jax.nn.sigmoid(gb_ref[...].astype(F32)) * u_b)
    x1 = x_ref[...] + jnp.dot(mix.astype(BF16), wo_ref[...], preferred_element_type=F32)
    x1_ref[...] = x1
    h2 = _rms(x1, nf_ref[...])
    h2_ref[...] = h2

    tm = h2.shape[0]
    split = lambda a: (a.astype(BF16), (a - a.astype(BF16).astype(F32)).astype(BF16))
    w_pair = jnp.concatenate(split(wr_ref[...]), axis=1)
    parts = sum(jnp.dot(a, w_pair, preferred_element_type=F32) for a in split(h2))
    logits = parts[:, :LANES] + parts[:, LANES:] + br_ref[...]
    lane = lax.broadcasted_iota(I32, (tm, LANES), 1)
    big = jnp.int32(LANES)
    g_mask = (lane >= GROUP_LANE0) & (lane < GROUP_LANE0 + N_GROUPS)
    gl = jnp.where(g_mask, logits, -jnp.inf)
    gmax = jnp.max(gl, axis=-1, keepdims=True)
    g_sel = jnp.min(jnp.where(gl == gmax, lane - GROUP_LANE0, big), axis=-1, keepdims=True)
    p_g = 1.0 / jnp.sum(jnp.where(g_mask, jnp.exp(logits - gmax), 0.0), axis=-1, keepdims=True)
    e_mask = (lane < N_EXPERTS) & ((lane >> EPG_SHIFT) == g_sel)
    el = jnp.where(e_mask, logits, -jnp.inf)
    v1 = jnp.max(el, axis=-1, keepdims=True)
    i1 = jnp.min(jnp.where(el == v1, lane, big), axis=-1, keepdims=True)
    el2 = jnp.where(lane == i1, -jnp.inf, el)
    v2 = jnp.max(el2, axis=-1, keepdims=True)
    i2 = jnp.min(jnp.where(el2 == v2, lane, big), axis=-1, keepdims=True)
    t = jnp.exp(v2 - v1)
    w1 = p_g / (1.0 + t)
    w2 = p_g * t / (1.0 + t)

    oh1 = lane == i1
    oh2 = lane == i2
    cnt = jnp.where(oh1, 1.0, 0.0) + jnp.where(oh2, 1.0, 0.0)
    r_i = lax.broadcasted_iota(I32, (tm, tm), 0)
    c_i = lax.broadcasted_iota(I32, (tm, tm), 1)
    before = jnp.where(c_i < r_i, 1.0, 0.0).astype(BF16)
    prior = jnp.dot(before, cnt.astype(BF16), preferred_element_type=F32) + run_ref[...]
    rank1 = jnp.sum(jnp.where(oh1, prior, 0.0), axis=-1, keepdims=True)
    rank2 = jnp.sum(jnp.where(oh2, prior, 0.0), axis=-1, keepdims=True)
    run = run_ref[...] + jnp.sum(cnt, axis=0, keepdims=True)
    run_ref[...] = run
    cnt_ref[...] = run

    route = jnp.zeros((tm, LANES), F32)
    for pos, val in ((ROUTE_E1, i1.astype(F32)), (ROUTE_E2, i2.astype(F32)), (ROUTE_W1, w1),
                     (ROUTE_W2, w2), (ROUTE_R1, rank1), (ROUTE_R2, rank2)):
        route = jnp.where(lane == pos, val, route)
    route_ref[...] = route
    route_t_ref[...] = route.T[:ROUTE_ROWS, :].astype(I32)


def _merge(x, oa, ob, ga, gb, wa, wb, wo, nf, wr, br):
    T = x.shape[0]
    tm = min(TOKEN_TILE, T)
    row = lambda i: (i, 0)
    const = lambda i: (0, 0)
    wspec = pl.BlockSpec((D_MODEL, D_MODEL), const)
    tile = pl.BlockSpec((tm, D_MODEL), row)
    return pl.pallas_call(
        _merge_kernel,
        grid=(T // tm,),
        in_specs=[tile, tile, tile, tile, tile, wspec, wspec, wspec,
                  pl.BlockSpec((1, D_MODEL), const),
                  pl.BlockSpec((D_MODEL, LANES), const), pl.BlockSpec((1, LANES), const)],
        out_specs=(tile, tile, pl.BlockSpec((tm, LANES), row),
                   pl.BlockSpec((ROUTE_ROWS, tm), lambda i: (0, i)), pl.BlockSpec((1, LANES), const)),
        out_shape=(jax.ShapeDtypeStruct((T, D_MODEL), F32),
                   jax.ShapeDtypeStruct((T, D_MODEL), F32),
                   jax.ShapeDtypeStruct((T, LANES), F32),
                   jax.ShapeDtypeStruct((ROUTE_ROWS, T), I32),
                   jax.ShapeDtypeStruct((1, LANES), F32)),
        scratch_shapes=[pltpu.VMEM((1, LANES), F32)],
        compiler_params=_cparams(("arbitrary",)),
        name="merge",
    )(x, oa, ob, ga, gb, wa, wb, wo, nf, wr, br)


def _slot_of(starts_ref, ridx_ref, tm, k, r):
    return starts_ref[ridx_ref[(ROUTE_E1 + k) * tm + r]] + ridx_ref[(ROUTE_R1 + k) * tm + r]


def _tile_major(ridx, tm):
    f, T = ridx.shape
    return ridx.reshape(f, T // tm, tm).transpose(1, 0, 2).reshape(f * T)


def _dispatch_kernel(ends_ref, starts_ref, ridx_ref, h2_ref, xs_ref, zero_ref, sem):
    tm = h2_ref.shape[0] * SUBLANES
    n_blocks = xs_ref.shape[0] // SLOT_BLOCK

    @pl.when(pl.program_id(0) == 0)
    def _():
        zero_ref[...] = jnp.zeros_like(zero_ref)

        def zero_block(blk):
            start = pl.multiple_of(blk * SLOT_BLOCK, SLOT_BLOCK)
            return pltpu.make_async_copy(zero_ref, xs_ref.at[pl.ds(start, SLOT_BLOCK)], sem)

        def expert_tail(e):
            return jnp.maximum((ends_ref[e] >> SLOT_SHIFT) - 1, 0)

        n_used = ends_ref[N_EXPERTS - 1] >> SLOT_SHIFT
        for e in range(N_EXPERTS):
            zero_block(expert_tail(e)).start()
        lax.fori_loop(n_used, n_blocks, lambda blk, c: (zero_block(blk).start(), c)[1], 0)
        for e in range(N_EXPERTS):
            zero_block(expert_tail(e)).wait()
        lax.fori_loop(n_used, n_blocks, lambda blk, c: (zero_block(blk).wait(), c)[1], 0)

    def row_copy(b, u, k, wait=False):
        d = 0 if wait else _slot_of(starts_ref, ridx_ref, tm, k, b * SUBLANES + u)
        return pltpu.make_async_copy(h2_ref.at[b, pl.ds(u, 1), :], xs_ref.at[pl.ds(d, 1)], sem)

    def start(b, c):
        for u in range(SUBLANES):
            row_copy(b, u, 0).start(priority=0)
            row_copy(b, u, 1).start(priority=1)
        return c

    def wait(b, c):
        for u in range(SUBLANES):
            row_copy(b, u, 0, wait=True).wait()
            row_copy(b, u, 1, wait=True).wait()
        return c

    lax.fori_loop(0, tm // SUBLANES, start, 0)
    lax.fori_loop(0, tm // SUBLANES, wait, 0)


def _dispatch(pad_ends, pad_starts, ridx, h2, n_slots):
    T = h2.shape[0]
    tm = min(TOKEN_TILE, T)
    return pl.pallas_call(
        _dispatch_kernel,
        grid=(T // tm,),
        in_specs=[pl.BlockSpec(memory_space=pltpu.SMEM), pl.BlockSpec(memory_space=pltpu.SMEM),
                  pl.BlockSpec((ROUTE_ROWS * tm,), lambda i: (i,), memory_space=pltpu.SMEM),
                  pl.BlockSpec((tm // SUBLANES, SUBLANES, D_MODEL), lambda i: (i, 0, 0))],
        out_specs=pl.BlockSpec(memory_space=pl.ANY),
        out_shape=jax.ShapeDtypeStruct((n_slots, D_MODEL), F32),
        scratch_shapes=[pltpu.VMEM((SLOT_BLOCK, D_MODEL), F32), pltpu.SemaphoreType.DMA(())],
        compiler_params=_cparams(("arbitrary",)),
        name="dispatch",
    )(pad_ends, pad_starts, ridx, h2.reshape(T // SUBLANES, SUBLANES, D_MODEL))


def _experts_kernel(be_ref, nu_ref, xs_ref, wg_ref, wu_ref, wd_ref, y_ref):
    del be_ref
    used = pl.program_id(0) < nu_ref[0]

    @pl.when(used)
    def _():
        x = xs_ref[...].astype(BF16)
        g = jnp.dot(x, wg_ref[0].astype(BF16), preferred_element_type=F32)
        u = jnp.dot(x, wu_ref[0].astype(BF16), preferred_element_type=F32)
        a = (g * jax.nn.sigmoid(g) * u).astype(BF16)
        y_ref[...] = jnp.dot(a, wd_ref[0].astype(BF16), preferred_element_type=F32)

    @pl.when(jnp.logical_not(used))
    def _():
        y_ref[...] = jnp.zeros_like(y_ref)


def _experts(block_e, n_used, xs, wg, wu, wd):
    n_blocks = xs.shape[0] // SLOT_BLOCK
    last = lambda i, nu: jnp.minimum(i, jnp.maximum(nu[0] - 1, 0))
    blk = lambda i, be, nu: (last(i, nu), 0)
    wmap = lambda i, be, nu: (be[last(i, nu)], 0, 0)
    grid_spec = pltpu.PrefetchScalarGridSpec(
        num_scalar_prefetch=2,
        grid=(n_blocks,),
        in_specs=[pl.BlockSpec((SLOT_BLOCK, D_MODEL), blk),
                  pl.BlockSpec((1, D_MODEL, D_EXPERT), wmap),
                  pl.BlockSpec((1, D_MODEL, D_EXPERT), wmap),
                  pl.BlockSpec((1, D_EXPERT, D_MODEL), wmap)],
        out_specs=pl.BlockSpec((SLOT_BLOCK, D_MODEL), lambda i, be, nu: (i, 0)),
    )
    return pl.pallas_call(
        _experts_kernel,
        grid_spec=grid_spec,
        out_shape=jax.ShapeDtypeStruct((xs.shape[0], D_MODEL), F32),
        compiler_params=_cparams(("arbitrary",)),
        name="experts",
    )(block_e, n_used, xs, wg, wu, wd)


def _combine_kernel(starts_ref, ridx_ref, ridx_next_ref, x1_ref, route_ref, nf_ref, yb_ref, y_ref,
                    g_ref, sem):
    i = pl.program_id(0)
    n = pl.num_programs(0)
    tm = x1_ref.shape[0]
    slot = i & 1

    def row_copy(d_ref, s, b, u, k, wait):
        d = 0 if wait else _slot_of(starts_ref, d_ref, tm, k, b * SUBLANES + u)
        return pltpu.make_async_copy(yb_ref.at[pl.ds(d, 1)], g_ref.at[s, k, b, pl.ds(u, 1), :],
                                     sem.at[s])

    def gather(d_ref, s, wait):
        def rows(b, c):
            for u in range(SUBLANES):
                for k in range(2):
                    cp = row_copy(d_ref, s, b, u, k, wait)
                    cp.wait() if wait else cp.start(priority=k)
            return c
        lax.fori_loop(0, tm // SUBLANES, rows, 0)

    @pl.when(i == 0)
    def _():
        gather(ridx_ref, slot, False)

    @pl.when(i + 1 < n)
    def _():
        gather(ridx_next_ref, 1 - slot, False)

    gather(ridx_ref, slot, True)
    route = route_ref[...]
    w1 = route[:, ROUTE_W1:ROUTE_W1 + 1]
    w2 = route[:, ROUTE_W2:ROUTE_W2 + 1]
    rows_of = lambda k: g_ref[slot, k].reshape(tm, D_MODEL)
    x2 = x1_ref[...] + (rows_of(0) * w1 + rows_of(1) * w2)
    y_ref[...] = _rms(x2, nf_ref[...])


def _combine(pad_starts, ridx, x1, route, nfinal, yb):
    T = x1.shape[0]
    tm = min(TOKEN_TILE, T)
    n = T // tm
    row = lambda i: (i, 0)
    return pl.pallas_call(
        _combine_kernel,
        grid=(n,),
        in_specs=[pl.BlockSpec(memory_space=pltpu.SMEM),
                  pl.BlockSpec((ROUTE_ROWS * tm,), lambda i: (i,), memory_space=pltpu.SMEM),
                  pl.BlockSpec((ROUTE_ROWS * tm,), lambda i: (jnp.minimum(i + 1, n - 1),),
                               memory_space=pltpu.SMEM),
                  pl.BlockSpec((tm, D_MODEL), row), pl.BlockSpec((tm, LANES), row),
                  pl.BlockSpec((1, D_MODEL), lambda i: (0, 0)),
                  pl.BlockSpec(memory_space=pl.ANY)],
        out_specs=pl.BlockSpec((tm, D_MODEL), row),
        out_shape=jax.ShapeDtypeStruct((T, D_MODEL), F32),
        scratch_shapes=[pltpu.VMEM((2, 2, tm // SUBLANES, SUBLANES, D_MODEL), F32),
                        pltpu.SemaphoreType.DMA((2,))],
        compiler_params=_cparams(("arbitrary",)),
        name="combine",
    )(pad_starts, ridx, ridx, x1, route, nfinal, yb)


def _moe(x1, h2, route, route_t, counts, wg, wu, wd, nfinal):
    T = x1.shape[0]
    n_slots = T * 2 + N_EXPERTS * SLOT_BLOCK
    n_blocks = n_slots // SLOT_BLOCK
    cnt = counts[0, :N_EXPERTS].astype(I32)
    padded = (cnt + SLOT_BLOCK - 1) // SLOT_BLOCK * SLOT_BLOCK
    pad_ends = jnp.cumsum(padded)
    pad_starts = (pad_ends - padded).astype(I32)
    pad_ends = pad_ends.astype(I32)
    block_start = jnp.arange(n_blocks, dtype=I32) * SLOT_BLOCK
    block_e = jnp.sum((block_start[:, None] >= pad_ends[None, :]).astype(I32), axis=1)
    block_e = jnp.minimum(block_e, N_EXPERTS - 1)
    n_used = pad_ends[-1:] // SLOT_BLOCK
    ridx = _tile_major(route_t, min(TOKEN_TILE, T))
    xs = _dispatch(pad_ends, pad_starts, ridx, h2, n_slots)
    yb = _experts(block_e, n_used, xs, wg, wu, wd)
    return _combine(pad_starts, ridx, x1, route, nfinal, yb)


def _layer(x, s0, k_past, v_past, p, B, L):
    row2 = lambda a: a.reshape(1, -1)
    qka, va, ra, qb, kb, vb, ga, gb, da, k5, v5 = _in_proj(x, row2(p["norm_mix"]), p["w_in"])
    oa, s_new = _gla(qka, va, ra, da, p["w_decay"], row2(p["b_decay"]), row2(p["gla_norm"]),
                     s0, B, L)
    lams = [row2(p[n]) for n in ("lambda_q1", "lambda_k1", "lambda_q2", "lambda_k2")]
    dn = row2(p["diff_norm"])
    if k_past is None:
        ob = _attn_prompt(qb, kb, vb, *lams, dn, B, L)
    else:
        ob = _attn_sample(qb, k_past, v_past, kb, vb, *lams, dn, B, L)
    x1, h2, route, route_t, counts = _merge(x, oa, ob, ga, gb, p["w_proj_a"], p["w_proj_b"],
                                            p["w_out"], row2(p["norm_ffn"]), p["w_router"],
                                            p["b_router"])
    y = _moe(x1, h2, route, route_t, counts, p["w_gate"], p["w_up"], p["w_down"],
             row2(p["norm_final"]))
    return y, s_new, k5, v5


def kernel(x_prompt, x_sample, cache_k, cache_v, state_gla, norm_mix, w_in, w_decay, b_decay,
           gla_norm, w_proj_a, lambda_q1, lambda_k1, lambda_q2, lambda_k2, diff_norm, w_proj_b,
           w_out, norm_ffn, w_router_group, b_router_group, w_router_expert, b_router_expert,
           w_gate, w_up, w_down, norm_final):
    B, L, D = x_prompt.shape
    Bs, Ls, _ = x_sample.shape
    w_router = jnp.concatenate(
        [w_router_expert[0], w_router_group[0],
         jnp.zeros((D, LANES - N_EXPERTS - N_GROUPS), F32)], axis=1)
    b_router = jnp.concatenate(
        [b_router_expert[0], b_router_group[0],
         jnp.zeros((LANES - N_EXPERTS - N_GROUPS,), F32)]).reshape(1, LANES)
    p = dict(
        norm_mix=norm_mix[0], w_in=_prep_w_in(w_in[0]),
        w_decay=jnp.pad(w_decay[0], ((0, LANES - GLA_RANK), (0, 0))), b_decay=b_decay[0],
        gla_norm=gla_norm[0], w_proj_a=w_proj_a[0].astype(BF16),
        lambda_q1=lambda_q1[0], lambda_k1=lambda_k1[0], lambda_q2=lambda_q2[0],
        lambda_k2=lambda_k2[0], diff_norm=diff_norm[0], w_proj_b=w_proj_b[0].astype(BF16),
        w_out=w_out[0].astype(BF16), norm_ffn=norm_ffn[0], w_router=w_router, b_router=b_router,
        w_gate=w_gate[0], w_up=w_up[0], w_down=w_down[0],
        norm_final=norm_final)

    s0p = jnp.zeros((B, GLA_HEADS, GLA_DK, GLA_DV), F32)
    yp, sp, kp, vp = _layer(x_prompt.reshape(B * L, D), s0p, None, None, p, B, L)
    ys, ss, ks, vs = _layer(x_sample.reshape(Bs * Ls, D), state_gla[0], cache_k, cache_v,
                            p, Bs, Ls)
    kv = lambda a, b, l: a.reshape(1, b, l, DIFF_HEADS, DIFF_DV)
    return (yp.reshape(B, L, D), ys.reshape(Bs, Ls, D), kv(kp, B, L), kv(vp, B, L), sp[None],
            kv(ks, Bs, Ls), kv(vs, Bs, Ls), ss[None])
```

```python
import functools

import jax
import jax.numpy as jnp
from jax import lax
from jax.experimental import pallas as pl
from jax.experimental.pallas import tpu as pltpu

F32 = jnp.float32
BF16 = jnp.bfloat16
I32 = jnp.int32

D_MODEL = 1024
CHUNK = 64
CHUNK_SHIFT = 6
EPS = 1e-6
GLA_HEADS = 4
GLA_DK = 128
GLA_DV = 256
GLA_RANK = 16
GLA_TAU = 16.0
DIFF_HEADS = 8
DIFF_DH = 64
DIFF_DV = 128
N_GROUPS = 4
EXPERTS_PER_GROUP = 4
EPG_SHIFT = 2
N_EXPERTS = 16
D_EXPERT = 512
LAM_INIT = 0.8 - 0.6

LANES = 128
SUBLANES = 8
TOKEN_TILE = 512
SLOT_BLOCK = 256
SLOT_SHIFT = 8
ATT_TILE = 512
ATT_HEADS_PER_STEP = 2
SAMPLE_KEY_CHUNK = 1024
GLA_CUMSUM_ROWS = 256
VMEM_LIMIT = 56 * 1024 * 1024
NEG = -1e30

HI = lax.Precision.HIGHEST


def _cparams(sem):
    return pltpu.CompilerParams(dimension_semantics=sem, vmem_limit_bytes=VMEM_LIMIT)


def _nt(a, b):
    return lax.dot_general(a, b, (((1,), (1,)), ((), ())), preferred_element_type=F32)


def _tn(a, b):
    return lax.dot_general(a, b, (((0,), (0,)), ((), ())), preferred_element_type=F32)


def _rms(x, g):
    return x * lax.rsqrt(jnp.mean(x * x, axis=-1, keepdims=True) + EPS) * g


def _inproj_kernel(x_ref, g_ref, wa_ref, wda_ref, wb_ref, qka_ref, va_ref, ra_ref, qb_ref, kb_ref,
                   vb_ref, ga_ref, gb_ref, da_ref, k5_ref, v5_ref, stage_ref, sem):
    i = pl.program_id(0)
    n = pl.num_programs(0)
    tm = x_ref.shape[0]
    slot = i & 1

    def head_copy(s, which, hd):
        dst = (k5_ref, v5_ref)[which]
        return pltpu.make_async_copy(stage_ref.at[s, which, hd],
                                     dst.at[pl.ds(i * tm, tm), hd, :], sem.at[s])

    def for_all(s, fn):
        for which in range(2):
            for hd in range(DIFF_HEADS):
                fn(head_copy(s, which, hd))

    @pl.when(i >= 2)
    def _():
        for_all(slot, lambda cp: cp.wait())

    h = _rms(x_ref[...], g_ref[...]).astype(BF16)
    outs = ((qka_ref, wa_ref, 0), (va_ref, wa_ref, 1), (ra_ref, wa_ref, 2), (qb_ref, wb_ref, 0),
            (kb_ref, wb_ref, 1), (vb_ref, wb_ref, 2), (ga_ref, wb_ref, 3), (gb_ref, wb_ref, 4))
    for o, w_ref, c in outs:
        z = jnp.dot(h, w_ref[:, c * D_MODEL:(c + 1) * D_MODEL], preferred_element_type=F32)
        o[...] = z.astype(o.dtype)
        which = 0 if o is kb_ref else 1 if o is vb_ref else None
        if which is not None:
            for hd in range(DIFF_HEADS):
                stage_ref[slot, which, hd] = z[:, hd * DIFF_DV:(hd + 1) * DIFF_DV]
    da_ref[...] = jnp.dot(h, wda_ref[...], preferred_element_type=F32)
    for_all(slot, lambda cp: cp.start())

    @pl.when(i == n - 1)
    def _():
        for_all(slot, lambda cp: cp.wait())

    @pl.when((i == n - 1) & (i >= 1))
    def _():
        for_all(1 - slot, lambda cp: cp.wait())


N_GLA_COLS = 3 * D_MODEL
N_DIFF_COLS = 5 * D_MODEL


def _prep_w_in(w_in):
    w = w_in.astype(BF16)
    return w, w[:, N_GLA_COLS + GLA_RANK:]


def _in_proj(x, g, w):
    w_all, w_diff = w
    T = x.shape[0]
    tm = min(TOKEN_TILE, T)
    resident = lambda cols, j: pl.BlockSpec((D_MODEL, cols), lambda i: (0, j),
                                            pipeline_mode=pl.Buffered(1))
    row = lambda i: (i, 0)
    const = lambda i: (0, 0)
    wide = lambda dt: jax.ShapeDtypeStruct((T, D_MODEL), dt)
    heads = jax.ShapeDtypeStruct((T, DIFF_HEADS, DIFF_DV), F32)
    out_shape = (wide(BF16),) * 8 + (jax.ShapeDtypeStruct((T, LANES), F32), heads, heads)
    out_specs = tuple([pl.BlockSpec((tm, D_MODEL), row)] * 8 + [pl.BlockSpec((tm, LANES), row)]
                      + [pl.BlockSpec(memory_space=pl.ANY)] * 2)
    return pl.pallas_call(
        _inproj_kernel,
        grid=(T // tm,),
        in_specs=[pl.BlockSpec((tm, D_MODEL), row),
                  pl.BlockSpec((1, D_MODEL), const),
                  resident(N_GLA_COLS, 0), resident(LANES, N_GLA_COLS // LANES),
                  resident(N_DIFF_COLS, 0)],
        out_specs=out_specs,
        out_shape=out_shape,
        scratch_shapes=[pltpu.VMEM((2, 2, DIFF_HEADS, tm, DIFF_DV), F32),
                        pltpu.SemaphoreType.DMA((2,))],
        compiler_params=_cparams(("arbitrary",)),
        name="in_proj",
    )(x, g, w_all, w_all, w_diff)


def _gla_kernel(qka_ref, va_ref, ra_ref, da_ref, wd_ref, bd_ref, gn_ref, s0_ref,
                oa_ref, sout_ref, s_ref, la_ref, *, chunk, n_chunks):
    l = pl.program_id(1)

    @pl.when(l == 0)
    def _():
        s_ref[...] = s0_ref[0]

    split = lambda a: (a.astype(BF16), (a - a.astype(BF16).astype(F32)).astype(BF16))
    n_k = GLA_HEADS * GLA_DK
    fold = lambda z: z[:, :n_k] + z[:, n_k:]

    wd_pair = jnp.concatenate(split(wd_ref[...]), axis=1)
    x = fold(sum(jnp.dot(a, wd_pair, preferred_element_type=F32) for a in split(da_ref[...])))
    x = x + bd_ref[...]
    log_a = (jnp.minimum(x, 0.0) - jnp.log1p(jnp.exp(-jnp.abs(x)))) * (1.0 / GLA_TAU)
    lb = log_a.shape[0]
    grp = min(lb, GLA_CUMSUM_ROWS)
    shift = chunk.bit_length() - 1
    r_b = lax.broadcasted_iota(I32, (grp, grp), 0)
    c_b = lax.broadcasted_iota(I32, (grp, grp), 1)
    tri = jnp.where((c_b <= r_b) & ((c_b >> shift) == (r_b >> shift)), 1.0, 0.0).astype(BF16)
    for g in range(lb // grp):
        rows = slice(g * grp, (g + 1) * grp)
        la_ref[rows, :] = fold(jnp.dot(tri, jnp.concatenate(split(log_a[rows, :]), axis=1),
                                       preferred_element_type=F32))

    r_i = lax.broadcasted_iota(I32, (chunk, chunk), 0)
    c_i = lax.broadcasted_iota(I32, (chunk, chunk), 1)
    causal = c_i <= r_i
    gn = gn_ref[...]
    scale = GLA_DK ** -0.5
    heads = range(GLA_HEADS)
    vcols = [slice(h * GLA_DV, (h + 1) * GLA_DV) for h in heads]

    @pl.loop(0, n_chunks)
    def _(c):
        rows = pl.ds(pl.multiple_of(c * chunk, chunk), chunk)
        b_all = la_ref[rows, :]
        q_t, k_t, k_end, decay = [], [], [], []
        for h in heads:
            b = b_all[:, h * GLA_DK:(h + 1) * GLA_DK]
            b_last = b[chunk - 1:chunk, :]
            q = qka_ref[rows, h * GLA_DK:(h + 1) * GLA_DK].astype(F32) * scale
            k = qka_ref[rows, n_k + h * GLA_DK:n_k + (h + 1) * GLA_DK].astype(F32)
            q_t.append((q * jnp.exp(b)).astype(BF16))
            k_t.append((k * jnp.exp(-b)).astype(BF16))
            k_end.append((k * jnp.exp(b_last - b)).astype(BF16))
            decay.append(jnp.broadcast_to(jnp.exp(b_last), (GLA_DK, GLA_DK)).T)
        s_old = [s_ref[h] for h in heads]
        o_state = [jnp.dot(q_t[h], s_old[h].astype(BF16), preferred_element_type=F32) for h in heads]
        att = [_nt(q_t[h], k_t[h]) for h in heads]
        kv = [_tn(k_end[h], va_ref[rows, vcols[h]]) for h in heads]
        for h in heads:
            a = jnp.where(causal, att[h], 0.0).astype(BF16)
            o = o_state[h] + jnp.dot(a, va_ref[rows, vcols[h]], preferred_element_type=F32)
            s_ref[h] = jnp.concatenate([decay[h], decay[h]], axis=1) * s_old[h] + kv[h]
            r = ra_ref[rows, vcols[h]].astype(F32)
            oa_ref[rows, vcols[h]] = (_rms(o, gn) * (r * jax.nn.sigmoid(r))).astype(oa_ref.dtype)

    @pl.when(l == pl.num_programs(1) - 1)
    def _():
        sout_ref[0] = s_ref[...]


def _gla(qka, va, ra, da, wd, bd, gn, s0, B, L):
    chunk = min(CHUNK, L)
    lb = min(TOKEN_TILE, L)
    nl = L // lb
    row = lambda b, l: (b * nl + l, 0)
    const2 = lambda b, l: (0, 0)
    st = lambda b, l: (b, 0, 0, 0)
    kern = functools.partial(_gla_kernel, chunk=chunk, n_chunks=lb // chunk)
    return pl.pallas_call(
        kern,
        grid=(B, nl),
        in_specs=[pl.BlockSpec((lb, D_MODEL), row), pl.BlockSpec((lb, D_MODEL), row),
                  pl.BlockSpec((lb, D_MODEL), row), pl.BlockSpec((lb, LANES), row),
                  pl.BlockSpec((LANES, GLA_HEADS * GLA_DK), const2),
                  pl.BlockSpec((1, GLA_HEADS * GLA_DK), const2),
                  pl.BlockSpec((1, GLA_DV), const2),
                  pl.BlockSpec((1, GLA_HEADS, GLA_DK, GLA_DV), st)],
        out_specs=(pl.BlockSpec((lb, D_MODEL), row),
                   pl.BlockSpec((1, GLA_HEADS, GLA_DK, GLA_DV), st)),
        out_shape=(jax.ShapeDtypeStruct((B * L, D_MODEL), BF16),
                   jax.ShapeDtypeStruct((B, GLA_HEADS, GLA_DK, GLA_DV), F32)),
        scratch_shapes=[pltpu.VMEM((GLA_HEADS, GLA_DK, GLA_DV), F32),
                        pltpu.VMEM((lb, GLA_HEADS * GLA_DK), F32)],
        compiler_params=_cparams(("arbitrary", "arbitrary")),
        name="gla",
    )(qka, va, ra, da, wd, bd, gn, s0)


def _lam(lq1, lk1, lq2, lk2):
    a = jnp.sum(lq1[...] * lk1[...], axis=-1, keepdims=True)
    b = jnp.sum(lq2[...] * lk2[...], axis=-1, keepdims=True)
    return jnp.exp(a) - jnp.exp(b) + LAM_INIT


def _split_q(q):
    lane = lax.broadcasted_iota(I32, q.shape, 1)
    qs = q * jnp.asarray(DIFF_DH ** -0.5, q.dtype)
    zero = jnp.zeros_like(qs)
    return jnp.where(lane < DIFF_DH, qs, zero), jnp.where(lane >= DIFF_DH, qs, zero)


def _alibi_slopes():
    return jnp.asarray([2.0 ** (-8.0 * (h + 1) / DIFF_HEADS) for h in range(DIFF_HEADS)], F32)


def _attn_prompt_kernel(slope_ref, q_ref, k_ref, v_ref, lq1, lk1, lq2, lk2, dn_ref, o_ref,
                        k1_ref, k2_ref, vt_ref, geo_ref, *, tile, nh):
    qi = pl.program_id(2)
    n_kv = vt_ref.shape[1]
    n_cols = tile // LANES
    lane_k = lax.broadcasted_iota(I32, (tile, DIFF_DV), 1)
    row_i = lax.broadcasted_iota(I32, (tile, DIFF_DV), 0)
    row_lo = (row_i & 255).astype(F32)
    row_hi = (row_i & -256).astype(F32)
    slopes = [slope_ref[pl.program_id(1) * nh + g] for g in range(nh)]
    hcols = [slice(g * DIFF_DV, (g + 1) * DIFF_DV) for g in range(nh)]

    @pl.when(qi == 0)
    def _():
        for g in range(nh):
            kfeat = jnp.where(lane_k == DIFF_DH, slopes[g] * row_lo,
                              jnp.where(lane_k == DIFF_DH + 1, slopes[g] * row_hi,
                                        jnp.where((lane_k == DIFF_DH + 2) | (lane_k == DIFF_DH + 3),
                                                  1.0, 0.0)))
            for c in range(n_kv):
                rows = slice(c * tile, (c + 1) * tile)
                kf = k_ref[rows, hcols[g]].astype(F32)
                k1_ref[g, rows, :] = jnp.where(lane_k < DIFF_DH, kf, kfeat).astype(BF16)
                k2_ref[g, rows, :] = jnp.where(lane_k < DIFF_DH, pltpu.roll(kf, DIFF_DH, 1),
                                               kfeat).astype(BF16)
                vt_ref[g, c] = v_ref[rows, hcols[g]].astype(F32).T.astype(BF16)

    qs = []
    for g in range(nh):
        qf = q_ref[:, hcols[g]].astype(F32) * (DIFF_DH ** -0.5)
        qfeat = jnp.where((lane_k == DIFF_DH) | (lane_k == DIFF_DH + 1), 1.0,
                          jnp.where(lane_k == DIFF_DH + 2, -slopes[g] * row_lo,
                                    jnp.where(lane_k == DIFF_DH + 3, -slopes[g] * row_hi, 0.0)))
        qs.append((jnp.where(lane_k < DIFF_DH, qf, qfeat).T.astype(BF16),
                   jnp.where(lane_k < DIFF_DH, pltpu.roll(qf, DIFF_DH, 1), qfeat).T.astype(BF16)))

    n_maps = 2 * nh
    n_stats = 2 * n_cols + 1

    def scores(i, j):
        rows = pl.ds(pl.multiple_of(j * tile, tile), tile)
        k_ref_i = k1_ref if i % 2 == 0 else k2_ref
        return jnp.dot(k_ref_i[i // 2, rows, :], qs[i // 2][i % 2], preferred_element_type=F32)

    def softmax_part(stats, s, c, fix):
        out, alphas, ps = (), [], []
        for col in range(n_cols):
            lanes = slice(col * LANES, (col + 1) * LANES)
            n_keys = tile if fix is None else (col + 1) * LANES
            sh = s[:n_keys, lanes]
            if fix is not None:
                sh = sh + fix[:n_keys, lanes]
            m, l = stats[2 * col], stats[2 * col + 1]
            m_new = jnp.maximum(m, jnp.max(sh, axis=0, keepdims=True) + c)
            alpha = jnp.exp(m - m_new)
            p = jnp.exp(sh - (m_new - c))
            out += (m_new, alpha * l + jnp.sum(p, axis=0, keepdims=True))
            p = p.astype(BF16)
            alphas.append(jnp.broadcast_to(alpha, (DIFF_DV, LANES)))
            if n_keys < tile:
                p = jnp.concatenate([p, jnp.zeros((tile - n_keys, LANES), BF16)], axis=0)
            ps.append(p)
        return out, jnp.concatenate(alphas, axis=1), jnp.concatenate(ps, axis=1)

    def step(stats, j, cs, fixes):
        ss = [scores(i, j) for i in range(n_maps)]
        parts = [softmax_part(stats[n_stats * i:n_stats * (i + 1)], ss[i], cs[i // 2],
                              None if fixes is None else fixes[i // 2])
                 for i in range(n_maps)]
        out = ()
        for i, (st, alpha, p) in enumerate(parts):
            a = alpha * stats[n_stats * i + n_stats - 1] + jnp.dot(
                vt_ref[i // 2, j], p, preferred_element_type=F32)
            out += st + (a,)
        return out

    def body(j, stats):
        off = jnp.full((1, LANES), (j - qi) * tile, I32).astype(F32)
        return step(stats, j, [off * slopes[g] for g in range(nh)], None)

    row = lambda v: jnp.full((1, LANES), v, F32)
    init = ((row(NEG), row(0.0)) * n_cols + (jnp.zeros((DIFF_DV, tile), F32),)) * n_maps
    stats = lax.fori_loop(0, qi, body, init)

    @pl.when((pl.program_id(0) == 0) & (pl.program_id(1) == 0) & (qi == 0))
    def _():
        r_i = lax.broadcasted_iota(I32, (tile, tile), 0)
        c_i = lax.broadcasted_iota(I32, (tile, tile), 1)
        allowed = (r_i >> CHUNK_SHIFT) <= (c_i >> CHUNK_SHIFT)
        geo_ref[0] = jnp.where(r_i > c_i, (c_i - r_i).astype(F32), 0.0)
        geo_ref[1] = jnp.where(allowed, 0.0, NEG)

    fixes = [(2.0 * slopes[g]) * geo_ref[0] + geo_ref[1] for g in range(nh)]
    stats = step(stats, qi, [row(0.0)] * nh, fixes)
    lam = _lam(lq1, lk1, lq2, lk2)
    for g in range(nh):
        res = []
        for i in (2 * g, 2 * g + 1):
            st = stats[n_stats * i:n_stats * (i + 1)]
            l = jnp.concatenate([jnp.broadcast_to(st[2 * col + 1], (DIFF_DV, LANES))
                                 for col in range(n_cols)], axis=1)
            res.append(st[-1] / l)
        o = (res[0] - lam * res[1]).T
        o_ref[:, hcols[g]] = (_rms(o, dn_ref[...]) * (1.0 - LAM_INIT)).astype(o_ref.dtype)


def _attn_prompt(qb, kb, vb, lq1, lk1, lq2, lk2, dn, B, L):
    tile = min(ATT_TILE, L)
    nq = L // tile
    nh = ATT_HEADS_PER_STEP
    qmap = lambda b, h, i: (b * nq + i, h)
    kvmap = lambda b, h, i: (b, h)
    cmap = lambda b, h, i: (0, 0)
    lspec = pl.BlockSpec((1, DIFF_DH), cmap)
    return pl.pallas_call(
        functools.partial(_attn_prompt_kernel, tile=tile, nh=nh),
        grid=(B, DIFF_HEADS // nh, nq),
        in_specs=[pl.BlockSpec(memory_space=pltpu.SMEM),
                  pl.BlockSpec((tile, nh * DIFF_DV), qmap),
                  pl.BlockSpec((L, nh * DIFF_DV), kvmap), pl.BlockSpec((L, nh * DIFF_DV), kvmap),
                  lspec, lspec, lspec, lspec, pl.BlockSpec((1, DIFF_DV), cmap)],
        out_specs=pl.BlockSpec((tile, nh * DIFF_DV), qmap),
        out_shape=jax.ShapeDtypeStruct((B * L, D_MODEL), BF16),
        scratch_shapes=[pltpu.VMEM((nh, L, DIFF_DV), BF16), pltpu.VMEM((nh, L, DIFF_DV), BF16),
                        pltpu.VMEM((nh, nq, DIFF_DV, tile), BF16),
                        pltpu.VMEM((2, tile, tile), F32)],
        compiler_params=_cparams(("arbitrary", "arbitrary", "arbitrary")),
        name="attn_prompt",
    )(_alibi_slopes(), qb, kb, vb, lq1, lk1, lq2, lk2, dn)


def _attn_sample_kernel(slope_ref, q_ref, kp_ref, vp_ref, kn_ref, vn_ref, lq1, lk1, lq2, lk2,
                        dn_ref, o_ref, m_ref, l_ref, acc_ref, kbuf, vbuf, sem, *, past, lq, chunk):
    b = pl.program_id(0)
    c = pl.program_id(1)
    n_c = pl.num_programs(1)
    step = b * n_c + c
    slot = step & 1

    def slab_copies(bb, cc, s, fn):
        rows = pl.ds(cc * chunk, chunk)
        for hd in range(DIFF_HEADS):
            fn(pltpu.make_async_copy(kp_ref.at[0, bb, rows, hd, :], kbuf.at[s, hd], sem.at[s]))
            fn(pltpu.make_async_copy(vp_ref.at[0, bb, rows, hd, :], vbuf.at[s, hd], sem.at[s]))

    @pl.when(step == 0)
    def _():
        slab_copies(b, c, slot, lambda cp: cp.start())

    @pl.when(step + 1 < pl.num_programs(0) * n_c)
    def _():
        wrap = c + 1 == n_c
        slab_copies(jnp.where(wrap, b + 1, b), jnp.where(wrap, 0, c + 1), 1 - slot,
                    lambda cp: cp.start())

    slab_copies(b, c, slot, lambda cp: cp.wait())

    @pl.when(c == 0)
    def _():
        m_ref[...] = jnp.full_like(m_ref, NEG)
        l_ref[...] = jnp.zeros_like(l_ref)
        acc_ref[...] = jnp.zeros_like(acc_ref)

    def geometry(key0, n_keys):
        r = lax.broadcasted_iota(I32, (2 * lq, n_keys), 0)
        qpos = past + jnp.where(r >= lq, r - lq, r)
        kpos = key0 + lax.broadcasted_iota(I32, (2 * lq, n_keys), 1)
        allowed = (kpos >> CHUNK_SHIFT) <= (qpos >> CHUNK_SHIFT)
        return jnp.abs(qpos - kpos).astype(F32), jnp.where(allowed, 0.0, NEG)

    def block(hd, k, v, geo):
        dist, mask = geo
        q1, q2 = _split_q(q_ref[:, hd * DIFF_DV:(hd + 1) * DIFF_DV])
        s = _nt(jnp.concatenate([q1, q2], axis=0), k)
        s = s + (mask - slope_ref[hd] * dist)
        m_old = m_ref[hd]
        m_new = jnp.maximum(m_old, jnp.max(s, axis=-1, keepdims=True))
        alpha = jnp.exp(m_old - m_new)
        p = jnp.exp(s - m_new)
        l_ref[hd] = alpha * l_ref[hd] + jnp.sum(p, axis=-1, keepdims=True)
        acc_ref[hd] = alpha * acc_ref[hd] + jnp.dot(p.astype(BF16), v, preferred_element_type=F32)
        m_ref[hd] = m_new

    geo = geometry(c * chunk, chunk)
    for hd in range(DIFF_HEADS):
        block(hd, kbuf[slot, hd].astype(BF16), vbuf[slot, hd].astype(BF16), geo)

    @pl.when(c == n_c - 1)
    def _():
        lam = _lam(lq1, lk1, lq2, lk2)
        geo_new = geometry(past, lq)
        for hd in range(DIFF_HEADS):
            cols = slice(hd * DIFF_DV, (hd + 1) * DIFF_DV)
            block(hd, kn_ref[:, cols], vn_ref[:, cols], geo_new)
            o = acc_ref[hd] / l_ref[hd]
            o = o[:lq] - lam * o[lq:]
            o_ref[:, hd * DIFF_DV:(hd + 1) * DIFF_DV] = (
                _rms(o, dn_ref[...]) * (1.0 - LAM_INIT)).astype(o_ref.dtype)


def _attn_sample(qb, cache_k, cache_v, kb, vb, lq1, lk1, lq2, lk2, dn, B, L):
    past = cache_k.shape[2]
    chunk = min(SAMPLE_KEY_CHUNK, past)
    n_chunks = past // chunk
    bmap = lambda b, c: (b, 0)
    cmap = lambda b, c: (0, 0)
    lspec = pl.BlockSpec((1, DIFF_DH), cmap)
    tok = pl.BlockSpec((L, D_MODEL), bmap)
    cache = pl.BlockSpec(memory_space=pl.ANY)
    slabs = pltpu.VMEM((2, DIFF_HEADS, chunk, DIFF_DV), F32)
    return pl.pallas_call(
        functools.partial(_attn_sample_kernel, past=past, lq=L, chunk=chunk),
        grid=(B, n_chunks),
        in_specs=[pl.BlockSpec(memory_space=pltpu.SMEM), tok, cache, cache, tok, tok,
                  lspec, lspec, lspec, lspec, pl.BlockSpec((1, DIFF_DV), cmap)],
        out_specs=tok,
        out_shape=jax.ShapeDtypeStruct((B * L, D_MODEL), BF16),
        scratch_shapes=[pltpu.VMEM((DIFF_HEADS, 2 * L, 1), F32),
                        pltpu.VMEM((DIFF_HEADS, 2 * L, 1), F32),
                        pltpu.VMEM((DIFF_HEADS, 2 * L, DIFF_DV), F32),
                        slabs, slabs, pltpu.SemaphoreType.DMA((2,))],
        compiler_params=_cparams(("arbitrary", "arbitrary")),
        name="attn_sample",
    )(_alibi_slopes(), qb, cache_k, cache_v, kb, vb, lq1, lk1, lq2, lk2, dn)


ROUTE_E1, ROUTE_E2, ROUTE_W1, ROUTE_W2, ROUTE_R1, ROUTE_R2, ROUTE_P1, ROUTE_P2 = range(8)
ROUTE_ROWS = 8
TABLE_RUN, TABLE_OFF, TABLE_BASE = range(3)
GROUP_LANE0 = N_EXPERTS


def _merge_kernel(x_ref, oa_ref, ob_ref, ga_ref, gb_ref, wa_ref, wb_ref, wo_ref, nf_ref,
                  wr_ref, br_ref, x1_ref, h2_ref, route_ref, route_t_ref, cnt_ref, tables_ref,
                  run_ref):
    i = pl.program_id(0)

    @pl.when(i == 0)
    def _():
        run_ref[...] = jnp.zeros_like(run_ref)

    u_a = jnp.dot(oa_ref[...], wa_ref[...], preferred_element_type=F32)
    u_b = jnp.dot(ob_ref[...], wb_ref[...], preferred_element_type=F32)
    mix = (jax.nn.sigmoid(ga_ref[...].astype(F32)) * u_a
           + jax.nn.sigmoid(gb_ref[...].astype(F32)) * u_b)
    x1 = x_ref[...] + jnp.dot(mix.astype(BF16), wo_ref[...], preferred_element_type=F32)
    x1_ref[...] = x1
    h2 = _rms(x1, nf_ref[...])
    h2_ref[...] = h2

    tm = h2.shape[0]
    split = lambda a: (a.astype(BF16), (a - a.astype(BF16).astype(F32)).astype(BF16))
    w_pair = jnp.concatenate(split(wr_ref[...]), axis=1)
    parts = sum(jnp.dot(a, w_pair, preferred_element_type=F32) for a in split(h2))
    logits = parts[:, :LANES] + parts[:, LANES:] + br_ref[...]
    lane = lax.broadcasted_iota(I32, (tm, LANES), 1)
    big = jnp.int32(LANES)
    g_mask = (lane >= GROUP_LANE0) & (lane < GROUP_LANE0 + N_GROUPS)
    gl = jnp.where(g_mask, logits, -jnp.inf)
    gmax = jnp.max(gl, axis=-1, keepdims=True)
    g_sel = jnp.min(jnp.where(gl == gmax, lane - GROUP_LANE0, big), axis=-1, keepdims=True)
    p_g = 1.0 / jnp.sum(jnp.where(g_mask, jnp.exp(logits - gmax), 0.0), axis=-1, keepdims=True)
    e_mask = (lane < N_EXPERTS) & ((lane >> EPG_SHIFT) == g_sel)
    el = jnp.where(e_mask, logits, -jnp.inf)
    v1 = jnp.max(el, axis=-1, keepdims=True)
    i1 = jnp.min(jnp.where(el == v1, lane, big), axis=-1, keepdims=True)
    el2 = jnp.where(lane == i1, -jnp.inf, el)
    v2 = jnp.max(el2, axis=-1, keepdims=True)
    i2 = jnp.min(jnp.where(el2 == v2, lane, big), axis=-1, keepdims=True)
    t = jnp.exp(v2 - v1)
    w1 = p_g / (1.0 + t)
    w2 = p_g * t / (1.0 + t)

    oh1 = lane == i1
    oh2 = lane == i2
    cnt = jnp.where(oh1, 1.0, 0.0) + jnp.where(oh2, 1.0, 0.0)
    r_i = lax.broadcasted_iota(I32, (tm, tm), 0)
    c_i = lax.broadcasted_iota(I32, (tm, tm), 1)
    before = jnp.where(c_i < r_i, 1.0, 0.0).astype(BF16)
    local = jnp.dot(before, cnt.astype(BF16), preferred_element_type=F32)
    n_run = jnp.floor((jnp.sum(cnt, axis=0, keepdims=True) + (SUBLANES - 1.0)) * (1.0 / SUBLANES))
    n_run = n_run * SUBLANES
    e_r = lax.broadcasted_iota(I32, (LANES, LANES), 0)
    e_c = lax.broadcasted_iota(I32, (LANES, LANES), 1)
    earlier = jnp.where(e_r < e_c, 1.0, 0.0).astype(BF16)
    t_off = jnp.dot(jnp.broadcast_to(n_run, (SUBLANES, LANES)).astype(BF16), earlier,
                    preferred_element_type=F32)[:1]
    run = run_ref[...]
    pick = lambda oh, v: jnp.sum(jnp.where(oh, v, 0.0), axis=-1, keepdims=True)
    rank1, rank2 = pick(oh1, run + local), pick(oh2, run + local)
    pos1, pos2 = pick(oh1, t_off + local), pick(oh2, t_off + local)
    run_ref[...] = run + n_run
    cnt_ref[...] = run + n_run
    tables_ref[0] = jnp.concatenate([n_run, t_off, run, jnp.zeros((SUBLANES - 3, LANES), F32)], axis=0)

    route = jnp.zeros((tm, LANES), F32)
    for pos, val in ((ROUTE_E1, i1.astype(F32)), (ROUTE_E2, i2.astype(F32)), (ROUTE_W1, w1),
                     (ROUTE_W2, w2), (ROUTE_R1, rank1), (ROUTE_R2, rank2), (ROUTE_P1, pos1),
                     (ROUTE_P2, pos2)):
        route = jnp.where(lane == pos, val, route)
    route_ref[...] = route
    route_t_ref[...] = route.T[:ROUTE_ROWS, :].astype(I32)


def _merge(x, oa, ob, ga, gb, wa, wb, wo, nf, wr, br):
    T = x.shape[0]
    tm = min(TOKEN_TILE, T)
    row = lambda i: (i, 0)
    const = lambda i: (0, 0)
    wspec = pl.BlockSpec((D_MODEL, D_MODEL), const)
    tile = pl.BlockSpec((tm, D_MODEL), row)
    return pl.pallas_call(
        _merge_kernel,
        grid=(T // tm,),
        in_specs=[tile, tile, tile, tile, tile, wspec, wspec, wspec,
                  pl.BlockSpec((1, D_MODEL), const),
                  pl.BlockSpec((D_MODEL, LANES), const), pl.BlockSpec((1, LANES), const)],
        out_specs=(tile, tile, pl.BlockSpec((tm, LANES), row),
                   pl.BlockSpec((ROUTE_ROWS, tm), lambda i: (0, i)), pl.BlockSpec((1, LANES), const),
                   pl.BlockSpec((1, SUBLANES, LANES), lambda i: (i, 0, 0))),
        out_shape=(jax.ShapeDtypeStruct((T, D_MODEL), F32),
                   jax.ShapeDtypeStruct((T, D_MODEL), F32),
                   jax.ShapeDtypeStruct((T, LANES), F32),
                   jax.ShapeDtypeStruct((ROUTE_ROWS, T), I32),
                   jax.ShapeDtypeStruct((1, LANES), F32),
                   jax.ShapeDtypeStruct((T // tm, SUBLANES, LANES), F32)),
        scratch_shapes=[pltpu.VMEM((1, LANES), F32)],
        compiler_params=_cparams(("arbitrary",)),
        name="merge",
    )(x, oa, ob, ga, gb, wa, wb, wo, nf, wr, br)


RUN_BLOCKS = tuple(1 << b for b in range(9, 2, -1))


def _for_each_run_block(run_ref, off_ref, base_ref, step, fn):
    for e in range(N_EXPERTS):
        idx = step * N_EXPERTS + e
        n_left, src, dst = run_ref[idx], off_ref[idx], base_ref[idx]
        for rows in RUN_BLOCKS:
            take = n_left & rows

            @pl.when(take != 0)
            def _(src=src, dst=dst, rows=rows):
                fn(pl.multiple_of(src, SUBLANES), pl.multiple_of(dst, SUBLANES), rows)
            src, dst = src + take, dst + take


def _dispatch_kernel(ends_ref, run_ref, off_ref, base_ref, ridx_ref, h2_ref, xs_ref, zero_ref,
                     buf_ref, sem):
    i = pl.program_id(0)
    n = pl.num_programs(0)
    slot = i & 1
    n_blocks = xs_ref.shape[0] // SLOT_BLOCK
    n_rows, tm = buf_ref.shape[1], h2_ref.shape[0]
    zero_sem = sem.at[2]

    def run_copies(step, s, fn):
        _for_each_run_block(run_ref, off_ref, base_ref, step, lambda src, dst, rows: fn(
            pltpu.make_async_copy(buf_ref.at[s, pl.ds(src, rows)], xs_ref.at[pl.ds(dst, rows)],
                                  sem.at[s])))

    @pl.when(i == 0)
    def _():
        zero_ref[...] = jnp.zeros_like(zero_ref)

        def zero_block(blk):
            start = pl.multiple_of(blk * SLOT_BLOCK, SLOT_BLOCK)
            return pltpu.make_async_copy(zero_ref, xs_ref.at[pl.ds(start, SLOT_BLOCK)], zero_sem)

        def expert_tail(e):
            return jnp.maximum((ends_ref[e] >> SLOT_SHIFT) - 1, 0)

        n_used = ends_ref[N_EXPERTS - 1] >> SLOT_SHIFT
        for e in range(N_EXPERTS):
            zero_block(expert_tail(e)).start()
        lax.fori_loop(n_used, n_blocks, lambda blk, c: (zero_block(blk).start(), c)[1], 0)
        for e in range(N_EXPERTS):
            zero_block(expert_tail(e)).wait()
        lax.fori_loop(n_used, n_blocks, lambda blk, c: (zero_block(blk).wait(), c)[1], 0)

    @pl.when(i >= 2)
    def _():
        run_copies(i - 2, slot, lambda cp: cp.wait())

    row = lax.broadcasted_iota(I32, (n_rows, tm), 0)
    picked = (row == ridx_ref[ROUTE_P1:ROUTE_P1 + 1, :]) | (row == ridx_ref[ROUTE_P2:ROUTE_P2 + 1, :])
    buf_ref[slot] = jnp.dot(jnp.where(picked, 1.0, 0.0).astype(BF16), h2_ref[...].astype(BF16),
                            preferred_element_type=F32)
    run_copies(i, slot, lambda cp: cp.start())

    @pl.when(i == n - 1)
    def _():
        run_copies(i, slot, lambda cp: cp.wait())

    @pl.when((i == n - 1) & (i >= 1))
    def _():
        run_copies(i - 1, 1 - slot, lambda cp: cp.wait())


def _dispatch(pad_ends, tables, ridx, h2, n_slots):
    T = h2.shape[0]
    tm = min(TOKEN_TILE, T)
    smem = pl.BlockSpec(memory_space=pltpu.SMEM)
    n_rows = 2 * tm + N_EXPERTS * SUBLANES
    return pl.pallas_call(
        _dispatch_kernel,
        grid=(T // tm,),
        in_specs=[smem, smem, smem, smem,
                  pl.BlockSpec((ROUTE_ROWS, tm), lambda i: (0, i)),
                  pl.BlockSpec((tm, D_MODEL), lambda i: (i, 0))],
        out_specs=pl.BlockSpec(memory_space=pl.ANY),
        out_shape=jax.ShapeDtypeStruct((n_slots, D_MODEL), F32),
        scratch_shapes=[pltpu.VMEM((SLOT_BLOCK, D_MODEL), F32),
                        pltpu.VMEM((2, n_rows, D_MODEL), F32), pltpu.SemaphoreType.DMA((3,))],
        compiler_params=_cparams(("arbitrary",)),
        name="dispatch",
    )(pad_ends, *tables, ridx, h2)


def _experts_kernel(be_ref, nu_ref, xs_ref, wg_ref, wu_ref, wd_ref, y_ref):
    del be_ref
    used = pl.program_id(0) < nu_ref[0]

    @pl.when(used)
    def _():
        x = xs_ref[...].astype(BF16)
        g = jnp.dot(x, wg_ref[0].astype(BF16), preferred_element_type=F32)
        u = jnp.dot(x, wu_ref[0].astype(BF16), preferred_element_type=F32)
        a = (g * jax.nn.sigmoid(g) * u).astype(BF16)
        y_ref[...] = jnp.dot(a, wd_ref[0].astype(BF16), preferred_element_type=F32)

    @pl.when(jnp.logical_not(used))
    def _():
        y_ref[...] = jnp.zeros_like(y_ref)


def _experts(block_e, n_used, xs, wg, wu, wd):
    n_blocks = xs.shape[0] // SLOT_BLOCK
    last = lambda i, nu: jnp.minimum(i, jnp.maximum(nu[0] - 1, 0))
    blk = lambda i, be, nu: (last(i, nu), 0)
    wmap = lambda i, be, nu: (be[last(i, nu)], 0, 0)
    grid_spec = pltpu.PrefetchScalarGridSpec(
        num_scalar_prefetch=2,
        grid=(n_blocks,),
        in_specs=[pl.BlockSpec((SLOT_BLOCK, D_MODEL), blk),
                  pl.BlockSpec((1, D_MODEL, D_EXPERT), wmap),
                  pl.BlockSpec((1, D_MODEL, D_EXPERT), wmap),
                  pl.BlockSpec((1, D_EXPERT, D_MODEL), wmap)],
        out_specs=pl.BlockSpec((SLOT_BLOCK, D_MODEL), lambda i, be, nu: (i, 0)),
    )
    return pl.pallas_call(
        _experts_kernel,
        grid_spec=grid_spec,
        out_shape=jax.ShapeDtypeStruct((xs.shape[0], D_MODEL), F32),
        compiler_params=_cparams(("arbitrary",)),
        name="experts",
    )(block_e, n_used, xs, wg, wu, wd)


def _combine_kernel(run_ref, off_ref, base_ref, x1_ref, route_ref, nf_ref, yb_ref, y_ref,
                    buf_ref, sem):
    i = pl.program_id(0)
    n = pl.num_programs(0)
    tm, n_rows = x1_ref.shape[0], buf_ref.shape[1]
    slot = i & 1

    def run_copies(step, s, fn):
        _for_each_run_block(run_ref, off_ref, base_ref, step, lambda dst, src, rows: fn(
            pltpu.make_async_copy(yb_ref.at[pl.ds(src, rows)], buf_ref.at[s, pl.ds(dst, rows)],
                                  sem.at[s])))

    @pl.when(i == 0)
    def _():
        buf_ref[...] = jnp.zeros_like(buf_ref)
        run_copies(i, slot, lambda cp: cp.start())

    @pl.when(i + 1 < n)
    def _():
        run_copies(i + 1, 1 - slot, lambda cp: cp.start())

    run_copies(i, slot, lambda cp: cp.wait())
    route = route_ref[...]
    rows_bf = buf_ref[slot].astype(BF16)
    col = lax.broadcasted_iota(I32, (tm, n_rows), 1)
    moe = jnp.zeros((tm, D_MODEL), F32)
    for w_lane, p_lane in ((ROUTE_W1, ROUTE_P1), (ROUTE_W2, ROUTE_P2)):
        pick = jnp.where(col == route[:, p_lane:p_lane + 1].astype(I32), 1.0, 0.0).astype(BF16)
        moe = moe + route[:, w_lane:w_lane + 1] * jnp.dot(pick, rows_bf, preferred_element_type=F32)
    y_ref[...] = _rms(x1_ref[...] + moe, nf_ref[...])


def _combine(tables, x1, route, nfinal, yb):
    T = x1.shape[0]
    tm = min(TOKEN_TILE, T)
    row = lambda i: (i, 0)
    smem = pl.BlockSpec(memory_space=pltpu.SMEM)
    n_rows = 2 * tm + N_EXPERTS * SUBLANES
    return pl.pallas_call(
        _combine_kernel,
        grid=(T // tm,),
        in_specs=[smem, smem, smem,
                  pl.BlockSpec((tm, D_MODEL), row), pl.BlockSpec((tm, LANES), row),
                  pl.BlockSpec((1, D_MODEL), lambda i: (0, 0)),
                  pl.BlockSpec(memory_space=pl.ANY)],
        out_specs=pl.BlockSpec((tm, D_MODEL), row),
        out_shape=jax.ShapeDtypeStruct((T, D_MODEL), F32),
        scratch_shapes=[pltpu.VMEM((2, n_rows, D_MODEL), F32), pltpu.SemaphoreType.DMA((2,))],
        compiler_params=_cparams(("arbitrary",)),
        name="combine",
    )(*tables, x1, route, nfinal, yb)


def _moe(x1, h2, route, route_t, counts, tables, wg, wu, wd, nfinal):
    T = x1.shape[0]
    tm = min(TOKEN_TILE, T)
    n_slots = T * 2 + (T // tm) * N_EXPERTS * SUBLANES + N_EXPERTS * SLOT_BLOCK
    n_blocks = -(-n_slots // SLOT_BLOCK)
    n_slots = n_blocks * SLOT_BLOCK
    cnt = counts[0, :N_EXPERTS].astype(I32)
    padded = (cnt + SLOT_BLOCK - 1) // SLOT_BLOCK * SLOT_BLOCK
    pad_ends = jnp.cumsum(padded)
    pad_starts = (pad_ends - padded).astype(I32)
    pad_ends = pad_ends.astype(I32)
    block_start = jnp.arange(n_blocks, dtype=I32) * SLOT_BLOCK
    block_e = jnp.sum((block_start[:, None] >= pad_ends[None, :]).astype(I32), axis=1)
    block_e = jnp.minimum(block_e, N_EXPERTS - 1)
    n_used = pad_ends[-1:] // SLOT_BLOCK
    per_run = lambda row: tables[:, row, :N_EXPERTS].astype(I32)
    run_tables = (per_run(TABLE_RUN).reshape(-1), per_run(TABLE_OFF).reshape(-1),
                  (per_run(TABLE_BASE) + pad_starts[None, :]).reshape(-1))
    xs = _dispatch(pad_ends, run_tables, route_t, h2, n_slots)
    yb = _experts(block_e, n_used, xs, wg, wu, wd)
    return _combine(run_tables, x1, route, nfinal, yb)


def _layer(x, s0, k_past, v_past, p, B, L):
    row2 = lambda a: a.reshape(1, -1)
    qka, va, ra, qb, kb, vb, ga, gb, da, k5, v5 = _in_proj(x, row2(p["norm_mix"]), p["w_in"])
    oa, s_new = _gla(qka, va, ra, da, p["w_decay"], row2(p["b_decay"]), row2(p["gla_norm"]),
                     s0, B, L)
    lams = [row2(p[n]) for n in ("lambda_q1", "lambda_k1", "lambda_q2", "lambda_k2")]
    dn = row2(p["diff_norm"])
    if k_past is None:
        ob = _attn_prompt(qb, kb, vb, *lams, dn, B, L)
    else:
        ob = _attn_sample(qb, k_past, v_past, kb, vb, *lams, dn, B, L)
    x1, h2, route, route_t, counts, tables = _merge(
        x, oa, ob, ga, gb, p["w_proj_a"], p["w_proj_b"], p["w_out"], row2(p["norm_ffn"]),
        p["w_router"], p["b_router"])
    y = _moe(x1, h2, route, route_t, counts, tables, p["w_gate"], p["w_up"], p["w_down"],
             row2(p["norm_final"]))
    return y, s_new, k5, v5


def kernel(x_prompt, x_sample, cache_k, cache_v, state_gla, norm_mix, w_in, w_decay, b_decay,
           gla_norm, w_proj_a, lambda_q1, lambda_k1, lambda_q2, lambda_k2, diff_norm, w_proj_b,
           w_out, norm_ffn, w_router_group, b_router_group, w_router_expert, b_router_expert,
           w_gate, w_up, w_down, norm_final):
    B, L, D = x_prompt.shape
    Bs, Ls, _ = x_sample.shape
    w_router = jnp.concatenate(
        [w_router_expert[0], w_router_group[0],
         jnp.zeros((D, LANES - N_EXPERTS - N_GROUPS), F32)], axis=1)
    b_router = jnp.concatenate(
        [b_router_expert[0], b_router_group[0],
         jnp.zeros((LANES - N_EXPERTS - N_GROUPS,), F32)]).reshape(1, LANES)
    p = dict(
        norm_mix=norm_mix[0], w_in=_prep_w_in(w_in[0]),
        w_decay=jnp.pad(w_decay[0], ((0, LANES - GLA_RANK), (0, 0))), b_decay=b_decay[0],
        gla_norm=gla_norm[0], w_proj_a=w_proj_a[0].astype(BF16),
        lambda_q1=lambda_q1[0], lambda_k1=lambda_k1[0], lambda_q2=lambda_q2[0],
        lambda_k2=lambda_k2[0], diff_norm=diff_norm[0], w_proj_b=w_proj_b[0].astype(BF16),
        w_out=w_out[0].astype(BF16), norm_ffn=norm_ffn[0], w_router=w_router, b_router=b_router,
        w_gate=w_gate[0], w_up=w_up[0], w_down=w_down[0],
        norm_final=norm_final)

    s0p = jnp.zeros((B, GLA_HEADS, GLA_DK, GLA_DV), F32)
    yp, sp, kp, vp = _layer(x_prompt.reshape(B * L, D), s0p, None, None, p, B, L)
    ys, ss, ks, vs = _layer(x_sample.reshape(Bs * Ls, D), state_gla[0], cache_k, cache_v,
                            p, Bs, Ls)
    kv = lambda a, b, l: a.reshape(1, b, l, DIFF_HEADS, DIFF_DV)
    return (yp.reshape(B, L, D), ys.reshape(Bs, Ls, D), kv(kp, B, L), kv(vp, B, L), sp[None],
            kv(ks, Bs, Ls), kv(vs, Bs, Ls), ss[None])
```

```python
import functools

import jax
import jax.numpy as jnp
from jax import lax
from jax.experimental import pallas as pl
from jax.experimental.pallas import tpu as pltpu

F32 = jnp.float32
BF16 = jnp.bfloat16
I32 = jnp.int32

D_MODEL = 1024
CHUNK = 64
CHUNK_SHIFT = 6
EPS = 1e-6
GLA_HEADS = 4
GLA_DK = 128
GLA_DV = 256
GLA_RANK = 16
GLA_TAU = 16.0
DIFF_HEADS = 8
DIFF_DH = 64
DIFF_DV = 128
N_GROUPS = 4
EXPERTS_PER_GROUP = 4
EPG_SHIFT = 2
N_EXPERTS = 16
D_EXPERT = 512
LAM_INIT = 0.8 - 0.6

LANES = 128
SUBLANES = 8
TOKEN_TILE = 512
SLOT_BLOCK = 512
SLOT_SHIFT = 9
ATT_TILE = 512
ATT_HEADS_PER_STEP = 2
SAMPLE_KEY_CHUNK = 1024
GLA_CUMSUM_ROWS = 256
VMEM_LIMIT = 56 * 1024 * 1024
NEG = -1e30

HI = lax.Precision.HIGHEST


def _cparams(sem):
    return pltpu.CompilerParams(dimension_semantics=sem, vmem_limit_bytes=VMEM_LIMIT)


def _nt(a, b):
    return lax.dot_general(a, b, (((1,), (1,)), ((), ())), preferred_element_type=F32)


def _tn(a, b):
    return lax.dot_general(a, b, (((0,), (0,)), ((), ())), preferred_element_type=F32)


def _rms(x, g):
    return x * lax.rsqrt(jnp.mean(x * x, axis=-1, keepdims=True) + EPS) * g


def _inproj_kernel(x_ref, g_ref, wa_ref, wda_ref, wb_ref, qka_ref, va_ref, ra_ref, qb_ref, kb_ref,
                   vb_ref, ga_ref, gb_ref, da_ref, k5_ref, v5_ref, stage_ref, sem):
    i = pl.program_id(0)
    n = pl.num_programs(0)
    tm = x_ref.shape[0]
    slot = i & 1

    def head_copy(s, which, hd):
        dst = (k5_ref, v5_ref)[which]
        return pltpu.make_async_copy(stage_ref.at[s, which, hd],
                                     dst.at[pl.ds(i * tm, tm), hd, :], sem.at[s])

    def for_all(s, fn):
        for which in range(2):
            for hd in range(DIFF_HEADS):
                fn(head_copy(s, which, hd))

    @pl.when(i >= 2)
    def _():
        for_all(slot, lambda cp: cp.wait())

    h = _rms(x_ref[...], g_ref[...]).astype(BF16)
    outs = ((qka_ref, wa_ref, 0), (va_ref, wa_ref, 1), (ra_ref, wa_ref, 2), (qb_ref, wb_ref, 0),
            (kb_ref, wb_ref, 1), (vb_ref, wb_ref, 2), (ga_ref, wb_ref, 3), (gb_ref, wb_ref, 4))
    for o, w_ref, c in outs:
        z = jnp.dot(h, w_ref[:, c * D_MODEL:(c + 1) * D_MODEL], preferred_element_type=F32)
        o[...] = z.astype(o.dtype)
        which = 0 if o is kb_ref else 1 if o is vb_ref else None
        if which is not None:
            for hd in range(DIFF_HEADS):
                stage_ref[slot, which, hd] = z[:, hd * DIFF_DV:(hd + 1) * DIFF_DV]
    da_ref[...] = jnp.dot(h, wda_ref[...], preferred_element_type=F32)
    for_all(slot, lambda cp: cp.start())

    @pl.when(i == n - 1)
    def _():
        for_all(slot, lambda cp: cp.wait())

    @pl.when((i == n - 1) & (i >= 1))
    def _():
        for_all(1 - slot, lambda cp: cp.wait())


N_GLA_COLS = 3 * D_MODEL
N_DIFF_COLS = 5 * D_MODEL


def _prep_w_in(w_in):
    w = w_in.astype(BF16)
    return w, w[:, N_GLA_COLS + GLA_RANK:]


def _in_proj(x, g, w):
    w_all, w_diff = w
    T = x.shape[0]
    tm = min(TOKEN_TILE, T)
    resident = lambda cols, j: pl.BlockSpec((D_MODEL, cols), lambda i: (0, j),
                                            pipeline_mode=pl.Buffered(1))
    row = lambda i: (i, 0)
    const = lambda i: (0, 0)
    wide = lambda dt: jax.ShapeDtypeStruct((T, D_MODEL), dt)
    heads = jax.ShapeDtypeStruct((T, DIFF_HEADS, DIFF_DV), F32)
    out_shape = (wide(BF16),) * 8 + (jax.ShapeDtypeStruct((T, LANES), F32), heads, heads)
    out_specs = tuple([pl.BlockSpec((tm, D_MODEL), row)] * 8 + [pl.BlockSpec((tm, LANES), row)]
                      + [pl.BlockSpec(memory_space=pl.ANY)] * 2)
    return pl.pallas_call(
        _inproj_kernel,
        grid=(T // tm,),
        in_specs=[pl.BlockSpec((tm, D_MODEL), row),
                  pl.BlockSpec((1, D_MODEL), const),
                  resident(N_GLA_COLS, 0), resident(LANES, N_GLA_COLS // LANES),
                  resident(N_DIFF_COLS, 0)],
        out_specs=out_specs,
        out_shape=out_shape,
        scratch_shapes=[pltpu.VMEM((2, 2, DIFF_HEADS, tm, DIFF_DV), F32),
                        pltpu.SemaphoreType.DMA((2,))],
        compiler_params=_cparams(("arbitrary",)),
        name="in_proj",
    )(x, g, w_all, w_all, w_diff)


def _gla_kernel(qka_ref, va_ref, ra_ref, da_ref, wd_ref, bd_ref, gn_ref, s0_ref,
                oa_ref, sout_ref, s_ref, la_ref, *, chunk, n_chunks):
    l = pl.program_id(1)

    @pl.when(l == 0)
    def _():
        s_ref[...] = s0_ref[0]

    split = lambda a: (a.astype(BF16), (a - a.astype(BF16).astype(F32)).astype(BF16))
    n_k = GLA_HEADS * GLA_DK
    fold = lambda z: z[:, :n_k] + z[:, n_k:]

    wd_pair = jnp.concatenate(split(wd_ref[...]), axis=1)
    x = fold(sum(jnp.dot(a, wd_pair, preferred_element_type=F32) for a in split(da_ref[...])))
    x = x + bd_ref[...]
    log_a = (jnp.minimum(x, 0.0) - jnp.log1p(jnp.exp(-jnp.abs(x)))) * (1.0 / GLA_TAU)
    lb = log_a.shape[0]
    grp = min(lb, GLA_CUMSUM_ROWS)
    shift = chunk.bit_length() - 1
    r_b = lax.broadcasted_iota(I32, (grp, grp), 0)
    c_b = lax.broadcasted_iota(I32, (grp, grp), 1)
    tri = jnp.where((c_b <= r_b) & ((c_b >> shift) == (r_b >> shift)), 1.0, 0.0).astype(BF16)
    for g in range(lb // grp):
        rows = slice(g * grp, (g + 1) * grp)
        la_ref[rows, :] = fold(jnp.dot(tri, jnp.concatenate(split(log_a[rows, :]), axis=1),
                                       preferred_element_type=F32))

    r_i = lax.broadcasted_iota(I32, (chunk, chunk), 0)
    c_i = lax.broadcasted_iota(I32, (chunk, chunk), 1)
    causal = c_i <= r_i
    gn = gn_ref[...]
    scale = GLA_DK ** -0.5
    heads = range(GLA_HEADS)
    vcols = [slice(h * GLA_DV, (h + 1) * GLA_DV) for h in heads]

    @pl.loop(0, n_chunks)
    def _(c):
        rows = pl.ds(pl.multiple_of(c * chunk, chunk), chunk)
        b_all = la_ref[rows, :]
        q_t, k_t, k_end, decay = [], [], [], []
        for h in heads:
            b = b_all[:, h * GLA_DK:(h + 1) * GLA_DK]
            b_last = b[chunk - 1:chunk, :]
            q = qka_ref[rows, h * GLA_DK:(h + 1) * GLA_DK].astype(F32) * scale
            k = qka_ref[rows, n_k + h * GLA_DK:n_k + (h + 1) * GLA_DK].astype(F32)
            q_t.append((q * jnp.exp(b)).astype(BF16))
            k_t.append((k * jnp.exp(-b)).astype(BF16))
            k_end.append((k * jnp.exp(b_last - b)).astype(BF16))
            decay.append(jnp.broadcast_to(jnp.exp(b_last), (GLA_DK, GLA_DK)).T)
        s_old = [s_ref[h] for h in heads]
        o_state = [jnp.dot(q_t[h], s_old[h].astype(BF16), preferred_element_type=F32) for h in heads]
        att = [_nt(q_t[h], k_t[h]) for h in heads]
        kv = [_tn(k_end[h], va_ref[rows, vcols[h]]) for h in heads]
        for h in heads:
            a = jnp.where(causal, att[h], 0.0).astype(BF16)
            o = o_state[h] + jnp.dot(a, va_ref[rows, vcols[h]], preferred_element_type=F32)
            s_ref[h] = jnp.concatenate([decay[h], decay[h]], axis=1) * s_old[h] + kv[h]
            r = ra_ref[rows, vcols[h]].astype(F32)
            oa_ref[rows, vcols[h]] = (_rms(o, gn) * (r * jax.nn.sigmoid(r))).astype(oa_ref.dtype)

    @pl.when(l == pl.num_programs(1) - 1)
    def _():
        sout_ref[0] = s_ref[...]


def _gla(qka, va, ra, da, wd, bd, gn, s0, B, L):
    chunk = min(CHUNK, L)
    lb = min(TOKEN_TILE, L)
    nl = L // lb
    row = lambda b, l: (b * nl + l, 0)
    const2 = lambda b, l: (0, 0)
    st = lambda b, l: (b, 0, 0, 0)
    kern = functools.partial(_gla_kernel, chunk=chunk, n_chunks=lb // chunk)
    return pl.pallas_call(
        kern,
        grid=(B, nl),
        in_specs=[pl.BlockSpec((lb, D_MODEL), row), pl.BlockSpec((lb, D_MODEL), row),
                  pl.BlockSpec((lb, D_MODEL), row), pl.BlockSpec((lb, LANES), row),
                  pl.BlockSpec((LANES, GLA_HEADS * GLA_DK), const2),
                  pl.BlockSpec((1, GLA_HEADS * GLA_DK), const2),
                  pl.BlockSpec((1, GLA_DV), const2),
                  pl.BlockSpec((1, GLA_HEADS, GLA_DK, GLA_DV), st)],
        out_specs=(pl.BlockSpec((lb, D_MODEL), row),
                   pl.BlockSpec((1, GLA_HEADS, GLA_DK, GLA_DV), st)),
        out_shape=(jax.ShapeDtypeStruct((B * L, D_MODEL), BF16),
                   jax.ShapeDtypeStruct((B, GLA_HEADS, GLA_DK, GLA_DV), F32)),
        scratch_shapes=[pltpu.VMEM((GLA_HEADS, GLA_DK, GLA_DV), F32),
                        pltpu.VMEM((lb, GLA_HEADS * GLA_DK), F32)],
        compiler_params=_cparams(("arbitrary", "arbitrary")),
        name="gla",
    )(qka, va, ra, da, wd, bd, gn, s0)


def _lam(lq1, lk1, lq2, lk2):
    a = jnp.sum(lq1[...] * lk1[...], axis=-1, keepdims=True)
    b = jnp.sum(lq2[...] * lk2[...], axis=-1, keepdims=True)
    return jnp.exp(a) - jnp.exp(b) + LAM_INIT


def _split_q(q):
    lane = lax.broadcasted_iota(I32, q.shape, 1)
    qs = q * jnp.asarray(DIFF_DH ** -0.5, q.dtype)
    zero = jnp.zeros_like(qs)
    return jnp.where(lane < DIFF_DH, qs, zero), jnp.where(lane >= DIFF_DH, qs, zero)


def _alibi_slopes():
    return jnp.asarray([2.0 ** (-8.0 * (h + 1) / DIFF_HEADS) for h in range(DIFF_HEADS)], F32)


def _attn_prompt_kernel(slope_ref, q_ref, k_ref, v_ref, lq1, lk1, lq2, lk2, dn_ref, o_ref,
                        k1_ref, k2_ref, vt_ref, geo_ref, *, tile, nh):
    qi = pl.program_id(2)
    n_kv = vt_ref.shape[1]
    n_cols = tile // LANES
    lane_k = lax.broadcasted_iota(I32, (tile, DIFF_DV), 1)
    row_i = lax.broadcasted_iota(I32, (tile, DIFF_DV), 0)
    row_lo = (row_i & 255).astype(F32)
    row_hi = (row_i & -256).astype(F32)
    slopes = [slope_ref[pl.program_id(1) * nh + g] for g in range(nh)]
    hcols = [slice(g * DIFF_DV, (g + 1) * DIFF_DV) for g in range(nh)]

    @pl.when(qi == 0)
    def _():
        for g in range(nh):
            kfeat = jnp.where(lane_k == DIFF_DH, slopes[g] * row_lo,
                              jnp.where(lane_k == DIFF_DH + 1, slopes[g] * row_hi,
                                        jnp.where((lane_k == DIFF_DH + 2) | (lane_k == DIFF_DH + 3),
                                                  1.0, 0.0)))
            for c in range(n_kv):
                rows = slice(c * tile, (c + 1) * tile)
                kf = k_ref[rows, hcols[g]].astype(F32)
                k1_ref[g, rows, :] = jnp.where(lane_k < DIFF_DH, kf, kfeat).astype(BF16)
                k2_ref[g, rows, :] = jnp.where(lane_k < DIFF_DH, pltpu.roll(kf, DIFF_DH, 1),
                                               kfeat).astype(BF16)
                vt_ref[g, c] = v_ref[rows, hcols[g]].astype(F32).T.astype(BF16)

    qs = []
    for g in range(nh):
        qf = q_ref[:, hcols[g]].astype(F32) * (DIFF_DH ** -0.5)
        qfeat = jnp.where((lane_k == DIFF_DH) | (lane_k == DIFF_DH + 1), 1.0,
                          jnp.where(lane_k == DIFF_DH + 2, -slopes[g] * row_lo,
                                    jnp.where(lane_k == DIFF_DH + 3, -slopes[g] * row_hi, 0.0)))
        qs.append((jnp.where(lane_k < DIFF_DH, qf, qfeat).T.astype(BF16),
                   jnp.where(lane_k < DIFF_DH, pltpu.roll(qf, DIFF_DH, 1), qfeat).T.astype(BF16)))

    n_maps = 2 * nh
    n_stats = 2 * n_cols + 1

    def scores(i, j):
        rows = pl.ds(pl.multiple_of(j * tile, tile), tile)
        k_ref_i = k1_ref if i % 2 == 0 else k2_ref
        return jnp.dot(k_ref_i[i // 2, rows, :], qs[i // 2][i % 2], preferred_element_type=F32)

    def softmax_part(stats, s, c, fix):
        out, alphas, ps = (), [], []
        for col in range(n_cols):
            lanes = slice(col * LANES, (col + 1) * LANES)
            n_keys = tile if fix is None else (col + 1) * LANES
            sh = s[:n_keys, lanes]
            if fix is not None:
                sh = sh + fix[:n_keys, lanes]
            m, l = stats[2 * col], stats[2 * col + 1]
            m_new = jnp.maximum(m, jnp.max(sh, axis=0, keepdims=True) + c)
            alpha = jnp.exp(m - m_new)
            p = jnp.exp(sh - (m_new - c))
            out += (m_new, alpha * l + jnp.sum(p, axis=0, keepdims=True))
            p = p.astype(BF16)
            alphas.append(jnp.broadcast_to(alpha, (DIFF_DV, LANES)))
            if n_keys < tile:
                p = jnp.concatenate([p, jnp.zeros((tile - n_keys, LANES), BF16)], axis=0)
            ps.append(p)
        return out, jnp.concatenate(alphas, axis=1), jnp.concatenate(ps, axis=1)

    def step(stats, j, cs, fixes):
        ss = [scores(i, j) for i in range(n_maps)]
        parts = [softmax_part(stats[n_stats * i:n_stats * (i + 1)], ss[i], cs[i // 2],
                              None if fixes is None else fixes[i // 2])
                 for i in range(n_maps)]
        out = ()
        for i, (st, alpha, p) in enumerate(parts):
            a = alpha * stats[n_stats * i + n_stats - 1] + jnp.dot(
                vt_ref[i // 2, j], p, preferred_element_type=F32)
            out += st + (a,)
        return out

    def body(j, stats):
        off = jnp.full((1, LANES), (j - qi) * tile, I32).astype(F32)
        return step(stats, j, [off * slopes[g] for g in range(nh)], None)

    row = lambda v: jnp.full((1, LANES), v, F32)
    init = ((row(NEG), row(0.0)) * n_cols + (jnp.zeros((DIFF_DV, tile), F32),)) * n_maps
    stats = lax.fori_loop(0, qi, body, init)

    @pl.when((pl.program_id(0) == 0) & (pl.program_id(1) == 0) & (qi == 0))
    def _():
        r_i = lax.broadcasted_iota(I32, (tile, tile), 0)
        c_i = lax.broadcasted_iota(I32, (tile, tile), 1)
        allowed = (r_i >> CHUNK_SHIFT) <= (c_i >> CHUNK_SHIFT)
        geo_ref[0] = jnp.where(r_i > c_i, (c_i - r_i).astype(F32), 0.0)
        geo_ref[1] = jnp.where(allowed, 0.0, NEG)

    fixes = [(2.0 * slopes[g]) * geo_ref[0] + geo_ref[1] for g in range(nh)]
    stats = step(stats, qi, [row(0.0)] * nh, fixes)
    lam = _lam(lq1, lk1, lq2, lk2)
    for g in range(nh):
        res = []
        for i in (2 * g, 2 * g + 1):
            st = stats[n_stats * i:n_stats * (i + 1)]
            l = jnp.concatenate([jnp.broadcast_to(st[2 * col + 1], (DIFF_DV, LANES))
                                 for col in range(n_cols)], axis=1)
            res.append(st[-1] / l)
        o = (res[0] - lam * res[1]).T
        o_ref[:, hcols[g]] = (_rms(o, dn_ref[...]) * (1.0 - LAM_INIT)).astype(o_ref.dtype)


def _attn_prompt(qb, kb, vb, lq1, lk1, lq2, lk2, dn, B, L):
    tile = min(ATT_TILE, L)
    nq = L // tile
    nh = ATT_HEADS_PER_STEP
    qmap = lambda b, h, i: (b * nq + i, h)
    kvmap = lambda b, h, i: (b, h)
    cmap = lambda b, h, i: (0, 0)
    lspec = pl.BlockSpec((1, DIFF_DH), cmap)
    return pl.pallas_call(
        functools.partial(_attn_prompt_kernel, tile=tile, nh=nh),
        grid=(B, DIFF_HEADS // nh, nq),
        in_specs=[pl.BlockSpec(memory_space=pltpu.SMEM),
                  pl.BlockSpec((tile, nh * DIFF_DV), qmap),
                  pl.BlockSpec((L, nh * DIFF_DV), kvmap), pl.BlockSpec((L, nh * DIFF_DV), kvmap),
                  lspec, lspec, lspec, lspec, pl.BlockSpec((1, DIFF_DV), cmap)],
        out_specs=pl.BlockSpec((tile, nh * DIFF_DV), qmap),
        out_shape=jax.ShapeDtypeStruct((B * L, D_MODEL), BF16),
        scratch_shapes=[pltpu.VMEM((nh, L, DIFF_DV), BF16), pltpu.VMEM((nh, L, DIFF_DV), BF16),
                        pltpu.VMEM((nh, nq, DIFF_DV, tile), BF16),
                        pltpu.VMEM((2, tile, tile), F32)],
        compiler_params=_cparams(("arbitrary", "arbitrary", "arbitrary")),
        name="attn_prompt",
    )(_alibi_slopes(), qb, kb, vb, lq1, lk1, lq2, lk2, dn)


def _attn_sample_kernel(slope_ref, q_ref, kp_ref, vp_ref, kn_ref, vn_ref, lq1, lk1, lq2, lk2,
                        dn_ref, o_ref, m_ref, l_ref, acc_ref, kbuf, vbuf, sem, *, past, lq, chunk):
    b = pl.program_id(0)
    c = pl.program_id(1)
    n_c = pl.num_programs(1)
    step = b * n_c + c
    slot = step & 1

    def slab_copies(bb, cc, s, fn):
        rows = pl.ds(cc * chunk, chunk)
        for hd in range(DIFF_HEADS):
            fn(pltpu.make_async_copy(kp_ref.at[0, bb, rows, hd, :], kbuf.at[s, hd], sem.at[s]))
            fn(pltpu.make_async_copy(vp_ref.at[0, bb, rows, hd, :], vbuf.at[s, hd], sem.at[s]))

    @pl.when(step == 0)
    def _():
        slab_copies(b, c, slot, lambda cp: cp.start())

    @pl.when(step + 1 < pl.num_programs(0) * n_c)
    def _():
        wrap = c + 1 == n_c
        slab_copies(jnp.where(wrap, b + 1, b), jnp.where(wrap, 0, c + 1), 1 - slot,
                    lambda cp: cp.start())

    slab_copies(b, c, slot, lambda cp: cp.wait())

    @pl.when(c == 0)
    def _():
        m_ref[...] = jnp.full_like(m_ref, NEG)
        l_ref[...] = jnp.zeros_like(l_ref)
        acc_ref[...] = jnp.zeros_like(acc_ref)

    def geometry(key0, n_keys):
        r = lax.broadcasted_iota(I32, (2 * lq, n_keys), 0)
        qpos = past + jnp.where(r >= lq, r - lq, r)
        kpos = key0 + lax.broadcasted_iota(I32, (2 * lq, n_keys), 1)
        allowed = (kpos >> CHUNK_SHIFT) <= (qpos >> CHUNK_SHIFT)
        return jnp.abs(qpos - kpos).astype(F32), jnp.where(allowed, 0.0, NEG)

    def block(hd, k, v, geo):
        dist, mask = geo
        q1, q2 = _split_q(q_ref[:, hd * DIFF_DV:(hd + 1) * DIFF_DV])
        s = _nt(jnp.concatenate([q1, q2], axis=0), k)
        s = s + (mask - slope_ref[hd] * dist)
        m_old = m_ref[hd]
        m_new = jnp.maximum(m_old, jnp.max(s, axis=-1, keepdims=True))
        alpha = jnp.exp(m_old - m_new)
        p = jnp.exp(s - m_new)
        l_ref[hd] = alpha * l_ref[hd] + jnp.sum(p, axis=-1, keepdims=True)
        acc_ref[hd] = alpha * acc_ref[hd] + jnp.dot(p.astype(BF16), v, preferred_element_type=F32)
        m_ref[hd] = m_new

    geo = geometry(c * chunk, chunk)
    for hd in range(DIFF_HEADS):
        block(hd, kbuf[slot, hd].astype(BF16), vbuf[slot, hd].astype(BF16), geo)

    @pl.when(c == n_c - 1)
    def _():
        lam = _lam(lq1, lk1, lq2, lk2)
        geo_new = geometry(past, lq)
        for hd in range(DIFF_HEADS):
            cols = slice(hd * DIFF_DV, (hd + 1) * DIFF_DV)
            block(hd, kn_ref[:, cols], vn_ref[:, cols], geo_new)
            o = acc_ref[hd] / l_ref[hd]
            o = o[:lq] - lam * o[lq:]
            o_ref[:, hd * DIFF_DV:(hd + 1) * DIFF_DV] = (
                _rms(o, dn_ref[...]) * (1.0 - LAM_INIT)).astype(o_ref.dtype)


def _attn_sample(qb, cache_k, cache_v, kb, vb, lq1, lk1, lq2, lk2, dn, B, L):
    past = cache_k.shape[2]
    chunk = min(SAMPLE_KEY_CHUNK, past)
    n_chunks = past // chunk
    bmap = lambda b, c: (b, 0)
    cmap = lambda b, c: (0, 0)
    lspec = pl.BlockSpec((1, DIFF_DH), cmap)
    tok = pl.BlockSpec((L, D_MODEL), bmap)
    cache = pl.BlockSpec(memory_space=pl.ANY)
    slabs = pltpu.VMEM((2, DIFF_HEADS, chunk, DIFF_DV), F32)
    return pl.pallas_call(
        functools.partial(_attn_sample_kernel, past=past, lq=L, chunk=chunk),
        grid=(B, n_chunks),
        in_specs=[pl.BlockSpec(memory_space=pltpu.SMEM), tok, cache, cache, tok, tok,
                  lspec, lspec, lspec, lspec, pl.BlockSpec((1, DIFF_DV), cmap)],
        out_specs=tok,
        out_shape=jax.ShapeDtypeStruct((B * L, D_MODEL), BF16),
        scratch_shapes=[pltpu.VMEM((DIFF_HEADS, 2 * L, 1), F32),
                        pltpu.VMEM((DIFF_HEADS, 2 * L, 1), F32),
                        pltpu.VMEM((DIFF_HEADS, 2 * L, DIFF_DV), F32),
                        slabs, slabs, pltpu.SemaphoreType.DMA((2,))],
        compiler_params=_cparams(("arbitrary", "arbitrary")),
        name="attn_sample",
    )(_alibi_slopes(), qb, cache_k, cache_v, kb, vb, lq1, lk1, lq2, lk2, dn)


ROUTE_E1, ROUTE_E2, ROUTE_W1, ROUTE_W2, ROUTE_R1, ROUTE_R2, ROUTE_P1, ROUTE_P2 = range(8)
ROUTE_ROWS = 8
TABLE_RUN, TABLE_OFF, TABLE_BASE = range(3)
GROUP_LANE0 = N_EXPERTS


def _merge_kernel(x_ref, oa_ref, ob_ref, ga_ref, gb_ref, wa_ref, wb_ref, wo_ref, nf_ref,
                  wr_ref, br_ref, x1_ref, h2_ref, route_ref, route_t_ref, cnt_ref, tables_ref,
                  run_ref, before_ref):
    i = pl.program_id(0)

    @pl.when(i == 0)
    def _():
        run_ref[...] = jnp.zeros_like(run_ref)
        r_i = lax.broadcasted_iota(I32, before_ref.shape, 0)
        c_i = lax.broadcasted_iota(I32, before_ref.shape, 1)
        before_ref[...] = jnp.where(c_i < r_i, 1.0, 0.0).astype(BF16)

    u_a = jnp.dot(oa_ref[...], wa_ref[...], preferred_element_type=F32)
    u_b = jnp.dot(ob_ref[...], wb_ref[...], preferred_element_type=F32)
    mix = (jax.nn.sigmoid(ga_ref[...].astype(F32)) * u_a
           + jax.nn.sigmoid(gb_ref[...].astype(F32)) * u_b)
    x1 = x_ref[...] + jnp.dot(mix.astype(BF16), wo_ref[...], preferred_element_type=F32)
    x1_ref[...] = x1
    h2 = _rms(x1, nf_ref[...])
    h2_ref[...] = h2

    tm = h2.shape[0]
    split = lambda a: (a.astype(BF16), (a - a.astype(BF16).astype(F32)).astype(BF16))
    w_pair = jnp.concatenate(split(wr_ref[...]), axis=1)
    parts = sum(jnp.dot(a, w_pair, preferred_element_type=F32) for a in split(h2))
    logits = parts[:, :LANES] + parts[:, LANES:] + br_ref[...]
    lane = lax.broadcasted_iota(I32, (tm, LANES), 1)
    big = jnp.int32(LANES)
    g_mask = (lane >= GROUP_LANE0) & (lane < GROUP_LANE0 + N_GROUPS)
    gl = jnp.where(g_mask, logits, -jnp.inf)
    gmax = jnp.max(gl, axis=-1, keepdims=True)
    g_sel = jnp.min(jnp.where(gl == gmax, lane - GROUP_LANE0, big), axis=-1, keepdims=True)
    p_g = 1.0 / jnp.sum(jnp.where(g_mask, jnp.exp(logits - gmax), 0.0), axis=-1, keepdims=True)
    e_mask = (lane < N_EXPERTS) & ((lane >> EPG_SHIFT) == g_sel)
    el = jnp.where(e_mask, logits, -jnp.inf)
    v1 = jnp.max(el, axis=-1, keepdims=True)
    i1 = jnp.min(jnp.where(el == v1, lane, big), axis=-1, keepdims=True)
    el2 = jnp.where(lane == i1, -jnp.inf, el)
    v2 = jnp.max(el2, axis=-1, keepdims=True)
    i2 = jnp.min(jnp.where(el2 == v2, lane, big), axis=-1, keepdims=True)
    t = jnp.exp(v2 - v1)
    w1 = p_g / (1.0 + t)
    w2 = p_g * t / (1.0 + t)

    oh1 = lane == i1
    oh2 = lane == i2
    cnt = jnp.where(oh1, 1.0, 0.0) + jnp.where(oh2, 1.0, 0.0)
    local = jnp.dot(before_ref[...], cnt.astype(BF16), preferred_element_type=F32)
    n_run = jnp.floor((jnp.sum(cnt, axis=0, keepdims=True) + (SUBLANES - 1.0)) * (1.0 / SUBLANES))
    n_run = n_run * SUBLANES
    e_r = lax.broadcasted_iota(I32, (LANES, LANES), 0)
    e_c = lax.broadcasted_iota(I32, (LANES, LANES), 1)
    earlier = jnp.where(e_r < e_c, 1.0, 0.0).astype(BF16)
    t_off = jnp.dot(jnp.broadcast_to(n_run, (SUBLANES, LANES)).astype(BF16), earlier,
                    preferred_element_type=F32)[:1]
    run = run_ref[...]
    pick = lambda oh, v: jnp.sum(jnp.where(oh, v, 0.0), axis=-1, keepdims=True)
    rank1, rank2 = pick(oh1, run + local), pick(oh2, run + local)
    pos1, pos2 = pick(oh1, t_off + local), pick(oh2, t_off + local)
    run_ref[...] = run + n_run
    cnt_ref[...] = run + n_run
    tables_ref[0] = jnp.concatenate([n_run, t_off, run, jnp.zeros((SUBLANES - 3, LANES), F32)], axis=0)

    route = jnp.zeros((tm, LANES), F32)
    for pos, val in ((ROUTE_E1, i1.astype(F32)), (ROUTE_E2, i2.astype(F32)), (ROUTE_W1, w1),
                     (ROUTE_W2, w2), (ROUTE_R1, rank1), (ROUTE_R2, rank2), (ROUTE_P1, pos1),
                     (ROUTE_P2, pos2)):
        route = jnp.where(lane == pos, val, route)
    route_ref[...] = route
    route_t_ref[...] = route.T[:ROUTE_ROWS, :].astype(I32)


def _merge(x, oa, ob, ga, gb, wa, wb, wo, nf, wr, br):
    T = x.shape[0]
    tm = min(TOKEN_TILE, T)
    row = lambda i: (i, 0)
    const = lambda i: (0, 0)
    wspec = pl.BlockSpec((D_MODEL, D_MODEL), const)
    tile = pl.BlockSpec((tm, D_MODEL), row)
    return pl.pallas_call(
        _merge_kernel,
        grid=(T // tm,),
        in_specs=[tile, tile, tile, tile, tile, wspec, wspec, wspec,
                  pl.BlockSpec((1, D_MODEL), const),
                  pl.BlockSpec((D_MODEL, LANES), const), pl.BlockSpec((1, LANES), const)],
        out_specs=(tile, tile, pl.BlockSpec((tm, LANES), row),
                   pl.BlockSpec((ROUTE_ROWS, tm), lambda i: (0, i)), pl.BlockSpec((1, LANES), const),
                   pl.BlockSpec((1, SUBLANES, LANES), lambda i: (i, 0, 0))),
        out_shape=(jax.ShapeDtypeStruct((T, D_MODEL), F32),
                   jax.ShapeDtypeStruct((T, D_MODEL), F32),
                   jax.ShapeDtypeStruct((T, LANES), F32),
                   jax.ShapeDtypeStruct((ROUTE_ROWS, T), I32),
                   jax.ShapeDtypeStruct((1, LANES), F32),
                   jax.ShapeDtypeStruct((T // tm, SUBLANES, LANES), F32)),
        scratch_shapes=[pltpu.VMEM((1, LANES), F32), pltpu.VMEM((tm, tm), BF16)],
        compiler_params=_cparams(("arbitrary",)),
        name="merge",
    )(x, oa, ob, ga, gb, wa, wb, wo, nf, wr, br)


RUN_BLOCKS = tuple(1 << b for b in range(9, 2, -1))


def _for_each_run_block(run_ref, off_ref, base_ref, step, fn):
    for e in range(N_EXPERTS):
        idx = step * N_EXPERTS + e
        n_left, src, dst = run_ref[idx], off_ref[idx], base_ref[idx]
        for rows in RUN_BLOCKS:
            take = n_left & rows

            @pl.when(take != 0)
            def _(src=src, dst=dst, rows=rows):
                fn(pl.multiple_of(src, SUBLANES), pl.multiple_of(dst, SUBLANES), rows)
            src, dst = src + take, dst + take


def _dispatch_kernel(ends_ref, run_ref, off_ref, base_ref, ridx_ref, h2_ref, xs_ref, zero_ref,
                     buf_ref, sem):
    i = pl.program_id(0)
    n = pl.num_programs(0)
    slot = i & 1
    n_blocks = xs_ref.shape[0] // SLOT_BLOCK
    n_rows, tm = buf_ref.shape[1], h2_ref.shape[0]
    zero_sem = sem.at[2]

    def run_copies(step, s, fn):
        _for_each_run_block(run_ref, off_ref, base_ref, step, lambda src, dst, rows: fn(
            pltpu.make_async_copy(buf_ref.at[s, pl.ds(src, rows)], xs_ref.at[pl.ds(dst, rows)],
                                  sem.at[s])))

    @pl.when(i == 0)
    def _():
        zero_ref[...] = jnp.zeros_like(zero_ref)

        def zero_block(blk):
            start = pl.multiple_of(blk * SLOT_BLOCK, SLOT_BLOCK)
            return pltpu.make_async_copy(zero_ref, xs_ref.at[pl.ds(start, SLOT_BLOCK)], zero_sem)

        def expert_tail(e):
            return jnp.maximum((ends_ref[e] >> SLOT_SHIFT) - 1, 0)

        n_used = ends_ref[N_EXPERTS - 1] >> SLOT_SHIFT
        for e in range(N_EXPERTS):
            zero_block(expert_tail(e)).start()
        lax.fori_loop(n_used, n_blocks, lambda blk, c: (zero_block(blk).start(), c)[1], 0)
        for e in range(N_EXPERTS):
            zero_block(expert_tail(e)).wait()
        lax.fori_loop(n_used, n_blocks, lambda blk, c: (zero_block(blk).wait(), c)[1], 0)

    @pl.when(i >= 2)
    def _():
        run_copies(i - 2, slot, lambda cp: cp.wait())

    row = lax.broadcasted_iota(I32, (n_rows, tm), 0)
    picked = (row == ridx_ref[ROUTE_P1:ROUTE_P1 + 1, :]) | (row == ridx_ref[ROUTE_P2:ROUTE_P2 + 1, :])
    buf_ref[slot] = jnp.dot(jnp.where(picked, 1.0, 0.0).astype(BF16), h2_ref[...].astype(BF16),
                            preferred_element_type=F32)
    run_copies(i, slot, lambda cp: cp.start())

    @pl.when(i == n - 1)
    def _():
        run_copies(i, slot, lambda cp: cp.wait())

    @pl.when((i == n - 1) & (i >= 1))
    def _():
        run_copies(i - 1, 1 - slot, lambda cp: cp.wait())


def _dispatch(pad_ends, tables, ridx, h2, n_slots):
    T = h2.shape[0]
    tm = min(TOKEN_TILE, T)
    smem = pl.BlockSpec(memory_space=pltpu.SMEM)
    n_rows = 2 * tm + N_EXPERTS * SUBLANES
    return pl.pallas_call(
        _dispatch_kernel,
        grid=(T // tm,),
        in_specs=[smem, smem, smem, smem,
                  pl.BlockSpec((ROUTE_ROWS, tm), lambda i: (0, i)),
                  pl.BlockSpec((tm, D_MODEL), lambda i: (i, 0))],
        out_specs=pl.BlockSpec(memory_space=pl.ANY),
        out_shape=jax.ShapeDtypeStruct((n_slots, D_MODEL), F32),
        scratch_shapes=[pltpu.VMEM((SLOT_BLOCK, D_MODEL), F32),
                        pltpu.VMEM((2, n_rows, D_MODEL), F32), pltpu.SemaphoreType.DMA((3,))],
        compiler_params=_cparams(("arbitrary",)),
        name="dispatch",
    )(pad_ends, *tables, ridx, h2)


def _experts_kernel(be_ref, nu_ref, xs_ref, wg_ref, wu_ref, wd_ref, y_ref, wg16, wu16, wd16):
    i = pl.program_id(0)
    used = i < nu_ref[0]

    @pl.when(used & ((i == 0) | (be_ref[i] != be_ref[jnp.maximum(i - 1, 0)])))
    def _():
        wg16[...] = wg_ref[0].astype(BF16)
        wu16[...] = wu_ref[0].astype(BF16)
        wd16[...] = wd_ref[0].astype(BF16)

    @pl.when(used)
    def _():
        x = xs_ref[...].astype(BF16)
        g = jnp.dot(x, wg16[...], preferred_element_type=F32)
        u = jnp.dot(x, wu16[...], preferred_element_type=F32)
        a = (g * jax.nn.sigmoid(g) * u).astype(BF16)
        y_ref[...] = jnp.dot(a, wd16[...], preferred_element_type=F32)

    @pl.when(jnp.logical_not(used))
    def _():
        y_ref[...] = jnp.zeros_like(y_ref)


def _experts(block_e, n_used, xs, wg, wu, wd):
    n_blocks = xs.shape[0] // SLOT_BLOCK
    last = lambda i, nu: jnp.minimum(i, jnp.maximum(nu[0] - 1, 0))
    blk = lambda i, be, nu: (last(i, nu), 0)
    wmap = lambda i, be, nu: (be[last(i, nu)], 0, 0)
    grid_spec = pltpu.PrefetchScalarGridSpec(
        num_scalar_prefetch=2,
        grid=(n_blocks,),
        in_specs=[pl.BlockSpec((SLOT_BLOCK, D_MODEL), blk),
                  pl.BlockSpec((1, D_MODEL, D_EXPERT), wmap),
                  pl.BlockSpec((1, D_MODEL, D_EXPERT), wmap),
                  pl.BlockSpec((1, D_EXPERT, D_MODEL), wmap)],
        out_specs=pl.BlockSpec((SLOT_BLOCK, D_MODEL), lambda i, be, nu: (i, 0)),
        scratch_shapes=[pltpu.VMEM((D_MODEL, D_EXPERT), BF16), pltpu.VMEM((D_MODEL, D_EXPERT), BF16),
                        pltpu.VMEM((D_EXPERT, D_MODEL), BF16)],
    )
    return pl.pallas_call(
        _experts_kernel,
        grid_spec=grid_spec,
        out_shape=jax.ShapeDtypeStruct((xs.shape[0], D_MODEL), F32),
        compiler_params=_cparams(("arbitrary",)),
        name="experts",
    )(block_e, n_used, xs, wg, wu, wd)


def _combine_kernel(run_ref, off_ref, base_ref, x1_ref, route_ref, nf_ref, yb_ref, y_ref,
                    buf_ref, sem):
    i = pl.program_id(0)
    n = pl.num_programs(0)
    tm, n_rows = x1_ref.shape[0], buf_ref.shape[1]
    slot = i & 1

    def run_copies(step, s, fn):
        _for_each_run_block(run_ref, off_ref, base_ref, step, lambda dst, src, rows: fn(
            pltpu.make_async_copy(yb_ref.at[pl.ds(src, rows)], buf_ref.at[s, pl.ds(dst, rows)],
                                  sem.at[s])))

    @pl.when(i == 0)
    def _():
        buf_ref[...] = jnp.zeros_like(buf_ref)
        run_copies(i, slot, lambda cp: cp.start())

    @pl.when(i + 1 < n)
    def _():
        run_copies(i + 1, 1 - slot, lambda cp: cp.start())

    run_copies(i, slot, lambda cp: cp.wait())
    route = route_ref[...]
    rows_bf = buf_ref[slot].astype(BF16)
    col = lax.broadcasted_iota(I32, (tm, n_rows), 1)
    moe = jnp.zeros((tm, D_MODEL), F32)
    for w_lane, p_lane in ((ROUTE_W1, ROUTE_P1), (ROUTE_W2, ROUTE_P2)):
        pick = jnp.where(col == route[:, p_lane:p_lane + 1].astype(I32), 1.0, 0.0).astype(BF16)
        moe = moe + route[:, w_lane:w_lane + 1] * jnp.dot(pick, rows_bf, preferred_element_type=F32)
    y_ref[...] = _rms(x1_ref[...] + moe, nf_ref[...])


def _combine(tables, x1, route, nfinal, yb):
    T = x1.shape[0]
    tm = min(TOKEN_TILE, T)
    row = lambda i: (i, 0)
    smem = pl.BlockSpec(memory_space=pltpu.SMEM)
    n_rows = 2 * tm + N_EXPERTS * SUBLANES
    return pl.pallas_call(
        _combine_kernel,
        grid=(T // tm,),
        in_specs=[smem, smem, smem,
                  pl.BlockSpec((tm, D_MODEL), row), pl.BlockSpec((tm, LANES), row),
                  pl.BlockSpec((1, D_MODEL), lambda i: (0, 0)),
                  pl.BlockSpec(memory_space=pl.ANY)],
        out_specs=pl.BlockSpec((tm, D_MODEL), row),
        out_shape=jax.ShapeDtypeStruct((T, D_MODEL), F32),
        scratch_shapes=[pltpu.VMEM((2, n_rows, D_MODEL), F32), pltpu.SemaphoreType.DMA((2,))],
        compiler_params=_cparams(("arbitrary",)),
        name="combine",
    )(*tables, x1, route, nfinal, yb)


def _moe(x1, h2, route, route_t, counts, tables, wg, wu, wd, nfinal):
    T = x1.shape[0]
    tm = min(TOKEN_TILE, T)
    n_slots = T * 2 + (T // tm) * N_EXPERTS * SUBLANES + N_EXPERTS * SLOT_BLOCK
    n_blocks = -(-n_slots // SLOT_BLOCK)
    n_slots = n_blocks * SLOT_BLOCK
    cnt = counts[0, :N_EXPERTS].astype(I32)
    padded = (cnt + SLOT_BLOCK - 1) // SLOT_BLOCK * SLOT_BLOCK
    pad_ends = jnp.cumsum(padded)
    pad_starts = (pad_ends - padded).astype(I32)
    pad_ends = pad_ends.astype(I32)
    block_start = jnp.arange(n_blocks, dtype=I32) * SLOT_BLOCK
    block_e = jnp.sum((block_start[:, None] >= pad_ends[None, :]).astype(I32), axis=1)
    block_e = jnp.minimum(block_e, N_EXPERTS - 1)
    n_used = pad_ends[-1:] // SLOT_BLOCK
    per_run = lambda row: tables[:, row, :N_EXPERTS].astype(I32)
    run_tables = (per_run(TABLE_RUN).reshape(-1), per_run(TABLE_OFF).reshape(-1),
                  (per_run(TABLE_BASE) + pad_starts[None, :]).reshape(-1))
    xs = _dispatch(pad_ends, run_tables, route_t, h2, n_slots)
    yb = _experts(block_e, n_used, xs, wg, wu, wd)
    return _combine(run_tables, x1, route, nfinal, yb)


def _layer(x, s0, k_past, v_past, p, B, L):
    row2 = lambda a: a.reshape(1, -1)
    qka, va, ra, qb, kb, vb, ga, gb, da, k5, v5 = _in_proj(x, row2(p["norm_mix"]), p["w_in"])
    oa, s_new = _gla(qka, va, ra, da, p["w_decay"], row2(p["b_decay"]), row2(p["gla_norm"]),
                     s0, B, L)
    lams = [row2(p[n]) for n in ("lambda_q1", "lambda_k1", "lambda_q2", "lambda_k2")]
    dn = row2(p["diff_norm"])
    if k_past is None:
        ob = _attn_prompt(qb, kb, vb, *lams, dn, B, L)
    else:
        ob = _attn_sample(qb, k_past, v_past, kb, vb, *lams, dn, B, L)
    x1, h2, route, route_t, counts, tables = _merge(
        x, oa, ob, ga, gb, p["w_proj_a"], p["w_proj_b"], p["w_out"], row2(p["norm_ffn"]),
        p["w_router"], p["b_router"])
    y = _moe(x1, h2, route, route_t, counts, tables, p["w_gate"], p["w_up"], p["w_down"],
             row2(p["norm_final"]))
    return y, s_new, k5, v5


def kernel(x_prompt, x_sample, cache_k, cache_v, state_gla, norm_mix, w_in, w_decay, b_decay,
           gla_norm, w_proj_a, lambda_q1, lambda_k1, lambda_q2, lambda_k2, diff_norm, w_proj_b,
           w_out, norm_ffn, w_router_group, b_router_group, w_router_expert, b_router_expert,
           w_gate, w_up, w_down, norm_final):
    B, L, D = x_prompt.shape
    Bs, Ls, _ = x_sample.shape
    w_router = jnp.concatenate(
        [w_router_expert[0], w_router_group[0],
         jnp.zeros((D, LANES - N_EXPERTS - N_GROUPS), F32)], axis=1)
    b_router = jnp.concatenate(
        [b_router_expert[0], b_router_group[0],
         jnp.zeros((LANES - N_EXPERTS - N_GROUPS,), F32)]).reshape(1, LANES)
    p = dict(
        norm_mix=norm_mix[0], w_in=_prep_w_in(w_in[0]),
        w_decay=jnp.pad(w_decay[0], ((0, LANES - GLA_RANK), (0, 0))), b_decay=b_decay[0],
        gla_norm=gla_norm[0], w_proj_a=w_proj_a[0].astype(BF16),
        lambda_q1=lambda_q1[0], lambda_k1=lambda_k1[0], lambda_q2=lambda_q2[0],
        lambda_k2=lambda_k2[0], diff_norm=diff_norm[0], w_proj_b=w_proj_b[0].astype(BF16),
        w_out=w_out[0].astype(BF16), norm_ffn=norm_ffn[0], w_router=w_router, b_router=b_router,
        w_gate=w_gate[0], w_up=w_up[0], w_down=w_down[0],
        norm_final=norm_final)

    s0p = jnp.zeros((B, GLA_HEADS, GLA_DK, GLA_DV), F32)
    yp, sp, kp, vp = _layer(x_prompt.reshape(B * L, D), s0p, None, None, p, B, L)
    ys, ss, ks, vs = _layer(x_sample.reshape(Bs * Ls, D), state_gla[0], cache_k, cache_v,
                            p, Bs, Ls)
    kv = lambda a, b, l: a.reshape(1, b, l, DIFF_HEADS, DIFF_DV)
    return (yp.reshape(B, L, D), ys.reshape(Bs, Ls, D), kv(kp, B, L), kv(vp, B, L), sp[None],
            kv(ks, Bs, Ls), kv(vs, Bs, Ls), ss[None])
```

```python
import functools

import jax
import jax.numpy as jnp
from jax import lax
from jax.experimental import pallas as pl
from jax.experimental.pallas import tpu as pltpu

F32 = jnp.float32
BF16 = jnp.bfloat16
I32 = jnp.int32

D_MODEL = 1024
CHUNK = 64
CHUNK_SHIFT = 6
EPS = 1e-6
GLA_HEADS = 4
GLA_DK = 128
GLA_DV = 256
GLA_RANK = 16
GLA_TAU = 16.0
DIFF_HEADS = 8
DIFF_DH = 64
DIFF_DV = 128
N_GROUPS = 4
EXPERTS_PER_GROUP = 4
EPG_SHIFT = 2
N_EXPERTS = 16
D_EXPERT = 512
LAM_INIT = 0.8 - 0.6

LANES = 128
SUBLANES = 8
TOKEN_TILE = 512
SLOT_BLOCK = 512
SLOT_BLOCK_SMALL = 128
ATT_TILE = 512
ATT_HEADS_PER_STEP = 2
SAMPLE_KEY_CHUNK = 1024
GLA_CUMSUM_ROWS = 256
VMEM_LIMIT = 56 * 1024 * 1024
NEG = -1e30

HI = lax.Precision.HIGHEST


def _cparams(sem):
    return pltpu.CompilerParams(dimension_semantics=sem, vmem_limit_bytes=VMEM_LIMIT)


def _nt(a, b):
    return lax.dot_general(a, b, (((1,), (1,)), ((), ())), preferred_element_type=F32)


def _tn(a, b):
    return lax.dot_general(a, b, (((0,), (0,)), ((), ())), preferred_element_type=F32)


def _rms(x, g):
    return x * lax.rsqrt(jnp.mean(x * x, axis=-1, keepdims=True) + EPS) * g


def _inproj_kernel(x_ref, g_ref, wa_ref, wda_ref, wb_ref, qka_ref, va_ref, ra_ref, qb_ref, kb_ref,
                   vb_ref, ga_ref, gb_ref, da_ref, k5_ref, v5_ref, stage_ref, sem):
    i = pl.program_id(0)
    n = pl.num_programs(0)
    tm = x_ref.shape[0]
    slot = i & 1

    def head_copy(s, which, hd):
        dst = (k5_ref, v5_ref)[which]
        return pltpu.make_async_copy(stage_ref.at[s, which, hd],
                                     dst.at[pl.ds(i * tm, tm), hd, :], sem.at[s])

    def for_all(s, fn):
        for which in range(2):
            for hd in range(DIFF_HEADS):
                fn(head_copy(s, which, hd))

    @pl.when(i >= 2)
    def _():
        for_all(slot, lambda cp: cp.wait())

    h = _rms(x_ref[...], g_ref[...]).astype(BF16)
    outs = ((qka_ref, wa_ref, 0), (va_ref, wa_ref, 1), (ra_ref, wa_ref, 2), (qb_ref, wb_ref, 0),
            (kb_ref, wb_ref, 1), (vb_ref, wb_ref, 2), (ga_ref, wb_ref, 3), (gb_ref, wb_ref, 4))
    for o, w_ref, c in outs:
        z = jnp.dot(h, w_ref[:, c * D_MODEL:(c + 1) * D_MODEL], preferred_element_type=F32)
        o[...] = z.astype(o.dtype)
        which = 0 if o is kb_ref else 1 if o is vb_ref else None
        if which is not None:
            for hd in range(DIFF_HEADS):
                stage_ref[slot, which, hd] = z[:, hd * DIFF_DV:(hd + 1) * DIFF_DV]
    da_ref[...] = jnp.dot(h, wda_ref[...], preferred_element_type=F32)
    for_all(slot, lambda cp: cp.start())

    @pl.when(i == n - 1)
    def _():
        for_all(slot, lambda cp: cp.wait())

    @pl.when((i == n - 1) & (i >= 1))
    def _():
        for_all(1 - slot, lambda cp: cp.wait())


N_GLA_COLS = 3 * D_MODEL
N_DIFF_COLS = 5 * D_MODEL


def _prep_w_in(w_in):
    w = w_in.astype(BF16)
    return w, w[:, N_GLA_COLS + GLA_RANK:]


def _in_proj(x, g, w):
    w_all, w_diff = w
    T = x.shape[0]
    tm = min(TOKEN_TILE, T)
    resident = lambda cols, j: pl.BlockSpec((D_MODEL, cols), lambda i: (0, j),
                                            pipeline_mode=pl.Buffered(1))
    row = lambda i: (i, 0)
    const = lambda i: (0, 0)
    wide = lambda dt: jax.ShapeDtypeStruct((T, D_MODEL), dt)
    heads = jax.ShapeDtypeStruct((T, DIFF_HEADS, DIFF_DV), F32)
    out_shape = (wide(BF16),) * 8 + (jax.ShapeDtypeStruct((T, LANES), F32), heads, heads)
    out_specs = tuple([pl.BlockSpec((tm, D_MODEL), row)] * 8 + [pl.BlockSpec((tm, LANES), row)]
                      + [pl.BlockSpec(memory_space=pl.ANY)] * 2)
    return pl.pallas_call(
        _inproj_kernel,
        grid=(T // tm,),
        in_specs=[pl.BlockSpec((tm, D_MODEL), row),
                  pl.BlockSpec((1, D_MODEL), const),
                  resident(N_GLA_COLS, 0), resident(LANES, N_GLA_COLS // LANES),
                  resident(N_DIFF_COLS, 0)],
        out_specs=out_specs,
        out_shape=out_shape,
        scratch_shapes=[pltpu.VMEM((2, 2, DIFF_HEADS, tm, DIFF_DV), F32),
                        pltpu.SemaphoreType.DMA((2,))],
        compiler_params=_cparams(("arbitrary",)),
        name="in_proj",
    )(x, g, w_all, w_all, w_diff)


def _gla_kernel(qka_ref, va_ref, ra_ref, da_ref, wd_ref, bd_ref, gn_ref, s0_ref,
                oa_ref, sout_ref, s_ref, la_ref, *, chunk, n_chunks):
    l = pl.program_id(1)

    @pl.when(l == 0)
    def _():
        s_ref[...] = s0_ref[0]

    split = lambda a: (a.astype(BF16), (a - a.astype(BF16).astype(F32)).astype(BF16))
    n_k = GLA_HEADS * GLA_DK
    fold = lambda z: z[:, :n_k] + z[:, n_k:]

    wd_pair = jnp.concatenate(split(wd_ref[...]), axis=1)
    x = fold(sum(jnp.dot(a, wd_pair, preferred_element_type=F32) for a in split(da_ref[...])))
    x = x + bd_ref[...]
    log_a = (jnp.minimum(x, 0.0) - jnp.log1p(jnp.exp(-jnp.abs(x)))) * (1.0 / GLA_TAU)
    lb = log_a.shape[0]
    grp = min(lb, GLA_CUMSUM_ROWS)
    shift = chunk.bit_length() - 1
    r_b = lax.broadcasted_iota(I32, (grp, grp), 0)
    c_b = lax.broadcasted_iota(I32, (grp, grp), 1)
    tri = jnp.where((c_b <= r_b) & ((c_b >> shift) == (r_b >> shift)), 1.0, 0.0).astype(BF16)
    for g in range(lb // grp):
        rows = slice(g * grp, (g + 1) * grp)
        la_ref[rows, :] = fold(jnp.dot(tri, jnp.concatenate(split(log_a[rows, :]), axis=1),
                                       preferred_element_type=F32))

    r_i = lax.broadcasted_iota(I32, (chunk, chunk), 0)
    c_i = lax.broadcasted_iota(I32, (chunk, chunk), 1)
    causal = c_i <= r_i
    gn = gn_ref[...]
    scale = GLA_DK ** -0.5
    heads = range(GLA_HEADS)
    vcols = [slice(h * GLA_DV, (h + 1) * GLA_DV) for h in heads]

    @pl.loop(0, n_chunks)
    def _(c):
        rows = pl.ds(pl.multiple_of(c * chunk, chunk), chunk)
        b_all = la_ref[rows, :]
        q_t, k_t, k_end, decay = [], [], [], []
        for h in heads:
            b = b_all[:, h * GLA_DK:(h + 1) * GLA_DK]
            b_last = b[chunk - 1:chunk, :]
            q = qka_ref[rows, h * GLA_DK:(h + 1) * GLA_DK].astype(F32) * scale
            k = qka_ref[rows, n_k + h * GLA_DK:n_k + (h + 1) * GLA_DK].astype(F32)
            q_t.append((q * jnp.exp(b)).astype(BF16))
            k_t.append((k * jnp.exp(-b)).astype(BF16))
            k_end.append((k * jnp.exp(b_last - b)).astype(BF16))
            decay.append(jnp.broadcast_to(jnp.exp(b_last), (GLA_DK, GLA_DK)).T)
        s_old = [s_ref[h] for h in heads]
        o_state = [jnp.dot(q_t[h], s_old[h].astype(BF16), preferred_element_type=F32) for h in heads]
        att = [_nt(q_t[h], k_t[h]) for h in heads]
        kv = [_tn(k_end[h], va_ref[rows, vcols[h]]) for h in heads]
        for h in heads:
            a = jnp.where(causal, att[h], 0.0).astype(BF16)
            o = o_state[h] + jnp.dot(a, va_ref[rows, vcols[h]], preferred_element_type=F32)
            s_ref[h] = jnp.concatenate([decay[h], decay[h]], axis=1) * s_old[h] + kv[h]
            r = ra_ref[rows, vcols[h]].astype(F32)
            oa_ref[rows, vcols[h]] = (_rms(o, gn) * (r * jax.nn.sigmoid(r))).astype(oa_ref.dtype)

    @pl.when(l == pl.num_programs(1) - 1)
    def _():
        sout_ref[0] = s_ref[...]


def _gla(qka, va, ra, da, wd, bd, gn, s0, B, L):
    chunk = min(CHUNK, L)
    lb = min(TOKEN_TILE, L)
    nl = L // lb
    row = lambda b, l: (b * nl + l, 0)
    const2 = lambda b, l: (0, 0)
    st = lambda b, l: (b, 0, 0, 0)
    kern = functools.partial(_gla_kernel, chunk=chunk, n_chunks=lb // chunk)
    return pl.pallas_call(
        kern,
        grid=(B, nl),
        in_specs=[pl.BlockSpec((lb, D_MODEL), row), pl.BlockSpec((lb, D_MODEL), row),
                  pl.BlockSpec((lb, D_MODEL), row), pl.BlockSpec((lb, LANES), row),
                  pl.BlockSpec((LANES, GLA_HEADS * GLA_DK), const2),
                  pl.BlockSpec((1, GLA_HEADS * GLA_DK), const2),
                  pl.BlockSpec((1, GLA_DV), const2),
                  pl.BlockSpec((1, GLA_HEADS, GLA_DK, GLA_DV), st)],
        out_specs=(pl.BlockSpec((lb, D_MODEL), row),
                   pl.BlockSpec((1, GLA_HEADS, GLA_DK, GLA_DV), st)),
        out_shape=(jax.ShapeDtypeStruct((B * L, D_MODEL), BF16),
                   jax.ShapeDtypeStruct((B, GLA_HEADS, GLA_DK, GLA_DV), F32)),
        scratch_shapes=[pltpu.VMEM((GLA_HEADS, GLA_DK, GLA_DV), F32),
                        pltpu.VMEM((lb, GLA_HEADS * GLA_DK), F32)],
        compiler_params=_cparams(("arbitrary", "arbitrary")),
        name="gla",
    )(qka, va, ra, da, wd, bd, gn, s0)


def _lam(lq1, lk1, lq2, lk2):
    a = jnp.sum(lq1[...] * lk1[...], axis=-1, keepdims=True)
    b = jnp.sum(lq2[...] * lk2[...], axis=-1, keepdims=True)
    return jnp.exp(a) - jnp.exp(b) + LAM_INIT


def _split_q(q):
    lane = lax.broadcasted_iota(I32, q.shape, 1)
    qs = q * jnp.asarray(DIFF_DH ** -0.5, q.dtype)
    zero = jnp.zeros_like(qs)
    return jnp.where(lane < DIFF_DH, qs, zero), jnp.where(lane >= DIFF_DH, qs, zero)


def _alibi_slopes():
    return jnp.asarray([2.0 ** (-8.0 * (h + 1) / DIFF_HEADS) for h in range(DIFF_HEADS)], F32)


def _attn_prompt_kernel(slope_ref, q_ref, k_ref, v_ref, lq1, lk1, lq2, lk2, dn_ref, o_ref,
                        k1_ref, k2_ref, vt_ref, geo_ref, *, tile, nh):
    qi = pl.program_id(2)
    n_kv = vt_ref.shape[1]
    n_cols = tile // LANES
    lane_k = lax.broadcasted_iota(I32, (tile, DIFF_DV), 1)
    row_i = lax.broadcasted_iota(I32, (tile, DIFF_DV), 0)
    row_lo = (row_i & 255).astype(F32)
    row_hi = (row_i & -256).astype(F32)
    slopes = [slope_ref[pl.program_id(1) * nh + g] for g in range(nh)]
    hcols = [slice(g * DIFF_DV, (g + 1) * DIFF_DV) for g in range(nh)]

    @pl.when(qi == 0)
    def _():
        for g in range(nh):
            kfeat = jnp.where(lane_k == DIFF_DH, slopes[g] * row_lo,
                              jnp.where(lane_k == DIFF_DH + 1, slopes[g] * row_hi,
                                        jnp.where((lane_k == DIFF_DH + 2) | (lane_k == DIFF_DH + 3),
                                                  1.0, 0.0)))
            for c in range(n_kv):
                rows = slice(c * tile, (c + 1) * tile)
                kf = k_ref[rows, hcols[g]].astype(F32)
                k1_ref[g, rows, :] = jnp.where(lane_k < DIFF_DH, kf, kfeat).astype(BF16)
                k2_ref[g, rows, :] = jnp.where(lane_k < DIFF_DH, pltpu.roll(kf, DIFF_DH, 1),
                                               kfeat).astype(BF16)
                vt_ref[g, c] = v_ref[rows, hcols[g]].astype(F32).T.astype(BF16)

    qs = []
    for g in range(nh):
        qf = q_ref[:, hcols[g]].astype(F32) * (DIFF_DH ** -0.5)
        qfeat = jnp.where((lane_k == DIFF_DH) | (lane_k == DIFF_DH + 1), 1.0,
                          jnp.where(lane_k == DIFF_DH + 2, -slopes[g] * row_lo,
                                    jnp.where(lane_k == DIFF_DH + 3, -slopes[g] * row_hi, 0.0)))
        qs.append((jnp.where(lane_k < DIFF_DH, qf, qfeat).T.astype(BF16),
                   jnp.where(lane_k < DIFF_DH, pltpu.roll(qf, DIFF_DH, 1), qfeat).T.astype(BF16)))

    n_maps = 2 * nh
    n_stats = 2 * n_cols + 1

    def scores(i, j):
        rows = pl.ds(pl.multiple_of(j * tile, tile), tile)
        k_ref_i = k1_ref if i % 2 == 0 else k2_ref
        return jnp.dot(k_ref_i[i // 2, rows, :], qs[i // 2][i % 2], preferred_element_type=F32)

    def softmax_part(stats, s, c, fix):
        out, alphas, ps = (), [], []
        for col in range(n_cols):
            lanes = slice(col * LANES, (col + 1) * LANES)
            n_keys = tile if fix is None else (col + 1) * LANES
            sh = s[:n_keys, lanes]
            if fix is not None:
                sh = sh + fix[:n_keys, lanes]
            m, l = stats[2 * col], stats[2 * col + 1]
            m_new = jnp.maximum(m, jnp.max(sh, axis=0, keepdims=True) + c)
            alpha = jnp.exp(m - m_new)
            p = jnp.exp(sh - (m_new - c))
            out += (m_new, alpha * l + jnp.sum(p, axis=0, keepdims=True))
            p = p.astype(BF16)
            alphas.append(jnp.broadcast_to(alpha, (DIFF_DV, LANES)))
            if n_keys < tile:
                p = jnp.concatenate([p, jnp.zeros((tile - n_keys, LANES), BF16)], axis=0)
            ps.append(p)
        return out, jnp.concatenate(alphas, axis=1), jnp.concatenate(ps, axis=1)

    def step(stats, j, cs, fixes):
        ss = [scores(i, j) for i in range(n_maps)]
        parts = [softmax_part(stats[n_stats * i:n_stats * (i + 1)], ss[i], cs[i // 2],
                              None if fixes is None else fixes[i // 2])
                 for i in range(n_maps)]
        out = ()
        for i, (st, alpha, p) in enumerate(parts):
            a = alpha * stats[n_stats * i + n_stats - 1] + jnp.dot(
                vt_ref[i // 2, j], p, preferred_element_type=F32)
            out += st + (a,)
        return out

    def body(j, stats):
        off = jnp.full((1, LANES), (j - qi) * tile, I32).astype(F32)
        return step(stats, j, [off * slopes[g] for g in range(nh)], None)

    row = lambda v: jnp.full((1, LANES), v, F32)
    init = ((row(NEG), row(0.0)) * n_cols + (jnp.zeros((DIFF_DV, tile), F32),)) * n_maps
    stats = lax.fori_loop(0, qi, body, init)

    @pl.when((pl.program_id(0) == 0) & (pl.program_id(1) == 0) & (qi == 0))
    def _():
        r_i = lax.broadcasted_iota(I32, (tile, tile), 0)
        c_i = lax.broadcasted_iota(I32, (tile, tile), 1)
        allowed = (r_i >> CHUNK_SHIFT) <= (c_i >> CHUNK_SHIFT)
        geo_ref[0] = jnp.where(r_i > c_i, (c_i - r_i).astype(F32), 0.0)
        geo_ref[1] = jnp.where(allowed, 0.0, NEG)

    fixes = [(2.0 * slopes[g]) * geo_ref[0] + geo_ref[1] for g in range(nh)]
    stats = step(stats, qi, [row(0.0)] * nh, fixes)
    lam = _lam(lq1, lk1, lq2, lk2)
    for g in range(nh):
        res = []
        for i in (2 * g, 2 * g + 1):
            st = stats[n_stats * i:n_stats * (i + 1)]
            l = jnp.concatenate([jnp.broadcast_to(st[2 * col + 1], (DIFF_DV, LANES))
                                 for col in range(n_cols)], axis=1)
            res.append(st[-1] / l)
        o = (res[0] - lam * res[1]).T
        o_ref[:, hcols[g]] = (_rms(o, dn_ref[...]) * (1.0 - LAM_INIT)).astype(o_ref.dtype)


def _attn_prompt(qb, kb, vb, lq1, lk1, lq2, lk2, dn, B, L):
    tile = min(ATT_TILE, L)
    nq = L // tile
    nh = ATT_HEADS_PER_STEP
    qmap = lambda b, h, i: (b * nq + i, h)
    kvmap = lambda b, h, i: (b, h)
    cmap = lambda b, h, i: (0, 0)
    lspec = pl.BlockSpec((1, DIFF_DH), cmap)
    return pl.pallas_call(
        functools.partial(_attn_prompt_kernel, tile=tile, nh=nh),
        grid=(B, DIFF_HEADS // nh, nq),
        in_specs=[pl.BlockSpec(memory_space=pltpu.SMEM),
                  pl.BlockSpec((tile, nh * DIFF_DV), qmap),
                  pl.BlockSpec((L, nh * DIFF_DV), kvmap), pl.BlockSpec((L, nh * DIFF_DV), kvmap),
                  lspec, lspec, lspec, lspec, pl.BlockSpec((1, DIFF_DV), cmap)],
        out_specs=pl.BlockSpec((tile, nh * DIFF_DV), qmap),
        out_shape=jax.ShapeDtypeStruct((B * L, D_MODEL), BF16),
        scratch_shapes=[pltpu.VMEM((nh, L, DIFF_DV), BF16), pltpu.VMEM((nh, L, DIFF_DV), BF16),
                        pltpu.VMEM((nh, nq, DIFF_DV, tile), BF16),
                        pltpu.VMEM((2, tile, tile), F32)],
        compiler_params=_cparams(("arbitrary", "arbitrary", "arbitrary")),
        name="attn_prompt",
    )(_alibi_slopes(), qb, kb, vb, lq1, lk1, lq2, lk2, dn)


def _attn_sample_kernel(slope_ref, q_ref, kp_ref, vp_ref, kn_ref, vn_ref, lq1, lk1, lq2, lk2,
                        dn_ref, o_ref, m_ref, l_ref, acc_ref, kbuf, vbuf, sem, *, past, lq, chunk):
    b = pl.program_id(0)
    c = pl.program_id(1)
    n_c = pl.num_programs(1)
    step = b * n_c + c
    slot = step & 1

    def slab_copies(bb, cc, s, fn):
        rows = pl.ds(cc * chunk, chunk)
        for hd in range(DIFF_HEADS):
            fn(pltpu.make_async_copy(kp_ref.at[0, bb, rows, hd, :], kbuf.at[s, hd], sem.at[s]))
            fn(pltpu.make_async_copy(vp_ref.at[0, bb, rows, hd, :], vbuf.at[s, hd], sem.at[s]))

    @pl.when(step == 0)
    def _():
        slab_copies(b, c, slot, lambda cp: cp.start())

    @pl.when(step + 1 < pl.num_programs(0) * n_c)
    def _():
        wrap = c + 1 == n_c
        slab_copies(jnp.where(wrap, b + 1, b), jnp.where(wrap, 0, c + 1), 1 - slot,
                    lambda cp: cp.start())

    slab_copies(b, c, slot, lambda cp: cp.wait())

    @pl.when(c == 0)
    def _():
        m_ref[...] = jnp.full_like(m_ref, NEG)
        l_ref[...] = jnp.zeros_like(l_ref)
        acc_ref[...] = jnp.zeros_like(acc_ref)

    def geometry(key0, n_keys):
        r = lax.broadcasted_iota(I32, (2 * lq, n_keys), 0)
        qpos = past + jnp.where(r >= lq, r - lq, r)
        kpos = key0 + lax.broadcasted_iota(I32, (2 * lq, n_keys), 1)
        allowed = (kpos >> CHUNK_SHIFT) <= (qpos >> CHUNK_SHIFT)
        return jnp.abs(qpos - kpos).astype(F32), jnp.where(allowed, 0.0, NEG)

    def block(hd, k, v, geo):
        dist, mask = geo
        q1, q2 = _split_q(q_ref[:, hd * DIFF_DV:(hd + 1) * DIFF_DV])
        s = _nt(jnp.concatenate([q1, q2], axis=0), k)
        s = s + (mask - slope_ref[hd] * dist)
        m_old = m_ref[hd]
        m_new = jnp.maximum(m_old, jnp.max(s, axis=-1, keepdims=True))
        alpha = jnp.exp(m_old - m_new)
        p = jnp.exp(s - m_new)
        l_ref[hd] = alpha * l_ref[hd] + jnp.sum(p, axis=-1, keepdims=True)
        acc_ref[hd] = alpha * acc_ref[hd] + jnp.dot(p.astype(BF16), v, preferred_element_type=F32)
        m_ref[hd] = m_new

    geo = geometry(c * chunk, chunk)
    for hd in range(DIFF_HEADS):
        block(hd, kbuf[slot, hd].astype(BF16), vbuf[slot, hd].astype(BF16), geo)

    @pl.when(c == n_c - 1)
    def _():
        lam = _lam(lq1, lk1, lq2, lk2)
        geo_new = geometry(past, lq)
        for hd in range(DIFF_HEADS):
            cols = slice(hd * DIFF_DV, (hd + 1) * DIFF_DV)
            block(hd, kn_ref[:, cols], vn_ref[:, cols], geo_new)
            o = acc_ref[hd] / l_ref[hd]
            o = o[:lq] - lam * o[lq:]
            o_ref[:, hd * DIFF_DV:(hd + 1) * DIFF_DV] = (
                _rms(o, dn_ref[...]) * (1.0 - LAM_INIT)).astype(o_ref.dtype)


def _attn_sample(qb, cache_k, cache_v, kb, vb, lq1, lk1, lq2, lk2, dn, B, L):
    past = cache_k.shape[2]
    chunk = min(SAMPLE_KEY_CHUNK, past)
    n_chunks = past // chunk
    bmap = lambda b, c: (b, 0)
    cmap = lambda b, c: (0, 0)
    lspec = pl.BlockSpec((1, DIFF_DH), cmap)
    tok = pl.BlockSpec((L, D_MODEL), bmap)
    cache = pl.BlockSpec(memory_space=pl.ANY)
    slabs = pltpu.VMEM((2, DIFF_HEADS, chunk, DIFF_DV), F32)
    return pl.pallas_call(
        functools.partial(_attn_sample_kernel, past=past, lq=L, chunk=chunk),
        grid=(B, n_chunks),
        in_specs=[pl.BlockSpec(memory_space=pltpu.SMEM), tok, cache, cache, tok, tok,
                  lspec, lspec, lspec, lspec, pl.BlockSpec((1, DIFF_DV), cmap)],
        out_specs=tok,
        out_shape=jax.ShapeDtypeStruct((B * L, D_MODEL), BF16),
        scratch_shapes=[pltpu.VMEM((DIFF_HEADS, 2 * L, 1), F32),
                        pltpu.VMEM((DIFF_HEADS, 2 * L, 1), F32),
                        pltpu.VMEM((DIFF_HEADS, 2 * L, DIFF_DV), F32),
                        slabs, slabs, pltpu.SemaphoreType.DMA((2,))],
        compiler_params=_cparams(("arbitrary", "arbitrary")),
        name="attn_sample",
    )(_alibi_slopes(), qb, cache_k, cache_v, kb, vb, lq1, lk1, lq2, lk2, dn)


ROUTE_E1, ROUTE_E2, ROUTE_W1, ROUTE_W2, ROUTE_R1, ROUTE_R2, ROUTE_P1, ROUTE_P2 = range(8)
ROUTE_ROWS = 8
TABLE_RUN, TABLE_OFF, TABLE_BASE = range(3)
GROUP_LANE0 = N_EXPERTS


def _merge_kernel(x_ref, oa_ref, ob_ref, ga_ref, gb_ref, wa_ref, wb_ref, wo_ref, nf_ref,
                  wr_ref, br_ref, x1_ref, h2_ref, route_ref, route_t_ref, cnt_ref, tables_ref,
                  run_ref, before_ref):
    i = pl.program_id(0)

    @pl.when(i == 0)
    def _():
        run_ref[...] = jnp.zeros_like(run_ref)
        r_i = lax.broadcasted_iota(I32, before_ref.shape, 0)
        c_i = lax.broadcasted_iota(I32, before_ref.shape, 1)
        before_ref[...] = jnp.where(c_i < r_i, 1.0, 0.0).astype(BF16)

    u_a = jnp.dot(oa_ref[...], wa_ref[...], preferred_element_type=F32)
    u_b = jnp.dot(ob_ref[...], wb_ref[...], preferred_element_type=F32)
    mix = (jax.nn.sigmoid(ga_ref[...].astype(F32)) * u_a
           + jax.nn.sigmoid(gb_ref[...].astype(F32)) * u_b)
    x1 = x_ref[...] + jnp.dot(mix.astype(BF16), wo_ref[...], preferred_element_type=F32)
    x1_ref[...] = x1
    h2 = _rms(x1, nf_ref[...])
    h2_ref[...] = h2

    tm = h2.shape[0]
    split = lambda a: (a.astype(BF16), (a - a.astype(BF16).astype(F32)).astype(BF16))
    w_pair = jnp.concatenate(split(wr_ref[...]), axis=1)
    parts = sum(jnp.dot(a, w_pair, preferred_element_type=F32) for a in split(h2))
    logits = parts[:, :LANES] + parts[:, LANES:] + br_ref[...]
    lane = lax.broadcasted_iota(I32, (tm, LANES), 1)
    big = jnp.int32(LANES)
    g_mask = (lane >= GROUP_LANE0) & (lane < GROUP_LANE0 + N_GROUPS)
    gl = jnp.where(g_mask, logits, -jnp.inf)
    gmax = jnp.max(gl, axis=-1, keepdims=True)
    g_sel = jnp.min(jnp.where(gl == gmax, lane - GROUP_LANE0, big), axis=-1, keepdims=True)
    p_g = 1.0 / jnp.sum(jnp.where(g_mask, jnp.exp(logits - gmax), 0.0), axis=-1, keepdims=True)
    e_mask = (lane < N_EXPERTS) & ((lane >> EPG_SHIFT) == g_sel)
    el = jnp.where(e_mask, logits, -jnp.inf)
    v1 = jnp.max(el, axis=-1, keepdims=True)
    i1 = jnp.min(jnp.where(el == v1, lane, big), axis=-1, keepdims=True)
    el2 = jnp.where(lane == i1, -jnp.inf, el)
    v2 = jnp.max(el2, axis=-1, keepdims=True)
    i2 = jnp.min(jnp.where(el2 == v2, lane, big), axis=-1, keepdims=True)
    t = jnp.exp(v2 - v1)
    w1 = p_g / (1.0 + t)
    w2 = p_g * t / (1.0 + t)

    oh1 = lane == i1
    oh2 = lane == i2
    cnt = jnp.where(oh1, 1.0, 0.0) + jnp.where(oh2, 1.0, 0.0)
    local = jnp.dot(before_ref[...], cnt.astype(BF16), preferred_element_type=F32)
    n_run = jnp.floor((jnp.sum(cnt, axis=0, keepdims=True) + (SUBLANES - 1.0)) * (1.0 / SUBLANES))
    n_run = n_run * SUBLANES
    e_r = lax.broadcasted_iota(I32, (LANES, LANES), 0)
    e_c = lax.broadcasted_iota(I32, (LANES, LANES), 1)
    earlier = jnp.where(e_r < e_c, 1.0, 0.0).astype(BF16)
    t_off = jnp.dot(jnp.broadcast_to(n_run, (SUBLANES, LANES)).astype(BF16), earlier,
                    preferred_element_type=F32)[:1]
    run = run_ref[...]
    pick = lambda oh, v: jnp.sum(jnp.where(oh, v, 0.0), axis=-1, keepdims=True)
    rank1, rank2 = pick(oh1, run + local), pick(oh2, run + local)
    pos1, pos2 = pick(oh1, t_off + local), pick(oh2, t_off + local)
    run_ref[...] = run + n_run
    cnt_ref[...] = run + n_run
    tables_ref[0] = jnp.concatenate([n_run, t_off, run, jnp.zeros((SUBLANES - 3, LANES), F32)], axis=0)

    route = jnp.zeros((tm, LANES), F32)
    for pos, val in ((ROUTE_E1, i1.astype(F32)), (ROUTE_E2, i2.astype(F32)), (ROUTE_W1, w1),
                     (ROUTE_W2, w2), (ROUTE_R1, rank1), (ROUTE_R2, rank2), (ROUTE_P1, pos1),
                     (ROUTE_P2, pos2)):
        route = jnp.where(lane == pos, val, route)
    route_ref[...] = route
    route_t_ref[...] = route.T[:ROUTE_ROWS, :].astype(I32)


def _merge(x, oa, ob, ga, gb, wa, wb, wo, nf, wr, br):
    T = x.shape[0]
    tm = min(TOKEN_TILE, T)
    row = lambda i: (i, 0)
    const = lambda i: (0, 0)
    wspec = pl.BlockSpec((D_MODEL, D_MODEL), const)
    tile = pl.BlockSpec((tm, D_MODEL), row)
    return pl.pallas_call(
        _merge_kernel,
        grid=(T // tm,),
        in_specs=[tile, tile, tile, tile, tile, wspec, wspec, wspec,
                  pl.BlockSpec((1, D_MODEL), const),
                  pl.BlockSpec((D_MODEL, LANES), const), pl.BlockSpec((1, LANES), const)],
        out_specs=(tile, tile, pl.BlockSpec((tm, LANES), row),
                   pl.BlockSpec((ROUTE_ROWS, tm), lambda i: (0, i)), pl.BlockSpec((1, LANES), const),
                   pl.BlockSpec((1, SUBLANES, LANES), lambda i: (i, 0, 0))),
        out_shape=(jax.ShapeDtypeStruct((T, D_MODEL), F32),
                   jax.ShapeDtypeStruct((T, D_MODEL), F32),
                   jax.ShapeDtypeStruct((T, LANES), F32),
                   jax.ShapeDtypeStruct((ROUTE_ROWS, T), I32),
                   jax.ShapeDtypeStruct((1, LANES), F32),
                   jax.ShapeDtypeStruct((T // tm, SUBLANES, LANES), F32)),
        scratch_shapes=[pltpu.VMEM((1, LANES), F32), pltpu.VMEM((tm, tm), BF16)],
        compiler_params=_cparams(("arbitrary",)),
        name="merge",
    )(x, oa, ob, ga, gb, wa, wb, wo, nf, wr, br)


RUN_BLOCKS = tuple(1 << b for b in range(9, 2, -1))


def _for_each_run_block(run_ref, off_ref, base_ref, step, fn):
    for e in range(N_EXPERTS):
        idx = step * N_EXPERTS + e
        n_left, src, dst = run_ref[idx], off_ref[idx], base_ref[idx]
        for rows in RUN_BLOCKS:
            take = n_left & rows

            @pl.when(take != 0)
            def _(src=src, dst=dst, rows=rows):
                fn(pl.multiple_of(src, SUBLANES), pl.multiple_of(dst, SUBLANES), rows)
            src, dst = src + take, dst + take


def _dispatch_kernel(ends_ref, run_ref, off_ref, base_ref, ridx_ref, h2_ref, xs_ref, zero_ref,
                     buf_ref, sem):
    i = pl.program_id(0)
    n = pl.num_programs(0)
    slot = i & 1
    slot_block = zero_ref.shape[0]
    slot_shift = slot_block.bit_length() - 1
    n_blocks = xs_ref.shape[0] // slot_block
    n_rows, tm = buf_ref.shape[1], h2_ref.shape[0]
    zero_sem = sem.at[2]

    def run_copies(step, s, fn):
        _for_each_run_block(run_ref, off_ref, base_ref, step, lambda src, dst, rows: fn(
            pltpu.make_async_copy(buf_ref.at[s, pl.ds(src, rows)], xs_ref.at[pl.ds(dst, rows)],
                                  sem.at[s])))

    @pl.when(i == 0)
    def _():
        zero_ref[...] = jnp.zeros_like(zero_ref)

        def zero_block(blk):
            start = pl.multiple_of(blk * slot_block, slot_block)
            return pltpu.make_async_copy(zero_ref, xs_ref.at[pl.ds(start, slot_block)], zero_sem)

        def expert_tail(e):
            return jnp.maximum((ends_ref[e] >> slot_shift) - 1, 0)

        n_used = ends_ref[N_EXPERTS - 1] >> slot_shift
        for e in range(N_EXPERTS):
            zero_block(expert_tail(e)).start()
        lax.fori_loop(n_used, n_blocks, lambda blk, c: (zero_block(blk).start(), c)[1], 0)
        for e in range(N_EXPERTS):
            zero_block(expert_tail(e)).wait()
        lax.fori_loop(n_used, n_blocks, lambda blk, c: (zero_block(blk).wait(), c)[1], 0)

    @pl.when(i >= 2)
    def _():
        run_copies(i - 2, slot, lambda cp: cp.wait())

    row = lax.broadcasted_iota(I32, (n_rows, tm), 0)
    picked = (row == ridx_ref[ROUTE_P1:ROUTE_P1 + 1, :]) | (row == ridx_ref[ROUTE_P2:ROUTE_P2 + 1, :])
    buf_ref[slot] = jnp.dot(jnp.where(picked, 1.0, 0.0).astype(BF16), h2_ref[...].astype(BF16),
                            preferred_element_type=F32)
    run_copies(i, slot, lambda cp: cp.start())

    @pl.when(i == n - 1)
    def _():
        run_copies(i, slot, lambda cp: cp.wait())

    @pl.when((i == n - 1) & (i >= 1))
    def _():
        run_copies(i - 1, 1 - slot, lambda cp: cp.wait())


def _dispatch(pad_ends, tables, ridx, h2, n_slots, slot_block):
    T = h2.shape[0]
    tm = min(TOKEN_TILE, T)
    smem = pl.BlockSpec(memory_space=pltpu.SMEM)
    n_rows = 2 * tm + N_EXPERTS * SUBLANES
    return pl.pallas_call(
        _dispatch_kernel,
        grid=(T // tm,),
        in_specs=[smem, smem, smem, smem,
                  pl.BlockSpec((ROUTE_ROWS, tm), lambda i: (0, i)),
                  pl.BlockSpec((tm, D_MODEL), lambda i: (i, 0))],
        out_specs=pl.BlockSpec(memory_space=pl.ANY),
        out_shape=jax.ShapeDtypeStruct((n_slots, D_MODEL), F32),
        scratch_shapes=[pltpu.VMEM((slot_block, D_MODEL), F32),
                        pltpu.VMEM((2, n_rows, D_MODEL), F32), pltpu.SemaphoreType.DMA((3,))],
        compiler_params=_cparams(("arbitrary",)),
        name="dispatch",
    )(pad_ends, *tables, ridx, h2)


def _experts_kernel(be_ref, nu_ref, xs_ref, wg_ref, wu_ref, wd_ref, y_ref, wg16, wu16, wd16):
    i = pl.program_id(0)
    used = i < nu_ref[0]

    @pl.when(used & ((i == 0) | (be_ref[i] != be_ref[jnp.maximum(i - 1, 0)])))
    def _():
        wg16[...] = wg_ref[0].astype(BF16)
        wu16[...] = wu_ref[0].astype(BF16)
        wd16[...] = wd_ref[0].astype(BF16)

    @pl.when(used)
    def _():
        x = xs_ref[...].astype(BF16)
        g = jnp.dot(x, wg16[...], preferred_element_type=F32)
        u = jnp.dot(x, wu16[...], preferred_element_type=F32)
        a = (g * jax.nn.sigmoid(g) * u).astype(BF16)
        y_ref[...] = jnp.dot(a, wd16[...], preferred_element_type=F32)

    @pl.when(jnp.logical_not(used))
    def _():
        y_ref[...] = jnp.zeros_like(y_ref)


def _experts(block_e, n_used, xs, wg, wu, wd, slot_block):
    n_blocks = xs.shape[0] // slot_block
    last = lambda i, nu: jnp.minimum(i, jnp.maximum(nu[0] - 1, 0))
    blk = lambda i, be, nu: (last(i, nu), 0)
    wmap = lambda i, be, nu: (be[last(i, nu)], 0, 0)
    grid_spec = pltpu.PrefetchScalarGridSpec(
        num_scalar_prefetch=2,
        grid=(n_blocks,),
        in_specs=[pl.BlockSpec((slot_block, D_MODEL), blk),
                  pl.BlockSpec((1, D_MODEL, D_EXPERT), wmap),
                  pl.BlockSpec((1, D_MODEL, D_EXPERT), wmap),
                  pl.BlockSpec((1, D_EXPERT, D_MODEL), wmap)],
        out_specs=pl.BlockSpec((slot_block, D_MODEL), lambda i, be, nu: (i, 0)),
        scratch_shapes=[pltpu.VMEM((D_MODEL, D_EXPERT), BF16), pltpu.VMEM((D_MODEL, D_EXPERT), BF16),
                        pltpu.VMEM((D_EXPERT, D_MODEL), BF16)],
    )
    return pl.pallas_call(
        _experts_kernel,
        grid_spec=grid_spec,
        out_shape=jax.ShapeDtypeStruct((xs.shape[0], D_MODEL), F32),
        compiler_params=_cparams(("arbitrary",)),
        name="experts",
    )(block_e, n_used, xs, wg, wu, wd)


def _combine_kernel(run_ref, off_ref, base_ref, x1_ref, route_ref, nf_ref, yb_ref, y_ref,
                    buf_ref, sem):
    i = pl.program_id(0)
    n = pl.num_programs(0)
    tm, n_rows = x1_ref.shape[0], buf_ref.shape[1]
    slot = i & 1

    def run_copies(step, s, fn):
        _for_each_run_block(run_ref, off_ref, base_ref, step, lambda dst, src, rows: fn(
            pltpu.make_async_copy(yb_ref.at[pl.ds(src, rows)], buf_ref.at[s, pl.ds(dst, rows)],
                                  sem.at[s])))

    @pl.when(i == 0)
    def _():
        buf_ref[...] = jnp.zeros_like(buf_ref)
        run_copies(i, slot, lambda cp: cp.start())

    @pl.when(i + 1 < n)
    def _():
        run_copies(i + 1, 1 - slot, lambda cp: cp.start())

    run_copies(i, slot, lambda cp: cp.wait())
    route = route_ref[...]
    rows_bf = buf_ref[slot].astype(BF16)
    col = lax.broadcasted_iota(I32, (tm, n_rows), 1)
    moe = jnp.zeros((tm, D_MODEL), F32)
    for w_lane, p_lane in ((ROUTE_W1, ROUTE_P1), (ROUTE_W2, ROUTE_P2)):
        pick = jnp.where(col == route[:, p_lane:p_lane + 1].astype(I32), 1.0, 0.0).astype(BF16)
        moe = moe + route[:, w_lane:w_lane + 1] * jnp.dot(pick, rows_bf, preferred_element_type=F32)
    y_ref[...] = _rms(x1_ref[...] + moe, nf_ref[...])


def _combine(tables, x1, route, nfinal, yb):
    T = x1.shape[0]
    tm = min(TOKEN_TILE, T)
    row = lambda i: (i, 0)
    smem = pl.BlockSpec(memory_space=pltpu.SMEM)
    n_rows = 2 * tm + N_EXPERTS * SUBLANES
    return pl.pallas_call(
        _combine_kernel,
        grid=(T // tm,),
        in_specs=[smem, smem, smem,
                  pl.BlockSpec((tm, D_MODEL), row), pl.BlockSpec((tm, LANES), row),
                  pl.BlockSpec((1, D_MODEL), lambda i: (0, 0)),
                  pl.BlockSpec(memory_space=pl.ANY)],
        out_specs=pl.BlockSpec((tm, D_MODEL), row),
        out_shape=jax.ShapeDtypeStruct((T, D_MODEL), F32),
        scratch_shapes=[pltpu.VMEM((2, n_rows, D_MODEL), F32), pltpu.SemaphoreType.DMA((2,))],
        compiler_params=_cparams(("arbitrary",)),
        name="combine",
    )(*tables, x1, route, nfinal, yb)


def _moe(x1, h2, route, route_t, counts, tables, wg, wu, wd, nfinal):
    T = x1.shape[0]
    tm = min(TOKEN_TILE, T)
    slot_block = SLOT_BLOCK if 2 * T >= N_EXPERTS * SLOT_BLOCK else SLOT_BLOCK_SMALL
    n_slots = T * 2 + (T // tm) * N_EXPERTS * SUBLANES + N_EXPERTS * slot_block
    n_blocks = -(-n_slots // slot_block)
    n_slots = n_blocks * slot_block
    cnt = counts[0, :N_EXPERTS].astype(I32)
    padded = (cnt + slot_block - 1) // slot_block * slot_block
    pad_ends = jnp.cumsum(padded)
    pad_starts = (pad_ends - padded).astype(I32)
    pad_ends = pad_ends.astype(I32)
    block_start = jnp.arange(n_blocks, dtype=I32) * slot_block
    block_e = jnp.sum((block_start[:, None] >= pad_ends[None, :]).astype(I32), axis=1)
    block_e = jnp.minimum(block_e, N_EXPERTS - 1)
    n_used = pad_ends[-1:] // slot_block
    per_run = lambda row: tables[:, row, :N_EXPERTS].astype(I32)
    run_tables = (per_run(TABLE_RUN).reshape(-1), per_run(TABLE_OFF).reshape(-1),
                  (per_run(TABLE_BASE) + pad_starts[None, :]).reshape(-1))
    xs = _dispatch(pad_ends, run_tables, route_t, h2, n_slots, slot_block)
    yb = _experts(block_e, n_used, xs, wg, wu, wd, slot_block)
    return _combine(run_tables, x1, route, nfinal, yb)


def _layer(x, s0, k_past, v_past, p, B, L):
    row2 = lambda a: a.reshape(1, -1)
    qka, va, ra, qb, kb, vb, ga, gb, da, k5, v5 = _in_proj(x, row2(p["norm_mix"]), p["w_in"])
    oa, s_new = _gla(qka, va, ra, da, p["w_decay"], row2(p["b_decay"]), row2(p["gla_norm"]),
                     s0, B, L)
    lams = [row2(p[n]) for n in ("lambda_q1", "lambda_k1", "lambda_q2", "lambda_k2")]
    dn = row2(p["diff_norm"])
    if k_past is None:
        ob = _attn_prompt(qb, kb, vb, *lams, dn, B, L)
    else:
        ob = _attn_sample(qb, k_past, v_past, kb, vb, *lams, dn, B, L)
    x1, h2, route, route_t, counts, tables = _merge(
        x, oa, ob, ga, gb, p["w_proj_a"], p["w_proj_b"], p["w_out"], row2(p["norm_ffn"]),
        p["w_router"], p["b_router"])
    y = _moe(x1, h2, route, route_t, counts, tables, p["w_gate"], p["w_up"], p["w_down"],
             row2(p["norm_final"]))
    return y, s_new, k5, v5


def kernel(x_prompt, x_sample, cache_k, cache_v, state_gla, norm_mix, w_in, w_decay, b_decay,
           gla_norm, w_proj_a, lambda_q1, lambda_k1, lambda_q2, lambda_k2, diff_norm, w_proj_b,
           w_out, norm_ffn, w_router_group, b_router_group, w_router_expert, b_router_expert,
           w_gate, w_up, w_down, norm_final):
    B, L, D = x_prompt.shape
    Bs, Ls, _ = x_sample.shape
    w_router = jnp.concatenate(
        [w_router_expert[0], w_router_group[0],
         jnp.zeros((D, LANES - N_EXPERTS - N_GROUPS), F32)], axis=1)
    b_router = jnp.concatenate(
        [b_router_expert[0], b_router_group[0],
         jnp.zeros((LANES - N_EXPERTS - N_GROUPS,), F32)]).reshape(1, LANES)
    p = dict(
        norm_mix=norm_mix[0], w_in=_prep_w_in(w_in[0]),
        w_decay=jnp.pad(w_decay[0], ((0, LANES - GLA_RANK), (0, 0))), b_decay=b_decay[0],
        gla_norm=gla_norm[0], w_proj_a=w_proj_a[0].astype(BF16),
        lambda_q1=lambda_q1[0], lambda_k1=lambda_k1[0], lambda_q2=lambda_q2[0],
        lambda_k2=lambda_k2[0], diff_norm=diff_norm[0], w_proj_b=w_proj_b[0].astype(BF16),
        w_out=w_out[0].astype(BF16), norm_ffn=norm_ffn[0], w_router=w_router, b_router=b_router,
        w_gate=w_gate[0], w_up=w_up[0], w_down=w_down[0],
        norm_final=norm_final)

    s0p = jnp.zeros((B, GLA_HEADS, GLA_DK, GLA_DV), F32)
    yp, sp, kp, vp = _layer(x_prompt.reshape(B * L, D), s0p, None, None, p, B, L)
    ys, ss, ks, vs = _layer(x_sample.reshape(Bs * Ls, D), state_gla[0], cache_k, cache_v,
                            p, Bs, Ls)
    kv = lambda a, b, l: a.reshape(1, b, l, DIFF_HEADS, DIFF_DV)
    return (yp.reshape(B, L, D), ys.reshape(Bs, Ls, D), kv(kp, B, L), kv(vp, B, L), sp[None],
            kv(ks, Bs, Ls), kv(vs, Bs, Ls), ss[None])
```

```python
import functools

import jax
import jax.numpy as jnp
from jax import lax
from jax.experimental import pallas as pl
from jax.experimental.pallas import tpu as pltpu

F32 = jnp.float32
BF16 = jnp.bfloat16
I32 = jnp.int32

D_MODEL = 1024
CHUNK = 64
CHUNK_SHIFT = 6
EPS = 1e-6
GLA_HEADS = 4
GLA_DK = 128
GLA_DV = 256
GLA_RANK = 16
GLA_TAU = 16.0
DIFF_HEADS = 8
DIFF_DH = 64
DIFF_DV = 128
N_GROUPS = 4
EXPERTS_PER_GROUP = 4
EPG_SHIFT = 2
N_EXPERTS = 16
D_EXPERT = 512
LAM_INIT = 0.8 - 0.6

LANES = 128
SUBLANES = 8
RUN_ALIGN = 16
TOKEN_TILE = 512
SLOT_BLOCK = 512
SLOT_BLOCK_SMALL = 128
ATT_TILE = 512
ATT_HEADS_PER_STEP = 2
SAMPLE_KEY_CHUNK = 1024
GLA_CUMSUM_ROWS = 256
VMEM_LIMIT = 56 * 1024 * 1024
NEG = -1e30

HI = lax.Precision.HIGHEST


def _cparams(sem):
    return pltpu.CompilerParams(dimension_semantics=sem, vmem_limit_bytes=VMEM_LIMIT)


def _nt(a, b):
    return lax.dot_general(a, b, (((1,), (1,)), ((), ())), preferred_element_type=F32)


def _tn(a, b):
    return lax.dot_general(a, b, (((0,), (0,)), ((), ())), preferred_element_type=F32)


def _rms(x, g):
    return x * lax.rsqrt(jnp.mean(x * x, axis=-1, keepdims=True) + EPS) * g


def _inproj_kernel(x_ref, g_ref, wa_ref, wda_ref, wb_ref, qka_ref, va_ref, ra_ref, qb_ref, kb_ref,
                   vb_ref, ga_ref, gb_ref, da_ref, k5_ref, v5_ref, stage_ref, sem):
    i = pl.program_id(0)
    n = pl.num_programs(0)
    tm = x_ref.shape[0]
    slot = i & 1

    def head_copy(s, which, hd):
        dst = (k5_ref, v5_ref)[which]
        return pltpu.make_async_copy(stage_ref.at[s, which, hd],
                                     dst.at[pl.ds(i * tm, tm), hd, :], sem.at[s])

    def for_all(s, fn):
        for which in range(2):
            for hd in range(DIFF_HEADS):
                fn(head_copy(s, which, hd))

    @pl.when(i >= 2)
    def _():
        for_all(slot, lambda cp: cp.wait())

    h = _rms(x_ref[...], g_ref[...]).astype(BF16)
    outs = ((qka_ref, wa_ref, 0), (va_ref, wa_ref, 1), (ra_ref, wa_ref, 2), (qb_ref, wb_ref, 0),
            (kb_ref, wb_ref, 1), (vb_ref, wb_ref, 2), (ga_ref, wb_ref, 3), (gb_ref, wb_ref, 4))
    for o, w_ref, c in outs:
        z = jnp.dot(h, w_ref[:, c * D_MODEL:(c + 1) * D_MODEL], preferred_element_type=F32)
        o[...] = z.astype(o.dtype)
        which = 0 if o is kb_ref else 1 if o is vb_ref else None
        if which is not None:
            for hd in range(DIFF_HEADS):
                stage_ref[slot, which, hd] = z[:, hd * DIFF_DV:(hd + 1) * DIFF_DV]
    da_ref[...] = jnp.dot(h, wda_ref[...], preferred_element_type=F32)
    for_all(slot, lambda cp: cp.start())

    @pl.when(i == n - 1)
    def _():
        for_all(slot, lambda cp: cp.wait())

    @pl.when((i == n - 1) & (i >= 1))
    def _():
        for_all(1 - slot, lambda cp: cp.wait())


N_GLA_COLS = 3 * D_MODEL
N_DIFF_COLS = 5 * D_MODEL


def _prep_w_in(w_in):
    w = w_in.astype(BF16)
    return w, w[:, N_GLA_COLS + GLA_RANK:]


def _in_proj(x, g, w):
    w_all, w_diff = w
    T = x.shape[0]
    tm = min(TOKEN_TILE, T)
    resident = lambda cols, j: pl.BlockSpec((D_MODEL, cols), lambda i: (0, j),
                                            pipeline_mode=pl.Buffered(1))
    row = lambda i: (i, 0)
    const = lambda i: (0, 0)
    wide = lambda dt: jax.ShapeDtypeStruct((T, D_MODEL), dt)
    heads = jax.ShapeDtypeStruct((T, DIFF_HEADS, DIFF_DV), F32)
    out_shape = (wide(BF16),) * 8 + (jax.ShapeDtypeStruct((T, LANES), F32), heads, heads)
    out_specs = tuple([pl.BlockSpec((tm, D_MODEL), row)] * 8 + [pl.BlockSpec((tm, LANES), row)]
                      + [pl.BlockSpec(memory_space=pl.ANY)] * 2)
    return pl.pallas_call(
        _inproj_kernel,
        grid=(T // tm,),
        in_specs=[pl.BlockSpec((tm, D_MODEL), row),
                  pl.BlockSpec((1, D_MODEL), const),
                  resident(N_GLA_COLS, 0), resident(LANES, N_GLA_COLS // LANES),
                  resident(N_DIFF_COLS, 0)],
        out_specs=out_specs,
        out_shape=out_shape,
        scratch_shapes=[pltpu.VMEM((2, 2, DIFF_HEADS, tm, DIFF_DV), F32),
                        pltpu.SemaphoreType.DMA((2,))],
        compiler_params=_cparams(("arbitrary",)),
        name="in_proj",
    )(x, g, w_all, w_all, w_diff)


def _gla_kernel(qka_ref, va_ref, ra_ref, da_ref, wd_ref, bd_ref, gn_ref, s0_ref,
                oa_ref, sout_ref, s_ref, la_ref, *, chunk, n_chunks):
    l = pl.program_id(1)

    @pl.when(l == 0)
    def _():
        s_ref[...] = s0_ref[0]

    split = lambda a: (a.astype(BF16), (a - a.astype(BF16).astype(F32)).astype(BF16))
    n_k = GLA_HEADS * GLA_DK
    fold = lambda z: z[:, :n_k] + z[:, n_k:]

    wd_pair = jnp.concatenate(split(wd_ref[...]), axis=1)
    x = fold(sum(jnp.dot(a, wd_pair, preferred_element_type=F32) for a in split(da_ref[...])))
    x = x + bd_ref[...]
    log_a = (jnp.minimum(x, 0.0) - jnp.log1p(jnp.exp(-jnp.abs(x)))) * (1.0 / GLA_TAU)
    lb = log_a.shape[0]
    grp = min(lb, GLA_CUMSUM_ROWS)
    shift = chunk.bit_length() - 1
    r_b = lax.broadcasted_iota(I32, (grp, grp), 0)
    c_b = lax.broadcasted_iota(I32, (grp, grp), 1)
    tri = jnp.where((c_b <= r_b) & ((c_b >> shift) == (r_b >> shift)), 1.0, 0.0).astype(BF16)
    for g in range(lb // grp):
        rows = slice(g * grp, (g + 1) * grp)
        la_ref[rows, :] = fold(jnp.dot(tri, jnp.concatenate(split(log_a[rows, :]), axis=1),
                                       preferred_element_type=F32))

    r_i = lax.broadcasted_iota(I32, (chunk, chunk), 0)
    c_i = lax.broadcasted_iota(I32, (chunk, chunk), 1)
    causal = c_i <= r_i
    gn = gn_ref[...]
    scale = GLA_DK ** -0.5
    heads = range(GLA_HEADS)
    vcols = [slice(h * GLA_DV, (h + 1) * GLA_DV) for h in heads]

    @pl.loop(0, n_chunks)
    def _(c):
        rows = pl.ds(pl.multiple_of(c * chunk, chunk), chunk)
        b_all = la_ref[rows, :]
        q_t, k_t, k_end, decay = [], [], [], []
        for h in heads:
            b = b_all[:, h * GLA_DK:(h + 1) * GLA_DK]
            b_last = b[chunk - 1:chunk, :]
            q = qka_ref[rows, h * GLA_DK:(h + 1) * GLA_DK].astype(F32) * scale
            k = qka_ref[rows, n_k + h * GLA_DK:n_k + (h + 1) * GLA_DK].astype(F32)
            q_t.append((q * jnp.exp(b)).astype(BF16))
            k_t.append((k * jnp.exp(-b)).astype(BF16))
            k_end.append((k * jnp.exp(b_last - b)).astype(BF16))
            decay.append(jnp.broadcast_to(jnp.exp(b_last), (GLA_DK, GLA_DK)).T)
        s_old = [s_ref[h] for h in heads]
        o_state = [jnp.dot(q_t[h], s_old[h].astype(BF16), preferred_element_type=F32) for h in heads]
        att = [_nt(q_t[h], k_t[h]) for h in heads]
        kv = [_tn(k_end[h], va_ref[rows, vcols[h]]) for h in heads]
        for h in heads:
            a = jnp.where(causal, att[h], 0.0).astype(BF16)
            o = o_state[h] + jnp.dot(a, va_ref[rows, vcols[h]], preferred_element_type=F32)
            s_ref[h] = jnp.concatenate([decay[h], decay[h]], axis=1) * s_old[h] + kv[h]
            r = ra_ref[rows, vcols[h]].astype(F32)
            oa_ref[rows, vcols[h]] = (_rms(o, gn) * (r * jax.nn.sigmoid(r))).astype(oa_ref.dtype)

    @pl.when(l == pl.num_programs(1) - 1)
    def _():
        sout_ref[0] = s_ref[...]


def _gla(qka, va, ra, da, wd, bd, gn, s0, B, L):
    chunk = min(CHUNK, L)
    lb = min(TOKEN_TILE, L)
    nl = L // lb
    row = lambda b, l: (b * nl + l, 0)
    const2 = lambda b, l: (0, 0)
    st = lambda b, l: (b, 0, 0, 0)
    kern = functools.partial(_gla_kernel, chunk=chunk, n_chunks=lb // chunk)
    return pl.pallas_call(
        kern,
        grid=(B, nl),
        in_specs=[pl.BlockSpec((lb, D_MODEL), row), pl.BlockSpec((lb, D_MODEL), row),
                  pl.BlockSpec((lb, D_MODEL), row), pl.BlockSpec((lb, LANES), row),
                  pl.BlockSpec((LANES, GLA_HEADS * GLA_DK), const2),
                  pl.BlockSpec((1, GLA_HEADS * GLA_DK), const2),
                  pl.BlockSpec((1, GLA_DV), const2),
                  pl.BlockSpec((1, GLA_HEADS, GLA_DK, GLA_DV), st)],
        out_specs=(pl.BlockSpec((lb, D_MODEL), row),
                   pl.BlockSpec((1, GLA_HEADS, GLA_DK, GLA_DV), st)),
        out_shape=(jax.ShapeDtypeStruct((B * L, D_MODEL), BF16),
                   jax.ShapeDtypeStruct((B, GLA_HEADS, GLA_DK, GLA_DV), F32)),
        scratch_shapes=[pltpu.VMEM((GLA_HEADS, GLA_DK, GLA_DV), F32),
                        pltpu.VMEM((lb, GLA_HEADS * GLA_DK), F32)],
        compiler_params=_cparams(("arbitrary", "arbitrary")),
        name="gla",
    )(qka, va, ra, da, wd, bd, gn, s0)


def _lam(lq1, lk1, lq2, lk2):
    a = jnp.sum(lq1[...] * lk1[...], axis=-1, keepdims=True)
    b = jnp.sum(lq2[...] * lk2[...], axis=-1, keepdims=True)
    return jnp.exp(a) - jnp.exp(b) + LAM_INIT


def _split_q(q):
    lane = lax.broadcasted_iota(I32, q.shape, 1)
    qs = q * jnp.asarray(DIFF_DH ** -0.5, q.dtype)
    zero = jnp.zeros_like(qs)
    return jnp.where(lane < DIFF_DH, qs, zero), jnp.where(lane >= DIFF_DH, qs, zero)


def _alibi_slopes():
    return jnp.asarray([2.0 ** (-8.0 * (h + 1) / DIFF_HEADS) for h in range(DIFF_HEADS)], F32)


def _attn_prompt_kernel(slope_ref, q_ref, k_ref, v_ref, lq1, lk1, lq2, lk2, dn_ref, o_ref,
                        k1_ref, k2_ref, vt_ref, geo_ref, *, tile, nh):
    qi = pl.program_id(2)
    n_kv = vt_ref.shape[1]
    n_cols = tile // LANES
    lane_k = lax.broadcasted_iota(I32, (tile, DIFF_DV), 1)
    row_i = lax.broadcasted_iota(I32, (tile, DIFF_DV), 0)
    row_lo = (row_i & 255).astype(F32)
    row_hi = (row_i & -256).astype(F32)
    slopes = [slope_ref[pl.program_id(1) * nh + g] for g in range(nh)]
    hcols = [slice(g * DIFF_DV, (g + 1) * DIFF_DV) for g in range(nh)]

    @pl.when(qi == 0)
    def _():
        for g in range(nh):
            kfeat = jnp.where(lane_k == DIFF_DH, slopes[g] * row_lo,
                              jnp.where(lane_k == DIFF_DH + 1, slopes[g] * row_hi,
                                        jnp.where((lane_k == DIFF_DH + 2) | (lane_k == DIFF_DH + 3),
                                                  1.0, 0.0)))
            for c in range(n_kv):
                rows = slice(c * tile, (c + 1) * tile)
                kf = k_ref[rows, hcols[g]].astype(F32)
                k1_ref[g, rows, :] = jnp.where(lane_k < DIFF_DH, kf, kfeat).astype(BF16)
                k2_ref[g, rows, :] = jnp.where(lane_k < DIFF_DH, pltpu.roll(kf, DIFF_DH, 1),
                                               kfeat).astype(BF16)
                vt_ref[g, c] = v_ref[rows, hcols[g]].astype(F32).T.astype(BF16)

    qs = []
    for g in range(nh):
        qf = q_ref[:, hcols[g]].astype(F32) * (DIFF_DH ** -0.5)
        qfeat = jnp.where((lane_k == DIFF_DH) | (lane_k == DIFF_DH + 1), 1.0,
                          jnp.where(lane_k == DIFF_DH + 2, -slopes[g] * row_lo,
                                    jnp.where(lane_k == DIFF_DH + 3, -slopes[g] * row_hi, 0.0)))
        qs.append((jnp.where(lane_k < DIFF_DH, qf, qfeat).T.astype(BF16),
                   jnp.where(lane_k < DIFF_DH, pltpu.roll(qf, DIFF_DH, 1), qfeat).T.astype(BF16)))

    n_maps = 2 * nh
    n_stats = 2 * n_cols + 1

    def scores(i, j):
        rows = pl.ds(pl.multiple_of(j * tile, tile), tile)
        k_ref_i = k1_ref if i % 2 == 0 else k2_ref
        return jnp.dot(k_ref_i[i // 2, rows, :], qs[i // 2][i % 2], preferred_element_type=F32)

    def softmax_part(stats, s, c, fix):
        out, alphas, ps = (), [], []
        for col in range(n_cols):
            lanes = slice(col * LANES, (col + 1) * LANES)
            n_keys = tile if fix is None else (col + 1) * LANES
            sh = s[:n_keys, lanes]
            if fix is not None:
                sh = sh + fix[:n_keys, lanes]
            m, l = stats[2 * col], stats[2 * col + 1]
            m_new = jnp.maximum(m, jnp.max(sh, axis=0, keepdims=True) + c)
            alpha = jnp.exp(m - m_new)
            p = jnp.exp(sh - (m_new - c))
            out += (m_new, alpha * l + jnp.sum(p, axis=0, keepdims=True))
            p = p.astype(BF16)
            alphas.append(jnp.broadcast_to(alpha, (DIFF_DV, LANES)))
            if n_keys < tile:
                p = jnp.concatenate([p, jnp.zeros((tile - n_keys, LANES), BF16)], axis=0)
            ps.append(p)
        return out, jnp.concatenate(alphas, axis=1), jnp.concatenate(ps, axis=1)

    def step(stats, j, cs, fixes):
        ss = [scores(i, j) for i in range(n_maps)]
        parts = [softmax_part(stats[n_stats * i:n_stats * (i + 1)], ss[i], cs[i // 2],
                              None if fixes is None else fixes[i // 2])
                 for i in range(n_maps)]
        out = ()
        for i, (st, alpha, p) in enumerate(parts):
            a = alpha * stats[n_stats * i + n_stats - 1] + jnp.dot(
                vt_ref[i // 2, j], p, preferred_element_type=F32)
            out += st + (a,)
        return out

    def body(j, stats):
        off = jnp.full((1, LANES), (j - qi) * tile, I32).astype(F32)
        return step(stats, j, [off * slopes[g] for g in range(nh)], None)

    row = lambda v: jnp.full((1, LANES), v, F32)
    init = ((row(NEG), row(0.0)) * n_cols + (jnp.zeros((DIFF_DV, tile), F32),)) * n_maps
    stats = lax.fori_loop(0, qi, body, init)

    @pl.when((pl.program_id(0) == 0) & (pl.program_id(1) == 0) & (qi == 0))
    def _():
        r_i = lax.broadcasted_iota(I32, (tile, tile), 0)
        c_i = lax.broadcasted_iota(I32, (tile, tile), 1)
        allowed = (r_i >> CHUNK_SHIFT) <= (c_i >> CHUNK_SHIFT)
        geo_ref[0] = jnp.where(r_i > c_i, (c_i - r_i).astype(F32), 0.0)
        geo_ref[1] = jnp.where(allowed, 0.0, NEG)

    fixes = [(2.0 * slopes[g]) * geo_ref[0] + geo_ref[1] for g in range(nh)]
    stats = step(stats, qi, [row(0.0)] * nh, fixes)
    lam = _lam(lq1, lk1, lq2, lk2)
    for g in range(nh):
        res = []
        for i in (2 * g, 2 * g + 1):
            st = stats[n_stats * i:n_stats * (i + 1)]
            l = jnp.concatenate([jnp.broadcast_to(st[2 * col + 1], (DIFF_DV, LANES))
                                 for col in range(n_cols)], axis=1)
            res.append(st[-1] / l)
        o = (res[0] - lam * res[1]).T
        o_ref[:, hcols[g]] = (_rms(o, dn_ref[...]) * (1.0 - LAM_INIT)).astype(o_ref.dtype)


def _attn_prompt(qb, kb, vb, lq1, lk1, lq2, lk2, dn, B, L):
    tile = min(ATT_TILE, L)
    nq = L // tile
    nh = ATT_HEADS_PER_STEP
    qmap = lambda b, h, i: (b * nq + i, h)
    kvmap = lambda b, h, i: (b, h)
    cmap = lambda b, h, i: (0, 0)
    lspec = pl.BlockSpec((1, DIFF_DH), cmap)
    return pl.pallas_call(
        functools.partial(_attn_prompt_kernel, tile=tile, nh=nh),
        grid=(B, DIFF_HEADS // nh, nq),
        in_specs=[pl.BlockSpec(memory_space=pltpu.SMEM),
                  pl.BlockSpec((tile, nh * DIFF_DV), qmap),
                  pl.BlockSpec((L, nh * DIFF_DV), kvmap), pl.BlockSpec((L, nh * DIFF_DV), kvmap),
                  lspec, lspec, lspec, lspec, pl.BlockSpec((1, DIFF_DV), cmap)],
        out_specs=pl.BlockSpec((tile, nh * DIFF_DV), qmap),
        out_shape=jax.ShapeDtypeStruct((B * L, D_MODEL), BF16),
        scratch_shapes=[pltpu.VMEM((nh, L, DIFF_DV), BF16), pltpu.VMEM((nh, L, DIFF_DV), BF16),
                        pltpu.VMEM((nh, nq, DIFF_DV, tile), BF16),
                        pltpu.VMEM((2, tile, tile), F32)],
        compiler_params=_cparams(("arbitrary", "arbitrary", "arbitrary")),
        name="attn_prompt",
    )(_alibi_slopes(), qb, kb, vb, lq1, lk1, lq2, lk2, dn)


def _attn_sample_kernel(slope_ref, q_ref, kp_ref, vp_ref, kn_ref, vn_ref, lq1, lk1, lq2, lk2,
                        dn_ref, o_ref, m_ref, l_ref, acc_ref, kbuf, vbuf, sem, *, past, lq, chunk):
    b = pl.program_id(0)
    c = pl.program_id(1)
    n_c = pl.num_programs(1)
    step = b * n_c + c
    slot = step & 1

    def slab_copies(bb, cc, s, fn):
        rows = pl.ds(cc * chunk, chunk)
        for hd in range(DIFF_HEADS):
            fn(pltpu.make_async_copy(kp_ref.at[0, bb, rows, hd, :], kbuf.at[s, hd], sem.at[s]))
            fn(pltpu.make_async_copy(vp_ref.at[0, bb, rows, hd, :], vbuf.at[s, hd], sem.at[s]))

    @pl.when(step == 0)
    def _():
        slab_copies(b, c, slot, lambda cp: cp.start())

    @pl.when(step + 1 < pl.num_programs(0) * n_c)
    def _():
        wrap = c + 1 == n_c
        slab_copies(jnp.where(wrap, b + 1, b), jnp.where(wrap, 0, c + 1), 1 - slot,
                    lambda cp: cp.start())

    slab_copies(b, c, slot, lambda cp: cp.wait())

    @pl.when(c == 0)
    def _():
        m_ref[...] = jnp.full_like(m_ref, NEG)
        l_ref[...] = jnp.zeros_like(l_ref)
        acc_ref[...] = jnp.zeros_like(acc_ref)

    def geometry(key0, n_keys):
        r = lax.broadcasted_iota(I32, (2 * lq, n_keys), 0)
        qpos = past + jnp.where(r >= lq, r - lq, r)
        kpos = key0 + lax.broadcasted_iota(I32, (2 * lq, n_keys), 1)
        allowed = (kpos >> CHUNK_SHIFT) <= (qpos >> CHUNK_SHIFT)
        return jnp.abs(qpos - kpos).astype(F32), jnp.where(allowed, 0.0, NEG)

    def block(hd, k, v, geo):
        dist, mask = geo
        q1, q2 = _split_q(q_ref[:, hd * DIFF_DV:(hd + 1) * DIFF_DV])
        s = _nt(jnp.concatenate([q1, q2], axis=0), k)
        s = s + (mask - slope_ref[hd] * dist)
        m_old = m_ref[hd]
        m_new = jnp.maximum(m_old, jnp.max(s, axis=-1, keepdims=True))
        alpha = jnp.exp(m_old - m_new)
        p = jnp.exp(s - m_new)
        l_ref[hd] = alpha * l_ref[hd] + jnp.sum(p, axis=-1, keepdims=True)
        acc_ref[hd] = alpha * acc_ref[hd] + jnp.dot(p.astype(BF16), v, preferred_element_type=F32)
        m_ref[hd] = m_new

    geo = geometry(c * chunk, chunk)
    for hd in range(DIFF_HEADS):
        block(hd, kbuf[slot, hd].astype(BF16), vbuf[slot, hd].astype(BF16), geo)

    @pl.when(c == n_c - 1)
    def _():
        lam = _lam(lq1, lk1, lq2, lk2)
        geo_new = geometry(past, lq)
        for hd in range(DIFF_HEADS):
            cols = slice(hd * DIFF_DV, (hd + 1) * DIFF_DV)
            block(hd, kn_ref[:, cols], vn_ref[:, cols], geo_new)
            o = acc_ref[hd] / l_ref[hd]
            o = o[:lq] - lam * o[lq:]
            o_ref[:, hd * DIFF_DV:(hd + 1) * DIFF_DV] = (
                _rms(o, dn_ref[...]) * (1.0 - LAM_INIT)).astype(o_ref.dtype)


def _attn_sample(qb, cache_k, cache_v, kb, vb, lq1, lk1, lq2, lk2, dn, B, L):
    past = cache_k.shape[2]
    chunk = min(SAMPLE_KEY_CHUNK, past)
    n_chunks = past // chunk
    bmap = lambda b, c: (b, 0)
    cmap = lambda b, c: (0, 0)
    lspec = pl.BlockSpec((1, DIFF_DH), cmap)
    tok = pl.BlockSpec((L, D_MODEL), bmap)
    cache = pl.BlockSpec(memory_space=pl.ANY)
    slabs = pltpu.VMEM((2, DIFF_HEADS, chunk, DIFF_DV), F32)
    return pl.pallas_call(
        functools.partial(_attn_sample_kernel, past=past, lq=L, chunk=chunk),
        grid=(B, n_chunks),
        in_specs=[pl.BlockSpec(memory_space=pltpu.SMEM), tok, cache, cache, tok, tok,
                  lspec, lspec, lspec, lspec, pl.BlockSpec((1, DIFF_DV), cmap)],
        out_specs=tok,
        out_shape=jax.ShapeDtypeStruct((B * L, D_MODEL), BF16),
        scratch_shapes=[pltpu.VMEM((DIFF_HEADS, 2 * L, 1), F32),
                        pltpu.VMEM((DIFF_HEADS, 2 * L, 1), F32),
                        pltpu.VMEM((DIFF_HEADS, 2 * L, DIFF_DV), F32),
                        slabs, slabs, pltpu.SemaphoreType.DMA((2,))],
        compiler_params=_cparams(("arbitrary", "arbitrary")),
        name="attn_sample",
    )(_alibi_slopes(), qb, cache_k, cache_v, kb, vb, lq1, lk1, lq2, lk2, dn)


ROUTE_E1, ROUTE_E2, ROUTE_W1, ROUTE_W2, ROUTE_R1, ROUTE_R2, ROUTE_P1, ROUTE_P2 = range(8)
ROUTE_ROWS = 8
TABLE_RUN, TABLE_OFF, TABLE_BASE = range(3)
GROUP_LANE0 = N_EXPERTS


def _merge_kernel(x_ref, oa_ref, ob_ref, ga_ref, gb_ref, wa_ref, wb_ref, wo_ref, nf_ref,
                  wr_ref, br_ref, x1_ref, h2_ref, route_ref, route_t_ref, cnt_ref, tables_ref,
                  run_ref, before_ref):
    i = pl.program_id(0)

    @pl.when(i == 0)
    def _():
        run_ref[...] = jnp.zeros_like(run_ref)
        r_i = lax.broadcasted_iota(I32, before_ref.shape, 0)
        c_i = lax.broadcasted_iota(I32, before_ref.shape, 1)
        before_ref[...] = jnp.where(c_i < r_i, 1.0, 0.0).astype(BF16)

    u_a = jnp.dot(oa_ref[...], wa_ref[...], preferred_element_type=F32)
    u_b = jnp.dot(ob_ref[...], wb_ref[...], preferred_element_type=F32)
    mix = (jax.nn.sigmoid(ga_ref[...].astype(F32)) * u_a
           + jax.nn.sigmoid(gb_ref[...].astype(F32)) * u_b)
    x1 = x_ref[...] + jnp.dot(mix.astype(BF16), wo_ref[...], preferred_element_type=F32)
    x1_ref[...] = x1
    h2 = _rms(x1, nf_ref[...])
    h2_ref[...] = h2.astype(h2_ref.dtype)

    tm = h2.shape[0]
    split = lambda a: (a.astype(BF16), (a - a.astype(BF16).astype(F32)).astype(BF16))
    w_pair = jnp.concatenate(split(wr_ref[...]), axis=1)
    parts = sum(jnp.dot(a, w_pair, preferred_element_type=F32) for a in split(h2))
    logits = parts[:, :LANES] + parts[:, LANES:] + br_ref[...]
    lane = lax.broadcasted_iota(I32, (tm, LANES), 1)
    big = jnp.int32(LANES)
    g_mask = (lane >= GROUP_LANE0) & (lane < GROUP_LANE0 + N_GROUPS)
    gl = jnp.where(g_mask, logits, -jnp.inf)
    gmax = jnp.max(gl, axis=-1, keepdims=True)
    g_sel = jnp.min(jnp.where(gl == gmax, lane - GROUP_LANE0, big), axis=-1, keepdims=True)
    p_g = 1.0 / jnp.sum(jnp.where(g_mask, jnp.exp(logits - gmax), 0.0), axis=-1, keepdims=True)
    e_mask = (lane < N_EXPERTS) & ((lane >> EPG_SHIFT) == g_sel)
    el = jnp.where(e_mask, logits, -jnp.inf)
    v1 = jnp.max(el, axis=-1, keepdims=True)
    i1 = jnp.min(jnp.where(el == v1, lane, big), axis=-1, keepdims=True)
    el2 = jnp.where(lane == i1, -jnp.inf, el)
    v2 = jnp.max(el2, axis=-1, keepdims=True)
    i2 = jnp.min(jnp.where(el2 == v2, lane, big), axis=-1, keepdims=True)
    t = jnp.exp(v2 - v1)
    w1 = p_g / (1.0 + t)
    w2 = p_g * t / (1.0 + t)

    oh1 = lane == i1
    oh2 = lane == i2
    cnt = jnp.where(oh1, 1.0, 0.0) + jnp.where(oh2, 1.0, 0.0)
    local = jnp.dot(before_ref[...], cnt.astype(BF16), preferred_element_type=F32)
    n_run = jnp.floor((jnp.sum(cnt, axis=0, keepdims=True) + (RUN_ALIGN - 1.0)) * (1.0 / RUN_ALIGN))
    n_run = n_run * RUN_ALIGN
    e_r = lax.broadcasted_iota(I32, (LANES, LANES), 0)
    e_c = lax.broadcasted_iota(I32, (LANES, LANES), 1)
    earlier = jnp.where(e_r < e_c, 1.0, 0.0).astype(BF16)
    t_off = jnp.dot(jnp.broadcast_to(n_run, (SUBLANES, LANES)).astype(BF16), earlier,
                    preferred_element_type=F32)[:1]
    run = run_ref[...]
    pick = lambda oh, v: jnp.sum(jnp.where(oh, v, 0.0), axis=-1, keepdims=True)
    rank1, rank2 = pick(oh1, run + local), pick(oh2, run + local)
    pos1, pos2 = pick(oh1, t_off + local), pick(oh2, t_off + local)
    run_ref[...] = run + n_run
    cnt_ref[...] = run + n_run
    tables_ref[0] = jnp.concatenate([n_run, t_off, run, jnp.zeros((SUBLANES - 3, LANES), F32)], axis=0)

    route = jnp.zeros((tm, LANES), F32)
    for pos, val in ((ROUTE_E1, i1.astype(F32)), (ROUTE_E2, i2.astype(F32)), (ROUTE_W1, w1),
                     (ROUTE_W2, w2), (ROUTE_R1, rank1), (ROUTE_R2, rank2), (ROUTE_P1, pos1),
                     (ROUTE_P2, pos2)):
        route = jnp.where(lane == pos, val, route)
    route_ref[...] = route
    route_t_ref[...] = route.T[:ROUTE_ROWS, :].astype(I32)


def _merge(x, oa, ob, ga, gb, wa, wb, wo, nf, wr, br):
    T = x.shape[0]
    tm = min(TOKEN_TILE, T)
    row = lambda i: (i, 0)
    const = lambda i: (0, 0)
    wspec = pl.BlockSpec((D_MODEL, D_MODEL), const)
    tile = pl.BlockSpec((tm, D_MODEL), row)
    return pl.pallas_call(
        _merge_kernel,
        grid=(T // tm,),
        in_specs=[tile, tile, tile, tile, tile, wspec, wspec, wspec,
                  pl.BlockSpec((1, D_MODEL), const),
                  pl.BlockSpec((D_MODEL, LANES), const), pl.BlockSpec((1, LANES), const)],
        out_specs=(tile, tile, pl.BlockSpec((tm, LANES), row),
                   pl.BlockSpec((ROUTE_ROWS, tm), lambda i: (0, i)), pl.BlockSpec((1, LANES), const),
                   pl.BlockSpec((1, SUBLANES, LANES), lambda i: (i, 0, 0))),
        out_shape=(jax.ShapeDtypeStruct((T, D_MODEL), F32),
                   jax.ShapeDtypeStruct((T, D_MODEL), BF16),
                   jax.ShapeDtypeStruct((T, LANES), F32),
                   jax.ShapeDtypeStruct((ROUTE_ROWS, T), I32),
                   jax.ShapeDtypeStruct((1, LANES), F32),
                   jax.ShapeDtypeStruct((T // tm, SUBLANES, LANES), F32)),
        scratch_shapes=[pltpu.VMEM((1, LANES), F32), pltpu.VMEM((tm, tm), BF16)],
        compiler_params=_cparams(("arbitrary",)),
        name="merge",
    )(x, oa, ob, ga, gb, wa, wb, wo, nf, wr, br)


RUN_BLOCKS = tuple(1 << b for b in range(9, 3, -1))


def _for_each_run_block(run_ref, off_ref, base_ref, step, fn):
    for e in range(N_EXPERTS):
        idx = step * N_EXPERTS + e
        n_left, src, dst = run_ref[idx], off_ref[idx], base_ref[idx]
        for rows in RUN_BLOCKS:
            take = n_left & rows

            @pl.when(take != 0)
            def _(src=src, dst=dst, rows=rows):
                fn(pl.multiple_of(src, RUN_ALIGN), pl.multiple_of(dst, RUN_ALIGN), rows)
            src, dst = src + take, dst + take


def _dispatch_kernel(ends_ref, run_ref, off_ref, base_ref, ridx_ref, h2_ref, xs_ref, zero_ref,
                     buf_ref, sem):
    i = pl.program_id(0)
    n = pl.num_programs(0)
    slot = i & 1
    slot_block = zero_ref.shape[0]
    slot_shift = slot_block.bit_length() - 1
    n_blocks = xs_ref.shape[0] // slot_block
    n_rows, tm = buf_ref.shape[1], h2_ref.shape[0]
    zero_sem = sem.at[2]

    def run_copies(step, s, fn):
        _for_each_run_block(run_ref, off_ref, base_ref, step, lambda src, dst, rows: fn(
            pltpu.make_async_copy(buf_ref.at[s, pl.ds(src, rows)], xs_ref.at[pl.ds(dst, rows)],
                                  sem.at[s])))

    @pl.when(i == 0)
    def _():
        zero_ref[...] = jnp.zeros_like(zero_ref)

        def zero_block(blk):
            start = pl.multiple_of(blk * slot_block, slot_block)
            return pltpu.make_async_copy(zero_ref, xs_ref.at[pl.ds(start, slot_block)], zero_sem)

        def expert_tail(e):
            return jnp.maximum((ends_ref[e] >> slot_shift) - 1, 0)

        n_used = ends_ref[N_EXPERTS - 1] >> slot_shift
        for e in range(N_EXPERTS):
            zero_block(expert_tail(e)).start()
        lax.fori_loop(n_used, n_blocks, lambda blk, c: (zero_block(blk).start(), c)[1], 0)
        for e in range(N_EXPERTS):
            zero_block(expert_tail(e)).wait()
        lax.fori_loop(n_used, n_blocks, lambda blk, c: (zero_block(blk).wait(), c)[1], 0)

    @pl.when(i >= 2)
    def _():
        run_copies(i - 2, slot, lambda cp: cp.wait())

    row = lax.broadcasted_iota(I32, (n_rows, tm), 0)
    picked = (row == ridx_ref[ROUTE_P1:ROUTE_P1 + 1, :]) | (row == ridx_ref[ROUTE_P2:ROUTE_P2 + 1, :])
    buf_ref[slot] = jnp.dot(jnp.where(picked, 1.0, 0.0).astype(BF16), h2_ref[...],
                            preferred_element_type=F32).astype(BF16)
    run_copies(i, slot, lambda cp: cp.start())

    @pl.when(i == n - 1)
    def _():
        run_copies(i, slot, lambda cp: cp.wait())

    @pl.when((i == n - 1) & (i >= 1))
    def _():
        run_copies(i - 1, 1 - slot, lambda cp: cp.wait())


def _dispatch(pad_ends, tables, ridx, h2, n_slots, slot_block):
    T = h2.shape[0]
    tm = min(TOKEN_TILE, T)
    smem = pl.BlockSpec(memory_space=pltpu.SMEM)
    n_rows = 2 * tm + N_EXPERTS * RUN_ALIGN
    return pl.pallas_call(
        _dispatch_kernel,
        grid=(T // tm,),
        in_specs=[smem, smem, smem, smem,
                  pl.BlockSpec((ROUTE_ROWS, tm), lambda i: (0, i)),
                  pl.BlockSpec((tm, D_MODEL), lambda i: (i, 0))],
        out_specs=pl.BlockSpec(memory_space=pl.ANY),
        out_shape=jax.ShapeDtypeStruct((n_slots, D_MODEL), BF16),
        scratch_shapes=[pltpu.VMEM((slot_block, D_MODEL), BF16),
                        pltpu.VMEM((2, n_rows, D_MODEL), BF16), pltpu.SemaphoreType.DMA((3,))],
        compiler_params=_cparams(("arbitrary",)),
        name="dispatch",
    )(pad_ends, *tables, ridx, h2)


def _experts_kernel(be_ref, nu_ref, xs_ref, wg_ref, wu_ref, wd_ref, y_ref, wg16, wu16, wd16):
    i = pl.program_id(0)
    used = i < nu_ref[0]

    @pl.when(used & ((i == 0) | (be_ref[i] != be_ref[jnp.maximum(i - 1, 0)])))
    def _():
        wg16[...] = wg_ref[0].astype(BF16)
        wu16[...] = wu_ref[0].astype(BF16)
        wd16[...] = wd_ref[0].astype(BF16)

    @pl.when(used)
    def _():
        x = xs_ref[...]
        g = jnp.dot(x, wg16[...], preferred_element_type=F32)
        u = jnp.dot(x, wu16[...], preferred_element_type=F32)
        a = (g * jax.nn.sigmoid(g) * u).astype(BF16)
        y_ref[...] = jnp.dot(a, wd16[...], preferred_element_type=F32).astype(y_ref.dtype)

    @pl.when(jnp.logical_not(used))
    def _():
        y_ref[...] = jnp.zeros_like(y_ref)


def _experts(block_e, n_used, xs, wg, wu, wd, slot_block):
    n_blocks = xs.shape[0] // slot_block
    last = lambda i, nu: jnp.minimum(i, jnp.maximum(nu[0] - 1, 0))
    blk = lambda i, be, nu: (last(i, nu), 0)
    wmap = lambda i, be, nu: (be[last(i, nu)], 0, 0)
    grid_spec = pltpu.PrefetchScalarGridSpec(
        num_scalar_prefetch=2,
        grid=(n_blocks,),
        in_specs=[pl.BlockSpec((slot_block, D_MODEL), blk),
                  pl.BlockSpec((1, D_MODEL, D_EXPERT), wmap),
                  pl.BlockSpec((1, D_MODEL, D_EXPERT), wmap),
                  pl.BlockSpec((1, D_EXPERT, D_MODEL), wmap)],
        out_specs=pl.BlockSpec((slot_block, D_MODEL), lambda i, be, nu: (i, 0)),
        scratch_shapes=[pltpu.VMEM((D_MODEL, D_EXPERT), BF16), pltpu.VMEM((D_MODEL, D_EXPERT), BF16),
                        pltpu.VMEM((D_EXPERT, D_MODEL), BF16)],
    )
    return pl.pallas_call(
        _experts_kernel,
        grid_spec=grid_spec,
        out_shape=jax.ShapeDtypeStruct((xs.shape[0], D_MODEL), BF16),
        compiler_params=_cparams(("arbitrary",)),
        name="experts",
    )(block_e, n_used, xs, wg, wu, wd)


def _combine_kernel(run_ref, off_ref, base_ref, x1_ref, route_ref, nf_ref, yb_ref, y_ref,
                    buf_ref, sem):
    i = pl.program_id(0)
    n = pl.num_programs(0)
    tm, n_rows = x1_ref.shape[0], buf_ref.shape[1]
    slot = i & 1

    def run_copies(step, s, fn):
        _for_each_run_block(run_ref, off_ref, base_ref, step, lambda dst, src, rows: fn(
            pltpu.make_async_copy(yb_ref.at[pl.ds(src, rows)], buf_ref.at[s, pl.ds(dst, rows)],
                                  sem.at[s])))

    @pl.when(i == 0)
    def _():
        buf_ref[...] = jnp.zeros_like(buf_ref)
        run_copies(i, slot, lambda cp: cp.start())

    @pl.when(i + 1 < n)
    def _():
        run_copies(i + 1, 1 - slot, lambda cp: cp.start())

    run_copies(i, slot, lambda cp: cp.wait())
    route = route_ref[...]
    rows_bf = buf_ref[slot]
    col = lax.broadcasted_iota(I32, (tm, n_rows), 1)
    moe = jnp.zeros((tm, D_MODEL), F32)
    for w_lane, p_lane in ((ROUTE_W1, ROUTE_P1), (ROUTE_W2, ROUTE_P2)):
        pick = jnp.where(col == route[:, p_lane:p_lane + 1].astype(I32), 1.0, 0.0).astype(BF16)
        moe = moe + route[:, w_lane:w_lane + 1] * jnp.dot(pick, rows_bf, preferred_element_type=F32)
    y_ref[...] = _rms(x1_ref[...] + moe, nf_ref[...])


def _combine(tables, x1, route, nfinal, yb):
    T = x1.shape[0]
    tm = min(TOKEN_TILE, T)
    row = lambda i: (i, 0)
    smem = pl.BlockSpec(memory_space=pltpu.SMEM)
    n_rows = 2 * tm + N_EXPERTS * RUN_ALIGN
    return pl.pallas_call(
        _combine_kernel,
        grid=(T // tm,),
        in_specs=[smem, smem, smem,
                  pl.BlockSpec((tm, D_MODEL), row), pl.BlockSpec((tm, LANES), row),
                  pl.BlockSpec((1, D_MODEL), lambda i: (0, 0)),
                  pl.BlockSpec(memory_space=pl.ANY)],
        out_specs=pl.BlockSpec((tm, D_MODEL), row),
        out_shape=jax.ShapeDtypeStruct((T, D_MODEL), F32),
        scratch_shapes=[pltpu.VMEM((2, n_rows, D_MODEL), BF16), pltpu.SemaphoreType.DMA((2,))],
        compiler_params=_cparams(("arbitrary",)),
        name="combine",
    )(*tables, x1, route, nfinal, yb)


def _moe(x1, h2, route, route_t, counts, tables, wg, wu, wd, nfinal):
    T = x1.shape[0]
    tm = min(TOKEN_TILE, T)
    slot_block = SLOT_BLOCK if 2 * T >= N_EXPERTS * SLOT_BLOCK else SLOT_BLOCK_SMALL
    n_slots = T * 2 + (T // tm) * N_EXPERTS * RUN_ALIGN + N_EXPERTS * slot_block
    n_blocks = -(-n_slots // slot_block)
    n_slots = n_blocks * slot_block
    cnt = counts[0, :N_EXPERTS].astype(I32)
    padded = (cnt + slot_block - 1) // slot_block * slot_block
    pad_ends = jnp.cumsum(padded)
    pad_starts = (pad_ends - padded).astype(I32)
    pad_ends = pad_ends.astype(I32)
    block_start = jnp.arange(n_blocks, dtype=I32) * slot_block
    block_e = jnp.sum((block_start[:, None] >= pad_ends[None, :]).astype(I32), axis=1)
    block_e = jnp.minimum(block_e, N_EXPERTS - 1)
    n_used = pad_ends[-1:] // slot_block
    per_run = lambda row: tables[:, row, :N_EXPERTS].astype(I32)
    run_tables = (per_run(TABLE_RUN).reshape(-1), per_run(TABLE_OFF).reshape(-1),
                  (per_run(TABLE_BASE) + pad_starts[None, :]).reshape(-1))
    xs = _dispatch(pad_ends, run_tables, route_t, h2, n_slots, slot_block)
    yb = _experts(block_e, n_used, xs, wg, wu, wd, slot_block)
    return _combine(run_tables, x1, route, nfinal, yb)


def _layer(x, s0, k_past, v_past, p, B, L):
    row2 = lambda a: a.reshape(1, -1)
    qka, va, ra, qb, kb, vb, ga, gb, da, k5, v5 = _in_proj(x, row2(p["norm_mix"]), p["w_in"])
    oa, s_new = _gla(qka, va, ra, da, p["w_decay"], row2(p["b_decay"]), row2(p["gla_norm"]),
                     s0, B, L)
    lams = [row2(p[n]) for n in ("lambda_q1", "lambda_k1", "lambda_q2", "lambda_k2")]
    dn = row2(p["diff_norm"])
    if k_past is None:
        ob = _attn_prompt(qb, kb, vb, *lams, dn, B, L)
    else:
        ob = _attn_sample(qb, k_past, v_past, kb, vb, *lams, dn, B, L)
    x1, h2, route, route_t, counts, tables = _merge(
        x, oa, ob, ga, gb, p["w_proj_a"], p["w_proj_b"], p["w_out"], row2(p["norm_ffn"]),
        p["w_router"], p["b_router"])
    y = _moe(x1, h2, route, route_t, counts, tables, p["w_gate"], p["w_up"], p["w_down"],
             row2(p["norm_final"]))
    return y, s_new, k5, v5


def kernel(x_prompt, x_sample, cache_k, cache_v, state_gla, norm_mix, w_in, w_decay, b_decay,
           gla_norm, w_proj_a, lambda_q1, lambda_k1, lambda_q2, lambda_k2, diff_norm, w_proj_b,
           w_out, norm_ffn, w_router_group, b_router_group, w_router_expert, b_router_expert,
           w_gate, w_up, w_down, norm_final):
    B, L, D = x_prompt.shape
    Bs, Ls, _ = x_sample.shape
    w_router = jnp.concatenate(
        [w_router_expert[0], w_router_group[0],
         jnp.zeros((D, LANES - N_EXPERTS - N_GROUPS), F32)], axis=1)
    b_router = jnp.concatenate(
        [b_router_expert[0], b_router_group[0],
         jnp.zeros((LANES - N_EXPERTS - N_GROUPS,), F32)]).reshape(1, LANES)
    p = dict(
        norm_mix=norm_mix[0], w_in=_prep_w_in(w_in[0]),
        w_decay=jnp.pad(w_decay[0], ((0, LANES - GLA_RANK), (0, 0))), b_decay=b_decay[0],
        gla_norm=gla_norm[0], w_proj_a=w_proj_a[0].astype(BF16),
        lambda_q1=lambda_q1[0], lambda_k1=lambda_k1[0], lambda_q2=lambda_q2[0],
        lambda_k2=lambda_k2[0], diff_norm=diff_norm[0], w_proj_b=w_proj_b[0].astype(BF16),
        w_out=w_out[0].astype(BF16), norm_ffn=norm_ffn[0], w_router=w_router, b_router=b_router,
        w_gate=w_gate[0], w_up=w_up[0], w_down=w_down[0],
        norm_final=norm_final)

    s0p = jnp.zeros((B, GLA_HEADS, GLA_DK, GLA_DV), F32)
    yp, sp, kp, vp = _layer(x_prompt.reshape(B * L, D), s0p, None, None, p, B, L)
    ys, ss, ks, vs = _layer(x_sample.reshape(Bs * Ls, D), state_gla[0], cache_k, cache_v,
                            p, Bs, Ls)
    kv = lambda a, b, l: a.reshape(1, b, l, DIFF_HEADS, DIFF_DV)
    return (yp.reshape(B, L, D), ys.reshape(Bs, Ls, D), kv(kp, B, L), kv(vp, B, L), sp[None],
            kv(ks, Bs, Ls), kv(vs, Bs, Ls), ss[None])
```

```python
import functools

import jax
import jax.numpy as jnp
from jax import lax
from jax.experimental import pallas as pl
from jax.experimental.pallas import tpu as pltpu

F32 = jnp.float32
BF16 = jnp.bfloat16
I32 = jnp.int32

D_MODEL = 1024
CHUNK = 64
CHUNK_SHIFT = 6
EPS = 1e-6
GLA_HEADS = 4
GLA_DK = 128
GLA_DV = 256
GLA_RANK = 16
GLA_TAU = 16.0
DIFF_HEADS = 8
DIFF_DH = 64
DIFF_DV = 128
N_GROUPS = 4
EXPERTS_PER_GROUP = 4
EPG_SHIFT = 2
N_EXPERTS = 16
D_EXPERT = 512
LAM_INIT = 0.8 - 0.6

LANES = 128
SUBLANES = 8
RUN_ALIGN = 16
TOKEN_TILE = 512
SLOT_BLOCK = 512
SLOT_BLOCK_SMALL = 128
ATT_TILE = 512
ATT_HEADS_PER_STEP = 2
SAMPLE_KEY_CHUNK = 1024
GLA_CUMSUM_ROWS = 256
VMEM_LIMIT = 56 * 1024 * 1024
NEG = -1e30

HI = lax.Precision.HIGHEST


def _cparams(sem):
    return pltpu.CompilerParams(dimension_semantics=sem, vmem_limit_bytes=VMEM_LIMIT)


def _nt(a, b):
    return lax.dot_general(a, b, (((1,), (1,)), ((), ())), preferred_element_type=F32)


def _tn(a, b):
    return lax.dot_general(a, b, (((0,), (0,)), ((), ())), preferred_element_type=F32)


def _rms(x, g):
    return x * lax.rsqrt(jnp.mean(x * x, axis=-1, keepdims=True) + EPS) * g


def _inproj_kernel(x_ref, g_ref, wa_ref, wda_ref, wb_ref, qka_ref, va_ref, ra_ref, qb_ref, kb_ref,
                   vb_ref, ga_ref, gb_ref, da_ref, k5_ref, v5_ref, stage_ref, sem):
    i = pl.program_id(0)
    n = pl.num_programs(0)
    tm = x_ref.shape[0]
    slot = i & 1

    def head_copy(s, which, hd):
        dst = (k5_ref, v5_ref)[which]
        return pltpu.make_async_copy(stage_ref.at[s, which, hd],
                                     dst.at[pl.ds(i * tm, tm), hd, :], sem.at[s])

    def for_all(s, fn):
        for which in range(2):
            for hd in range(DIFF_HEADS):
                fn(head_copy(s, which, hd))

    @pl.when(i >= 2)
    def _():
        for_all(slot, lambda cp: cp.wait())

    h = _rms(x_ref[...], g_ref[...]).astype(BF16)
    outs = ((qka_ref, wa_ref, 0), (va_ref, wa_ref, 1), (ra_ref, wa_ref, 2), (qb_ref, wb_ref, 0),
            (kb_ref, wb_ref, 1), (vb_ref, wb_ref, 2), (ga_ref, wb_ref, 3), (gb_ref, wb_ref, 4))
    for o, w_ref, c in outs:
        z = jnp.dot(h, w_ref[:, c * D_MODEL:(c + 1) * D_MODEL], preferred_element_type=F32)
        o[...] = z.astype(o.dtype)
        which = 0 if o is kb_ref else 1 if o is vb_ref else None
        if which is not None:
            for hd in range(DIFF_HEADS):
                stage_ref[slot, which, hd] = z[:, hd * DIFF_DV:(hd + 1) * DIFF_DV]
    da_ref[...] = jnp.dot(h, wda_ref[...], preferred_element_type=F32)
    for_all(slot, lambda cp: cp.start())

    @pl.when(i == n - 1)
    def _():
        for_all(slot, lambda cp: cp.wait())

    @pl.when((i == n - 1) & (i >= 1))
    def _():
        for_all(1 - slot, lambda cp: cp.wait())


N_GLA_COLS = 3 * D_MODEL
N_DIFF_COLS = 5 * D_MODEL


def _prep_w_in(w_in):
    return (w_in[:, :N_GLA_COLS + LANES].astype(BF16), w_in[:, N_GLA_COLS + GLA_RANK:].astype(BF16))


def _in_proj(x, g, w):
    w_all, w_diff = w
    T = x.shape[0]
    tm = min(TOKEN_TILE, T)
    resident = lambda cols, j: pl.BlockSpec((D_MODEL, cols), lambda i: (0, j),
                                            pipeline_mode=pl.Buffered(1))
    row = lambda i: (i, 0)
    const = lambda i: (0, 0)
    wide = lambda dt: jax.ShapeDtypeStruct((T, D_MODEL), dt)
    heads = jax.ShapeDtypeStruct((T, DIFF_HEADS, DIFF_DV), F32)
    out_shape = (wide(BF16),) * 8 + (jax.ShapeDtypeStruct((T, LANES), F32), heads, heads)
    out_specs = tuple([pl.BlockSpec((tm, D_MODEL), row)] * 8 + [pl.BlockSpec((tm, LANES), row)]
                      + [pl.BlockSpec(memory_space=pl.ANY)] * 2)
    return pl.pallas_call(
        _inproj_kernel,
        grid=(T // tm,),
        in_specs=[pl.BlockSpec((tm, D_MODEL), row),
                  pl.BlockSpec((1, D_MODEL), const),
                  resident(N_GLA_COLS, 0), resident(LANES, N_GLA_COLS // LANES),
                  resident(N_DIFF_COLS, 0)],
        out_specs=out_specs,
        out_shape=out_shape,
        scratch_shapes=[pltpu.VMEM((2, 2, DIFF_HEADS, tm, DIFF_DV), F32),
                        pltpu.SemaphoreType.DMA((2,))],
        compiler_params=_cparams(("arbitrary",)),
        name="in_proj",
    )(x, g, w_all, w_all, w_diff)


def _gla_kernel(qka_ref, va_ref, ra_ref, da_ref, wd_ref, bd_ref, gn_ref, s0_ref,
                oa_ref, sout_ref, s_ref, la_ref, *, chunk, n_chunks):
    l = pl.program_id(1)

    @pl.when(l == 0)
    def _():
        s_ref[...] = s0_ref[0]

    split = lambda a: (a.astype(BF16), (a - a.astype(BF16).astype(F32)).astype(BF16))
    n_k = GLA_HEADS * GLA_DK
    fold = lambda z: z[:, :n_k] + z[:, n_k:]

    wd_pair = jnp.concatenate(split(wd_ref[...]), axis=1)
    x = fold(sum(jnp.dot(a, wd_pair, preferred_element_type=F32) for a in split(da_ref[...])))
    x = x + bd_ref[...]
    log_a = (jnp.minimum(x, 0.0) - jnp.log1p(jnp.exp(-jnp.abs(x)))) * (1.0 / GLA_TAU)
    lb = log_a.shape[0]
    grp = min(lb, GLA_CUMSUM_ROWS)
    shift = chunk.bit_length() - 1
    r_b = lax.broadcasted_iota(I32, (grp, grp), 0)
    c_b = lax.broadcasted_iota(I32, (grp, grp), 1)
    tri = jnp.where((c_b <= r_b) & ((c_b >> shift) == (r_b >> shift)), 1.0, 0.0).astype(BF16)
    for g in range(lb // grp):
        rows = slice(g * grp, (g + 1) * grp)
        la_ref[rows, :] = fold(jnp.dot(tri, jnp.concatenate(split(log_a[rows, :]), axis=1),
                                       preferred_element_type=F32))

    r_i = lax.broadcasted_iota(I32, (chunk, chunk), 0)
    c_i = lax.broadcasted_iota(I32, (chunk, chunk), 1)
    causal = c_i <= r_i
    gn = gn_ref[...]
    scale = GLA_DK ** -0.5
    heads = range(GLA_HEADS)
    vcols = [slice(h * GLA_DV, (h + 1) * GLA_DV) for h in heads]

    @pl.loop(0, n_chunks)
    def _(c):
        rows = pl.ds(pl.multiple_of(c * chunk, chunk), chunk)
        b_all = la_ref[rows, :]
        q_t, k_t, k_end, decay = [], [], [], []
        for h in heads:
            b = b_all[:, h * GLA_DK:(h + 1) * GLA_DK]
            b_last = b[chunk - 1:chunk, :]
            q = qka_ref[rows, h * GLA_DK:(h + 1) * GLA_DK].astype(F32) * scale
            k = qka_ref[rows, n_k + h * GLA_DK:n_k + (h + 1) * GLA_DK].astype(F32)
            q_t.append((q * jnp.exp(b)).astype(BF16))
            k_t.append((k * jnp.exp(-b)).astype(BF16))
            k_end.append((k * jnp.exp(b_last - b)).astype(BF16))
            decay.append(jnp.broadcast_to(jnp.exp(b_last), (GLA_DK, GLA_DK)).T)
        s_old = [s_ref[h] for h in heads]
        o_state = [jnp.dot(q_t[h], s_old[h].astype(BF16), preferred_element_type=F32) for h in heads]
        att = [_nt(q_t[h], k_t[h]) for h in heads]
        kv = [_tn(k_end[h], va_ref[rows, vcols[h]]) for h in heads]
        for h in heads:
            a = jnp.where(causal, att[h], 0.0).astype(BF16)
            o = o_state[h] + jnp.dot(a, va_ref[rows, vcols[h]], preferred_element_type=F32)
            s_ref[h] = jnp.concatenate([decay[h], decay[h]], axis=1) * s_old[h] + kv[h]
            r = ra_ref[rows, vcols[h]].astype(F32)
            oa_ref[rows, vcols[h]] = (_rms(o, gn) * (r * jax.nn.sigmoid(r))).astype(oa_ref.dtype)

    @pl.when(l == pl.num_programs(1) - 1)
    def _():
        sout_ref[0] = s_ref[...]


def _gla(qka, va, ra, da, wd, bd, gn, s0, B, L):
    chunk = min(CHUNK, L)
    lb = min(TOKEN_TILE, L)
    nl = L // lb
    row = lambda b, l: (b * nl + l, 0)
    const2 = lambda b, l: (0, 0)
    st = lambda b, l: (b, 0, 0, 0)
    kern = functools.partial(_gla_kernel, chunk=chunk, n_chunks=lb // chunk)
    return pl.pallas_call(
        kern,
        grid=(B, nl),
        in_specs=[pl.BlockSpec((lb, D_MODEL), row), pl.BlockSpec((lb, D_MODEL), row),
                  pl.BlockSpec((lb, D_MODEL), row), pl.BlockSpec((lb, LANES), row),
                  pl.BlockSpec((LANES, GLA_HEADS * GLA_DK), const2),
                  pl.BlockSpec((1, GLA_HEADS * GLA_DK), const2),
                  pl.BlockSpec((1, GLA_DV), const2),
                  pl.BlockSpec((1, GLA_HEADS, GLA_DK, GLA_DV), st)],
        out_specs=(pl.BlockSpec((lb, D_MODEL), row),
                   pl.BlockSpec((1, GLA_HEADS, GLA_DK, GLA_DV), st)),
        out_shape=(jax.ShapeDtypeStruct((B * L, D_MODEL), BF16),
                   jax.ShapeDtypeStruct((B, GLA_HEADS, GLA_DK, GLA_DV), F32)),
        scratch_shapes=[pltpu.VMEM((GLA_HEADS, GLA_DK, GLA_DV), F32),
                        pltpu.VMEM((lb, GLA_HEADS * GLA_DK), F32)],
        compiler_params=_cparams(("arbitrary", "arbitrary")),
        name="gla",
    )(qka, va, ra, da, wd, bd, gn, s0)


def _lam(lq1, lk1, lq2, lk2):
    a = jnp.sum(lq1[...] * lk1[...], axis=-1, keepdims=True)
    b = jnp.sum(lq2[...] * lk2[...], axis=-1, keepdims=True)
    return jnp.exp(a) - jnp.exp(b) + LAM_INIT


def _split_q(q):
    lane = lax.broadcasted_iota(I32, q.shape, 1)
    qs = q * jnp.asarray(DIFF_DH ** -0.5, q.dtype)
    zero = jnp.zeros_like(qs)
    return jnp.where(lane < DIFF_DH, qs, zero), jnp.where(lane >= DIFF_DH, qs, zero)


def _alibi_slopes():
    return jnp.asarray([2.0 ** (-8.0 * (h + 1) / DIFF_HEADS) for h in range(DIFF_HEADS)], F32)


def _attn_prompt_kernel(slope_ref, q_ref, k_ref, v_ref, lq1, lk1, lq2, lk2, dn_ref, o_ref,
                        k1_ref, k2_ref, vt_ref, geo_ref, *, tile, nh):
    qi = pl.program_id(2)
    n_kv = vt_ref.shape[1]
    n_cols = tile // LANES
    lane_k = lax.broadcasted_iota(I32, (tile, DIFF_DV), 1)
    row_i = lax.broadcasted_iota(I32, (tile, DIFF_DV), 0)
    row_lo = (row_i & 255).astype(F32)
    row_hi = (row_i & -256).astype(F32)
    slopes = [slope_ref[pl.program_id(1) * nh + g] for g in range(nh)]
    hcols = [slice(g * DIFF_DV, (g + 1) * DIFF_DV) for g in range(nh)]

    @pl.when(qi == 0)
    def _():
        for g in range(nh):
            kfeat = jnp.where(lane_k == DIFF_DH, slopes[g] * row_lo,
                              jnp.where(lane_k == DIFF_DH + 1, slopes[g] * row_hi,
                                        jnp.where((lane_k == DIFF_DH + 2) | (lane_k == DIFF_DH + 3),
                                                  1.0, 0.0)))
            for c in range(n_kv):
                rows = slice(c * tile, (c + 1) * tile)
                kf = k_ref[rows, hcols[g]].astype(F32)
                k1_ref[g, rows, :] = jnp.where(lane_k < DIFF_DH, kf, kfeat).astype(BF16)
                k2_ref[g, rows, :] = jnp.where(lane_k < DIFF_DH, pltpu.roll(kf, DIFF_DH, 1),
                                               kfeat).astype(BF16)
                vt_ref[g, c] = v_ref[rows, hcols[g]].astype(F32).T.astype(BF16)

    qs = []
    for g in range(nh):
        qf = q_ref[:, hcols[g]].astype(F32) * (DIFF_DH ** -0.5)
        qfeat = jnp.where((lane_k == DIFF_DH) | (lane_k == DIFF_DH + 1), 1.0,
                          jnp.where(lane_k == DIFF_DH + 2, -slopes[g] * row_lo,
                                    jnp.where(lane_k == DIFF_DH + 3, -slopes[g] * row_hi, 0.0)))
        qs.append((jnp.where(lane_k < DIFF_DH, qf, qfeat).T.astype(BF16),
                   jnp.where(lane_k < DIFF_DH, pltpu.roll(qf, DIFF_DH, 1), qfeat).T.astype(BF16)))

    n_maps = 2 * nh
    n_stats = 2 * n_cols + 1

    def scores(i, j):
        rows = pl.ds(pl.multiple_of(j * tile, tile), tile)
        k_ref_i = k1_ref if i % 2 == 0 else k2_ref
        return jnp.dot(k_ref_i[i // 2, rows, :], qs[i // 2][i % 2], preferred_element_type=F32)

    def softmax_part(stats, s, c, fix):
        out, alphas, ps = (), [], []
        for col in range(n_cols):
            lanes = slice(col * LANES, (col + 1) * LANES)
            n_keys = tile if fix is None else (col + 1) * LANES
            sh = s[:n_keys, lanes]
            if fix is not None:
                sh = sh + fix[:n_keys, lanes]
            m, l = stats[2 * col], stats[2 * col + 1]
            m_new = jnp.maximum(m, jnp.max(sh, axis=0, keepdims=True) + c)
            alpha = jnp.exp(m - m_new)
            p = jnp.exp(sh - (m_new - c))
            out += (m_new, alpha * l + jnp.sum(p, axis=0, keepdims=True))
            p = p.astype(BF16)
            alphas.append(jnp.broadcast_to(alpha, (DIFF_DV, LANES)))
            if n_keys < tile:
                p = jnp.concatenate([p, jnp.zeros((tile - n_keys, LANES), BF16)], axis=0)
            ps.append(p)
        return out, jnp.concatenate(alphas, axis=1), jnp.concatenate(ps, axis=1)

    def step(stats, j, cs, fixes):
        ss = [scores(i, j) for i in range(n_maps)]
        parts = [softmax_part(stats[n_stats * i:n_stats * (i + 1)], ss[i], cs[i // 2],
                              None if fixes is None else fixes[i // 2])
                 for i in range(n_maps)]
        out = ()
        for i, (st, alpha, p) in enumerate(parts):
            a = alpha * stats[n_stats * i + n_stats - 1] + jnp.dot(
                vt_ref[i // 2, j], p, preferred_element_type=F32)
            out += st + (a,)
        return out

    def body(j, stats):
        off = jnp.full((1, LANES), (j - qi) * tile, I32).astype(F32)
        return step(stats, j, [off * slopes[g] for g in range(nh)], None)

    row = lambda v: jnp.full((1, LANES), v, F32)
    init = ((row(NEG), row(0.0)) * n_cols + (jnp.zeros((DIFF_DV, tile), F32),)) * n_maps
    stats = lax.fori_loop(0, qi, body, init)

    @pl.when((pl.program_id(0) == 0) & (pl.program_id(1) == 0) & (qi == 0))
    def _():
        r_i = lax.broadcasted_iota(I32, (tile, tile), 0)
        c_i = lax.broadcasted_iota(I32, (tile, tile), 1)
        allowed = (r_i >> CHUNK_SHIFT) <= (c_i >> CHUNK_SHIFT)
        geo_ref[0] = jnp.where(r_i > c_i, (c_i - r_i).astype(F32), 0.0)
        geo_ref[1] = jnp.where(allowed, 0.0, NEG)

    fixes = [(2.0 * slopes[g]) * geo_ref[0] + geo_ref[1] for g in range(nh)]
    stats = step(stats, qi, [row(0.0)] * nh, fixes)
    lam = _lam(lq1, lk1, lq2, lk2)
    for g in range(nh):
        res = []
        for i in (2 * g, 2 * g + 1):
            st = stats[n_stats * i:n_stats * (i + 1)]
            l = jnp.concatenate([jnp.broadcast_to(st[2 * col + 1], (DIFF_DV, LANES))
                                 for col in range(n_cols)], axis=1)
            res.append(st[-1] / l)
        o = (res[0] - lam * res[1]).T
        o_ref[:, hcols[g]] = (_rms(o, dn_ref[...]) * (1.0 - LAM_INIT)).astype(o_ref.dtype)


def _attn_prompt(qb, kb, vb, lq1, lk1, lq2, lk2, dn, B, L):
    tile = min(ATT_TILE, L)
    nq = L // tile
    nh = ATT_HEADS_PER_STEP
    qmap = lambda b, h, i: (b * nq + i, h)
    kvmap = lambda b, h, i: (b, h)
    cmap = lambda b, h, i: (0, 0)
    lspec = pl.BlockSpec((1, DIFF_DH), cmap)
    return pl.pallas_call(
        functools.partial(_attn_prompt_kernel, tile=tile, nh=nh),
        grid=(B, DIFF_HEADS // nh, nq),
        in_specs=[pl.BlockSpec(memory_space=pltpu.SMEM),
                  pl.BlockSpec((tile, nh * DIFF_DV), qmap),
                  pl.BlockSpec((L, nh * DIFF_DV), kvmap), pl.BlockSpec((L, nh * DIFF_DV), kvmap),
                  lspec, lspec, lspec, lspec, pl.BlockSpec((1, DIFF_DV), cmap)],
        out_specs=pl.BlockSpec((tile, nh * DIFF_DV), qmap),
        out_shape=jax.ShapeDtypeStruct((B * L, D_MODEL), BF16),
        scratch_shapes=[pltpu.VMEM((nh, L, DIFF_DV), BF16), pltpu.VMEM((nh, L, DIFF_DV), BF16),
                        pltpu.VMEM((nh, nq, DIFF_DV, tile), BF16),
                        pltpu.VMEM((2, tile, tile), F32)],
        compiler_params=_cparams(("arbitrary", "arbitrary", "arbitrary")),
        name="attn_prompt",
    )(_alibi_slopes(), qb, kb, vb, lq1, lk1, lq2, lk2, dn)


def _attn_sample_kernel(slope_ref, q_ref, kp_ref, vp_ref, kn_ref, vn_ref, lq1, lk1, lq2, lk2,
                        dn_ref, o_ref, m_ref, l_ref, acc_ref, kbuf, vbuf, sem, *, past, lq, chunk):
    b = pl.program_id(0)
    c = pl.program_id(1)
    n_c = pl.num_programs(1)
    step = b * n_c + c
    slot = step & 1

    def slab_copies(bb, cc, s, fn):
        rows = pl.ds(cc * chunk, chunk)
        for hd in range(DIFF_HEADS):
            fn(pltpu.make_async_copy(kp_ref.at[0, bb, rows, hd, :], kbuf.at[s, hd], sem.at[s]))
            fn(pltpu.make_async_copy(vp_ref.at[0, bb, rows, hd, :], vbuf.at[s, hd], sem.at[s]))

    @pl.when(step == 0)
    def _():
        slab_copies(b, c, slot, lambda cp: cp.start())

    @pl.when(step + 1 < pl.num_programs(0) * n_c)
    def _():
        wrap = c + 1 == n_c
        slab_copies(jnp.where(wrap, b + 1, b), jnp.where(wrap, 0, c + 1), 1 - slot,
                    lambda cp: cp.start())

    slab_copies(b, c, slot, lambda cp: cp.wait())

    @pl.when(c == 0)
    def _():
        m_ref[...] = jnp.full_like(m_ref, NEG)
        l_ref[...] = jnp.zeros_like(l_ref)
        acc_ref[...] = jnp.zeros_like(acc_ref)

    def geometry(key0, n_keys):
        r = lax.broadcasted_iota(I32, (2 * lq, n_keys), 0)
        qpos = past + jnp.where(r >= lq, r - lq, r)
        kpos = key0 + lax.broadcasted_iota(I32, (2 * lq, n_keys), 1)
        allowed = (kpos >> CHUNK_SHIFT) <= (qpos >> CHUNK_SHIFT)
        return jnp.abs(qpos - kpos).astype(F32), jnp.where(allowed, 0.0, NEG)

    def block(hd, k, v, geo):
        dist, mask = geo
        q1, q2 = _split_q(q_ref[:, hd * DIFF_DV:(hd + 1) * DIFF_DV])
        s = _nt(jnp.concatenate([q1, q2], axis=0), k)
        s = s + (mask - slope_ref[hd] * dist)
        m_old = m_ref[hd]
        m_new = jnp.maximum(m_old, jnp.max(s, axis=-1, keepdims=True))
        alpha = jnp.exp(m_old - m_new)
        p = jnp.exp(s - m_new)
        l_ref[hd] = alpha * l_ref[hd] + jnp.sum(p, axis=-1, keepdims=True)
        acc_ref[hd] = alpha * acc_ref[hd] + jnp.dot(p.astype(BF16), v, preferred_element_type=F32)
        m_ref[hd] = m_new

    geo = geometry(c * chunk, chunk)
    for hd in range(DIFF_HEADS):
        block(hd, kbuf[slot, hd].astype(BF16), vbuf[slot, hd].astype(BF16), geo)

    @pl.when(c == n_c - 1)
    def _():
        lam = _lam(lq1, lk1, lq2, lk2)
        geo_new = geometry(past, lq)
        for hd in range(DIFF_HEADS):
            cols = slice(hd * DIFF_DV, (hd + 1) * DIFF_DV)
            block(hd, kn_ref[:, cols], vn_ref[:, cols], geo_new)
            o = acc_ref[hd] / l_ref[hd]
            o = o[:lq] - lam * o[lq:]
            o_ref[:, hd * DIFF_DV:(hd + 1) * DIFF_DV] = (
                _rms(o, dn_ref[...]) * (1.0 - LAM_INIT)).astype(o_ref.dtype)


def _attn_sample(qb, cache_k, cache_v, kb, vb, lq1, lk1, lq2, lk2, dn, B, L):
    past = cache_k.shape[2]
    chunk = min(SAMPLE_KEY_CHUNK, past)
    n_chunks = past // chunk
    bmap = lambda b, c: (b, 0)
    cmap = lambda b, c: (0, 0)
    lspec = pl.BlockSpec((1, DIFF_DH), cmap)
    tok = pl.BlockSpec((L, D_MODEL), bmap)
    cache = pl.BlockSpec(memory_space=pl.ANY)
    slabs = pltpu.VMEM((2, DIFF_HEADS, chunk, DIFF_DV), F32)
    return pl.pallas_call(
        functools.partial(_attn_sample_kernel, past=past, lq=L, chunk=chunk),
        grid=(B, n_chunks),
        in_specs=[pl.BlockSpec(memory_space=pltpu.SMEM), tok, cache, cache, tok, tok,
                  lspec, lspec, lspec, lspec, pl.BlockSpec((1, DIFF_DV), cmap)],
        out_specs=tok,
        out_shape=jax.ShapeDtypeStruct((B * L, D_MODEL), BF16),
        scratch_shapes=[pltpu.VMEM((DIFF_HEADS, 2 * L, 1), F32),
                        pltpu.VMEM((DIFF_HEADS, 2 * L, 1), F32),
                        pltpu.VMEM((DIFF_HEADS, 2 * L, DIFF_DV), F32),
                        slabs, slabs, pltpu.SemaphoreType.DMA((2,))],
        compiler_params=_cparams(("arbitrary", "arbitrary")),
        name="attn_sample",
    )(_alibi_slopes(), qb, cache_k, cache_v, kb, vb, lq1, lk1, lq2, lk2, dn)


ROUTE_E1, ROUTE_E2, ROUTE_W1, ROUTE_W2, ROUTE_R1, ROUTE_R2, ROUTE_P1, ROUTE_P2 = range(8)
ROUTE_ROWS = 8
TABLE_RUN, TABLE_OFF, TABLE_BASE = range(3)
GROUP_LANE0 = N_EXPERTS


def _merge_kernel(x_ref, oa_ref, ob_ref, ga_ref, gb_ref, wa_ref, wb_ref, wo_ref, nf_ref,
                  wr_ref, br_ref, x1_ref, h2_ref, route_ref, route_t_ref, cnt_ref, tables_ref,
                  run_ref, before_ref):
    i = pl.program_id(0)

    @pl.when(i == 0)
    def _():
        run_ref[...] = jnp.zeros_like(run_ref)
        r_i = lax.broadcasted_iota(I32, before_ref.shape, 0)
        c_i = lax.broadcasted_iota(I32, before_ref.shape, 1)
        before_ref[...] = jnp.where(c_i < r_i, 1.0, 0.0).astype(BF16)

    u_a = jnp.dot(oa_ref[...], wa_ref[...], preferred_element_type=F32)
    u_b = jnp.dot(ob_ref[...], wb_ref[...], preferred_element_type=F32)
    mix = (jax.nn.sigmoid(ga_ref[...].astype(F32)) * u_a
           + jax.nn.sigmoid(gb_ref[...].astype(F32)) * u_b)
    x1 = x_ref[...] + jnp.dot(mix.astype(BF16), wo_ref[...], preferred_element_type=F32)
    x1_ref[...] = x1
    h2 = _rms(x1, nf_ref[...])
    h2_ref[...] = h2.astype(h2_ref.dtype)

    tm = h2.shape[0]
    split = lambda a: (a.astype(BF16), (a - a.astype(BF16).astype(F32)).astype(BF16))
    w_pair = jnp.concatenate(split(wr_ref[...]), axis=1)
    parts = sum(jnp.dot(a, w_pair, preferred_element_type=F32) for a in split(h2))
    logits = parts[:, :LANES] + parts[:, LANES:] + br_ref[...]
    lane = lax.broadcasted_iota(I32, (tm, LANES), 1)
    big = jnp.int32(LANES)
    g_mask = (lane >= GROUP_LANE0) & (lane < GROUP_LANE0 + N_GROUPS)
    gl = jnp.where(g_mask, logits, -jnp.inf)
    gmax = jnp.max(gl, axis=-1, keepdims=True)
    g_sel = jnp.min(jnp.where(gl == gmax, lane - GROUP_LANE0, big), axis=-1, keepdims=True)
    p_g = 1.0 / jnp.sum(jnp.where(g_mask, jnp.exp(logits - gmax), 0.0), axis=-1, keepdims=True)
    e_mask = (lane < N_EXPERTS) & ((lane >> EPG_SHIFT) == g_sel)
    el = jnp.where(e_mask, logits, -jnp.inf)
    v1 = jnp.max(el, axis=-1, keepdims=True)
    i1 = jnp.min(jnp.where(el == v1, lane, big), axis=-1, keepdims=True)
    el2 = jnp.where(lane == i1, -jnp.inf, el)
    v2 = jnp.max(el2, axis=-1, keepdims=True)
    i2 = jnp.min(jnp.where(el2 == v2, lane, big), axis=-1, keepdims=True)
    t = jnp.exp(v2 - v1)
    w1 = p_g / (1.0 + t)
    w2 = p_g * t / (1.0 + t)

    oh1 = lane == i1
    oh2 = lane == i2
    cnt = jnp.where(oh1, 1.0, 0.0) + jnp.where(oh2, 1.0, 0.0)
    local = jnp.dot(before_ref[...], cnt.astype(BF16), preferred_element_type=F32)
    n_run = jnp.floor((jnp.sum(cnt, axis=0, keepdims=True) + (RUN_ALIGN - 1.0)) * (1.0 / RUN_ALIGN))
    n_run = n_run * RUN_ALIGN
    e_r = lax.broadcasted_iota(I32, (LANES, LANES), 0)
    e_c = lax.broadcasted_iota(I32, (LANES, LANES), 1)
    earlier = jnp.where(e_r < e_c, 1.0, 0.0).astype(BF16)
    t_off = jnp.dot(jnp.broadcast_to(n_run, (SUBLANES, LANES)).astype(BF16), earlier,
                    preferred_element_type=F32)[:1]
    run = run_ref[...]
    pick = lambda oh, v: jnp.sum(jnp.where(oh, v, 0.0), axis=-1, keepdims=True)
    rank1, rank2 = pick(oh1, run + local), pick(oh2, run + local)
    pos1, pos2 = pick(oh1, t_off + local), pick(oh2, t_off + local)
    run_ref[...] = run + n_run
    cnt_ref[...] = run + n_run
    tables_ref[0] = jnp.concatenate([n_run, t_off, run, jnp.zeros((SUBLANES - 3, LANES), F32)], axis=0)

    route = jnp.zeros((tm, LANES), F32)
    for pos, val in ((ROUTE_E1, i1.astype(F32)), (ROUTE_E2, i2.astype(F32)), (ROUTE_W1, w1),
                     (ROUTE_W2, w2), (ROUTE_R1, rank1), (ROUTE_R2, rank2), (ROUTE_P1, pos1),
                     (ROUTE_P2, pos2)):
        route = jnp.where(lane == pos, val, route)
    route_ref[...] = route
    route_t_ref[...] = route.T[:ROUTE_ROWS, :].astype(I32)


def _merge(x, oa, ob, ga, gb, wa, wb, wo, nf, wr, br):
    T = x.shape[0]
    tm = min(TOKEN_TILE, T)
    row = lambda i: (i, 0)
    const = lambda i: (0, 0)
    wspec = pl.BlockSpec((D_MODEL, D_MODEL), const)
    tile = pl.BlockSpec((tm, D_MODEL), row)
    return pl.pallas_call(
        _merge_kernel,
        grid=(T // tm,),
        in_specs=[tile, tile, tile, tile, tile, wspec, wspec, wspec,
                  pl.BlockSpec((1, D_MODEL), const),
                  pl.BlockSpec((D_MODEL, LANES), const), pl.BlockSpec((1, LANES), const)],
        out_specs=(tile, tile, pl.BlockSpec((tm, LANES), row),
                   pl.BlockSpec((ROUTE_ROWS, tm), lambda i: (0, i)), pl.BlockSpec((1, LANES), const),
                   pl.BlockSpec((1, SUBLANES, LANES), lambda i: (i, 0, 0))),
        out_shape=(jax.ShapeDtypeStruct((T, D_MODEL), F32),
                   jax.ShapeDtypeStruct((T, D_MODEL), BF16),
                   jax.ShapeDtypeStruct((T, LANES), F32),
                   jax.ShapeDtypeStruct((ROUTE_ROWS, T), I32),
                   jax.ShapeDtypeStruct((1, LANES), F32),
                   jax.ShapeDtypeStruct((T // tm, SUBLANES, LANES), F32)),
        scratch_shapes=[pltpu.VMEM((1, LANES), F32), pltpu.VMEM((tm, tm), BF16)],
        compiler_params=_cparams(("arbitrary",)),
        name="merge",
    )(x, oa, ob, ga, gb, wa, wb, wo, nf, wr, br)


RUN_BLOCKS = tuple(1 << b for b in range(9, 3, -1))


def _for_each_run_block(run_ref, off_ref, base_ref, step, fn):
    for e in range(N_EXPERTS):
        idx = step * N_EXPERTS + e
        n_left, src, dst = run_ref[idx], off_ref[idx], base_ref[idx]
        for rows in RUN_BLOCKS:
            take = n_left & rows

            @pl.when(take != 0)
            def _(src=src, dst=dst, rows=rows):
                fn(pl.multiple_of(src, RUN_ALIGN), pl.multiple_of(dst, RUN_ALIGN), rows)
            src, dst = src + take, dst + take


def _dispatch_kernel(ends_ref, run_ref, off_ref, base_ref, ridx_ref, h2_ref, xs_ref, zero_ref,
                     buf_ref, sem):
    i = pl.program_id(0)
    n = pl.num_programs(0)
    slot = i & 1
    slot_block = zero_ref.shape[0]
    slot_shift = slot_block.bit_length() - 1
    n_blocks = xs_ref.shape[0] // slot_block
    n_rows, tm = buf_ref.shape[1], h2_ref.shape[0]
    zero_sem = sem.at[2]

    def run_copies(step, s, fn):
        _for_each_run_block(run_ref, off_ref, base_ref, step, lambda src, dst, rows: fn(
            pltpu.make_async_copy(buf_ref.at[s, pl.ds(src, rows)], xs_ref.at[pl.ds(dst, rows)],
                                  sem.at[s])))

    @pl.when(i == 0)
    def _():
        zero_ref[...] = jnp.zeros_like(zero_ref)

        def zero_block(blk):
            start = pl.multiple_of(blk * slot_block, slot_block)
            return pltpu.make_async_copy(zero_ref, xs_ref.at[pl.ds(start, slot_block)], zero_sem)

        def expert_tail(e):
            return jnp.maximum((ends_ref[e] >> slot_shift) - 1, 0)

        n_used = ends_ref[N_EXPERTS - 1] >> slot_shift
        for e in range(N_EXPERTS):
            zero_block(expert_tail(e)).start()
        lax.fori_loop(n_used, n_blocks, lambda blk, c: (zero_block(blk).start(), c)[1], 0)
        for e in range(N_EXPERTS):
            zero_block(expert_tail(e)).wait()
        lax.fori_loop(n_used, n_blocks, lambda blk, c: (zero_block(blk).wait(), c)[1], 0)

    @pl.when(i >= 2)
    def _():
        run_copies(i - 2, slot, lambda cp: cp.wait())

    row = lax.broadcasted_iota(I32, (n_rows, tm), 0)
    picked = (row == ridx_ref[ROUTE_P1:ROUTE_P1 + 1, :]) | (row == ridx_ref[ROUTE_P2:ROUTE_P2 + 1, :])
    buf_ref[slot] = jnp.dot(jnp.where(picked, 1.0, 0.0).astype(BF16), h2_ref[...],
                            preferred_element_type=F32).astype(BF16)
    run_copies(i, slot, lambda cp: cp.start())

    @pl.when(i == n - 1)
    def _():
        run_copies(i, slot, lambda cp: cp.wait())

    @pl.when((i == n - 1) & (i >= 1))
    def _():
        run_copies(i - 1, 1 - slot, lambda cp: cp.wait())


def _dispatch(pad_ends, tables, ridx, h2, n_slots, slot_block):
    T = h2.shape[0]
    tm = min(TOKEN_TILE, T)
    smem = pl.BlockSpec(memory_space=pltpu.SMEM)
    n_rows = 2 * tm + N_EXPERTS * RUN_ALIGN
    return pl.pallas_call(
        _dispatch_kernel,
        grid=(T // tm,),
        in_specs=[smem, smem, smem, smem,
                  pl.BlockSpec((ROUTE_ROWS, tm), lambda i: (0, i)),
                  pl.BlockSpec((tm, D_MODEL), lambda i: (i, 0))],
        out_specs=pl.BlockSpec(memory_space=pl.ANY),
        out_shape=jax.ShapeDtypeStruct((n_slots, D_MODEL), BF16),
        scratch_shapes=[pltpu.VMEM((slot_block, D_MODEL), BF16),
                        pltpu.VMEM((2, n_rows, D_MODEL), BF16), pltpu.SemaphoreType.DMA((3,))],
        compiler_params=_cparams(("arbitrary",)),
        name="dispatch",
    )(pad_ends, *tables, ridx, h2)


def _experts_kernel(be_ref, nu_ref, xs_ref, wg_ref, wu_ref, wd_ref, y_ref, wg16, wu16, wd16):
    i = pl.program_id(0)
    used = i < nu_ref[0]

    @pl.when(used & ((i == 0) | (be_ref[i] != be_ref[jnp.maximum(i - 1, 0)])))
    def _():
        wg16[...] = wg_ref[0].astype(BF16)
        wu16[...] = wu_ref[0].astype(BF16)
        wd16[...] = wd_ref[0].astype(BF16)

    @pl.when(used)
    def _():
        x = xs_ref[...]
        g = jnp.dot(x, wg16[...], preferred_element_type=F32)
        u = jnp.dot(x, wu16[...], preferred_element_type=F32)
        a = (g * jax.nn.sigmoid(g) * u).astype(BF16)
        y_ref[...] = jnp.dot(a, wd16[...], preferred_element_type=F32).astype(y_ref.dtype)

    @pl.when(jnp.logical_not(used))
    def _():
        y_ref[...] = jnp.zeros_like(y_ref)


def _experts(block_e, n_used, xs, wg, wu, wd, slot_block):
    n_blocks = xs.shape[0] // slot_block
    last = lambda i, nu: jnp.minimum(i, jnp.maximum(nu[0] - 1, 0))
    blk = lambda i, be, nu: (last(i, nu), 0)
    wmap = lambda i, be, nu: (be[last(i, nu)], 0, 0)
    grid_spec = pltpu.PrefetchScalarGridSpec(
        num_scalar_prefetch=2,
        grid=(n_blocks,),
        in_specs=[pl.BlockSpec((slot_block, D_MODEL), blk),
                  pl.BlockSpec((1, D_MODEL, D_EXPERT), wmap),
                  pl.BlockSpec((1, D_MODEL, D_EXPERT), wmap),
                  pl.BlockSpec((1, D_EXPERT, D_MODEL), wmap)],
        out_specs=pl.BlockSpec((slot_block, D_MODEL), lambda i, be, nu: (i, 0)),
        scratch_shapes=[pltpu.VMEM((D_MODEL, D_EXPERT), BF16), pltpu.VMEM((D_MODEL, D_EXPERT), BF16),
                        pltpu.VMEM((D_EXPERT, D_MODEL), BF16)],
    )
    return pl.pallas_call(
        _experts_kernel,
        grid_spec=grid_spec,
        out_shape=jax.ShapeDtypeStruct((xs.shape[0], D_MODEL), BF16),
        compiler_params=_cparams(("arbitrary",)),
        name="experts",
    )(block_e, n_used, xs, wg, wu, wd)


def _combine_kernel(run_ref, off_ref, base_ref, x1_ref, route_ref, nf_ref, yb_ref, y_ref,
                    buf_ref, sem):
    i = pl.program_id(0)
    n = pl.num_programs(0)
    tm, n_rows = x1_ref.shape[0], buf_ref.shape[1]
    slot = i & 1

    def run_copies(step, s, fn):
        _for_each_run_block(run_ref, off_ref, base_ref, step, lambda dst, src, rows: fn(
            pltpu.make_async_copy(yb_ref.at[pl.ds(src, rows)], buf_ref.at[s, pl.ds(dst, rows)],
                                  sem.at[s])))

    @pl.when(i == 0)
    def _():
        buf_ref[...] = jnp.zeros_like(buf_ref)
        run_copies(i, slot, lambda cp: cp.start())

    @pl.when(i + 1 < n)
    def _():
        run_copies(i + 1, 1 - slot, lambda cp: cp.start())

    run_copies(i, slot, lambda cp: cp.wait())
    route = route_ref[...]
    rows_bf = buf_ref[slot]
    col = lax.broadcasted_iota(I32, (tm, n_rows), 1)
    moe = jnp.zeros((tm, D_MODEL), F32)
    for w_lane, p_lane in ((ROUTE_W1, ROUTE_P1), (ROUTE_W2, ROUTE_P2)):
        pick = jnp.where(col == route[:, p_lane:p_lane + 1].astype(I32), 1.0, 0.0).astype(BF16)
        moe = moe + route[:, w_lane:w_lane + 1] * jnp.dot(pick, rows_bf, preferred_element_type=F32)
    y_ref[...] = _rms(x1_ref[...] + moe, nf_ref[...])


def _combine(tables, x1, route, nfinal, yb):
    T = x1.shape[0]
    tm = min(TOKEN_TILE, T)
    row = lambda i: (i, 0)
    smem = pl.BlockSpec(memory_space=pltpu.SMEM)
    n_rows = 2 * tm + N_EXPERTS * RUN_ALIGN
    return pl.pallas_call(
        _combine_kernel,
        grid=(T // tm,),
        in_specs=[smem, smem, smem,
                  pl.BlockSpec((tm, D_MODEL), row), pl.BlockSpec((tm, LANES), row),
                  pl.BlockSpec((1, D_MODEL), lambda i: (0, 0)),
                  pl.BlockSpec(memory_space=pl.ANY)],
        out_specs=pl.BlockSpec((tm, D_MODEL), row),
        out_shape=jax.ShapeDtypeStruct((T, D_MODEL), F32),
        scratch_shapes=[pltpu.VMEM((2, n_rows, D_MODEL), BF16), pltpu.SemaphoreType.DMA((2,))],
        compiler_params=_cparams(("arbitrary",)),
        name="combine",
    )(*tables, x1, route, nfinal, yb)


def _moe(x1, h2, route, route_t, counts, tables, wg, wu, wd, nfinal):
    T = x1.shape[0]
    tm = min(TOKEN_TILE, T)
    slot_block = SLOT_BLOCK if 2 * T >= N_EXPERTS * SLOT_BLOCK else SLOT_BLOCK_SMALL
    n_slots = T * 2 + (T // tm) * N_EXPERTS * RUN_ALIGN + N_EXPERTS * slot_block
    n_blocks = -(-n_slots // slot_block)
    n_slots = n_blocks * slot_block
    cnt = counts[0, :N_EXPERTS].astype(I32)
    padded = (cnt + slot_block - 1) // slot_block * slot_block
    pad_ends = jnp.cumsum(padded)
    pad_starts = (pad_ends - padded).astype(I32)
    pad_ends = pad_ends.astype(I32)
    block_start = jnp.arange(n_blocks, dtype=I32) * slot_block
    block_e = jnp.sum((block_start[:, None] >= pad_ends[None, :]).astype(I32), axis=1)
    block_e = jnp.minimum(block_e, N_EXPERTS - 1)
    n_used = pad_ends[-1:] // slot_block
    per_run = lambda row: tables[:, row, :N_EXPERTS].astype(I32)
    run_tables = (per_run(TABLE_RUN).reshape(-1), per_run(TABLE_OFF).reshape(-1),
                  (per_run(TABLE_BASE) + pad_starts[None, :]).reshape(-1))
    xs = _dispatch(pad_ends, run_tables, route_t, h2, n_slots, slot_block)
    yb = _experts(block_e, n_used, xs, wg, wu, wd, slot_block)
    return _combine(run_tables, x1, route, nfinal, yb)


def _layer(x, s0, k_past, v_past, p, B, L):
    row2 = lambda a: a.reshape(1, -1)
    qka, va, ra, qb, kb, vb, ga, gb, da, k5, v5 = _in_proj(x, row2(p["norm_mix"]), p["w_in"])
    oa, s_new = _gla(qka, va, ra, da, p["w_decay"], row2(p["b_decay"]), row2(p["gla_norm"]),
                     s0, B, L)
    lams = [row2(p[n]) for n in ("lambda_q1", "lambda_k1", "lambda_q2", "lambda_k2")]
    dn = row2(p["diff_norm"])
    if k_past is None:
        ob = _attn_prompt(qb, kb, vb, *lams, dn, B, L)
    else:
        ob = _attn_sample(qb, k_past, v_past, kb, vb, *lams, dn, B, L)
    x1, h2, route, route_t, counts, tables = _merge(
        x, oa, ob, ga, gb, p["w_proj_a"], p["w_proj_b"], p["w_out"], row2(p["norm_ffn"]),
        p["w_router"], p["b_router"])
    y = _moe(x1, h2, route, route_t, counts, tables, p["w_gate"], p["w_up"], p["w_down"],
             row2(p["norm_final"]))
    return y, s_new, k5, v5


def kernel(x_prompt, x_sample, cache_k, cache_v, state_gla, norm_mix, w_in, w_decay, b_decay,
           gla_norm, w_proj_a, lambda_q1, lambda_k1, lambda_q2, lambda_k2, diff_norm, w_proj_b,
           w_out, norm_ffn, w_router_group, b_router_group, w_router_expert, b_router_expert,
           w_gate, w_up, w_down, norm_final):
    B, L, D = x_prompt.shape
    Bs, Ls, _ = x_sample.shape
    w_router = jnp.concatenate(
        [w_router_expert[0], w_router_group[0],
         jnp.zeros((D, LANES - N_EXPERTS - N_GROUPS), F32)], axis=1)
    b_router = jnp.concatenate(
        [b_router_expert[0], b_router_group[0],
         jnp.zeros((LANES - N_EXPERTS - N_GROUPS,), F32)]).reshape(1, LANES)
    p = dict(
        norm_mix=norm_mix[0], w_in=_prep_w_in(w_in[0]),
        w_decay=jnp.pad(w_decay[0], ((0, LANES - GLA_RANK), (0, 0))), b_decay=b_decay[0],
        gla_norm=gla_norm[0], w_proj_a=w_proj_a[0].astype(BF16),
        lambda_q1=lambda_q1[0], lambda_k1=lambda_k1[0], lambda_q2=lambda_q2[0],
        lambda_k2=lambda_k2[0], diff_norm=diff_norm[0], w_proj_b=w_proj_b[0].astype(BF16),
        w_out=w_out[0].astype(BF16), norm_ffn=norm_ffn[0], w_router=w_router, b_router=b_router,
        w_gate=w_gate[0], w_up=w_up[0], w_down=w_down[0],
        norm_final=norm_final)

    s0p = jnp.zeros((B, GLA_HEADS, GLA_DK, GLA_DV), F32)
    yp, sp, kp, vp = _layer(x_prompt.reshape(B * L, D), s0p, None, None, p, B, L)
    ys, ss, ks, vs = _layer(x_sample.reshape(Bs * Ls, D), state_gla[0], cache_k, cache_v,
                            p, Bs, Ls)
    kv = lambda a, b, l: a.reshape(1, b, l, DIFF_HEADS, DIFF_DV)
    return (yp.reshape(B, L, D), ys.reshape(Bs, Ls, D), kv(kp, B, L), kv(vp, B, L), sp[None],
            kv(ks, Bs, Ls), kv(vs, Bs, Ls), ss[None])
```

```python
import functools

import jax
import jax.numpy as jnp
from jax import lax
from jax.experimental import pallas as pl
from jax.experimental.pallas import tpu as pltpu

F32 = jnp.float32
BF16 = jnp.bfloat16
I32 = jnp.int32

D_MODEL = 1024
CHUNK = 64
CHUNK_SHIFT = 6
EPS = 1e-6
GLA_HEADS = 4
GLA_DK = 128
GLA_DV = 256
GLA_RANK = 16
GLA_TAU = 16.0
DIFF_HEADS = 8
DIFF_DH = 64
DIFF_DV = 128
N_GROUPS = 4
EXPERTS_PER_GROUP = 4
EPG_SHIFT = 2
N_EXPERTS = 16
D_EXPERT = 512
LAM_INIT = 0.8 - 0.6

LANES = 128
SUBLANES = 8
RUN_ALIGN = 16
TOKEN_TILE = 512
SLOT_BLOCK = 512
SLOT_BLOCK_SMALL = 128
ATT_TILE = 512
ATT_HEADS_PER_STEP = 4
SAMPLE_KEY_CHUNK = 1024
GLA_CUMSUM_ROWS = 256
VMEM_LIMIT = 56 * 1024 * 1024
NEG = -1e30

HI = lax.Precision.HIGHEST


def _cparams(sem):
    return pltpu.CompilerParams(dimension_semantics=sem, vmem_limit_bytes=VMEM_LIMIT)


def _nt(a, b):
    return lax.dot_general(a, b, (((1,), (1,)), ((), ())), preferred_element_type=F32)


def _tn(a, b):
    return lax.dot_general(a, b, (((0,), (0,)), ((), ())), preferred_element_type=F32)


def _rms(x, g):
    return x * lax.rsqrt(jnp.mean(x * x, axis=-1, keepdims=True) + EPS) * g


def _inproj_kernel(x_ref, g_ref, wa_ref, wda_ref, wb_ref, qka_ref, va_ref, ra_ref, qb_ref, kb_ref,
                   vb_ref, ga_ref, gb_ref, da_ref, k5_ref, v5_ref, stage_ref, sem):
    i = pl.program_id(0)
    n = pl.num_programs(0)
    tm = x_ref.shape[0]
    slot = i & 1

    def head_copy(s, which, hd):
        dst = (k5_ref, v5_ref)[which]
        return pltpu.make_async_copy(stage_ref.at[s, which, hd],
                                     dst.at[pl.ds(i * tm, tm), hd, :], sem.at[s])

    def for_all(s, fn):
        for which in range(2):
            for hd in range(DIFF_HEADS):
                fn(head_copy(s, which, hd))

    @pl.when(i >= 2)
    def _():
        for_all(slot, lambda cp: cp.wait())

    h = _rms(x_ref[...], g_ref[...]).astype(BF16)
    outs = ((qka_ref, wa_ref, 0), (va_ref, wa_ref, 1), (ra_ref, wa_ref, 2), (qb_ref, wb_ref, 0),
            (kb_ref, wb_ref, 1), (vb_ref, wb_ref, 2), (ga_ref, wb_ref, 3), (gb_ref, wb_ref, 4))
    for o, w_ref, c in outs:
        z = jnp.dot(h, w_ref[:, c * D_MODEL:(c + 1) * D_MODEL], preferred_element_type=F32)
        o[...] = z.astype(o.dtype)
        which = 0 if o is kb_ref else 1 if o is vb_ref else None
        if which is not None:
            for hd in range(DIFF_HEADS):
                stage_ref[slot, which, hd] = z[:, hd * DIFF_DV:(hd + 1) * DIFF_DV]
    da_ref[...] = jnp.dot(h, wda_ref[...], preferred_element_type=F32)
    for_all(slot, lambda cp: cp.start())

    @pl.when(i == n - 1)
    def _():
        for_all(slot, lambda cp: cp.wait())

    @pl.when((i == n - 1) & (i >= 1))
    def _():
        for_all(1 - slot, lambda cp: cp.wait())


N_GLA_COLS = 3 * D_MODEL
N_DIFF_COLS = 5 * D_MODEL


def _prep_w_in(w_in):
    w = w_in.astype(BF16)
    return w, w[:, N_GLA_COLS + GLA_RANK:]


def _in_proj(x, g, w):
    w_all, w_diff = w
    T = x.shape[0]
    tm = min(TOKEN_TILE, T)
    resident = lambda cols, j: pl.BlockSpec((D_MODEL, cols), lambda i: (0, j),
                                            pipeline_mode=pl.Buffered(1))
    row = lambda i: (i, 0)
    const = lambda i: (0, 0)
    wide = lambda dt: jax.ShapeDtypeStruct((T, D_MODEL), dt)
    heads = jax.ShapeDtypeStruct((T, DIFF_HEADS, DIFF_DV), F32)
    out_shape = (wide(BF16),) * 8 + (jax.ShapeDtypeStruct((T, LANES), F32), heads, heads)
    out_specs = tuple([pl.BlockSpec((tm, D_MODEL), row)] * 8 + [pl.BlockSpec((tm, LANES), row)]
                      + [pl.BlockSpec(memory_space=pl.ANY)] * 2)
    return pl.pallas_call(
        _inproj_kernel,
        grid=(T // tm,),
        in_specs=[pl.BlockSpec((tm, D_MODEL), row),
                  pl.BlockSpec((1, D_MODEL), const),
                  resident(N_GLA_COLS, 0), resident(LANES, N_GLA_COLS // LANES),
                  resident(N_DIFF_COLS, 0)],
        out_specs=out_specs,
        out_shape=out_shape,
        scratch_shapes=[pltpu.VMEM((2, 2, DIFF_HEADS, tm, DIFF_DV), F32),
                        pltpu.SemaphoreType.DMA((2,))],
        compiler_params=_cparams(("arbitrary",)),
        name="in_proj",
    )(x, g, w_all, w_all, w_diff)


def _gla_kernel(qka_ref, va_ref, ra_ref, da_ref, wd_ref, bd_ref, gn_ref, s0_ref,
                oa_ref, sout_ref, s_ref, la_ref, *, chunk, n_chunks):
    l = pl.program_id(1)

    @pl.when(l == 0)
    def _():
        s_ref[...] = s0_ref[0]

    split = lambda a: (a.astype(BF16), (a - a.astype(BF16).astype(F32)).astype(BF16))
    n_k = GLA_HEADS * GLA_DK
    fold = lambda z: z[:, :n_k] + z[:, n_k:]

    wd_pair = jnp.concatenate(split(wd_ref[...]), axis=1)
    x = fold(sum(jnp.dot(a, wd_pair, preferred_element_type=F32) for a in split(da_ref[...])))
    x = x + bd_ref[...]
    log_a = (jnp.minimum(x, 0.0) - jnp.log1p(jnp.exp(-jnp.abs(x)))) * (1.0 / GLA_TAU)
    lb = log_a.shape[0]
    grp = min(lb, GLA_CUMSUM_ROWS)
    shift = chunk.bit_length() - 1
    r_b = lax.broadcasted_iota(I32, (grp, grp), 0)
    c_b = lax.broadcasted_iota(I32, (grp, grp), 1)
    tri = jnp.where((c_b <= r_b) & ((c_b >> shift) == (r_b >> shift)), 1.0, 0.0).astype(BF16)
    for g in range(lb // grp):
        rows = slice(g * grp, (g + 1) * grp)
        la_ref[rows, :] = fold(jnp.dot(tri, jnp.concatenate(split(log_a[rows, :]), axis=1),
                                       preferred_element_type=F32))

    r_i = lax.broadcasted_iota(I32, (chunk, chunk), 0)
    c_i = lax.broadcasted_iota(I32, (chunk, chunk), 1)
    causal = c_i <= r_i
    gn = gn_ref[...]
    scale = GLA_DK ** -0.5
    heads = range(GLA_HEADS)
    vcols = [slice(h * GLA_DV, (h + 1) * GLA_DV) for h in heads]

    @pl.loop(0, n_chunks)
    def _(c):
        rows = pl.ds(pl.multiple_of(c * chunk, chunk), chunk)
        b_all = la_ref[rows, :]
        q_t, k_t, k_end, decay = [], [], [], []
        for h in heads:
            b = b_all[:, h * GLA_DK:(h + 1) * GLA_DK]
            b_last = b[chunk - 1:chunk, :]
            q = qka_ref[rows, h * GLA_DK:(h + 1) * GLA_DK].astype(F32) * scale
            k = qka_ref[rows, n_k + h * GLA_DK:n_k + (h + 1) * GLA_DK].astype(F32)
            q_t.append((q * jnp.exp(b)).astype(BF16))
            k_t.append((k * jnp.exp(-b)).astype(BF16))
            k_end.append((k * jnp.exp(b_last - b)).astype(BF16))
            decay.append(jnp.broadcast_to(jnp.exp(b_last), (GLA_DK, GLA_DK)).T)
        s_old = [s_ref[h] for h in heads]
        o_state = [jnp.dot(q_t[h], s_old[h].astype(BF16), preferred_element_type=F32) for h in heads]
        att = [_nt(q_t[h], k_t[h]) for h in heads]
        kv = [_tn(k_end[h], va_ref[rows, vcols[h]]) for h in heads]
        for h in heads:
            a = jnp.where(causal, att[h], 0.0).astype(BF16)
            o = o_state[h] + jnp.dot(a, va_ref[rows, vcols[h]], preferred_element_type=F32)
            s_ref[h] = jnp.concatenate([decay[h], decay[h]], axis=1) * s_old[h] + kv[h]
            r = ra_ref[rows, vcols[h]].astype(F32)
            oa_ref[rows, vcols[h]] = (_rms(o, gn) * (r * jax.nn.sigmoid(r))).astype(oa_ref.dtype)

    @pl.when(l == pl.num_programs(1) - 1)
    def _():
        sout_ref[0] = s_ref[...]


def _gla(qka, va, ra, da, wd, bd, gn, s0, B, L):
    chunk = min(CHUNK, L)
    lb = min(TOKEN_TILE, L)
    nl = L // lb
    row = lambda b, l: (b * nl + l, 0)
    const2 = lambda b, l: (0, 0)
    st = lambda b, l: (b, 0, 0, 0)
    kern = functools.partial(_gla_kernel, chunk=chunk, n_chunks=lb // chunk)
    return pl.pallas_call(
        kern,
        grid=(B, nl),
        in_specs=[pl.BlockSpec((lb, D_MODEL), row), pl.BlockSpec((lb, D_MODEL), row),
                  pl.BlockSpec((lb, D_MODEL), row), pl.BlockSpec((lb, LANES), row),
                  pl.BlockSpec((LANES, GLA_HEADS * GLA_DK), const2),
                  pl.BlockSpec((1, GLA_HEADS * GLA_DK), const2),
                  pl.BlockSpec((1, GLA_DV), const2),
                  pl.BlockSpec((1, GLA_HEADS, GLA_DK, GLA_DV), st)],
        out_specs=(pl.BlockSpec((lb, D_MODEL), row),
                   pl.BlockSpec((1, GLA_HEADS, GLA_DK, GLA_DV), st)),
        out_shape=(jax.ShapeDtypeStruct((B * L, D_MODEL), BF16),
                   jax.ShapeDtypeStruct((B, GLA_HEADS, GLA_DK, GLA_DV), F32)),
        scratch_shapes=[pltpu.VMEM((GLA_HEADS, GLA_DK, GLA_DV), F32),
                        pltpu.VMEM((lb, GLA_HEADS * GLA_DK), F32)],
        compiler_params=_cparams(("arbitrary", "arbitrary")),
        name="gla",
    )(qka, va, ra, da, wd, bd, gn, s0)


def _lam(lq1, lk1, lq2, lk2):
    a = jnp.sum(lq1[...] * lk1[...], axis=-1, keepdims=True)
    b = jnp.sum(lq2[...] * lk2[...], axis=-1, keepdims=True)
    return jnp.exp(a) - jnp.exp(b) + LAM_INIT


def _split_q(q):
    lane = lax.broadcasted_iota(I32, q.shape, 1)
    qs = q * jnp.asarray(DIFF_DH ** -0.5, q.dtype)
    zero = jnp.zeros_like(qs)
    return jnp.where(lane < DIFF_DH, qs, zero), jnp.where(lane >= DIFF_DH, qs, zero)


def _alibi_slopes():
    return jnp.asarray([2.0 ** (-8.0 * (h + 1) / DIFF_HEADS) for h in range(DIFF_HEADS)], F32)


def _attn_prompt_kernel(slope_ref, q_ref, k_ref, v_ref, lq1, lk1, lq2, lk2, dn_ref, o_ref,
                        k1_ref, k2_ref, vt_ref, geo_ref, *, tile, nh):
    qi = pl.program_id(2)
    n_kv = vt_ref.shape[1]
    n_cols = tile // LANES
    lane_k = lax.broadcasted_iota(I32, (tile, DIFF_DV), 1)
    row_i = lax.broadcasted_iota(I32, (tile, DIFF_DV), 0)
    row_lo = (row_i & 255).astype(F32)
    row_hi = (row_i & -256).astype(F32)
    slopes = [slope_ref[pl.program_id(1) * nh + g] for g in range(nh)]
    hcols = [slice(g * DIFF_DV, (g + 1) * DIFF_DV) for g in range(nh)]

    @pl.when(qi == 0)
    def _():
        for g in range(nh):
            kfeat = jnp.where(lane_k == DIFF_DH, slopes[g] * row_lo,
                              jnp.where(lane_k == DIFF_DH + 1, slopes[g] * row_hi,
                                        jnp.where((lane_k == DIFF_DH + 2) | (lane_k == DIFF_DH + 3),
                                                  1.0, 0.0)))
            for c in range(n_kv):
                rows = slice(c * tile, (c + 1) * tile)
                kf = k_ref[rows, hcols[g]].astype(F32)
                k1_ref[g, rows, :] = jnp.where(lane_k < DIFF_DH, kf, kfeat).astype(BF16)
                k2_ref[g, rows, :] = jnp.where(lane_k < DIFF_DH, pltpu.roll(kf, DIFF_DH, 1),
                                               kfeat).astype(BF16)
                vt_ref[g, c] = v_ref[rows, hcols[g]].astype(F32).T.astype(BF16)

    qs = []
    for g in range(nh):
        qf = q_ref[:, hcols[g]].astype(F32) * (DIFF_DH ** -0.5)
        qfeat = jnp.where((lane_k == DIFF_DH) | (lane_k == DIFF_DH + 1), 1.0,
                          jnp.where(lane_k == DIFF_DH + 2, -slopes[g] * row_lo,
                                    jnp.where(lane_k == DIFF_DH + 3, -slopes[g] * row_hi, 0.0)))
        qs.append((jnp.where(lane_k < DIFF_DH, qf, qfeat).T.astype(BF16),
                   jnp.where(lane_k < DIFF_DH, pltpu.roll(qf, DIFF_DH, 1), qfeat).T.astype(BF16)))

    n_maps = 2 * nh
    n_stats = 2 * n_cols + 1

    def scores(i, j):
        rows = pl.ds(pl.multiple_of(j * tile, tile), tile)
        k_ref_i = k1_ref if i % 2 == 0 else k2_ref
        return jnp.dot(k_ref_i[i // 2, rows, :], qs[i // 2][i % 2], preferred_element_type=F32)

    def softmax_part(stats, s, c, fix):
        out, alphas, ps = (), [], []
        for col in range(n_cols):
            lanes = slice(col * LANES, (col + 1) * LANES)
            n_keys = tile if fix is None else (col + 1) * LANES
            sh = s[:n_keys, lanes]
            if fix is not None:
                sh = sh + fix[:n_keys, lanes]
            m, l = stats[2 * col], stats[2 * col + 1]
            m_new = jnp.maximum(m, jnp.max(sh, axis=0, keepdims=True) + c)
            alpha = jnp.exp(m - m_new)
            p = jnp.exp(sh - (m_new - c))
            out += (m_new, alpha * l + jnp.sum(p, axis=0, keepdims=True))
            p = p.astype(BF16)
            alphas.append(jnp.broadcast_to(alpha, (DIFF_DV, LANES)))
            if n_keys < tile:
                p = jnp.concatenate([p, jnp.zeros((tile - n_keys, LANES), BF16)], axis=0)
            ps.append(p)
        return out, jnp.concatenate(alphas, axis=1), jnp.concatenate(ps, axis=1)

    def step(stats, j, cs, fixes):
        ss = [scores(i, j) for i in range(n_maps)]
        parts = [softmax_part(stats[n_stats * i:n_stats * (i + 1)], ss[i], cs[i // 2],
                              None if fixes is None else fixes[i // 2])
                 for i in range(n_maps)]
        out = ()
        for i, (st, alpha, p) in enumerate(parts):
            a = alpha * stats[n_stats * i + n_stats - 1] + jnp.dot(
                vt_ref[i // 2, j], p, preferred_element_type=F32)
            out += st + (a,)
        return out

    def body(j, stats):
        off = jnp.full((1, LANES), (j - qi) * tile, I32).astype(F32)
        return step(stats, j, [off * slopes[g] for g in range(nh)], None)

    row = lambda v: jnp.full((1, LANES), v, F32)
    init = ((row(NEG), row(0.0)) * n_cols + (jnp.zeros((DIFF_DV, tile), F32),)) * n_maps
    stats = lax.fori_loop(0, qi, body, init)

    @pl.when((pl.program_id(0) == 0) & (pl.program_id(1) == 0) & (qi == 0))
    def _():
        r_i = lax.broadcasted_iota(I32, (tile, tile), 0)
        c_i = lax.broadcasted_iota(I32, (tile, tile), 1)
        allowed = (r_i >> CHUNK_SHIFT) <= (c_i >> CHUNK_SHIFT)
        geo_ref[0] = jnp.where(r_i > c_i, (c_i - r_i).astype(F32), 0.0)
        geo_ref[1] = jnp.where(allowed, 0.0, NEG)

    fixes = [(2.0 * slopes[g]) * geo_ref[0] + geo_ref[1] for g in range(nh)]
    stats = step(stats, qi, [row(0.0)] * nh, fixes)
    lam = _lam(lq1, lk1, lq2, lk2)
    for g in range(nh):
        res = []
        for i in (2 * g, 2 * g + 1):
            st = stats[n_stats * i:n_stats * (i + 1)]
            l = jnp.concatenate([jnp.broadcast_to(st[2 * col + 1], (DIFF_DV, LANES))
                                 for col in range(n_cols)], axis=1)
            res.append(st[-1] / l)
        o = (res[0] - lam * res[1]).T
        o_ref[:, hcols[g]] = (_rms(o, dn_ref[...]) * (1.0 - LAM_INIT)).astype(o_ref.dtype)


def _attn_prompt(qb, kb, vb, lq1, lk1, lq2, lk2, dn, B, L):
    tile = min(ATT_TILE, L)
    nq = L // tile
    nh = ATT_HEADS_PER_STEP
    qmap = lambda b, h, i: (b * nq + i, h)
    kvmap = lambda b, h, i: (b, h)
    cmap = lambda b, h, i: (0, 0)
    lspec = pl.BlockSpec((1, DIFF_DH), cmap)
    return pl.pallas_call(
        functools.partial(_attn_prompt_kernel, tile=tile, nh=nh),
        grid=(B, DIFF_HEADS // nh, nq),
        in_specs=[pl.BlockSpec(memory_space=pltpu.SMEM),
                  pl.BlockSpec((tile, nh * DIFF_DV), qmap),
                  pl.BlockSpec((L, nh * DIFF_DV), kvmap), pl.BlockSpec((L, nh * DIFF_DV), kvmap),
                  lspec, lspec, lspec, lspec, pl.BlockSpec((1, DIFF_DV), cmap)],
        out_specs=pl.BlockSpec((tile, nh * DIFF_DV), qmap),
        out_shape=jax.ShapeDtypeStruct((B * L, D_MODEL), BF16),
        scratch_shapes=[pltpu.VMEM((nh, L, DIFF_DV), BF16), pltpu.VMEM((nh, L, DIFF_DV), BF16),
                        pltpu.VMEM((nh, nq, DIFF_DV, tile), BF16),
                        pltpu.VMEM((2, tile, tile), F32)],
        compiler_params=_cparams(("arbitrary", "arbitrary", "arbitrary")),
        name="attn_prompt",
    )(_alibi_slopes(), qb, kb, vb, lq1, lk1, lq2, lk2, dn)


def _attn_sample_kernel(slope_ref, q_ref, kp_ref, vp_ref, kn_ref, vn_ref, lq1, lk1, lq2, lk2,
                        dn_ref, o_ref, m_ref, l_ref, acc_ref, kbuf, vbuf, sem, *, past, lq, chunk):
    b = pl.program_id(0)
    c = pl.program_id(1)
    n_c = pl.num_programs(1)
    step = b * n_c + c
    slot = step & 1

    def slab_copies(bb, cc, s, fn):
        rows = pl.ds(cc * chunk, chunk)
        for hd in range(DIFF_HEADS):
            fn(pltpu.make_async_copy(kp_ref.at[0, bb, rows, hd, :], kbuf.at[s, hd], sem.at[s]))
            fn(pltpu.make_async_copy(vp_ref.at[0, bb, rows, hd, :], vbuf.at[s, hd], sem.at[s]))

    @pl.when(step == 0)
    def _():
        slab_copies(b, c, slot, lambda cp: cp.start())

    @pl.when(step + 1 < pl.num_programs(0) * n_c)
    def _():
        wrap = c + 1 == n_c
        slab_copies(jnp.where(wrap, b + 1, b), jnp.where(wrap, 0, c + 1), 1 - slot,
                    lambda cp: cp.start())

    slab_copies(b, c, slot, lambda cp: cp.wait())

    @pl.when(c == 0)
    def _():
        m_ref[...] = jnp.full_like(m_ref, NEG)
        l_ref[...] = jnp.zeros_like(l_ref)
        acc_ref[...] = jnp.zeros_like(acc_ref)

    def geometry(key0, n_keys):
        r = lax.broadcasted_iota(I32, (2 * lq, n_keys), 0)
        qpos = past + jnp.where(r >= lq, r - lq, r)
        kpos = key0 + lax.broadcasted_iota(I32, (2 * lq, n_keys), 1)
        allowed = (kpos >> CHUNK_SHIFT) <= (qpos >> CHUNK_SHIFT)
        return jnp.abs(qpos - kpos).astype(F32), jnp.where(allowed, 0.0, NEG)

    def block(hd, k, v, geo):
        dist, mask = geo
        q1, q2 = _split_q(q_ref[:, hd * DIFF_DV:(hd + 1) * DIFF_DV])
        s = _nt(jnp.concatenate([q1, q2], axis=0), k)
        s = s + (mask - slope_ref[hd] * dist)
        m_old = m_ref[hd]
        m_new = jnp.maximum(m_old, jnp.max(s, axis=-1, keepdims=True))
        alpha = jnp.exp(m_old - m_new)
        p = jnp.exp(s - m_new)
        l_ref[hd] = alpha * l_ref[hd] + jnp.sum(p, axis=-1, keepdims=True)
        acc_ref[hd] = alpha * acc_ref[hd] + jnp.dot(p.astype(BF16), v, preferred_element_type=F32)
        m_ref[hd] = m_new

    geo = geometry(c * chunk, chunk)
    for hd in range(DIFF_HEADS):
        block(hd, kbuf[slot, hd].astype(BF16), vbuf[slot, hd].astype(BF16), geo)

    @pl.when(c == n_c - 1)
    def _():
        lam = _lam(lq1, lk1, lq2, lk2)
        geo_new = geometry(past, lq)
        for hd in range(DIFF_HEADS):
            cols = slice(hd * DIFF_DV, (hd + 1) * DIFF_DV)
            block(hd, kn_ref[:, cols], vn_ref[:, cols], geo_new)
            o = acc_ref[hd] / l_ref[hd]
            o = o[:lq] - lam * o[lq:]
            o_ref[:, hd * DIFF_DV:(hd + 1) * DIFF_DV] = (
                _rms(o, dn_ref[...]) * (1.0 - LAM_INIT)).astype(o_ref.dtype)


def _attn_sample(qb, cache_k, cache_v, kb, vb, lq1, lk1, lq2, lk2, dn, B, L):
    past = cache_k.shape[2]
    chunk = min(SAMPLE_KEY_CHUNK, past)
    n_chunks = past // chunk
    bmap = lambda b, c: (b, 0)
    cmap = lambda b, c: (0, 0)
    lspec = pl.BlockSpec((1, DIFF_DH), cmap)
    tok = pl.BlockSpec((L, D_MODEL), bmap)
    cache = pl.BlockSpec(memory_space=pl.ANY)
    slabs = pltpu.VMEM((2, DIFF_HEADS, chunk, DIFF_DV), F32)
    return pl.pallas_call(
        functools.partial(_attn_sample_kernel, past=past, lq=L, chunk=chunk),
        grid=(B, n_chunks),
        in_specs=[pl.BlockSpec(memory_space=pltpu.SMEM), tok, cache, cache, tok, tok,
                  lspec, lspec, lspec, lspec, pl.BlockSpec((1, DIFF_DV), cmap)],
        out_specs=tok,
        out_shape=jax.ShapeDtypeStruct((B * L, D_MODEL), BF16),
        scratch_shapes=[pltpu.VMEM((DIFF_HEADS, 2 * L, 1), F32),
                        pltpu.VMEM((DIFF_HEADS, 2 * L, 1), F32),
                        pltpu.VMEM((DIFF_HEADS, 2 * L, DIFF_DV), F32),
                        slabs, slabs, pltpu.SemaphoreType.DMA((2,))],
        compiler_params=_cparams(("arbitrary", "arbitrary")),
        name="attn_sample",
    )(_alibi_slopes(), qb, cache_k, cache_v, kb, vb, lq1, lk1, lq2, lk2, dn)


ROUTE_E1, ROUTE_E2, ROUTE_W1, ROUTE_W2, ROUTE_R1, ROUTE_R2, ROUTE_P1, ROUTE_P2 = range(8)
ROUTE_ROWS = 8
TABLE_RUN, TABLE_OFF, TABLE_BASE = range(3)
GROUP_LANE0 = N_EXPERTS


def _merge_kernel(x_ref, oa_ref, ob_ref, ga_ref, gb_ref, wa_ref, wb_ref, wo_ref, nf_ref,
                  wr_ref, br_ref, x1_ref, h2_ref, route_ref, route_t_ref, cnt_ref, tables_ref,
                  run_ref, before_ref):
    i = pl.program_id(0)

    @pl.when(i == 0)
    def _():
        run_ref[...] = jnp.zeros_like(run_ref)
        r_i = lax.broadcasted_iota(I32, before_ref.shape, 0)
        c_i = lax.broadcasted_iota(I32, before_ref.shape, 1)
        before_ref[...] = jnp.where(c_i < r_i, 1.0, 0.0).astype(BF16)

    u_a = jnp.dot(oa_ref[...], wa_ref[...], preferred_element_type=F32)
    u_b = jnp.dot(ob_ref[...], wb_ref[...], preferred_element_type=F32)
    mix = (jax.nn.sigmoid(ga_ref[...].astype(F32)) * u_a
           + jax.nn.sigmoid(gb_ref[...].astype(F32)) * u_b)
    x1 = x_ref[...] + jnp.dot(mix.astype(BF16), wo_ref[...], preferred_element_type=F32)
    x1_ref[...] = x1
    h2 = _rms(x1, nf_ref[...])
    h2_ref[...] = h2.astype(h2_ref.dtype)

    tm = h2.shape[0]
    split = lambda a: (a.astype(BF16), (a - a.astype(BF16).astype(F32)).astype(BF16))
    w_pair = jnp.concatenate(split(wr_ref[...]), axis=1)
    parts = sum(jnp.dot(a, w_pair, preferred_element_type=F32) for a in split(h2))
    logits = parts[:, :LANES] + parts[:, LANES:] + br_ref[...]
    lane = lax.broadcasted_iota(I32, (tm, LANES), 1)
    big = jnp.int32(LANES)
    g_mask = (lane >= GROUP_LANE0) & (lane < GROUP_LANE0 + N_GROUPS)
    gl = jnp.where(g_mask, logits, -jnp.inf)
    gmax = jnp.max(gl, axis=-1, keepdims=True)
    g_sel = jnp.min(jnp.where(gl == gmax, lane - GROUP_LANE0, big), axis=-1, keepdims=True)
    p_g = 1.0 / jnp.sum(jnp.where(g_mask, jnp.exp(logits - gmax), 0.0), axis=-1, keepdims=True)
    e_mask = (lane < N_EXPERTS) & ((lane >> EPG_SHIFT) == g_sel)
    el = jnp.where(e_mask, logits, -jnp.inf)
    v1 = jnp.max(el, axis=-1, keepdims=True)
    i1 = jnp.min(jnp.where(el == v1, lane, big), axis=-1, keepdims=True)
    el2 = jnp.where(lane == i1, -jnp.inf, el)
    v2 = jnp.max(el2, axis=-1, keepdims=True)
    i2 = jnp.min(jnp.where(el2 == v2, lane, big), axis=-1, keepdims=True)
    t = jnp.exp(v2 - v1)
    w1 = p_g / (1.0 + t)
    w2 = p_g * t / (1.0 + t)

    oh1 = lane == i1
    oh2 = lane == i2
    cnt = jnp.where(oh1, 1.0, 0.0) + jnp.where(oh2, 1.0, 0.0)
    local = jnp.dot(before_ref[...], cnt.astype(BF16), preferred_element_type=F32)
    n_run = jnp.floor((jnp.sum(cnt, axis=0, keepdims=True) + (RUN_ALIGN - 1.0)) * (1.0 / RUN_ALIGN))
    n_run = n_run * RUN_ALIGN
    e_r = lax.broadcasted_iota(I32, (LANES, LANES), 0)
    e_c = lax.broadcasted_iota(I32, (LANES, LANES), 1)
    earlier = jnp.where(e_r < e_c, 1.0, 0.0).astype(BF16)
    t_off = jnp.dot(jnp.broadcast_to(n_run, (SUBLANES, LANES)).astype(BF16), earlier,
                    preferred_element_type=F32)[:1]
    run = run_ref[...]
    pick = lambda oh, v: jnp.sum(jnp.where(oh, v, 0.0), axis=-1, keepdims=True)
    rank1, rank2 = pick(oh1, run + local), pick(oh2, run + local)
    pos1, pos2 = pick(oh1, t_off + local), pick(oh2, t_off + local)
    run_ref[...] = run + n_run
    cnt_ref[...] = run + n_run
    tables_ref[0] = jnp.concatenate([n_run, t_off, run, jnp.zeros((SUBLANES - 3, LANES), F32)], axis=0)

    route = jnp.zeros((tm, LANES), F32)
    for pos, val in ((ROUTE_E1, i1.astype(F32)), (ROUTE_E2, i2.astype(F32)), (ROUTE_W1, w1),
                     (ROUTE_W2, w2), (ROUTE_R1, rank1), (ROUTE_R2, rank2), (ROUTE_P1, pos1),
                     (ROUTE_P2, pos2)):
        route = jnp.where(lane == pos, val, route)
    route_ref[...] = route
    route_t_ref[...] = route.T[:ROUTE_ROWS, :].astype(I32)


def _merge(x, oa, ob, ga, gb, wa, wb, wo, nf, wr, br):
    T = x.shape[0]
    tm = min(TOKEN_TILE, T)
    row = lambda i: (i, 0)
    const = lambda i: (0, 0)
    wspec = pl.BlockSpec((D_MODEL, D_MODEL), const)
    tile = pl.BlockSpec((tm, D_MODEL), row)
    return pl.pallas_call(
        _merge_kernel,
        grid=(T // tm,),
        in_specs=[tile, tile, tile, tile, tile, wspec, wspec, wspec,
                  pl.BlockSpec((1, D_MODEL), const),
                  pl.BlockSpec((D_MODEL, LANES), const), pl.BlockSpec((1, LANES), const)],
        out_specs=(tile, tile, pl.BlockSpec((tm, LANES), row),
                   pl.BlockSpec((ROUTE_ROWS, tm), lambda i: (0, i)), pl.BlockSpec((1, LANES), const),
                   pl.BlockSpec((1, SUBLANES, LANES), lambda i: (i, 0, 0))),
        out_shape=(jax.ShapeDtypeStruct((T, D_MODEL), F32),
                   jax.ShapeDtypeStruct((T, D_MODEL), BF16),
                   jax.ShapeDtypeStruct((T, LANES), F32),
                   jax.ShapeDtypeStruct((ROUTE_ROWS, T), I32),
                   jax.ShapeDtypeStruct((1, LANES), F32),
                   jax.ShapeDtypeStruct((T // tm, SUBLANES, LANES), F32)),
        scratch_shapes=[pltpu.VMEM((1, LANES), F32), pltpu.VMEM((tm, tm), BF16)],
        compiler_params=_cparams(("arbitrary",)),
        name="merge",
    )(x, oa, ob, ga, gb, wa, wb, wo, nf, wr, br)


RUN_BLOCKS = tuple(1 << b for b in range(9, 3, -1))


def _for_each_run_block(run_ref, off_ref, base_ref, step, fn):
    for e in range(N_EXPERTS):
        idx = step * N_EXPERTS + e
        n_left, src, dst = run_ref[idx], off_ref[idx], base_ref[idx]
        for rows in RUN_BLOCKS:
            take = n_left & rows

            @pl.when(take != 0)
            def _(src=src, dst=dst, rows=rows):
                fn(pl.multiple_of(src, RUN_ALIGN), pl.multiple_of(dst, RUN_ALIGN), rows)
            src, dst = src + take, dst + take


def _dispatch_kernel(ends_ref, run_ref, off_ref, base_ref, ridx_ref, h2_ref, xs_ref, zero_ref,
                     buf_ref, sem):
    i = pl.program_id(0)
    n = pl.num_programs(0)
    slot = i & 1
    slot_block = zero_ref.shape[0]
    slot_shift = slot_block.bit_length() - 1
    n_blocks = xs_ref.shape[0] // slot_block
    n_rows, tm = buf_ref.shape[1], h2_ref.shape[0]
    zero_sem = sem.at[2]

    def run_copies(step, s, fn):
        _for_each_run_block(run_ref, off_ref, base_ref, step, lambda src, dst, rows: fn(
            pltpu.make_async_copy(buf_ref.at[s, pl.ds(src, rows)], xs_ref.at[pl.ds(dst, rows)],
                                  sem.at[s])))

    @pl.when(i == 0)
    def _():
        zero_ref[...] = jnp.zeros_like(zero_ref)

        def zero_block(blk):
            start = pl.multiple_of(blk * slot_block, slot_block)
            return pltpu.make_async_copy(zero_ref, xs_ref.at[pl.ds(start, slot_block)], zero_sem)

        def expert_tail(e):
            return jnp.maximum((ends_ref[e] >> slot_shift) - 1, 0)

        n_used = ends_ref[N_EXPERTS - 1] >> slot_shift
        for e in range(N_EXPERTS):
            zero_block(expert_tail(e)).start()
        lax.fori_loop(n_used, n_blocks, lambda blk, c: (zero_block(blk).start(), c)[1], 0)
        for e in range(N_EXPERTS):
            zero_block(expert_tail(e)).wait()
        lax.fori_loop(n_used, n_blocks, lambda blk, c: (zero_block(blk).wait(), c)[1], 0)

    @pl.when(i >= 2)
    def _():
        run_copies(i - 2, slot, lambda cp: cp.wait())

    row = lax.broadcasted_iota(I32, (n_rows, tm), 0)
    picked = (row == ridx_ref[ROUTE_P1:ROUTE_P1 + 1, :]) | (row == ridx_ref[ROUTE_P2:ROUTE_P2 + 1, :])
    buf_ref[slot] = jnp.dot(jnp.where(picked, 1.0, 0.0).astype(BF16), h2_ref[...],
                            preferred_element_type=F32).astype(BF16)
    run_copies(i, slot, lambda cp: cp.start())

    @pl.when(i == n - 1)
    def _():
        run_copies(i, slot, lambda cp: cp.wait())

    @pl.when((i == n - 1) & (i >= 1))
    def _():
        run_copies(i - 1, 1 - slot, lambda cp: cp.wait())


def _dispatch(pad_ends, tables, ridx, h2, n_slots, slot_block):
    T = h2.shape[0]
    tm = min(TOKEN_TILE, T)
    smem = pl.BlockSpec(memory_space=pltpu.SMEM)
    n_rows = 2 * tm + N_EXPERTS * RUN_ALIGN
    return pl.pallas_call(
        _dispatch_kernel,
        grid=(T // tm,),
        in_specs=[smem, smem, smem, smem,
                  pl.BlockSpec((ROUTE_ROWS, tm), lambda i: (0, i)),
                  pl.BlockSpec((tm, D_MODEL), lambda i: (i, 0))],
        out_specs=pl.BlockSpec(memory_space=pl.ANY),
        out_shape=jax.ShapeDtypeStruct((n_slots, D_MODEL), BF16),
        scratch_shapes=[pltpu.VMEM((slot_block, D_MODEL), BF16),
                        pltpu.VMEM((2, n_rows, D_MODEL), BF16), pltpu.SemaphoreType.DMA((3,))],
        compiler_params=_cparams(("arbitrary",)),
        name="dispatch",
    )(pad_ends, *tables, ridx, h2)


def _experts_kernel(be_ref, nu_ref, xs_ref, wg_ref, wu_ref, wd_ref, y_ref, wg16, wu16, wd16):
    i = pl.program_id(0)
    used = i < nu_ref[0]

    @pl.when(used & ((i == 0) | (be_ref[i] != be_ref[jnp.maximum(i - 1, 0)])))
    def _():
        wg16[...] = wg_ref[0].astype(BF16)
        wu16[...] = wu_ref[0].astype(BF16)
        wd16[...] = wd_ref[0].astype(BF16)

    @pl.when(used)
    def _():
        x = xs_ref[...]
        g = jnp.dot(x, wg16[...], preferred_element_type=F32)
        u = jnp.dot(x, wu16[...], preferred_element_type=F32)
        a = (g * jax.nn.sigmoid(g) * u).astype(BF16)
        y_ref[...] = jnp.dot(a, wd16[...], preferred_element_type=F32).astype(y_ref.dtype)

    @pl.when(jnp.logical_not(used))
    def _():
        y_ref[...] = jnp.zeros_like(y_ref)


def _experts(block_e, n_used, xs, wg, wu, wd, slot_block):
    n_blocks = xs.shape[0] // slot_block
    last = lambda i, nu: jnp.minimum(i, jnp.maximum(nu[0] - 1, 0))
    blk = lambda i, be, nu: (last(i, nu), 0)
    wmap = lambda i, be, nu: (be[last(i, nu)], 0, 0)
    grid_spec = pltpu.PrefetchScalarGridSpec(
        num_scalar_prefetch=2,
        grid=(n_blocks,),
        in_specs=[pl.BlockSpec((slot_block, D_MODEL), blk),
                  pl.BlockSpec((1, D_MODEL, D_EXPERT), wmap),
                  pl.BlockSpec((1, D_MODEL, D_EXPERT), wmap),
                  pl.BlockSpec((1, D_EXPERT, D_MODEL), wmap)],
        out_specs=pl.BlockSpec((slot_block, D_MODEL), lambda i, be, nu: (i, 0)),
        scratch_shapes=[pltpu.VMEM((D_MODEL, D_EXPERT), BF16), pltpu.VMEM((D_MODEL, D_EXPERT), BF16),
                        pltpu.VMEM((D_EXPERT, D_MODEL), BF16)],
    )
    return pl.pallas_call(
        _experts_kernel,
        grid_spec=grid_spec,
        out_shape=jax.ShapeDtypeStruct((xs.shape[0], D_MODEL), BF16),
        compiler_params=_cparams(("arbitrary",)),
        name="experts",
    )(block_e, n_used, xs, wg, wu, wd)


def _combine_kernel(run_ref, off_ref, base_ref, x1_ref, route_ref, nf_ref, yb_ref, y_ref,
                    buf_ref, sem):
    i = pl.program_id(0)
    n = pl.num_programs(0)
    tm, n_rows = x1_ref.shape[0], buf_ref.shape[1]
    slot = i & 1

    def run_copies(step, s, fn):
        _for_each_run_block(run_ref, off_ref, base_ref, step, lambda dst, src, rows: fn(
            pltpu.make_async_copy(yb_ref.at[pl.ds(src, rows)], buf_ref.at[s, pl.ds(dst, rows)],
                                  sem.at[s])))

    @pl.when(i == 0)
    def _():
        buf_ref[...] = jnp.zeros_like(buf_ref)
        run_copies(i, slot, lambda cp: cp.start())

    @pl.when(i + 1 < n)
    def _():
        run_copies(i + 1, 1 - slot, lambda cp: cp.start())

    run_copies(i, slot, lambda cp: cp.wait())
    route = route_ref[...]
    rows_bf = buf_ref[slot]
    col = lax.broadcasted_iota(I32, (tm, n_rows), 1)
    pick = jnp.concatenate(
        [jnp.where(col == route[:, p:p + 1].astype(I32), 1.0, 0.0).astype(BF16)
         for p in (ROUTE_P1, ROUTE_P2)], axis=0)
    got = jnp.dot(pick, rows_bf, preferred_element_type=F32)
    moe = route[:, ROUTE_W1:ROUTE_W1 + 1] * got[:tm] + route[:, ROUTE_W2:ROUTE_W2 + 1] * got[tm:]
    y_ref[...] = _rms(x1_ref[...] + moe, nf_ref[...])


def _combine(tables, x1, route, nfinal, yb):
    T = x1.shape[0]
    tm = min(TOKEN_TILE, T)
    row = lambda i: (i, 0)
    smem = pl.BlockSpec(memory_space=pltpu.SMEM)
    n_rows = 2 * tm + N_EXPERTS * RUN_ALIGN
    return pl.pallas_call(
        _combine_kernel,
        grid=(T // tm,),
        in_specs=[smem, smem, smem,
                  pl.BlockSpec((tm, D_MODEL), row), pl.BlockSpec((tm, LANES), row),
                  pl.BlockSpec((1, D_MODEL), lambda i: (0, 0)),
                  pl.BlockSpec(memory_space=pl.ANY)],
        out_specs=pl.BlockSpec((tm, D_MODEL), row),
        out_shape=jax.ShapeDtypeStruct((T, D_MODEL), F32),
        scratch_shapes=[pltpu.VMEM((2, n_rows, D_MODEL), BF16), pltpu.SemaphoreType.DMA((2,))],
        compiler_params=_cparams(("arbitrary",)),
        name="combine",
    )(*tables, x1, route, nfinal, yb)


def _moe(x1, h2, route, route_t, counts, tables, wg, wu, wd, nfinal):
    T = x1.shape[0]
    tm = min(TOKEN_TILE, T)
    slot_block = SLOT_BLOCK if 2 * T >= N_EXPERTS * SLOT_BLOCK else SLOT_BLOCK_SMALL
    n_slots = T * 2 + (T // tm) * N_EXPERTS * RUN_ALIGN + N_EXPERTS * slot_block
    n_blocks = -(-n_slots // slot_block)
    n_slots = n_blocks * slot_block
    cnt = counts[0, :N_EXPERTS].astype(I32)
    padded = (cnt + slot_block - 1) // slot_block * slot_block
    pad_ends = jnp.cumsum(padded)
    pad_starts = (pad_ends - padded).astype(I32)
    pad_ends = pad_ends.astype(I32)
    block_start = jnp.arange(n_blocks, dtype=I32) * slot_block
    block_e = jnp.sum((block_start[:, None] >= pad_ends[None, :]).astype(I32), axis=1)
    block_e = jnp.minimum(block_e, N_EXPERTS - 1)
    n_used = pad_ends[-1:] // slot_block
    per_run = lambda row: tables[:, row, :N_EXPERTS].astype(I32)
    run_tables = (per_run(TABLE_RUN).reshape(-1), per_run(TABLE_OFF).reshape(-1),
                  (per_run(TABLE_BASE) + pad_starts[None, :]).reshape(-1))
    xs = _dispatch(pad_ends, run_tables, route_t, h2, n_slots, slot_block)
    yb = _experts(block_e, n_used, xs, wg, wu, wd, slot_block)
    return _combine(run_tables, x1, route, nfinal, yb)


def _layer(x, s0, k_past, v_past, p, B, L):
    row2 = lambda a: a.reshape(1, -1)
    qka, va, ra, qb, kb, vb, ga, gb, da, k5, v5 = _in_proj(x, row2(p["norm_mix"]), p["w_in"])
    oa, s_new = _gla(qka, va, ra, da, p["w_decay"], row2(p["b_decay"]), row2(p["gla_norm"]),
                     s0, B, L)
    lams = [row2(p[n]) for n in ("lambda_q1", "lambda_k1", "lambda_q2", "lambda_k2")]
    dn = row2(p["diff_norm"])
    if k_past is None:
        ob = _attn_prompt(qb, kb, vb, *lams, dn, B, L)
    else:
        ob = _attn_sample(qb, k_past, v_past, kb, vb, *lams, dn, B, L)
    x1, h2, route, route_t, counts, tables = _merge(
        x, oa, ob, ga, gb, p["w_proj_a"], p["w_proj_b"], p["w_out"], row2(p["norm_ffn"]),
        p["w_router"], p["b_router"])
    y = _moe(x1, h2, route, route_t, counts, tables, p["w_gate"], p["w_up"], p["w_down"],
             row2(p["norm_final"]))
    return y, s_new, k5, v5


def kernel(x_prompt, x_sample, cache_k, cache_v, state_gla, norm_mix, w_in, w_decay, b_decay,
           gla_norm, w_proj_a, lambda_q1, lambda_k1, lambda_q2, lambda_k2, diff_norm, w_proj_b,
           w_out, norm_ffn, w_router_group, b_router_group, w_router_expert, b_router_expert,
           w_gate, w_up, w_down, norm_final):
    B, L, D = x_prompt.shape
    Bs, Ls, _ = x_sample.shape
    w_router = jnp.concatenate(
        [w_router_expert[0], w_router_group[0],
         jnp.zeros((D, LANES - N_EXPERTS - N_GROUPS), F32)], axis=1)
    b_router = jnp.concatenate(
        [b_router_expert[0], b_router_group[0],
         jnp.zeros((LANES - N_EXPERTS - N_GROUPS,), F32)]).reshape(1, LANES)
    p = dict(
        norm_mix=norm_mix[0], w_in=_prep_w_in(w_in[0]),
        w_decay=jnp.pad(w_decay[0], ((0, LANES - GLA_RANK), (0, 0))), b_decay=b_decay[0],
        gla_norm=gla_norm[0], w_proj_a=w_proj_a[0].astype(BF16),
        lambda_q1=lambda_q1[0], lambda_k1=lambda_k1[0], lambda_q2=lambda_q2[0],
        lambda_k2=lambda_k2[0], diff_norm=diff_norm[0], w_proj_b=w_proj_b[0].astype(BF16),
        w_out=w_out[0].astype(BF16), norm_ffn=norm_ffn[0], w_router=w_router, b_router=b_router,
        w_gate=w_gate[0], w_up=w_up[0], w_down=w_down[0],
        norm_final=norm_final)

    s0p = jnp.zeros((B, GLA_HEADS, GLA_DK, GLA_DV), F32)
    yp, sp, kp, vp = _layer(x_prompt.reshape(B * L, D), s0p, None, None, p, B, L)
    ys, ss, ks, vs = _layer(x_sample.reshape(Bs * Ls, D), state_gla[0], cache_k, cache_v,
                            p, Bs, Ls)
    kv = lambda a, b, l: a.reshape(1, b, l, DIFF_HEADS, DIFF_DV)
    return (yp.reshape(B, L, D), ys.reshape(Bs, Ls, D), kv(kp, B, L), kv(vp, B, L), sp[None],
            kv(ks, Bs, Ls), kv(vs, Bs, Ls), ss[None])
```

```python
import functools

import jax
import jax.numpy as jnp
from jax import lax
from jax.experimental import pallas as pl
from jax.experimental.pallas import tpu as pltpu

F32 = jnp.float32
BF16 = jnp.bfloat16
I32 = jnp.int32

D_MODEL = 1024
CHUNK = 64
CHUNK_SHIFT = 6
EPS = 1e-6
GLA_HEADS = 4
GLA_DK = 128
GLA_DV = 256
GLA_RANK = 16
GLA_TAU = 16.0
DIFF_HEADS = 8
DIFF_DH = 64
DIFF_DV = 128
N_GROUPS = 4
EXPERTS_PER_GROUP = 4
EPG_SHIFT = 2
N_EXPERTS = 16
D_EXPERT = 512
LAM_INIT = 0.8 - 0.6

LANES = 128
SUBLANES = 8
RUN_ALIGN = 16
TOKEN_TILE = 512
SLOT_BLOCK = 512
SLOT_BLOCK_SMALL = 128
ATT_TILE = 512
ATT_HEADS_PER_STEP = 4
SAMPLE_KEY_CHUNK = 1024
GLA_CUMSUM_ROWS = 256
GLA_SEQS_PER_STEP = 2
VMEM_LIMIT = 56 * 1024 * 1024
NEG = -1e30

HI = lax.Precision.HIGHEST


def _cparams(sem):
    return pltpu.CompilerParams(dimension_semantics=sem, vmem_limit_bytes=VMEM_LIMIT)


def _nt(a, b):
    return lax.dot_general(a, b, (((1,), (1,)), ((), ())), preferred_element_type=F32)


def _tn(a, b):
    return lax.dot_general(a, b, (((0,), (0,)), ((), ())), preferred_element_type=F32)


def _rms(x, g):
    return x * lax.rsqrt(jnp.mean(x * x, axis=-1, keepdims=True) + EPS) * g


def _inproj_kernel(x_ref, g_ref, wa_ref, wda_ref, wb_ref, qka_ref, va_ref, ra_ref, qb_ref, kb_ref,
                   vb_ref, ga_ref, gb_ref, da_ref, k5_ref, v5_ref, stage_ref, sem):
    i = pl.program_id(0)
    n = pl.num_programs(0)
    tm = x_ref.shape[0]
    slot = i & 1

    def head_copy(s, which, hd):
        dst = (k5_ref, v5_ref)[which]
        return pltpu.make_async_copy(stage_ref.at[s, which, hd],
                                     dst.at[pl.ds(i * tm, tm), hd, :], sem.at[s])

    def for_all(s, fn):
        for which in range(2):
            for hd in range(DIFF_HEADS):
                fn(head_copy(s, which, hd))

    @pl.when(i >= 2)
    def _():
        for_all(slot, lambda cp: cp.wait())

    h = _rms(x_ref[...], g_ref[...]).astype(BF16)
    outs = ((qka_ref, wa_ref, 0), (va_ref, wa_ref, 1), (ra_ref, wa_ref, 2), (qb_ref, wb_ref, 0),
            (kb_ref, wb_ref, 1), (vb_ref, wb_ref, 2), (ga_ref, wb_ref, 3), (gb_ref, wb_ref, 4))
    for o, w_ref, c in outs:
        z = jnp.dot(h, w_ref[:, c * D_MODEL:(c + 1) * D_MODEL], preferred_element_type=F32)
        o[...] = z.astype(o.dtype)
        which = 0 if o is kb_ref else 1 if o is vb_ref else None
        if which is not None:
            for hd in range(DIFF_HEADS):
                stage_ref[slot, which, hd] = z[:, hd * DIFF_DV:(hd + 1) * DIFF_DV]
    da_ref[...] = jnp.dot(h, wda_ref[...], preferred_element_type=F32)
    for_all(slot, lambda cp: cp.start())

    @pl.when(i == n - 1)
    def _():
        for_all(slot, lambda cp: cp.wait())

    @pl.when((i == n - 1) & (i >= 1))
    def _():
        for_all(1 - slot, lambda cp: cp.wait())


N_GLA_COLS = 3 * D_MODEL
N_DIFF_COLS = 5 * D_MODEL


def _prep_w_in(w_in):
    w = w_in.astype(BF16)
    return w, w[:, N_GLA_COLS + GLA_RANK:]


def _in_proj(x, g, w):
    w_all, w_diff = w
    T = x.shape[0]
    tm = min(TOKEN_TILE, T)
    resident = lambda cols, j: pl.BlockSpec((D_MODEL, cols), lambda i: (0, j),
                                            pipeline_mode=pl.Buffered(1))
    row = lambda i: (i, 0)
    const = lambda i: (0, 0)
    wide = lambda dt: jax.ShapeDtypeStruct((T, D_MODEL), dt)
    heads = jax.ShapeDtypeStruct((T, DIFF_HEADS, DIFF_DV), F32)
    out_shape = (wide(BF16),) * 8 + (jax.ShapeDtypeStruct((T, LANES), F32), heads, heads)
    out_specs = tuple([pl.BlockSpec((tm, D_MODEL), row)] * 8 + [pl.BlockSpec((tm, LANES), row)]
                      + [pl.BlockSpec(memory_space=pl.ANY)] * 2)
    return pl.pallas_call(
        _inproj_kernel,
        grid=(T // tm,),
        in_specs=[pl.BlockSpec((tm, D_MODEL), row),
                  pl.BlockSpec((1, D_MODEL), const),
                  resident(N_GLA_COLS, 0), resident(LANES, N_GLA_COLS // LANES),
                  resident(N_DIFF_COLS, 0)],
        out_specs=out_specs,
        out_shape=out_shape,
        scratch_shapes=[pltpu.VMEM((2, 2, DIFF_HEADS, tm, DIFF_DV), F32),
                        pltpu.SemaphoreType.DMA((2,))],
        compiler_params=_cparams(("arbitrary",)),
        name="in_proj",
    )(x, g, w_all, w_all, w_diff)


def _gla_kernel(qka_ref, va_ref, ra_ref, da_ref, wd_ref, bd_ref, gn_ref, s0_ref,
                oa_ref, sout_ref, s_ref, la_ref, *, chunk, n_chunks):
    l = pl.program_id(1)

    n_seq = s_ref.shape[0]

    @pl.when(l == 0)
    def _():
        s_ref[...] = s0_ref[...]

    split = lambda a: (a.astype(BF16), (a - a.astype(BF16).astype(F32)).astype(BF16))
    n_k = GLA_HEADS * GLA_DK
    fold = lambda z: z[:, :n_k] + z[:, n_k:]

    wd_pair = jnp.concatenate(split(wd_ref[...]), axis=1)
    lb = da_ref.shape[1]
    grp = min(lb, GLA_CUMSUM_ROWS)
    shift = chunk.bit_length() - 1
    r_b = lax.broadcasted_iota(I32, (grp, grp), 0)
    c_b = lax.broadcasted_iota(I32, (grp, grp), 1)
    tri = jnp.where((c_b <= r_b) & ((c_b >> shift) == (r_b >> shift)), 1.0, 0.0).astype(BF16)
    for s in range(n_seq):
        x = fold(sum(jnp.dot(a, wd_pair, preferred_element_type=F32) for a in split(da_ref[s])))
        x = x + bd_ref[...]
        log_a = (jnp.minimum(x, 0.0) - jnp.log1p(jnp.exp(-jnp.abs(x)))) * (1.0 / GLA_TAU)
        for g in range(lb // grp):
            rows = slice(g * grp, (g + 1) * grp)
            la_ref[s, rows, :] = fold(jnp.dot(tri, jnp.concatenate(split(log_a[rows, :]), axis=1),
                                              preferred_element_type=F32))

    r_i = lax.broadcasted_iota(I32, (chunk, chunk), 0)
    c_i = lax.broadcasted_iota(I32, (chunk, chunk), 1)
    causal = c_i <= r_i
    gn = gn_ref[...]
    scale = GLA_DK ** -0.5
    units = [(s, h) for s in range(n_seq) for h in range(GLA_HEADS)]
    vcols = [slice(h * GLA_DV, (h + 1) * GLA_DV) for h in range(GLA_HEADS)]

    @pl.loop(0, n_chunks)
    def _(c):
        rows = pl.ds(pl.multiple_of(c * chunk, chunk), chunk)
        q_t, k_t, k_end, decay = [], [], [], []
        for s, h in units:
            b = la_ref[s, rows, h * GLA_DK:(h + 1) * GLA_DK]
            b_last = b[chunk - 1:chunk, :]
            q = qka_ref[s, rows, h * GLA_DK:(h + 1) * GLA_DK].astype(F32) * scale
            k = qka_ref[s, rows, n_k + h * GLA_DK:n_k + (h + 1) * GLA_DK].astype(F32)
            q_t.append((q * jnp.exp(b)).astype(BF16))
            k_t.append((k * jnp.exp(-b)).astype(BF16))
            k_end.append((k * jnp.exp(b_last - b)).astype(BF16))
            decay.append(jnp.broadcast_to(jnp.exp(b_last), (GLA_DK, GLA_DK)).T)
        n_u = range(len(units))
        v = [va_ref[s, rows, vcols[h]] for s, h in units]
        s_old = [s_ref[s, h] for s, h in units]
        o_state = [jnp.dot(q_t[u], s_old[u].astype(BF16), preferred_element_type=F32) for u in n_u]
        att = [_nt(q_t[u], k_t[u]) for u in n_u]
        kv = [_tn(k_end[u], v[u]) for u in n_u]
        for u, (s, h) in enumerate(units):
            a = jnp.where(causal, att[u], 0.0).astype(BF16)
            o = o_state[u] + jnp.dot(a, v[u], preferred_element_type=F32)
            s_ref[s, h] = jnp.concatenate([decay[u], decay[u]], axis=1) * s_old[u] + kv[u]
            r = ra_ref[s, rows, vcols[h]].astype(F32)
            oa_ref[s, rows, vcols[h]] = (_rms(o, gn) * (r * jax.nn.sigmoid(r))).astype(oa_ref.dtype)

    @pl.when(l == pl.num_programs(1) - 1)
    def _():
        sout_ref[...] = s_ref[...]


def _gla(qka, va, ra, da, wd, bd, gn, s0, B, L):
    chunk = min(CHUNK, L)
    lb = min(TOKEN_TILE, L)
    nl = L // lb
    ns = GLA_SEQS_PER_STEP if B % GLA_SEQS_PER_STEP == 0 else 1
    tok = lambda w: pl.BlockSpec((ns, lb, w), lambda b, l: (b, l, 0))
    const2 = lambda b, l: (0, 0)
    st = pl.BlockSpec((ns, GLA_HEADS, GLA_DK, GLA_DV), lambda b, l: (b, 0, 0, 0))
    seq = lambda a: a.reshape(B, L, a.shape[-1])
    kern = functools.partial(_gla_kernel, chunk=chunk, n_chunks=lb // chunk)
    oa, s_new = pl.pallas_call(
        kern,
        grid=(B // ns, nl),
        in_specs=[tok(D_MODEL), tok(D_MODEL), tok(D_MODEL), tok(LANES),
                  pl.BlockSpec((LANES, GLA_HEADS * GLA_DK), const2),
                  pl.BlockSpec((1, GLA_HEADS * GLA_DK), const2),
                  pl.BlockSpec((1, GLA_DV), const2), st],
        out_specs=(tok(D_MODEL), st),
        out_shape=(jax.ShapeDtypeStruct((B, L, D_MODEL), BF16),
                   jax.ShapeDtypeStruct((B, GLA_HEADS, GLA_DK, GLA_DV), F32)),
        scratch_shapes=[pltpu.VMEM((ns, GLA_HEADS, GLA_DK, GLA_DV), F32),
                        pltpu.VMEM((ns, lb, GLA_HEADS * GLA_DK), F32)],
        compiler_params=_cparams(("arbitrary", "arbitrary")),
        name="gla",
    )(seq(qka), seq(va), seq(ra), seq(da), wd, bd, gn, s0)
    return oa.reshape(B * L, D_MODEL), s_new


def _lam(lq1, lk1, lq2, lk2):
    a = jnp.sum(lq1[...] * lk1[...], axis=-1, keepdims=True)
    b = jnp.sum(lq2[...] * lk2[...], axis=-1, keepdims=True)
    return jnp.exp(a) - jnp.exp(b) + LAM_INIT


def _split_q(q):
    lane = lax.broadcasted_iota(I32, q.shape, 1)
    qs = q * jnp.asarray(DIFF_DH ** -0.5, q.dtype)
    zero = jnp.zeros_like(qs)
    return jnp.where(lane < DIFF_DH, qs, zero), jnp.where(lane >= DIFF_DH, qs, zero)


def _alibi_slopes():
    return jnp.asarray([2.0 ** (-8.0 * (h + 1) / DIFF_HEADS) for h in range(DIFF_HEADS)], F32)


def _attn_prompt_kernel(slope_ref, q_ref, k_ref, v_ref, lq1, lk1, lq2, lk2, dn_ref, o_ref,
                        k1_ref, k2_ref, vt_ref, geo_ref, *, tile, nh):
    qi = pl.program_id(2)
    n_kv = vt_ref.shape[1]
    n_cols = tile // LANES
    lane_k = lax.broadcasted_iota(I32, (tile, DIFF_DV), 1)
    row_i = lax.broadcasted_iota(I32, (tile, DIFF_DV), 0)
    row_lo = (row_i & 255).astype(F32)
    row_hi = (row_i & -256).astype(F32)
    slopes = [slope_ref[pl.program_id(1) * nh + g] for g in range(nh)]
    hcols = [slice(g * DIFF_DV, (g + 1) * DIFF_DV) for g in range(nh)]

    @pl.when(qi == 0)
    def _():
        for g in range(nh):
            kfeat = jnp.where(lane_k == DIFF_DH, slopes[g] * row_lo,
                              jnp.where(lane_k == DIFF_DH + 1, slopes[g] * row_hi,
                                        jnp.where((lane_k == DIFF_DH + 2) | (lane_k == DIFF_DH + 3),
                                                  1.0, 0.0)))
            for c in range(n_kv):
                rows = slice(c * tile, (c + 1) * tile)
                kf = k_ref[rows, hcols[g]].astype(F32)
                k1_ref[g, rows, :] = jnp.where(lane_k < DIFF_DH, kf, kfeat).astype(BF16)
                k2_ref[g, rows, :] = jnp.where(lane_k < DIFF_DH, pltpu.roll(kf, DIFF_DH, 1),
                                               kfeat).astype(BF16)
                vt_ref[g, c] = v_ref[rows, hcols[g]].astype(F32).T.astype(BF16)

    qs = []
    for g in range(nh):
        qf = q_ref[:, hcols[g]].astype(F32) * (DIFF_DH ** -0.5)
        qfeat = jnp.where((lane_k == DIFF_DH) | (lane_k == DIFF_DH + 1), 1.0,
                          jnp.where(lane_k == DIFF_DH + 2, -slopes[g] * row_lo,
                                    jnp.where(lane_k == DIFF_DH + 3, -slopes[g] * row_hi, 0.0)))
        qs.append((jnp.where(lane_k < DIFF_DH, qf, qfeat).T.astype(BF16),
                   jnp.where(lane_k < DIFF_DH, pltpu.roll(qf, DIFF_DH, 1), qfeat).T.astype(BF16)))

    n_maps = 2 * nh
    n_stats = 2 * n_cols + 1

    def scores(i, j):
        rows = pl.ds(pl.multiple_of(j * tile, tile), tile)
        k_ref_i = k1_ref if i % 2 == 0 else k2_ref
        return jnp.dot(k_ref_i[i // 2, rows, :], qs[i // 2][i % 2], preferred_element_type=F32)

    def softmax_part(stats, s, c, fix):
        out, alphas, ps = (), [], []
        for col in range(n_cols):
            lanes = slice(col * LANES, (col + 1) * LANES)
            n_keys = tile if fix is None else (col + 1) * LANES
            sh = s[:n_keys, lanes]
            if fix is not None:
                sh = sh + fix[:n_keys, lanes]
            m, l = stats[2 * col], stats[2 * col + 1]
            m_new = jnp.maximum(m, jnp.max(sh, axis=0, keepdims=True) + c)
            alpha = jnp.exp(m - m_new)
            p = jnp.exp(sh - (m_new - c))
            out += (m_new, alpha * l + jnp.sum(p, axis=0, keepdims=True))
            p = p.astype(BF16)
            alphas.append(jnp.broadcast_to(alpha, (DIFF_DV, LANES)))
            if n_keys < tile:
                p = jnp.concatenate([p, jnp.zeros((tile - n_keys, LANES), BF16)], axis=0)
            ps.append(p)
        return out, jnp.concatenate(alphas, axis=1), jnp.concatenate(ps, axis=1)

    def step(stats, j, cs, fixes):
        ss = [scores(i, j) for i in range(n_maps)]
        parts = [softmax_part(stats[n_stats * i:n_stats * (i + 1)], ss[i], cs[i // 2],
                              None if fixes is None else fixes[i // 2])
                 for i in range(n_maps)]
        out = ()
        for i, (st, alpha, p) in enumerate(parts):
            a = alpha * stats[n_stats * i + n_stats - 1] + jnp.dot(
                vt_ref[i // 2, j], p, preferred_element_type=F32)
            out += st + (a,)
        return out

    def body(j, stats):
        off = jnp.full((1, LANES), (j - qi) * tile, I32).astype(F32)
        return step(stats, j, [off * slopes[g] for g in range(nh)], None)

    row = lambda v: jnp.full((1, LANES), v, F32)
    init = ((row(NEG), row(0.0)) * n_cols + (jnp.zeros((DIFF_DV, tile), F32),)) * n_maps
    stats = lax.fori_loop(0, qi, body, init)

    @pl.when((pl.program_id(0) == 0) & (pl.program_id(1) == 0) & (qi == 0))
    def _():
        r_i = lax.broadcasted_iota(I32, (tile, tile), 0)
        c_i = lax.broadcasted_iota(I32, (tile, tile), 1)
        allowed = (r_i >> CHUNK_SHIFT) <= (c_i >> CHUNK_SHIFT)
        geo_ref[0] = jnp.where(r_i > c_i, (c_i - r_i).astype(F32), 0.0)
        geo_ref[1] = jnp.where(allowed, 0.0, NEG)

    fixes = [(2.0 * slopes[g]) * geo_ref[0] + geo_ref[1] for g in range(nh)]
    stats = step(stats, qi, [row(0.0)] * nh, fixes)
    lam = _lam(lq1, lk1, lq2, lk2)
    for g in range(nh):
        res = []
        for i in (2 * g, 2 * g + 1):
            st = stats[n_stats * i:n_stats * (i + 1)]
            l = jnp.concatenate([jnp.broadcast_to(st[2 * col + 1], (DIFF_DV, LANES))
                                 for col in range(n_cols)], axis=1)
            res.append(st[-1] / l)
        o = (res[0] - lam * res[1]).T
        o_ref[:, hcols[g]] = (_rms(o, dn_ref[...]) * (1.0 - LAM_INIT)).astype(o_ref.dtype)


def _attn_prompt(qb, kb, vb, lq1, lk1, lq2, lk2, dn, B, L):
    tile = min(ATT_TILE, L)
    nq = L // tile
    nh = ATT_HEADS_PER_STEP
    qmap = lambda b, h, i: (b * nq + i, h)
    kvmap = lambda b, h, i: (b, h)
    cmap = lambda b, h, i: (0, 0)
    lspec = pl.BlockSpec((1, DIFF_DH), cmap)
    return pl.pallas_call(
        functools.partial(_attn_prompt_kernel, tile=tile, nh=nh),
        grid=(B, DIFF_HEADS // nh, nq),
        in_specs=[pl.BlockSpec(memory_space=pltpu.SMEM),
                  pl.BlockSpec((tile, nh * DIFF_DV), qmap),
                  pl.BlockSpec((L, nh * DIFF_DV), kvmap), pl.BlockSpec((L, nh * DIFF_DV), kvmap),
                  lspec, lspec, lspec, lspec, pl.BlockSpec((1, DIFF_DV), cmap)],
        out_specs=pl.BlockSpec((tile, nh * DIFF_DV), qmap),
        out_shape=jax.ShapeDtypeStruct((B * L, D_MODEL), BF16),
        scratch_shapes=[pltpu.VMEM((nh, L, DIFF_DV), BF16), pltpu.VMEM((nh, L, DIFF_DV), BF16),
                        pltpu.VMEM((nh, nq, DIFF_DV, tile), BF16),
                        pltpu.VMEM((2, tile, tile), F32)],
        compiler_params=_cparams(("arbitrary", "arbitrary", "arbitrary")),
        name="attn_prompt",
    )(_alibi_slopes(), qb, kb, vb, lq1, lk1, lq2, lk2, dn)


def _attn_sample_kernel(slope_ref, q_ref, kp_ref, vp_ref, kn_ref, vn_ref, lq1, lk1, lq2, lk2,
                        dn_ref, o_ref, m_ref, l_ref, acc_ref, kbuf, vbuf, sem, *, past, lq, chunk):
    b = pl.program_id(0)
    c = pl.program_id(1)
    n_c = pl.num_programs(1)
    step = b * n_c + c
    slot = step & 1

    def slab_copies(bb, cc, s, fn):
        rows = pl.ds(cc * chunk, chunk)
        for hd in range(DIFF_HEADS):
            fn(pltpu.make_async_copy(kp_ref.at[0, bb, rows, hd, :], kbuf.at[s, hd], sem.at[s]))
            fn(pltpu.make_async_copy(vp_ref.at[0, bb, rows, hd, :], vbuf.at[s, hd], sem.at[s]))

    @pl.when(step == 0)
    def _():
        slab_copies(b, c, slot, lambda cp: cp.start())

    @pl.when(step + 1 < pl.num_programs(0) * n_c)
    def _():
        wrap = c + 1 == n_c
        slab_copies(jnp.where(wrap, b + 1, b), jnp.where(wrap, 0, c + 1), 1 - slot,
                    lambda cp: cp.start())

    slab_copies(b, c, slot, lambda cp: cp.wait())

    @pl.when(c == 0)
    def _():
        m_ref[...] = jnp.full_like(m_ref, NEG)
        l_ref[...] = jnp.zeros_like(l_ref)
        acc_ref[...] = jnp.zeros_like(acc_ref)

    def geometry(key0, n_keys):
        r = lax.broadcasted_iota(I32, (2 * lq, n_keys), 0)
        qpos = past + jnp.where(r >= lq, r - lq, r)
        kpos = key0 + lax.broadcasted_iota(I32, (2 * lq, n_keys), 1)
        allowed = (kpos >> CHUNK_SHIFT) <= (qpos >> CHUNK_SHIFT)
        return jnp.abs(qpos - kpos).astype(F32), jnp.where(allowed, 0.0, NEG)

    def block(hd, k, v, geo):
        dist, mask = geo
        q1, q2 = _split_q(q_ref[:, hd * DIFF_DV:(hd + 1) * DIFF_DV])
        s = _nt(jnp.concatenate([q1, q2], axis=0), k)
        s = s + (mask - slope_ref[hd] * dist)
        m_old = m_ref[hd]
        m_new = jnp.maximum(m_old, jnp.max(s, axis=-1, keepdims=True))
        alpha = jnp.exp(m_old - m_new)
        p = jnp.exp(s - m_new)
        l_ref[hd] = alpha * l_ref[hd] + jnp.sum(p, axis=-1, keepdims=True)
        acc_ref[hd] = alpha * acc_ref[hd] + jnp.dot(p.astype(BF16), v, preferred_element_type=F32)
        m_ref[hd] = m_new

    geo = geometry(c * chunk, chunk)
    for hd in range(DIFF_HEADS):
        block(hd, kbuf[slot, hd].astype(BF16), vbuf[slot, hd].astype(BF16), geo)

    @pl.when(c == n_c - 1)
    def _():
        lam = _lam(lq1, lk1, lq2, lk2)
        geo_new = geometry(past, lq)
        for hd in range(DIFF_HEADS):
            cols = slice(hd * DIFF_DV, (hd + 1) * DIFF_DV)
            block(hd, kn_ref[:, cols], vn_ref[:, cols], geo_new)
            o = acc_ref[hd] / l_ref[hd]
            o = o[:lq] - lam * o[lq:]
            o_ref[:, hd * DIFF_DV:(hd + 1) * DIFF_DV] = (
                _rms(o, dn_ref[...]) * (1.0 - LAM_INIT)).astype(o_ref.dtype)


def _attn_sample(qb, cache_k, cache_v, kb, vb, lq1, lk1, lq2, lk2, dn, B, L):
    past = cache_k.shape[2]
    chunk = min(SAMPLE_KEY_CHUNK, past)
    n_chunks = past // chunk
    bmap = lambda b, c: (b, 0)
    cmap = lambda b, c: (0, 0)
    lspec = pl.BlockSpec((1, DIFF_DH), cmap)
    tok = pl.BlockSpec((L, D_MODEL), bmap)
    cache = pl.BlockSpec(memory_space=pl.ANY)
    slabs = pltpu.VMEM((2, DIFF_HEADS, chunk, DIFF_DV), F32)
    return pl.pallas_call(
        functools.partial(_attn_sample_kernel, past=past, lq=L, chunk=chunk),
        grid=(B, n_chunks),
        in_specs=[pl.BlockSpec(memory_space=pltpu.SMEM), tok, cache, cache, tok, tok,
                  lspec, lspec, lspec, lspec, pl.BlockSpec((1, DIFF_DV), cmap)],
        out_specs=tok,
        out_shape=jax.ShapeDtypeStruct((B * L, D_MODEL), BF16),
        scratch_shapes=[pltpu.VMEM((DIFF_HEADS, 2 * L, 1), F32),
                        pltpu.VMEM((DIFF_HEADS, 2 * L, 1), F32),
                        pltpu.VMEM((DIFF_HEADS, 2 * L, DIFF_DV), F32),
                        slabs, slabs, pltpu.SemaphoreType.DMA((2,))],
        compiler_params=_cparams(("arbitrary", "arbitrary")),
        name="attn_sample",
    )(_alibi_slopes(), qb, cache_k, cache_v, kb, vb, lq1, lk1, lq2, lk2, dn)


ROUTE_E1, ROUTE_E2, ROUTE_W1, ROUTE_W2, ROUTE_R1, ROUTE_R2, ROUTE_P1, ROUTE_P2 = range(8)
ROUTE_ROWS = 8
TABLE_RUN, TABLE_OFF, TABLE_BASE = range(3)
GROUP_LANE0 = N_EXPERTS


def _merge_kernel(x_ref, oa_ref, ob_ref, ga_ref, gb_ref, wa_ref, wb_ref, wo_ref, nf_ref,
                  wr_ref, br_ref, x1_ref, h2_ref, route_ref, route_t_ref, cnt_ref, tables_ref,
                  run_ref, before_ref):
    i = pl.program_id(0)

    @pl.when(i == 0)
    def _():
        run_ref[...] = jnp.zeros_like(run_ref)
        r_i = lax.broadcasted_iota(I32, before_ref.shape, 0)
        c_i = lax.broadcasted_iota(I32, before_ref.shape, 1)
        before_ref[...] = jnp.where(c_i < r_i, 1.0, 0.0).astype(BF16)

    u_a = jnp.dot(oa_ref[...], wa_ref[...], preferred_element_type=F32)
    u_b = jnp.dot(ob_ref[...], wb_ref[...], preferred_element_type=F32)
    mix = (jax.nn.sigmoid(ga_ref[...].astype(F32)) * u_a
           + jax.nn.sigmoid(gb_ref[...].astype(F32)) * u_b)
    x1 = x_ref[...] + jnp.dot(mix.astype(BF16), wo_ref[...], preferred_element_type=F32)
    x1_ref[...] = x1
    h2 = _rms(x1, nf_ref[...])
    h2_ref[...] = h2.astype(h2_ref.dtype)

    tm = h2.shape[0]
    split = lambda a: (a.astype(BF16), (a - a.astype(BF16).astype(F32)).astype(BF16))
    w_pair = jnp.concatenate(split(wr_ref[...]), axis=1)
    parts = sum(jnp.dot(a, w_pair, preferred_element_type=F32) for a in split(h2))
    logits = parts[:, :LANES] + parts[:, LANES:] + br_ref[...]
    lane = lax.broadcasted_iota(I32, (tm, LANES), 1)
    big = jnp.int32(LANES)
    g_mask = (lane >= GROUP_LANE0) & (lane < GROUP_LANE0 + N_GROUPS)
    gl = jnp.where(g_mask, logits, -jnp.inf)
    gmax = jnp.max(gl, axis=-1, keepdims=True)
    g_sel = jnp.min(jnp.where(gl == gmax, lane - GROUP_LANE0, big), axis=-1, keepdims=True)
    p_g = 1.0 / jnp.sum(jnp.where(g_mask, jnp.exp(logits - gmax), 0.0), axis=-1, keepdims=True)
    e_mask = (lane < N_EXPERTS) & ((lane >> EPG_SHIFT) == g_sel)
    el = jnp.where(e_mask, logits, -jnp.inf)
    v1 = jnp.max(el, axis=-1, keepdims=True)
    i1 = jnp.min(jnp.where(el == v1, lane, big), axis=-1, keepdims=True)
    el2 = jnp.where(lane == i1, -jnp.inf, el)
    v2 = jnp.max(el2, axis=-1, keepdims=True)
    i2 = jnp.min(jnp.where(el2 == v2, lane, big), axis=-1, keepdims=True)
    t = jnp.exp(v2 - v1)
    w1 = p_g / (1.0 + t)
    w2 = p_g * t / (1.0 + t)

    oh1 = lane == i1
    oh2 = lane == i2
    cnt = jnp.where(oh1, 1.0, 0.0) + jnp.where(oh2, 1.0, 0.0)
    local = jnp.dot(before_ref[...], cnt.astype(BF16), preferred_element_type=F32)
    n_run = jnp.floor((jnp.sum(cnt, axis=0, keepdims=True) + (RUN_ALIGN - 1.0)) * (1.0 / RUN_ALIGN))
    n_run = n_run * RUN_ALIGN
    e_r = lax.broadcasted_iota(I32, (LANES, LANES), 0)
    e_c = lax.broadcasted_iota(I32, (LANES, LANES), 1)
    earlier = jnp.where(e_r < e_c, 1.0, 0.0).astype(BF16)
    t_off = jnp.dot(jnp.broadcast_to(n_run, (SUBLANES, LANES)).astype(BF16), earlier,
                    preferred_element_type=F32)[:1]
    run = run_ref[...]
    pick = lambda oh, v: jnp.sum(jnp.where(oh, v, 0.0), axis=-1, keepdims=True)
    rank1, rank2 = pick(oh1, run + local), pick(oh2, run + local)
    pos1, pos2 = pick(oh1, t_off + local), pick(oh2, t_off + local)
    run_ref[...] = run + n_run
    cnt_ref[...] = run + n_run
    tables_ref[0] = jnp.concatenate([n_run, t_off, run, jnp.zeros((SUBLANES - 3, LANES), F32)], axis=0)

    route = jnp.zeros((tm, LANES), F32)
    for pos, val in ((ROUTE_E1, i1.astype(F32)), (ROUTE_E2, i2.astype(F32)), (ROUTE_W1, w1),
                     (ROUTE_W2, w2), (ROUTE_R1, rank1), (ROUTE_R2, rank2), (ROUTE_P1, pos1),
                     (ROUTE_P2, pos2)):
        route = jnp.where(lane == pos, val, route)
    route_ref[...] = route
    route_t_ref[...] = route.T[:ROUTE_ROWS, :].astype(I32)


def _merge(x, oa, ob, ga, gb, wa, wb, wo, nf, wr, br):
    T = x.shape[0]
    tm = min(TOKEN_TILE, T)
    row = lambda i: (i, 0)
    const = lambda i: (0, 0)
    wspec = pl.BlockSpec((D_MODEL, D_MODEL), const)
    tile = pl.BlockSpec((tm, D_MODEL), row)
    return pl.pallas_call(
        _merge_kernel,
        grid=(T // tm,),
        in_specs=[tile, tile, tile, tile, tile, wspec, wspec, wspec,
                  pl.BlockSpec((1, D_MODEL), const),
                  pl.BlockSpec((D_MODEL, LANES), const), pl.BlockSpec((1, LANES), const)],
        out_specs=(tile, tile, pl.BlockSpec((tm, LANES), row),
                   pl.BlockSpec((ROUTE_ROWS, tm), lambda i: (0, i)), pl.BlockSpec((1, LANES), const),
                   pl.BlockSpec((1, SUBLANES, LANES), lambda i: (i, 0, 0))),
        out_shape=(jax.ShapeDtypeStruct((T, D_MODEL), F32),
                   jax.ShapeDtypeStruct((T, D_MODEL), BF16),
                   jax.ShapeDtypeStruct((T, LANES), F32),
                   jax.ShapeDtypeStruct((ROUTE_ROWS, T), I32),
                   jax.ShapeDtypeStruct((1, LANES), F32),
                   jax.ShapeDtypeStruct((T // tm, SUBLANES, LANES), F32)),
        scratch_shapes=[pltpu.VMEM((1, LANES), F32), pltpu.VMEM((tm, tm), BF16)],
        compiler_params=_cparams(("arbitrary",)),
        name="merge",
    )(x, oa, ob, ga, gb, wa, wb, wo, nf, wr, br)


RUN_BLOCKS = tuple(1 << b for b in range(9, 3, -1))


def _for_each_run_block(run_ref, off_ref, base_ref, step, fn):
    for e in range(N_EXPERTS):
        idx = step * N_EXPERTS + e
        n_left, src, dst = run_ref[idx], off_ref[idx], base_ref[idx]
        for rows in RUN_BLOCKS:
            take = n_left & rows

            @pl.when(take != 0)
            def _(src=src, dst=dst, rows=rows):
                fn(pl.multiple_of(src, RUN_ALIGN), pl.multiple_of(dst, RUN_ALIGN), rows)
            src, dst = src + take, dst + take


def _dispatch_kernel(ends_ref, run_ref, off_ref, base_ref, ridx_ref, h2_ref, xs_ref, zero_ref,
                     buf_ref, sem):
    i = pl.program_id(0)
    n = pl.num_programs(0)
    slot = i & 1
    slot_block = zero_ref.shape[0]
    slot_shift = slot_block.bit_length() - 1
    n_blocks = xs_ref.shape[0] // slot_block
    n_rows, tm = buf_ref.shape[1], h2_ref.shape[0]
    zero_sem = sem.at[2]

    def run_copies(step, s, fn):
        _for_each_run_block(run_ref, off_ref, base_ref, step, lambda src, dst, rows: fn(
            pltpu.make_async_copy(buf_ref.at[s, pl.ds(src, rows)], xs_ref.at[pl.ds(dst, rows)],
                                  sem.at[s])))

    @pl.when(i == 0)
    def _():
        zero_ref[...] = jnp.zeros_like(zero_ref)

        def zero_block(blk):
            start = pl.multiple_of(blk * slot_block, slot_block)
            return pltpu.make_async_copy(zero_ref, xs_ref.at[pl.ds(start, slot_block)], zero_sem)

        def expert_tail(e):
            return jnp.maximum((ends_ref[e] >> slot_shift) - 1, 0)

        n_used = ends_ref[N_EXPERTS - 1] >> slot_shift
        for e in range(N_EXPERTS):
            zero_block(expert_tail(e)).start()
        lax.fori_loop(n_used, n_blocks, lambda blk, c: (zero_block(blk).start(), c)[1], 0)
        for e in range(N_EXPERTS):
            zero_block(expert_tail(e)).wait()
        lax.fori_loop(n_used, n_blocks, lambda blk, c: (zero_block(blk).wait(), c)[1], 0)

    @pl.when(i >= 2)
    def _():
        run_copies(i - 2, slot, lambda cp: cp.wait())

    row = lax.broadcasted_iota(I32, (n_rows, tm), 0)
    picked = (row == ridx_ref[ROUTE_P1:ROUTE_P1 + 1, :]) | (row == ridx_ref[ROUTE_P2:ROUTE_P2 + 1, :])
    buf_ref[slot] = jnp.dot(jnp.where(picked, 1.0, 0.0).astype(BF16), h2_ref[...],
                            preferred_element_type=F32).astype(BF16)
    run_copies(i, slot, lambda cp: cp.start())

    @pl.when(i == n - 1)
    def _():
        run_copies(i, slot, lambda cp: cp.wait())

    @pl.when((i == n - 1) & (i >= 1))
    def _():
        run_copies(i - 1, 1 - slot, lambda cp: cp.wait())


def _dispatch(pad_ends, tables, ridx, h2, n_slots, slot_block):
    T = h2.shape[0]
    tm = min(TOKEN_TILE, T)
    smem = pl.BlockSpec(memory_space=pltpu.SMEM)
    n_rows = 2 * tm + N_EXPERTS * RUN_ALIGN
    return pl.pallas_call(
        _dispatch_kernel,
        grid=(T // tm,),
        in_specs=[smem, smem, smem, smem,
                  pl.BlockSpec((ROUTE_ROWS, tm), lambda i: (0, i)),
                  pl.BlockSpec((tm, D_MODEL), lambda i: (i, 0))],
        out_specs=pl.BlockSpec(memory_space=pl.ANY),
        out_shape=jax.ShapeDtypeStruct((n_slots, D_MODEL), BF16),
        scratch_shapes=[pltpu.VMEM((slot_block, D_MODEL), BF16),
                        pltpu.VMEM((2, n_rows, D_MODEL), BF16), pltpu.SemaphoreType.DMA((3,))],
        compiler_params=_cparams(("arbitrary",)),
        name="dispatch",
    )(pad_ends, *tables, ridx, h2)


def _experts_kernel(be_ref, nu_ref, xs_ref, wg_ref, wu_ref, wd_ref, y_ref, wg16, wu16, wd16):
    i = pl.program_id(0)
    used = i < nu_ref[0]

    @pl.when(used & ((i == 0) | (be_ref[i] != be_ref[jnp.maximum(i - 1, 0)])))
    def _():
        wg16[...] = wg_ref[0].astype(BF16)
        wu16[...] = wu_ref[0].astype(BF16)
        wd16[...] = wd_ref[0].astype(BF16)

    @pl.when(used)
    def _():
        x = xs_ref[...]
        g = jnp.dot(x, wg16[...], preferred_element_type=F32)
        u = jnp.dot(x, wu16[...], preferred_element_type=F32)
        a = (g * jax.nn.sigmoid(g) * u).astype(BF16)
        y_ref[...] = jnp.dot(a, wd16[...], preferred_element_type=F32).astype(y_ref.dtype)

    @pl.when(jnp.logical_not(used))
    def _():
        y_ref[...] = jnp.zeros_like(y_ref)


def _experts(block_e, n_used, xs, wg, wu, wd, slot_block):
    n_blocks = xs.shape[0] // slot_block
    last = lambda i, nu: jnp.minimum(i, jnp.maximum(nu[0] - 1, 0))
    blk = lambda i, be, nu: (last(i, nu), 0)
    wmap = lambda i, be, nu: (be[last(i, nu)], 0, 0)
    grid_spec = pltpu.PrefetchScalarGridSpec(
        num_scalar_prefetch=2,
        grid=(n_blocks,),
        in_specs=[pl.BlockSpec((slot_block, D_MODEL), blk),
                  pl.BlockSpec((1, D_MODEL, D_EXPERT), wmap),
                  pl.BlockSpec((1, D_MODEL, D_EXPERT), wmap),
                  pl.BlockSpec((1, D_EXPERT, D_MODEL), wmap)],
        out_specs=pl.BlockSpec((slot_block, D_MODEL), lambda i, be, nu: (i, 0)),
        scratch_shapes=[pltpu.VMEM((D_MODEL, D_EXPERT), BF16), pltpu.VMEM((D_MODEL, D_EXPERT), BF16),
                        pltpu.VMEM((D_EXPERT, D_MODEL), BF16)],
    )
    return pl.pallas_call(
        _experts_kernel,
        grid_spec=grid_spec,
        out_shape=jax.ShapeDtypeStruct((xs.shape[0], D_MODEL), BF16),
        compiler_params=_cparams(("arbitrary",)),
        name="experts",
    )(block_e, n_used, xs, wg, wu, wd)


def _combine_kernel(run_ref, off_ref, base_ref, x1_ref, route_ref, nf_ref, yb_ref, y_ref,
                    buf_ref, sem):
    i = pl.program_id(0)
    n = pl.num_programs(0)
    tm, n_rows = x1_ref.shape[0], buf_ref.shape[1]
    slot = i & 1

    def run_copies(step, s, fn):
        _for_each_run_block(run_ref, off_ref, base_ref, step, lambda dst, src, rows: fn(
            pltpu.make_async_copy(yb_ref.at[pl.ds(src, rows)], buf_ref.at[s, pl.ds(dst, rows)],
                                  sem.at[s])))

    @pl.when(i == 0)
    def _():
        buf_ref[...] = jnp.zeros_like(buf_ref)
        run_copies(i, slot, lambda cp: cp.start())

    @pl.when(i + 1 < n)
    def _():
        run_copies(i + 1, 1 - slot, lambda cp: cp.start())

    run_copies(i, slot, lambda cp: cp.wait())
    route = route_ref[...]
    rows_bf = buf_ref[slot]
    col = lax.broadcasted_iota(I32, (tm, n_rows), 1)
    pick = jnp.concatenate(
        [jnp.where(col == route[:, p:p + 1].astype(I32), 1.0, 0.0).astype(BF16)
         for p in (ROUTE_P1, ROUTE_P2)], axis=0)
    got = jnp.dot(pick, rows_bf, preferred_element_type=F32)
    moe = route[:, ROUTE_W1:ROUTE_W1 + 1] * got[:tm] + route[:, ROUTE_W2:ROUTE_W2 + 1] * got[tm:]
    y_ref[...] = _rms(x1_ref[...] + moe, nf_ref[...])


def _combine(tables, x1, route, nfinal, yb):
    T = x1.shape[0]
    tm = min(TOKEN_TILE, T)
    row = lambda i: (i, 0)
    smem = pl.BlockSpec(memory_space=pltpu.SMEM)
    n_rows = 2 * tm + N_EXPERTS * RUN_ALIGN
    return pl.pallas_call(
        _combine_kernel,
        grid=(T // tm,),
        in_specs=[smem, smem, smem,
                  pl.BlockSpec((tm, D_MODEL), row), pl.BlockSpec((tm, LANES), row),
                  pl.BlockSpec((1, D_MODEL), lambda i: (0, 0)),
                  pl.BlockSpec(memory_space=pl.ANY)],
        out_specs=pl.BlockSpec((tm, D_MODEL), row),
        out_shape=jax.ShapeDtypeStruct((T, D_MODEL), F32),
        scratch_shapes=[pltpu.VMEM((2, n_rows, D_MODEL), BF16), pltpu.SemaphoreType.DMA((2,))],
        compiler_params=_cparams(("arbitrary",)),
        name="combine",
    )(*tables, x1, route, nfinal, yb)


def _moe(x1, h2, route, route_t, counts, tables, wg, wu, wd, nfinal):
    T = x1.shape[0]
    tm = min(TOKEN_TILE, T)
    slot_block = SLOT_BLOCK if 2 * T >= N_EXPERTS * SLOT_BLOCK else SLOT_BLOCK_SMALL
    n_slots = T * 2 + (T // tm) * N_EXPERTS * RUN_ALIGN + N_EXPERTS * slot_block
    n_blocks = -(-n_slots // slot_block)
    n_slots = n_blocks * slot_block
    cnt = counts[0, :N_EXPERTS].astype(I32)
    padded = (cnt + slot_block - 1) // slot_block * slot_block
    pad_ends = jnp.cumsum(padded)
    pad_starts = (pad_ends - padded).astype(I32)
    pad_ends = pad_ends.astype(I32)
    block_start = jnp.arange(n_blocks, dtype=I32) * slot_block
    block_e = jnp.sum((block_start[:, None] >= pad_ends[None, :]).astype(I32), axis=1)
    block_e = jnp.minimum(block_e, N_EXPERTS - 1)
    n_used = pad_ends[-1:] // slot_block
    per_run = lambda row: tables[:, row, :N_EXPERTS].astype(I32)
    run_tables = (per_run(TABLE_RUN).reshape(-1), per_run(TABLE_OFF).reshape(-1),
                  (per_run(TABLE_BASE) + pad_starts[None, :]).reshape(-1))
    xs = _dispatch(pad_ends, run_tables, route_t, h2, n_slots, slot_block)
    yb = _experts(block_e, n_used, xs, wg, wu, wd, slot_block)
    return _combine(run_tables, x1, route, nfinal, yb)


def _layer(x, s0, k_past, v_past, p, B, L):
    row2 = lambda a: a.reshape(1, -1)
    qka, va, ra, qb, kb, vb, ga, gb, da, k5, v5 = _in_proj(x, row2(p["norm_mix"]), p["w_in"])
    oa, s_new = _gla(qka, va, ra, da, p["w_decay"], row2(p["b_decay"]), row2(p["gla_norm"]),
                     s0, B, L)
    lams = [row2(p[n]) for n in ("lambda_q1", "lambda_k1", "lambda_q2", "lambda_k2")]
    dn = row2(p["diff_norm"])
    if k_past is None:
        ob = _attn_prompt(qb, kb, vb, *lams, dn, B, L)
    else:
        ob = _attn_sample(qb, k_past, v_past, kb, vb, *lams, dn, B, L)
    x1, h2, route, route_t, counts, tables = _merge(
        x, oa, ob, ga, gb, p["w_proj_a"], p["w_proj_b"], p["w_out"], row2(p["norm_ffn"]),
        p["w_router"], p["b_router"])
    y = _moe(x1, h2, route, route_t, counts, tables, p["w_gate"], p["w_up"], p["w_down"],
             row2(p["norm_final"]))
    return y, s_new, k5, v5


def kernel(x_prompt, x_sample, cache_k, cache_v, state_gla, norm_mix, w_in, w_decay, b_decay,
           gla_norm, w_proj_a, lambda_q1, lambda_k1, lambda_q2, lambda_k2, diff_norm, w_proj_b,
           w_out, norm_ffn, w_router_group, b_router_group, w_router_expert, b_router_expert,
           w_gate, w_up, w_down, norm_final):
    B, L, D = x_prompt.shape
    Bs, Ls, _ = x_sample.shape
    w_router = jnp.concatenate(
        [w_router_expert[0], w_router_group[0],
         jnp.zeros((D, LANES - N_EXPERTS - N_GROUPS), F32)], axis=1)
    b_router = jnp.concatenate(
        [b_router_expert[0], b_router_group[0],
         jnp.zeros((LANES - N_EXPERTS - N_GROUPS,), F32)]).reshape(1, LANES)
    p = dict(
        norm_mix=norm_mix[0], w_in=_prep_w_in(w_in[0]),
        w_decay=jnp.pad(w_decay[0], ((0, LANES - GLA_RANK), (0, 0))), b_decay=b_decay[0],
        gla_norm=gla_norm[0], w_proj_a=w_proj_a[0].astype(BF16),
        lambda_q1=lambda_q1[0], lambda_k1=lambda_k1[0], lambda_q2=lambda_q2[0],
        lambda_k2=lambda_k2[0], diff_norm=diff_norm[0], w_proj_b=w_proj_b[0].astype(BF16),
        w_out=w_out[0].astype(BF16), norm_ffn=norm_ffn[0], w_router=w_router, b_router=b_router,
        w_gate=w_gate[0], w_up=w_up[0], w_down=w_down[0],
        norm_final=norm_final)

    s0p = jnp.zeros((B, GLA_HEADS, GLA_DK, GLA_DV), F32)
    yp, sp, kp, vp = _layer(x_prompt.reshape(B * L, D), s0p, None, None, p, B, L)
    ys, ss, ks, vs = _layer(x_sample.reshape(Bs * Ls, D), state_gla[0], cache_k, cache_v,
                            p, Bs, Ls)
    kv = lambda a, b, l: a.reshape(1, b, l, DIFF_HEADS, DIFF_DV)
    return (yp.reshape(B, L, D), ys.reshape(Bs, Ls, D), kv(kp, B, L), kv(vp, B, L), sp[None],
            kv(ks, Bs, Ls), kv(vs, Bs, Ls), ss[None])
```

```python
import functools

import jax
import jax.numpy as jnp
from jax import lax
from jax.experimental import pallas as pl
from jax.experimental.pallas import tpu as pltpu

F32 = jnp.float32
BF16 = jnp.bfloat16
I32 = jnp.int32

D_MODEL = 1024
CHUNK = 64
CHUNK_SHIFT = 6
EPS = 1e-6
GLA_HEADS = 4
GLA_DK = 128
GLA_DV = 256
GLA_RANK = 16
GLA_TAU = 16.0
DIFF_HEADS = 8
DIFF_DH = 64
DIFF_DV = 128
N_GROUPS = 4
EXPERTS_PER_GROUP = 4
EPG_SHIFT = 2
N_EXPERTS = 16
D_EXPERT = 512
LAM_INIT = 0.8 - 0.6

LANES = 128
SUBLANES = 8
RUN_ALIGN = 16
TOKEN_TILE = 512
SLOT_BLOCK = 512
SLOT_BLOCK_SMALL = 128
ATT_TILE = 512
ATT_HEADS_PER_STEP = 4
SAMPLE_KEY_CHUNK = 1024
GLA_CUMSUM_ROWS = 256
GLA_SEQS_PER_STEP = 4
VMEM_LIMIT = 56 * 1024 * 1024
NEG = -1e30


def _cparams(sem):
    return pltpu.CompilerParams(dimension_semantics=sem, vmem_limit_bytes=VMEM_LIMIT)


def _nt(a, b):
    return lax.dot_general(a, b, (((1,), (1,)), ((), ())), preferred_element_type=F32)


def _tn(a, b):
    return lax.dot_general(a, b, (((0,), (0,)), ((), ())), preferred_element_type=F32)


def _rms(x, g):
    return x * lax.rsqrt(jnp.mean(x * x, axis=-1, keepdims=True) + EPS) * g


def _inproj_kernel(x_ref, g_ref, wa_ref, wda_ref, wb_ref, qka_ref, va_ref, ra_ref, qb_ref, kb_ref,
                   vb_ref, ga_ref, gb_ref, da_ref, k5_ref, v5_ref, stage_ref, sem):
    i = pl.program_id(0)
    n = pl.num_programs(0)
    tm = x_ref.shape[0]
    slot = i & 1

    def head_copy(s, which, hd):
        dst = (k5_ref, v5_ref)[which]
        return pltpu.make_async_copy(stage_ref.at[s, which, hd],
                                     dst.at[pl.ds(i * tm, tm), hd, :], sem.at[s])

    def for_all(s, fn):
        for which in range(2):
            for hd in range(DIFF_HEADS):
                fn(head_copy(s, which, hd))

    @pl.when(i >= 2)
    def _():
        for_all(slot, lambda cp: cp.wait())

    h = _rms(x_ref[...], g_ref[...]).astype(BF16)
    outs = ((qka_ref, wa_ref, 0), (va_ref, wa_ref, 1), (ra_ref, wa_ref, 2), (qb_ref, wb_ref, 0),
            (kb_ref, wb_ref, 1), (vb_ref, wb_ref, 2), (ga_ref, wb_ref, 3), (gb_ref, wb_ref, 4))
    for o, w_ref, c in outs:
        z = jnp.dot(h, w_ref[:, c * D_MODEL:(c + 1) * D_MODEL], preferred_element_type=F32)
        o[...] = z.astype(o.dtype)
        which = 0 if o is kb_ref else 1 if o is vb_ref else None
        if which is not None:
            for hd in range(DIFF_HEADS):
                stage_ref[slot, which, hd] = z[:, hd * DIFF_DV:(hd + 1) * DIFF_DV]
    da_ref[...] = jnp.dot(h, wda_ref[...], preferred_element_type=F32)
    for_all(slot, lambda cp: cp.start())

    @pl.when(i == n - 1)
    def _():
        for_all(slot, lambda cp: cp.wait())

    @pl.when((i == n - 1) & (i >= 1))
    def _():
        for_all(1 - slot, lambda cp: cp.wait())


N_GLA_COLS = 3 * D_MODEL
N_DIFF_COLS = 5 * D_MODEL


def _prep_w_in(w_in):
    w = w_in.astype(BF16)
    return w, w[:, N_GLA_COLS + GLA_RANK:]


def _in_proj(x, g, w):
    w_all, w_diff = w
    T = x.shape[0]
    tm = min(TOKEN_TILE, T)
    resident = lambda cols, j: pl.BlockSpec((D_MODEL, cols), lambda i: (0, j),
                                            pipeline_mode=pl.Buffered(1))
    row = lambda i: (i, 0)
    const = lambda i: (0, 0)
    wide = lambda dt: jax.ShapeDtypeStruct((T, D_MODEL), dt)
    heads = jax.ShapeDtypeStruct((T, DIFF_HEADS, DIFF_DV), F32)
    out_shape = (wide(BF16),) * 8 + (jax.ShapeDtypeStruct((T, LANES), F32), heads, heads)
    out_specs = tuple([pl.BlockSpec((tm, D_MODEL), row)] * 8 + [pl.BlockSpec((tm, LANES), row)]
                      + [pl.BlockSpec(memory_space=pl.ANY)] * 2)
    return pl.pallas_call(
        _inproj_kernel,
        grid=(T // tm,),
        in_specs=[pl.BlockSpec((tm, D_MODEL), row),
                  pl.BlockSpec((1, D_MODEL), const),
                  resident(N_GLA_COLS, 0), resident(LANES, N_GLA_COLS // LANES),
                  resident(N_DIFF_COLS, 0)],
        out_specs=out_specs,
        out_shape=out_shape,
        scratch_shapes=[pltpu.VMEM((2, 2, DIFF_HEADS, tm, DIFF_DV), F32),
                        pltpu.SemaphoreType.DMA((2,))],
        compiler_params=_cparams(("arbitrary",)),
        name="in_proj",
    )(x, g, w_all, w_all, w_diff)


def _gla_kernel(qka_ref, va_ref, ra_ref, da_ref, wd_ref, bd_ref, gn_ref, s0_ref,
                oa_ref, sout_ref, s_ref, la_ref, *, chunk, n_chunks):
    l = pl.program_id(1)

    n_seq = s_ref.shape[0]

    @pl.when(l == 0)
    def _():
        s_ref[...] = s0_ref[...]

    split = lambda a: (a.astype(BF16), (a - a.astype(BF16).astype(F32)).astype(BF16))
    n_k = GLA_HEADS * GLA_DK
    fold = lambda z: z[:, :n_k] + z[:, n_k:]

    wd_pair = jnp.concatenate(split(wd_ref[...]), axis=1)
    lb = da_ref.shape[1]
    grp = min(lb, GLA_CUMSUM_ROWS)
    shift = chunk.bit_length() - 1
    r_b = lax.broadcasted_iota(I32, (grp, grp), 0)
    c_b = lax.broadcasted_iota(I32, (grp, grp), 1)
    tri = jnp.where((c_b <= r_b) & ((c_b >> shift) == (r_b >> shift)), 1.0, 0.0).astype(BF16)
    for s in range(n_seq):
        x = fold(sum(jnp.dot(a, wd_pair, preferred_element_type=F32) for a in split(da_ref[s])))
        x = x + bd_ref[...]
        log_a = (jnp.minimum(x, 0.0) - jnp.log1p(jnp.exp(-jnp.abs(x)))) * (1.0 / GLA_TAU)
        for g in range(lb // grp):
            rows = slice(g * grp, (g + 1) * grp)
            la_ref[s, rows, :] = fold(jnp.dot(tri, jnp.concatenate(split(log_a[rows, :]), axis=1),
                                              preferred_element_type=F32))

    r_i = lax.broadcasted_iota(I32, (chunk, chunk), 0)
    c_i = lax.broadcasted_iota(I32, (chunk, chunk), 1)
    causal = c_i <= r_i
    gn = gn_ref[...]
    scale = GLA_DK ** -0.5
    units = [(s, h) for s in range(n_seq) for h in range(GLA_HEADS)]
    vcols = [slice(h * GLA_DV, (h + 1) * GLA_DV) for h in range(GLA_HEADS)]

    @pl.loop(0, n_chunks)
    def _(c):
        rows = pl.ds(pl.multiple_of(c * chunk, chunk), chunk)
        q_t, k_t, k_end, decay = [], [], [], []
        for s, h in units:
            b = la_ref[s, rows, h * GLA_DK:(h + 1) * GLA_DK]
            b_last = b[chunk - 1:chunk, :]
            q = qka_ref[s, rows, h * GLA_DK:(h + 1) * GLA_DK].astype(F32) * scale
            k = qka_ref[s, rows, n_k + h * GLA_DK:n_k + (h + 1) * GLA_DK].astype(F32)
            q_t.append((q * jnp.exp(b)).astype(BF16))
            k_t.append((k * jnp.exp(-b)).astype(BF16))
            k_end.append((k * jnp.exp(b_last - b)).astype(BF16))
            decay.append(jnp.broadcast_to(jnp.exp(b_last), (GLA_DK, GLA_DK)).T)
        n_u = range(len(units))
        v = [va_ref[s, rows, vcols[h]] for s, h in units]
        s_old = [s_ref[s, h] for s, h in units]
        o_state = [jnp.dot(q_t[u], s_old[u].astype(BF16), preferred_element_type=F32) for u in n_u]
        att = [_nt(q_t[u], k_t[u]) for u in n_u]
        kv = [_tn(k_end[u], v[u]) for u in n_u]
        for u, (s, h) in enumerate(units):
            a = jnp.where(causal, att[u], 0.0).astype(BF16)
            o = o_state[u] + jnp.dot(a, v[u], preferred_element_type=F32)
            s_ref[s, h] = jnp.concatenate([decay[u], decay[u]], axis=1) * s_old[u] + kv[u]
            r = ra_ref[s, rows, vcols[h]].astype(F32)
            oa_ref[s, rows, vcols[h]] = (_rms(o, gn) * (r * jax.nn.sigmoid(r))).astype(oa_ref.dtype)

    @pl.when(l == pl.num_programs(1) - 1)
    def _():
        sout_ref[...] = s_ref[...]


def _gla(qka, va, ra, da, wd, bd, gn, s0, B, L):
    chunk = min(CHUNK, L)
    lb = min(TOKEN_TILE, L)
    nl = L // lb
    ns = GLA_SEQS_PER_STEP if B % GLA_SEQS_PER_STEP == 0 else 1
    tok = lambda w: pl.BlockSpec((ns, lb, w), lambda b, l: (b, l, 0))
    const2 = lambda b, l: (0, 0)
    st = pl.BlockSpec((ns, GLA_HEADS, GLA_DK, GLA_DV), lambda b, l: (b, 0, 0, 0))
    seq = lambda a: a.reshape(B, L, a.shape[-1])
    kern = functools.partial(_gla_kernel, chunk=chunk, n_chunks=lb // chunk)
    oa, s_new = pl.pallas_call(
        kern,
        grid=(B // ns, nl),
        in_specs=[tok(D_MODEL), tok(D_MODEL), tok(D_MODEL), tok(LANES),
                  pl.BlockSpec((LANES, GLA_HEADS * GLA_DK), const2),
                  pl.BlockSpec((1, GLA_HEADS * GLA_DK), const2),
                  pl.BlockSpec((1, GLA_DV), const2), st],
        out_specs=(tok(D_MODEL), st),
        out_shape=(jax.ShapeDtypeStruct((B, L, D_MODEL), BF16),
                   jax.ShapeDtypeStruct((B, GLA_HEADS, GLA_DK, GLA_DV), F32)),
        scratch_shapes=[pltpu.VMEM((ns, GLA_HEADS, GLA_DK, GLA_DV), F32),
                        pltpu.VMEM((ns, lb, GLA_HEADS * GLA_DK), F32)],
        compiler_params=_cparams(("arbitrary", "arbitrary")),
        name="gla",
    )(seq(qka), seq(va), seq(ra), seq(da), wd, bd, gn, s0)
    return oa.reshape(B * L, D_MODEL), s_new


def _lam(lq1, lk1, lq2, lk2):
    a = jnp.sum(lq1[...] * lk1[...], axis=-1, keepdims=True)
    b = jnp.sum(lq2[...] * lk2[...], axis=-1, keepdims=True)
    return jnp.exp(a) - jnp.exp(b) + LAM_INIT


def _split_q(q):
    lane = lax.broadcasted_iota(I32, q.shape, 1)
    qs = q * jnp.asarray(DIFF_DH ** -0.5, q.dtype)
    zero = jnp.zeros_like(qs)
    return jnp.where(lane < DIFF_DH, qs, zero), jnp.where(lane >= DIFF_DH, qs, zero)


def _alibi_slopes():
    return jnp.asarray([2.0 ** (-8.0 * (h + 1) / DIFF_HEADS) for h in range(DIFF_HEADS)], F32)


def _attn_prompt_kernel(slope_ref, q_ref, k_ref, v_ref, lq1, lk1, lq2, lk2, dn_ref, o_ref,
                        k1_ref, k2_ref, vt_ref, geo_ref, *, tile, nh):
    qi = pl.program_id(2)
    n_kv = vt_ref.shape[1]
    n_cols = tile // LANES
    lane_k = lax.broadcasted_iota(I32, (tile, DIFF_DV), 1)
    row_i = lax.broadcasted_iota(I32, (tile, DIFF_DV), 0)
    row_lo = (row_i & 255).astype(F32)
    row_hi = (row_i & -256).astype(F32)
    slopes = [slope_ref[pl.program_id(1) * nh + g] for g in range(nh)]
    hcols = [slice(g * DIFF_DV, (g + 1) * DIFF_DV) for g in range(nh)]

    @pl.when(qi == 0)
    def _():
        for g in range(nh):
            kfeat = jnp.where(lane_k == DIFF_DH, slopes[g] * row_lo,
                              jnp.where(lane_k == DIFF_DH + 1, slopes[g] * row_hi,
                                        jnp.where((lane_k == DIFF_DH + 2) | (lane_k == DIFF_DH + 3),
                                                  1.0, 0.0)))
            for c in range(n_kv):
                rows = slice(c * tile, (c + 1) * tile)
                kf = k_ref[rows, hcols[g]].astype(F32)
                k1_ref[g, rows, :] = jnp.where(lane_k < DIFF_DH, kf, kfeat).astype(BF16)
                k2_ref[g, rows, :] = jnp.where(lane_k < DIFF_DH, pltpu.roll(kf, DIFF_DH, 1),
                                               kfeat).astype(BF16)
                vt_ref[g, c] = v_ref[rows, hcols[g]].astype(F32).T.astype(BF16)

    qs = []
    for g in range(nh):
        qf = q_ref[:, hcols[g]].astype(F32) * (DIFF_DH ** -0.5)
        qfeat = jnp.where((lane_k == DIFF_DH) | (lane_k == DIFF_DH + 1), 1.0,
                          jnp.where(lane_k == DIFF_DH + 2, -slopes[g] * row_lo,
                                    jnp.where(lane_k == DIFF_DH + 3, -slopes[g] * row_hi, 0.0)))
        qs.append((jnp.where(lane_k < DIFF_DH, qf, qfeat).T.astype(BF16),
                   jnp.where(lane_k < DIFF_DH, pltpu.roll(qf, DIFF_DH, 1), qfeat).T.astype(BF16)))

    n_maps = 2 * nh
    n_stats = 2 * n_cols + 1

    def scores(i, j):
        rows = pl.ds(pl.multiple_of(j * tile, tile), tile)
        k_ref_i = k1_ref if i % 2 == 0 else k2_ref
        return jnp.dot(k_ref_i[i // 2, rows, :], qs[i // 2][i % 2], preferred_element_type=F32)

    def softmax_part(stats, s, c, fix):
        out, alphas, ps = (), [], []
        for col in range(n_cols):
            lanes = slice(col * LANES, (col + 1) * LANES)
            n_keys = tile if fix is None else (col + 1) * LANES
            sh = s[:n_keys, lanes]
            if fix is not None:
                sh = sh + fix[:n_keys, lanes]
            m, l = stats[2 * col], stats[2 * col + 1]
            m_new = jnp.maximum(m, jnp.max(sh, axis=0, keepdims=True) + c)
            alpha = jnp.exp(m - m_new)
            p = jnp.exp(sh - (m_new - c))
            out += (m_new, alpha * l + jnp.sum(p, axis=0, keepdims=True))
            p = p.astype(BF16)
            alphas.append(jnp.broadcast_to(alpha, (DIFF_DV, LANES)))
            if n_keys < tile:
                p = jnp.concatenate([p, jnp.zeros((tile - n_keys, LANES), BF16)], axis=0)
            ps.append(p)
        return out, jnp.concatenate(alphas, axis=1), jnp.concatenate(ps, axis=1)

    def step(stats, j, cs, fixes):
        ss = [scores(i, j) for i in range(n_maps)]
        parts = [softmax_part(stats[n_stats * i:n_stats * (i + 1)], ss[i], cs[i // 2],
                              None if fixes is None else fixes[i // 2])
                 for i in range(n_maps)]
        out = ()
        for i, (st, alpha, p) in enumerate(parts):
            a = alpha * stats[n_stats * i + n_stats - 1] + jnp.dot(
                vt_ref[i // 2, j], p, preferred_element_type=F32)
            out += st + (a,)
        return out

    def body(j, stats):
        off = jnp.full((1, LANES), (j - qi) * tile, I32).astype(F32)
        return step(stats, j, [off * slopes[g] for g in range(nh)], None)

    row = lambda v: jnp.full((1, LANES), v, F32)
    init = ((row(NEG), row(0.0)) * n_cols + (jnp.zeros((DIFF_DV, tile), F32),)) * n_maps
    stats = lax.fori_loop(0, qi, body, init)

    @pl.when((pl.program_id(0) == 0) & (pl.program_id(1) == 0) & (qi == 0))
    def _():
        r_i = lax.broadcasted_iota(I32, (tile, tile), 0)
        c_i = lax.broadcasted_iota(I32, (tile, tile), 1)
        allowed = (r_i >> CHUNK_SHIFT) <= (c_i >> CHUNK_SHIFT)
        geo_ref[0] = jnp.where(r_i > c_i, (c_i - r_i).astype(F32), 0.0)
        geo_ref[1] = jnp.where(allowed, 0.0, NEG)

    fixes = [(2.0 * slopes[g]) * geo_ref[0] + geo_ref[1] for g in range(nh)]
    stats = step(stats, qi, [row(0.0)] * nh, fixes)
    lam = _lam(lq1, lk1, lq2, lk2)
    for g in range(nh):
        res = []
        for i in (2 * g, 2 * g + 1):
            st = stats[n_stats * i:n_stats * (i + 1)]
            l = jnp.concatenate([jnp.broadcast_to(st[2 * col + 1], (DIFF_DV, LANES))
                                 for col in range(n_cols)], axis=1)
            res.append(st[-1] / l)
        o = (res[0] - lam * res[1]).T
        o_ref[:, hcols[g]] = (_rms(o, dn_ref[...]) * (1.0 - LAM_INIT)).astype(o_ref.dtype)


def _attn_prompt(qb, kb, vb, lq1, lk1, lq2, lk2, dn, B, L):
    tile = min(ATT_TILE, L)
    nq = L // tile
    nh = ATT_HEADS_PER_STEP
    qmap = lambda b, h, i: (b * nq + i, h)
    kvmap = lambda b, h, i: (b, h)
    cmap = lambda b, h, i: (0, 0)
    lspec = pl.BlockSpec((1, DIFF_DH), cmap)
    return pl.pallas_call(
        functools.partial(_attn_prompt_kernel, tile=tile, nh=nh),
        grid=(B, DIFF_HEADS // nh, nq),
        in_specs=[pl.BlockSpec(memory_space=pltpu.SMEM),
                  pl.BlockSpec((tile, nh * DIFF_DV), qmap),
                  pl.BlockSpec((L, nh * DIFF_DV), kvmap), pl.BlockSpec((L, nh * DIFF_DV), kvmap),
                  lspec, lspec, lspec, lspec, pl.BlockSpec((1, DIFF_DV), cmap)],
        out_specs=pl.BlockSpec((tile, nh * DIFF_DV), qmap),
        out_shape=jax.ShapeDtypeStruct((B * L, D_MODEL), BF16),
        scratch_shapes=[pltpu.VMEM((nh, L, DIFF_DV), BF16), pltpu.VMEM((nh, L, DIFF_DV), BF16),
                        pltpu.VMEM((nh, nq, DIFF_DV, tile), BF16),
                        pltpu.VMEM((2, tile, tile), F32)],
        compiler_params=_cparams(("arbitrary", "arbitrary", "arbitrary")),
        name="attn_prompt",
    )(_alibi_slopes(), qb, kb, vb, lq1, lk1, lq2, lk2, dn)


def _attn_sample_kernel(slope_ref, q_ref, kp_ref, vp_ref, kn_ref, vn_ref, lq1, lk1, lq2, lk2,
                        dn_ref, o_ref, m_ref, l_ref, acc_ref, kbuf, vbuf, sem, *, past, lq, chunk):
    b = pl.program_id(0)
    c = pl.program_id(1)
    n_c = pl.num_programs(1)
    step = b * n_c + c
    slot = step & 1

    def slab_copies(bb, cc, s, fn):
        rows = pl.ds(cc * chunk, chunk)
        for hd in range(DIFF_HEADS):
            fn(pltpu.make_async_copy(kp_ref.at[0, bb, rows, hd, :], kbuf.at[s, hd], sem.at[s]))
            fn(pltpu.make_async_copy(vp_ref.at[0, bb, rows, hd, :], vbuf.at[s, hd], sem.at[s]))

    @pl.when(step == 0)
    def _():
        slab_copies(b, c, slot, lambda cp: cp.start())

    @pl.when(step + 1 < pl.num_programs(0) * n_c)
    def _():
        wrap = c + 1 == n_c
        slab_copies(jnp.where(wrap, b + 1, b), jnp.where(wrap, 0, c + 1), 1 - slot,
                    lambda cp: cp.start())

    slab_copies(b, c, slot, lambda cp: cp.wait())

    @pl.when(c == 0)
    def _():
        m_ref[...] = jnp.full_like(m_ref, NEG)
        l_ref[...] = jnp.zeros_like(l_ref)
        acc_ref[...] = jnp.zeros_like(acc_ref)

    def geometry(key0, n_keys):
        r = lax.broadcasted_iota(I32, (2 * lq, n_keys), 0)
        qpos = past + jnp.where(r >= lq, r - lq, r)
        kpos = key0 + lax.broadcasted_iota(I32, (2 * lq, n_keys), 1)
        allowed = (kpos >> CHUNK_SHIFT) <= (qpos >> CHUNK_SHIFT)
        return jnp.abs(qpos - kpos).astype(F32), jnp.where(allowed, 0.0, NEG)

    def block(hd, k, v, geo):
        dist, mask = geo
        q1, q2 = _split_q(q_ref[:, hd * DIFF_DV:(hd + 1) * DIFF_DV])
        s = _nt(jnp.concatenate([q1, q2], axis=0), k)
        s = s + (mask - slope_ref[hd] * dist)
        m_old = m_ref[hd]
        m_new = jnp.maximum(m_old, jnp.max(s, axis=-1, keepdims=True))
        alpha = jnp.exp(m_old - m_new)
        p = jnp.exp(s - m_new)
        l_ref[hd] = alpha * l_ref[hd] + jnp.sum(p, axis=-1, keepdims=True)
        acc_ref[hd] = alpha * acc_ref[hd] + jnp.dot(p.astype(BF16), v, preferred_element_type=F32)
        m_ref[hd] = m_new

    geo = geometry(c * chunk, chunk)
    for hd in range(DIFF_HEADS):
        block(hd, kbuf[slot, hd].astype(BF16), vbuf[slot, hd].astype(BF16), geo)

    @pl.when(c == n_c - 1)
    def _():
        lam = _lam(lq1, lk1, lq2, lk2)
        geo_new = geometry(past, lq)
        for hd in range(DIFF_HEADS):
            cols = slice(hd * DIFF_DV, (hd + 1) * DIFF_DV)
            block(hd, kn_ref[:, cols], vn_ref[:, cols], geo_new)
            o = acc_ref[hd] / l_ref[hd]
            o = o[:lq] - lam * o[lq:]
            o_ref[:, hd * DIFF_DV:(hd + 1) * DIFF_DV] = (
                _rms(o, dn_ref[...]) * (1.0 - LAM_INIT)).astype(o_ref.dtype)


def _attn_sample(qb, cache_k, cache_v, kb, vb, lq1, lk1, lq2, lk2, dn, B, L):
    past = cache_k.shape[2]
    chunk = min(SAMPLE_KEY_CHUNK, past)
    n_chunks = past // chunk
    bmap = lambda b, c: (b, 0)
    cmap = lambda b, c: (0, 0)
    lspec = pl.BlockSpec((1, DIFF_DH), cmap)
    tok = pl.BlockSpec((L, D_MODEL), bmap)
    cache = pl.BlockSpec(memory_space=pl.ANY)
    slabs = pltpu.VMEM((2, DIFF_HEADS, chunk, DIFF_DV), F32)
    return pl.pallas_call(
        functools.partial(_attn_sample_kernel, past=past, lq=L, chunk=chunk),
        grid=(B, n_chunks),
        in_specs=[pl.BlockSpec(memory_space=pltpu.SMEM), tok, cache, cache, tok, tok,
                  lspec, lspec, lspec, lspec, pl.BlockSpec((1, DIFF_DV), cmap)],
        out_specs=tok,
        out_shape=jax.ShapeDtypeStruct((B * L, D_MODEL), BF16),
        scratch_shapes=[pltpu.VMEM((DIFF_HEADS, 2 * L, 1), F32),
                        pltpu.VMEM((DIFF_HEADS, 2 * L, 1), F32),
                        pltpu.VMEM((DIFF_HEADS, 2 * L, DIFF_DV), F32),
                        slabs, slabs, pltpu.SemaphoreType.DMA((2,))],
        compiler_params=_cparams(("arbitrary", "arbitrary")),
        name="attn_sample",
    )(_alibi_slopes(), qb, cache_k, cache_v, kb, vb, lq1, lk1, lq2, lk2, dn)


ROUTE_E1, ROUTE_E2, ROUTE_W1, ROUTE_W2, ROUTE_R1, ROUTE_R2, ROUTE_P1, ROUTE_P2 = range(8)
ROUTE_ROWS = 8
TABLE_RUN, TABLE_OFF, TABLE_BASE = range(3)
GROUP_LANE0 = N_EXPERTS


def _merge_kernel(x_ref, oa_ref, ob_ref, ga_ref, gb_ref, wa_ref, wb_ref, wo_ref, nf_ref,
                  wr_ref, br_ref, x1_ref, h2_ref, route_ref, route_t_ref, cnt_ref, tables_ref,
                  run_ref, before_ref):
    i = pl.program_id(0)

    @pl.when(i == 0)
    def _():
        run_ref[...] = jnp.zeros_like(run_ref)
        r_i = lax.broadcasted_iota(I32, before_ref.shape, 0)
        c_i = lax.broadcasted_iota(I32, before_ref.shape, 1)
        before_ref[...] = jnp.where(c_i < r_i, 1.0, 0.0).astype(BF16)

    u_a = jnp.dot(oa_ref[...], wa_ref[...], preferred_element_type=F32)
    u_b = jnp.dot(ob_ref[...], wb_ref[...], preferred_element_type=F32)
    mix = (jax.nn.sigmoid(ga_ref[...].astype(F32)) * u_a
           + jax.nn.sigmoid(gb_ref[...].astype(F32)) * u_b)
    x1 = x_ref[...] + jnp.dot(mix.astype(BF16), wo_ref[...], preferred_element_type=F32)
    x1_ref[...] = x1
    h2 = _rms(x1, nf_ref[...])
    h2_ref[...] = h2.astype(h2_ref.dtype)

    tm = h2.shape[0]
    split = lambda a: (a.astype(BF16), (a - a.astype(BF16).astype(F32)).astype(BF16))
    w_pair = jnp.concatenate(split(wr_ref[...]), axis=1)
    parts = sum(jnp.dot(a, w_pair, preferred_element_type=F32) for a in split(h2))
    logits = parts[:, :LANES] + parts[:, LANES:] + br_ref[...]
    lane = lax.broadcasted_iota(I32, (tm, LANES), 1)
    big = jnp.int32(LANES)
    g_mask = (lane >= GROUP_LANE0) & (lane < GROUP_LANE0 + N_GROUPS)
    gl = jnp.where(g_mask, logits, -jnp.inf)
    gmax = jnp.max(gl, axis=-1, keepdims=True)
    g_sel = jnp.min(jnp.where(gl == gmax, lane - GROUP_LANE0, big), axis=-1, keepdims=True)
    p_g = 1.0 / jnp.sum(jnp.where(g_mask, jnp.exp(logits - gmax), 0.0), axis=-1, keepdims=True)
    e_mask = (lane < N_EXPERTS) & ((lane >> EPG_SHIFT) == g_sel)
    el = jnp.where(e_mask, logits, -jnp.inf)
    v1 = jnp.max(el, axis=-1, keepdims=True)
    i1 = jnp.min(jnp.where(el == v1, lane, big), axis=-1, keepdims=True)
    el2 = jnp.where(lane == i1, -jnp.inf, el)
    v2 = jnp.max(el2, axis=-1, keepdims=True)
    i2 = jnp.min(jnp.where(el2 == v2, lane, big), axis=-1, keepdims=True)
    t = jnp.exp(v2 - v1)
    w1 = p_g / (1.0 + t)
    w2 = p_g * t / (1.0 + t)

    oh1 = lane == i1
    oh2 = lane == i2
    cnt = jnp.where(oh1, 1.0, 0.0) + jnp.where(oh2, 1.0, 0.0)
    local = jnp.dot(before_ref[...], cnt.astype(BF16), preferred_element_type=F32)
    n_run = jnp.floor((jnp.sum(cnt, axis=0, keepdims=True) + (RUN_ALIGN - 1.0)) * (1.0 / RUN_ALIGN))
    n_run = n_run * RUN_ALIGN
    e_r = lax.broadcasted_iota(I32, (LANES, LANES), 0)
    e_c = lax.broadcasted_iota(I32, (LANES, LANES), 1)
    earlier = jnp.where(e_r < e_c, 1.0, 0.0).astype(BF16)
    t_off = jnp.dot(jnp.broadcast_to(n_run, (SUBLANES, LANES)).astype(BF16), earlier,
                    preferred_element_type=F32)[:1]
    run = run_ref[...]
    pick = lambda oh, v: jnp.sum(jnp.where(oh, v, 0.0), axis=-1, keepdims=True)
    rank1, rank2 = pick(oh1, run + local), pick(oh2, run + local)
    pos1, pos2 = pick(oh1, t_off + local), pick(oh2, t_off + local)
    run_ref[...] = run + n_run
    cnt_ref[...] = run + n_run
    tables_ref[0] = jnp.concatenate([n_run, t_off, run, jnp.zeros((SUBLANES - 3, LANES), F32)], axis=0)

    route = jnp.zeros((tm, LANES), F32)
    for pos, val in ((ROUTE_E1, i1.astype(F32)), (ROUTE_E2, i2.astype(F32)), (ROUTE_W1, w1),
                     (ROUTE_W2, w2), (ROUTE_R1, rank1), (ROUTE_R2, rank2), (ROUTE_P1, pos1),
                     (ROUTE_P2, pos2)):
        route = jnp.where(lane == pos, val, route)
    route_ref[...] = route
    route_t_ref[...] = route.T[:ROUTE_ROWS, :].astype(I32)


def _merge(x, oa, ob, ga, gb, wa, wb, wo, nf, wr, br):
    T = x.shape[0]
    tm = min(TOKEN_TILE, T)
    row = lambda i: (i, 0)
    const = lambda i: (0, 0)
    wspec = pl.BlockSpec((D_MODEL, D_MODEL), const)
    tile = pl.BlockSpec((tm, D_MODEL), row)
    return pl.pallas_call(
        _merge_kernel,
        grid=(T // tm,),
        in_specs=[tile, tile, tile, tile, tile, wspec, wspec, wspec,
                  pl.BlockSpec((1, D_MODEL), const),
                  pl.BlockSpec((D_MODEL, LANES), const), pl.BlockSpec((1, LANES), const)],
        out_specs=(tile, tile, pl.BlockSpec((tm, LANES), row),
                   pl.BlockSpec((ROUTE_ROWS, tm), lambda i: (0, i)), pl.BlockSpec((1, LANES), const),
                   pl.BlockSpec((1, SUBLANES, LANES), lambda i: (i, 0, 0))),
        out_shape=(jax.ShapeDtypeStruct((T, D_MODEL), F32),
                   jax.ShapeDtypeStruct((T, D_MODEL), BF16),
                   jax.ShapeDtypeStruct((T, LANES), F32),
                   jax.ShapeDtypeStruct((ROUTE_ROWS, T), I32),
                   jax.ShapeDtypeStruct((1, LANES), F32),
                   jax.ShapeDtypeStruct((T // tm, SUBLANES, LANES), F32)),
        scratch_shapes=[pltpu.VMEM((1, LANES), F32), pltpu.VMEM((tm, tm), BF16)],
        compiler_params=_cparams(("arbitrary",)),
        name="merge",
    )(x, oa, ob, ga, gb, wa, wb, wo, nf, wr, br)


RUN_BLOCKS = tuple(1 << b for b in range(9, 3, -1))


def _for_each_run_block(run_ref, off_ref, base_ref, step, fn):
    for e in range(N_EXPERTS):
        idx = step * N_EXPERTS + e
        n_left, src, dst = run_ref[idx], off_ref[idx], base_ref[idx]
        for rows in RUN_BLOCKS:
            take = n_left & rows

            @pl.when(take != 0)
            def _(src=src, dst=dst, rows=rows):
                fn(pl.multiple_of(src, RUN_ALIGN), pl.multiple_of(dst, RUN_ALIGN), rows)
            src, dst = src + take, dst + take


def _dispatch_kernel(ends_ref, run_ref, off_ref, base_ref, ridx_ref, h2_ref, xs_ref, zero_ref,
                     buf_ref, sem):
    i = pl.program_id(0)
    n = pl.num_programs(0)
    slot = i & 1
    slot_block = zero_ref.shape[0]
    slot_shift = slot_block.bit_length() - 1
    n_blocks = xs_ref.shape[0] // slot_block
    n_rows, tm = buf_ref.shape[1], h2_ref.shape[0]
    zero_sem = sem.at[2]

    def run_copies(step, s, fn):
        _for_each_run_block(run_ref, off_ref, base_ref, step, lambda src, dst, rows: fn(
            pltpu.make_async_copy(buf_ref.at[s, pl.ds(src, rows)], xs_ref.at[pl.ds(dst, rows)],
                                  sem.at[s])))

    @pl.when(i == 0)
    def _():
        zero_ref[...] = jnp.zeros_like(zero_ref)

        def zero_block(blk):
            start = pl.multiple_of(blk * slot_block, slot_block)
            return pltpu.make_async_copy(zero_ref, xs_ref.at[pl.ds(start, slot_block)], zero_sem)

        def expert_tail(e):
            return jnp.maximum((ends_ref[e] >> slot_shift) - 1, 0)

        n_used = ends_ref[N_EXPERTS - 1] >> slot_shift
        for e in range(N_EXPERTS):
            zero_block(expert_tail(e)).start()
        lax.fori_loop(n_used, n_blocks, lambda blk, c: (zero_block(blk).start(), c)[1], 0)
        for e in range(N_EXPERTS):
            zero_block(expert_tail(e)).wait()
        lax.fori_loop(n_used, n_blocks, lambda blk, c: (zero_block(blk).wait(), c)[1], 0)

    @pl.when(i >= 2)
    def _():
        run_copies(i - 2, slot, lambda cp: cp.wait())

    row = lax.broadcasted_iota(I32, (n_rows, tm), 0)
    picked = (row == ridx_ref[ROUTE_P1:ROUTE_P1 + 1, :]) | (row == ridx_ref[ROUTE_P2:ROUTE_P2 + 1, :])
    buf_ref[slot] = jnp.dot(jnp.where(picked, 1.0, 0.0).astype(BF16), h2_ref[...],
                            preferred_element_type=F32).astype(BF16)
    run_copies(i, slot, lambda cp: cp.start())

    @pl.when(i == n - 1)
    def _():
        run_copies(i, slot, lambda cp: cp.wait())

    @pl.when((i == n - 1) & (i >= 1))
    def _():
        run_copies(i - 1, 1 - slot, lambda cp: cp.wait())


def _dispatch(pad_ends, tables, ridx, h2, n_slots, slot_block):
    T = h2.shape[0]
    tm = min(TOKEN_TILE, T)
    smem = pl.BlockSpec(memory_space=pltpu.SMEM)
    n_rows = 2 * tm + N_EXPERTS * RUN_ALIGN
    return pl.pallas_call(
        _dispatch_kernel,
        grid=(T // tm,),
        in_specs=[smem, smem, smem, smem,
                  pl.BlockSpec((ROUTE_ROWS, tm), lambda i: (0, i)),
                  pl.BlockSpec((tm, D_MODEL), lambda i: (i, 0))],
        out_specs=pl.BlockSpec(memory_space=pl.ANY),
        out_shape=jax.ShapeDtypeStruct((n_slots, D_MODEL), BF16),
        scratch_shapes=[pltpu.VMEM((slot_block, D_MODEL), BF16),
                        pltpu.VMEM((2, n_rows, D_MODEL), BF16), pltpu.SemaphoreType.DMA((3,))],
        compiler_params=_cparams(("arbitrary",)),
        name="dispatch",
    )(pad_ends, *tables, ridx, h2)


def _experts_kernel(be_ref, nu_ref, xs_ref, wg_ref, wu_ref, wd_ref, y_ref, wg16, wu16, wd16):
    i = pl.program_id(0)
    used = i < nu_ref[0]

    @pl.when(used & ((i == 0) | (be_ref[i] != be_ref[jnp.maximum(i - 1, 0)])))
    def _():
        wg16[...] = wg_ref[0].astype(BF16)
        wu16[...] = wu_ref[0].astype(BF16)
        wd16[...] = wd_ref[0].astype(BF16)

    @pl.when(used)
    def _():
        x = xs_ref[...]
        g = jnp.dot(x, wg16[...], preferred_element_type=F32)
        u = jnp.dot(x, wu16[...], preferred_element_type=F32)
        a = (g * jax.nn.sigmoid(g) * u).astype(BF16)
        y_ref[...] = jnp.dot(a, wd16[...], preferred_element_type=F32).astype(y_ref.dtype)

    @pl.when(jnp.logical_not(used))
    def _():
        y_ref[...] = jnp.zeros_like(y_ref)


def _experts(block_e, n_used, xs, wg, wu, wd, slot_block):
    n_blocks = xs.shape[0] // slot_block
    last = lambda i, nu: jnp.minimum(i, jnp.maximum(nu[0] - 1, 0))
    blk = lambda i, be, nu: (last(i, nu), 0)
    wmap = lambda i, be, nu: (be[last(i, nu)], 0, 0)
    grid_spec = pltpu.PrefetchScalarGridSpec(
        num_scalar_prefetch=2,
        grid=(n_blocks,),
        in_specs=[pl.BlockSpec((slot_block, D_MODEL), blk),
                  pl.BlockSpec((1, D_MODEL, D_EXPERT), wmap),
                  pl.BlockSpec((1, D_MODEL, D_EXPERT), wmap),
                  pl.BlockSpec((1, D_EXPERT, D_MODEL), wmap)],
        out_specs=pl.BlockSpec((slot_block, D_MODEL), lambda i, be, nu: (i, 0)),
        scratch_shapes=[pltpu.VMEM((D_MODEL, D_EXPERT), BF16), pltpu.VMEM((D_MODEL, D_EXPERT), BF16),
                        pltpu.VMEM((D_EXPERT, D_MODEL), BF16)],
    )
    return pl.pallas_call(
        _experts_kernel,
        grid_spec=grid_spec,
        out_shape=jax.ShapeDtypeStruct((xs.shape[0], D_MODEL), BF16),
        compiler_params=_cparams(("arbitrary",)),
        name="experts",
    )(block_e, n_used, xs, wg, wu, wd)


def _combine_kernel(run_ref, off_ref, base_ref, x1_ref, route_ref, nf_ref, yb_ref, y_ref,
                    buf_ref, sem):
    i = pl.program_id(0)
    n = pl.num_programs(0)
    tm, n_rows = x1_ref.shape[0], buf_ref.shape[1]
    slot = i & 1

    def run_copies(step, s, fn):
        _for_each_run_block(run_ref, off_ref, base_ref, step, lambda dst, src, rows: fn(
            pltpu.make_async_copy(yb_ref.at[pl.ds(src, rows)], buf_ref.at[s, pl.ds(dst, rows)],
                                  sem.at[s])))

    @pl.when(i == 0)
    def _():
        buf_ref[...] = jnp.zeros_like(buf_ref)
        run_copies(i, slot, lambda cp: cp.start())

    @pl.when(i + 1 < n)
    def _():
        run_copies(i + 1, 1 - slot, lambda cp: cp.start())

    run_copies(i, slot, lambda cp: cp.wait())
    route = route_ref[...]
    rows_bf = buf_ref[slot]
    col = lax.broadcasted_iota(I32, (tm, n_rows), 1)
    pick = jnp.concatenate(
        [jnp.where(col == route[:, p:p + 1].astype(I32), 1.0, 0.0).astype(BF16)
         for p in (ROUTE_P1, ROUTE_P2)], axis=0)
    got = jnp.dot(pick, rows_bf, preferred_element_type=F32)
    moe = route[:, ROUTE_W1:ROUTE_W1 + 1] * got[:tm] + route[:, ROUTE_W2:ROUTE_W2 + 1] * got[tm:]
    y_ref[...] = _rms(x1_ref[...] + moe, nf_ref[...])


def _combine(tables, x1, route, nfinal, yb):
    T = x1.shape[0]
    tm = min(TOKEN_TILE, T)
    row = lambda i: (i, 0)
    smem = pl.BlockSpec(memory_space=pltpu.SMEM)
    n_rows = 2 * tm + N_EXPERTS * RUN_ALIGN
    return pl.pallas_call(
        _combine_kernel,
        grid=(T // tm,),
        in_specs=[smem, smem, smem,
                  pl.BlockSpec((tm, D_MODEL), row), pl.BlockSpec((tm, LANES), row),
                  pl.BlockSpec((1, D_MODEL), lambda i: (0, 0)),
                  pl.BlockSpec(memory_space=pl.ANY)],
        out_specs=pl.BlockSpec((tm, D_MODEL), row),
        out_shape=jax.ShapeDtypeStruct((T, D_MODEL), F32),
        scratch_shapes=[pltpu.VMEM((2, n_rows, D_MODEL), BF16), pltpu.SemaphoreType.DMA((2,))],
        compiler_params=_cparams(("arbitrary",)),
        name="combine",
    )(*tables, x1, route, nfinal, yb)


def _moe(x1, h2, route, route_t, counts, tables, wg, wu, wd, nfinal):
    T = x1.shape[0]
    tm = min(TOKEN_TILE, T)
    slot_block = SLOT_BLOCK if 2 * T >= N_EXPERTS * SLOT_BLOCK else SLOT_BLOCK_SMALL
    n_slots = T * 2 + (T // tm) * N_EXPERTS * RUN_ALIGN + N_EXPERTS * slot_block
    n_blocks = -(-n_slots // slot_block)
    n_slots = n_blocks * slot_block
    cnt = counts[0, :N_EXPERTS].astype(I32)
    padded = (cnt + slot_block - 1) // slot_block * slot_block
    pad_ends = jnp.cumsum(padded)
    pad_starts = (pad_ends - padded).astype(I32)
    pad_ends = pad_ends.astype(I32)
    block_start = jnp.arange(n_blocks, dtype=I32) * slot_block
    block_e = jnp.sum((block_start[:, None] >= pad_ends[None, :]).astype(I32), axis=1)
    block_e = jnp.minimum(block_e, N_EXPERTS - 1)
    n_used = pad_ends[-1:] // slot_block
    per_run = lambda row: tables[:, row, :N_EXPERTS].astype(I32)
    run_tables = (per_run(TABLE_RUN).reshape(-1), per_run(TABLE_OFF).reshape(-1),
                  (per_run(TABLE_BASE) + pad_starts[None, :]).reshape(-1))
    xs = _dispatch(pad_ends, run_tables, route_t, h2, n_slots, slot_block)
    yb = _experts(block_e, n_used, xs, wg, wu, wd, slot_block)
    return _combine(run_tables, x1, route, nfinal, yb)


def _layer(x, s0, k_past, v_past, p, B, L):
    row2 = lambda a: a.reshape(1, -1)
    qka, va, ra, qb, kb, vb, ga, gb, da, k5, v5 = _in_proj(x, row2(p["norm_mix"]), p["w_in"])
    oa, s_new = _gla(qka, va, ra, da, p["w_decay"], row2(p["b_decay"]), row2(p["gla_norm"]),
                     s0, B, L)
    lams = [row2(p[n]) for n in ("lambda_q1", "lambda_k1", "lambda_q2", "lambda_k2")]
    dn = row2(p["diff_norm"])
    if k_past is None:
        ob = _attn_prompt(qb, kb, vb, *lams, dn, B, L)
    else:
        ob = _attn_sample(qb, k_past, v_past, kb, vb, *lams, dn, B, L)
    x1, h2, route, route_t, counts, tables = _merge(
        x, oa, ob, ga, gb, p["w_proj_a"], p["w_proj_b"], p["w_out"], row2(p["norm_ffn"]),
        p["w_router"], p["b_router"])
    y = _moe(x1, h2, route, route_t, counts, tables, p["w_gate"], p["w_up"], p["w_down"],
             row2(p["norm_final"]))
    return y, s_new, k5, v5


def kernel(x_prompt, x_sample, cache_k, cache_v, state_gla, norm_mix, w_in, w_decay, b_decay,
           gla_norm, w_proj_a, lambda_q1, lambda_k1, lambda_q2, lambda_k2, diff_norm, w_proj_b,
           w_out, norm_ffn, w_router_group, b_router_group, w_router_expert, b_router_expert,
           w_gate, w_up, w_down, norm_final):
    B, L, D = x_prompt.shape
    Bs, Ls, _ = x_sample.shape
    w_router = jnp.concatenate(
        [w_router_expert[0], w_router_group[0],
         jnp.zeros((D, LANES - N_EXPERTS - N_GROUPS), F32)], axis=1)
    b_router = jnp.concatenate(
        [b_router_expert[0], b_router_group[0],
         jnp.zeros((LANES - N_EXPERTS - N_GROUPS,), F32)]).reshape(1, LANES)
    p = dict(
        norm_mix=norm_mix[0], w_in=_prep_w_in(w_in[0]),
        w_decay=jnp.pad(w_decay[0], ((0, LANES - GLA_RANK), (0, 0))), b_decay=b_decay[0],
        gla_norm=gla_norm[0], w_proj_a=w_proj_a[0].astype(BF16),
        lambda_q1=lambda_q1[0], lambda_k1=lambda_k1[0], lambda_q2=lambda_q2[0],
        lambda_k2=lambda_k2[0], diff_norm=diff_norm[0], w_proj_b=w_proj_b[0].astype(BF16),
        w_out=w_out[0].astype(BF16), norm_ffn=norm_ffn[0], w_router=w_router, b_router=b_router,
        w_gate=w_gate[0], w_up=w_up[0], w_down=w_down[0],
        norm_final=norm_final)

    s0p = jnp.zeros((B, GLA_HEADS, GLA_DK, GLA_DV), F32)
    yp, sp, kp, vp = _layer(x_prompt.reshape(B * L, D), s0p, None, None, p, B, L)
    ys, ss, ks, vs = _layer(x_sample.reshape(Bs * Ls, D), state_gla[0], cache_k, cache_v,
                            p, Bs, Ls)
    kv = lambda a, b, l: a.reshape(1, b, l, DIFF_HEADS, DIFF_DV)
    return (yp.reshape(B, L, D), ys.reshape(Bs, Ls, D), kv(kp, B, L), kv(vp, B, L), sp[None],
            kv(ks, Bs, Ls), kv(vs, Bs, Ls), ss[None])
```
